```python
import math
import jax, jax.numpy as jnp
from jax import lax
import numpy as np

D_MODEL = 1024
BATCH = 8
SEQ = 4096
DEPTH = 4

CTX_LEN = 256
GRID_W = 64
HEAD_DIM = 64
ROPE_BASE = 10000.0
EPS = 1e-6
BLOCK = 128
DIFF_HEADS = D_MODEL // (2 * HEAD_DIM)
DIFF_V_DIM = 2 * HEAD_DIM
HY_WIDTH = D_MODEL
HY_BANDS = 16
HY_EMB_DIM = 1 + 2 * HY_BANDS
HY_FF = 64
HY_FAST_DECAY = 0.3
HY_SLOW_DECAY = 1.5
HY_TARGET = 1e-2
SHORT_CONV = 3
SWA_Q_HEADS = D_MODEL // HEAD_DIM
SWA_KV_HEADS = 4
SWA_GROUP = SWA_Q_HEADS // SWA_KV_HEADS
WINDOW = 128
N_BRANCH = 3
N_EXPERTS = 16
D_EXPERT = 1024
CAPACITY_FACTOR = 2
IN_WIDTHS = (DIFF_HEADS * 2 * HEAD_DIM, DIFF_HEADS * 2 * HEAD_DIM, DIFF_HEADS * DIFF_V_DIM,
             3 * HY_WIDTH, SWA_Q_HEADS * HEAD_DIM, SWA_KV_HEADS * HEAD_DIM, SWA_KV_HEADS * HEAD_DIM,
             N_BRANCH * D_MODEL)
D_IN = sum(IN_WIDTHS)

kernel_name = "hybrid_diffattn_hyena_swa_ecmoe_dit"


def rmsnorm(x, g):
    x32 = x.astype(jnp.float32)
    y = x32 * lax.rsqrt(jnp.mean(x32 * x32, axis=-1, keepdims=True) + EPS)
    return y.astype(x.dtype) * g


def modulate(h, shift, scale):
    return h * (1 + scale) + shift


def split_in(proj):
    offs, acc = [], 0
    for w in IN_WIDTHS[:-1]:
        acc += w
        offs.append(acc)
    return jnp.split(proj, offs, axis=-1)


def axial_rope(n_tok, dtype):
    rows = n_tok // GRID_W
    row = jnp.broadcast_to(jnp.arange(rows, dtype=jnp.float32)[:, None], (rows, GRID_W)).reshape(-1)
    col = jnp.broadcast_to(jnp.arange(GRID_W, dtype=jnp.float32)[None, :], (rows, GRID_W)).reshape(-1)
    n_freq = HEAD_DIM // 4
    inv = ROPE_BASE ** (-jnp.arange(n_freq, dtype=jnp.float32) / n_freq)
    ang = jnp.stack([row[:, None] * inv, col[:, None] * inv], axis=1)
    return jnp.cos(ang).astype(dtype), jnp.sin(ang).astype(dtype)


def apply_rope(x, cos, sin):
    shape = x.shape
    xr = x.reshape(shape[:-1] + (2, 2, HEAD_DIM // 4))
    bshape = (shape[1],) + (1,) * (x.ndim - 3) + (2, HEAD_DIM // 4)
    c, s = cos.reshape(bshape), sin.reshape(bshape)
    x1, x2 = xr[..., 0, :], xr[..., 1, :]
    return jnp.stack([x1 * c - x2 * s, x2 * c + x1 * s], axis=-2).reshape(shape)


def diff_attend(q, k, v, lam):
    s = jnp.einsum('bqhmd,bkhmd->bhmqk', q, k).astype(jnp.float32) * HEAD_DIM ** -0.5
    a = jax.nn.softmax(s, axis=-1)
    w = a[:, :, 0] - lam * a[:, :, 1]
    return jnp.einsum('bhqk,bkhe->bqhe', w.astype(v.dtype), v)


def diff_latent(q, k_all, v_all, lam):
    B, L = q.shape[:2]
    nb = L // BLOCK
    qb = jnp.moveaxis(q.reshape((B, nb, BLOCK) + q.shape[2:]), 1, 0)
    o = lax.map(lambda qi: diff_attend(qi, k_all, v_all, lam), qb)
    return jnp.moveaxis(o, 0, 1).reshape(B, L, DIFF_HEADS, DIFF_V_DIM)


def diff_post(o, g, lambda_init):
    return (rmsnorm(o, g) * (1.0 - lambda_init)).reshape(o.shape[0], o.shape[1], -1)


def short_conv(u, w, b):
    L = u.shape[1]
    up = jnp.pad(u, ((0, 0), (SHORT_CONV // 2, SHORT_CONV // 2), (0, 0)))
    return sum(up[:, j:j + L] * w[j] for j in range(SHORT_CONV)) + b


def hyena_filter(L, p):
    f32 = jnp.float32
    t = jnp.linspace(0.0, 1.0, L, dtype=f32)[:, None]
    w = 2.0 * math.pi * jnp.arange(L, dtype=f32)[:, None] / L
    f = jnp.linspace(1e-4, HY_BANDS - 1, HY_BANDS, dtype=f32)[None, :]
    emb = jnp.concatenate([t, jnp.cos(f * w), -jnp.sin(f * w)], axis=-1)
    freq = p['hy_sin_freq'].astype(f32)
    z = jnp.sin(freq * (emb @ p['hy_ff_w1'].astype(f32) + p['hy_ff_b1'].astype(f32)))
    z = jnp.sin(freq * (z @ p['hy_ff_w2'].astype(f32) + p['hy_ff_b2'].astype(f32)))
    h = (z @ p['hy_ff_w3'].astype(f32)).reshape(L, 2, HY_WIDTH)
    deltas = jnp.abs(jnp.linspace(math.log(HY_TARGET) / HY_FAST_DECAY,
                                  math.log(HY_TARGET) / HY_SLOW_DECAY, HY_WIDTH, dtype=f32))
    h = h * jnp.exp(-t * deltas)[:, None, :]
    return h / (jnp.sum(jnp.abs(h), axis=(0, 1), keepdims=True) + EPS)


def fft_long_conv(u, h, bias):
    L = u.shape[1]
    k = jnp.concatenate([h[:, 0], jnp.zeros((1, HY_WIDTH), jnp.float32), h[:0:-1, 1]], axis=0)
    u32 = u.astype(jnp.float32)
    uf = jnp.fft.rfft(u32, n=2 * L, axis=1)
    kf = jnp.fft.rfft(k, n=2 * L, axis=0)
    y = jnp.fft.irfft(uf * kf[None], n=2 * L, axis=1)[:, :L]
    return (y + u32 * bias.astype(jnp.float32)).astype(u.dtype)


def hyena(u_in, p):
    L = u_in.shape[1]
    u = short_conv(u_in, p['hy_conv_w'], p['hy_conv_b'])
    x0, x1, v = jnp.split(u, 3, axis=-1)
    return x0 * fft_long_conv(x1 * v, hyena_filter(L, p), p['hy_bias'])


def attend_sink(q, ks, vs, masks, sink):
    B, Lq = q.shape[:2]
    parts = []
    for k, m in zip(ks, masks):
        s = jnp.einsum('bqhgd,bkhd->bhgqk', q, k).astype(jnp.float32) * HEAD_DIM ** -0.5
        parts.append(s if m is None else jnp.where(m, s, -jnp.inf))
    sink_col = jnp.broadcast_to(sink.astype(jnp.float32)[None, :, :, None, None],
                                (B, SWA_KV_HEADS, SWA_GROUP, Lq, 1))
    pr = jax.nn.softmax(jnp.concatenate(parts + [sink_col], axis=-1), axis=-1)
    out, off = 0, 0
    for v in vs:
        n = v.shape[1]
        out = out + jnp.einsum('bhgqk,bkhd->bqhgd', pr[..., off:off + n].astype(v.dtype), v)
        off += n
    return out


def swa_latent(q, k, v, kc, vc, sink):
    B, L = q.shape[:2]
    nb = L // BLOCK
    pad = ((0, 0), (BLOCK, BLOCK), (0, 0), (0, 0))
    kp, vp = jnp.pad(k, pad), jnp.pad(v, pad)

    def block(i):
        start = i * BLOCK
        qb = lax.dynamic_slice_in_dim(q, start, BLOCK, axis=1)
        kb = lax.dynamic_slice_in_dim(kp, start, 3 * BLOCK, axis=1)
        vb = lax.dynamic_slice_in_dim(vp, start, 3 * BLOCK, axis=1)
        q_pos = start + jnp.arange(BLOCK)
        k_pos = start - BLOCK + jnp.arange(3 * BLOCK)
        valid = ((jnp.abs(k_pos[None, :] - q_pos[:, None]) <= WINDOW)
                 & (k_pos >= 0)[None, :] & (k_pos < L)[None, :])
        return attend_sink(qb, [kb, kc], [vb, vc], [valid, None], sink)

    o = lax.map(block, jnp.arange(nb))
    return jnp.moveaxis(o, 0, 1).reshape(B, L, SWA_Q_HEADS * HEAD_DIM)


def merge(ys, gate_pre, p):
    gates = jax.nn.sigmoid(gate_pre.astype(jnp.float32)).astype(gate_pre.dtype)
    gs = jnp.split(gates, N_BRANCH, axis=-1)
    merged = sum(gs[b] * (ys[b] @ p['w_branch'][b]) for b in range(N_BRANCH))
    return merged @ p['w_out']


def token_mixer(h, hc, p, lambda_init, ctx_out):
    B, L, _ = h.shape
    Lc = hc.shape[1]
    dq, dk, dv, hy, sq, sk, sv, gt = split_in(h @ p['w_in'])
    dqc, dkc, dvc, hyc, sqc, skc, svc, gtc = split_in(hc @ p['w_in'])
    cos, sin = axial_rope(L, h.dtype)
    lp = p['diff_lambda'].astype(jnp.float32)
    lam = jnp.exp(jnp.sum(lp[0] * lp[1])) - jnp.exp(jnp.sum(lp[2] * lp[3])) + lambda_init
    qk_shape = (DIFF_HEADS, 2, HEAD_DIM)
    q = apply_rope(dq.reshape((B, L) + qk_shape), cos, sin)
    k = apply_rope(dk.reshape((B, L) + qk_shape), cos, sin)
    v = dv.reshape(B, L, DIFF_HEADS, DIFF_V_DIM)
    kc = dkc.reshape((B, Lc) + qk_shape)
    vc = dvc.reshape(B, Lc, DIFF_HEADS, DIFF_V_DIM)
    y_diff = diff_post(diff_latent(q, jnp.concatenate([k, kc], axis=1), jnp.concatenate([v, vc], axis=1), lam),
                       p['diff_subln_g'], lambda_init)
    y_hy = hyena(hy, p)
    sink = p['swa_sink'].reshape(SWA_KV_HEADS, SWA_GROUP)
    s_q = apply_rope(sq.reshape(B, L, SWA_KV_HEADS, SWA_GROUP, HEAD_DIM), cos, sin)
    s_k = apply_rope(sk.reshape(B, L, SWA_KV_HEADS, HEAD_DIM), cos, sin)
    s_v = sv.reshape(B, L, SWA_KV_HEADS, HEAD_DIM)
    s_kc = skc.reshape(B, Lc, SWA_KV_HEADS, HEAD_DIM)
    s_vc = svc.reshape(B, Lc, SWA_KV_HEADS, HEAD_DIM)
    y_swa = swa_latent(s_q, s_k, s_v, s_kc, s_vc, sink)
    y = merge([y_diff, y_hy, y_swa], gt, p)
    if not ctx_out:
        return y, None
    qc = dqc.reshape((B, Lc) + qk_shape)
    yc_diff = diff_post(diff_attend(qc, kc, vc, lam), p['diff_subln_g'], lambda_init)
    yc_hy = hyena(hyc, p)
    yc_swa = attend_sink(sqc.reshape(B, Lc, SWA_KV_HEADS, SWA_GROUP, HEAD_DIM), [s_kc], [s_vc], [None],
                         sink).reshape(B, Lc, SWA_Q_HEADS * HEAD_DIM)
    yc = merge([yc_diff, yc_hy, yc_swa], gtc, p)
    return y, yc


def ec_moe(h, p):
    B, N, D = h.shape
    cap = CAPACITY_FACTOR * N // N_EXPERTS
    logits = jnp.einsum('bnd,de->bne', h, p['router_w']).astype(jnp.float32)
    aff = jax.nn.softmax(logits, axis=-1)
    g, idx = lax.top_k(jnp.swapaxes(aff, 1, 2), cap)
    flat = idx.reshape(B, N_EXPERTS * cap)
    xg = jnp.take_along_axis(h, flat[:, :, None], axis=1).reshape(B, N_EXPERTS, cap, D)
    a = jnp.einsum('becd,edf->becf', xg, p['moe_w1'])
    b = jnp.einsum('becd,edf->becf', xg, p['moe_w3'])
    y = jnp.einsum('becf,efd->becd', jax.nn.silu(a) * b, p['moe_w2']) * g[..., None].astype(h.dtype)
    return jnp.zeros_like(h).at[jnp.arange(B)[:, None], flat].add(y.reshape(B, N_EXPERTS * cap, D))


def _layer(x, xc, sc, sc_ctx, p, lambda_init, ctx_out):
    mod = jnp.split((sc @ p['w_mod'] + p['b_mod'])[:, None, :], 6, axis=-1)
    modc = jnp.split(sc_ctx @ p['w_mod'] + p['b_mod'], 6, axis=-1)
    h = modulate(rmsnorm(x, p['norm1_g']), mod[0], mod[1])
    hc = modulate(rmsnorm(xc, p['norm1_g']), modc[0], modc[1])
    y, yc = token_mixer(h, hc, p, lambda_init, ctx_out)
    x = x + mod[2] * y
    h2 = modulate(rmsnorm(x, p['norm2_g']), mod[3], mod[4])
    x = x + mod[5] * ec_moe(h2, p)
    if ctx_out:
        xc = xc + modc[2] * yc
        h2c = modulate(rmsnorm(xc, p['norm2_g']), modc[3], modc[4])
        xc = xc + modc[5] * ec_moe(h2c, p)
    return x, xc


def setup_inputs(seed: int = 0) -> dict:
    key = jax.random.key(seed)
    ks = jax.random.split(key, 28)
    D, L = D_MODEL, DEPTH

    def nrm(k, shape, s):
        return jax.random.normal(k, shape, jnp.float32) * s

    return {
        'x': nrm(ks[0], (BATCH, SEQ, D), 1.0),
        'c': nrm(ks[1], (BATCH, D), 1.0),
        'ctx': nrm(ks[2], (BATCH, CTX_LEN, D), 1.0),
        'c_ctx': nrm(ks[3], (D,), 1.0),
        'w_mod': nrm(ks[4], (L, D, 6 * D), 0.5 * D ** -0.5),
        'b_mod': nrm(ks[5], (L, 6 * D), 0.02),
        'norm1_g': 1.0 + nrm(ks[6], (L, D), 0.02),
        'norm2_g': 1.0 + nrm(ks[7], (L, D), 0.02),
        'w_in': nrm(ks[8], (L, D, D_IN), D ** -0.5),
        'diff_lambda': nrm(ks[9], (L, 4, HEAD_DIM), 0.1),
        'diff_subln_g': 1.0 + nrm(ks[10], (L, DIFF_V_DIM), 0.02),
        'hy_conv_w': nrm(ks[11], (L, SHORT_CONV, 3 * HY_WIDTH), SHORT_CONV ** -0.5),
        'hy_conv_b': nrm(ks[12], (L, 3 * HY_WIDTH), 0.02),
        'hy_ff_w1': nrm(ks[13], (L, HY_EMB_DIM, HY_FF), HY_EMB_DIM ** -0.5),
        'hy_ff_b1': nrm(ks[14], (L, HY_FF), 0.02),
        'hy_ff_w2': nrm(ks[15], (L, HY_FF, HY_FF), HY_FF ** -0.5),
        'hy_ff_b2': nrm(ks[16], (L, HY_FF), 0.02),
        'hy_ff_w3': nrm(ks[17], (L, HY_FF, 2 * HY_WIDTH), HY_FF ** -0.5),
        'hy_sin_freq': 1.0 + nrm(ks[18], (L, HY_FF), 0.02),
        'hy_bias': nrm(ks[19], (L, HY_WIDTH), 1.0),
        'swa_sink': nrm(ks[20], (L, SWA_Q_HEADS), 0.5),
        'w_branch': nrm(ks[21], (L, N_BRANCH, D, D), D ** -0.5),
        'w_out': nrm(ks[22], (L, D, D), D ** -0.5),
        'router_w': nrm(ks[23], (L, D, N_EXPERTS), D ** -0.5),
        'moe_w1': nrm(ks[24], (L, N_EXPERTS, D, D_EXPERT), D ** -0.5),
        'moe_w3': nrm(ks[25], (L, N_EXPERTS, D, D_EXPERT), D ** -0.5),
        'moe_w2': nrm(ks[26], (L, N_EXPERTS, D_EXPERT, D), D_EXPERT ** -0.5),
        'final_g': 1.0 + nrm(ks[27], (D,), 0.02),
    }


def reference(x, c, ctx, c_ctx, w_mod, b_mod, norm1_g, norm2_g, w_in, diff_lambda, diff_subln_g,
              hy_conv_w, hy_conv_b, hy_ff_w1, hy_ff_b1, hy_ff_w2, hy_ff_b2, hy_ff_w3, hy_sin_freq,
              hy_bias, swa_sink, w_branch, w_out, router_w, moe_w1, moe_w3, moe_w2, final_g):
    sc = jax.nn.silu(c)
    sc_ctx = jax.nn.silu(c_ctx)
    xc = ctx
    for l in range(DEPTH):
        p = {'w_mod': w_mod[l], 'b_mod': b_mod[l], 'norm1_g': norm1_g[l], 'norm2_g': norm2_g[l],
             'w_in': w_in[l], 'diff_lambda': diff_lambda[l], 'diff_subln_g': diff_subln_g[l],
             'hy_conv_w': hy_conv_w[l], 'hy_conv_b': hy_conv_b[l], 'hy_ff_w1': hy_ff_w1[l],
             'hy_ff_b1': hy_ff_b1[l], 'hy_ff_w2': hy_ff_w2[l], 'hy_ff_b2': hy_ff_b2[l],
             'hy_ff_w3': hy_ff_w3[l], 'hy_sin_freq': hy_sin_freq[l], 'hy_bias': hy_bias[l],
             'swa_sink': swa_sink[l], 'w_branch': w_branch[l], 'w_out': w_out[l],
             'router_w': router_w[l], 'moe_w1': moe_w1[l], 'moe_w3': moe_w3[l], 'moe_w2': moe_w2[l]}
        lambda_init = 0.8 - 0.6 * math.exp(-0.3 * l)
        x, xc = _layer(x, xc, sc, sc_ctx, p, lambda_init, l < DEPTH - 1)
    return rmsnorm(x, final_g)
```

```python
import functools
import math

import jax
import jax.numpy as jnp
from jax import lax
from jax.experimental import pallas as pl
from jax.experimental.pallas import tpu as pltpu

F32 = jnp.float32
BF16 = jnp.bfloat16
HIGHEST = lax.Precision.HIGHEST

D = 1024
HEAD = 64
GRID_W = 64
ROPE_BASE = 10000.0
EPS = 1e-6
N_DIFF_HEADS = D // (2 * HEAD)
DIFF_V = 2 * HEAD
SWA_KV = 4
SWA_GROUP = 4
WINDOW = 128
N_EXPERTS = 16
CAPACITY_FACTOR = 2
HY_BANDS = 16
HY_EMB = 1 + 2 * HY_BANDS
HY_FF = 64
HY_FAST_DECAY = 0.3
HY_SLOW_DECAY = 1.5
HY_TARGET = 1e-2

LANES = 128
TM = 256
DFT_N2 = 128
VMEM_LIMIT = 56 * 1024 * 1024

N_ATTN = 3 * D + D + 2 * SWA_KV * HEAD
N_HG = 3 * D + 3 * D


def _params(*sem):
    return pltpu.CompilerParams(dimension_semantics=sem, vmem_limit_bytes=VMEM_LIMIT)


def _dot(a, b, precision=None):
    return jnp.dot(a, b, preferred_element_type=F32, precision=precision)


def _dot_nt(a, b, precision=None):
    return lax.dot_general(a, b, (((1,), (1,)), ((), ())), preferred_element_type=F32, precision=precision)


def _rms(x):
    return x * lax.rsqrt(jnp.mean(x * x, axis=-1, keepdims=True) + EPS)


def _mod_kernel(c_ref, w_ref, b_ref, o_ref):
    c = c_ref[...]
    sc = c * jax.nn.sigmoid(c)
    o_ref[0] = _dot(sc, w_ref[0], HIGHEST) + b_ref[0]


def _mods(c, c_ctx, w_mod, b_mod):
    B = c.shape[0]
    depth = w_mod.shape[0]
    rows = -(-(B + 1) // 8) * 8
    cc = jnp.zeros((rows, D), F32).at[:B].set(c).at[B].set(c_ctx)
    out = pl.pallas_call(
        _mod_kernel,
        grid=(depth, 6),
        in_specs=[pl.BlockSpec((rows, D), lambda l, j: (0, 0)),
                  pl.BlockSpec((1, D, D), lambda l, j: (l, 0, j)),
                  pl.BlockSpec((1, 1, D), lambda l, j: (l, 0, j))],
        out_specs=pl.BlockSpec((1, rows, D), lambda l, j: (l, 0, j)),
        out_shape=jax.ShapeDtypeStruct((depth, rows, 6 * D), F32),
        compiler_params=_params("arbitrary", "arbitrary"),
        name="mods",
    )(cc, w_mod, b_mod.reshape(depth, 1, 6 * D))
    return out.reshape(depth, rows, 6, D)


def _norm_proj_kernel(x_ref, mod_ref, g_ref, w_ref, cos_ref, sin_ref, o_ref, *, rope_flags):
    x = x_ref[0]
    h = _rms(x) * g_ref[...]
    h = h * (1.0 + mod_ref[0, 1:2, :]) + mod_ref[0, 0:1, :]
    hb = h.astype(BF16)
    n_sub = len(rope_flags)
    sub_per_chunk = 4
    if any(rope_flags):
        cos = cos_ref[...]
        sin = sin_ref[...]
        lane = lax.broadcasted_iota(jnp.int32, cos.shape, 1)
        first_half = (lane % (HEAD // 2)) < (HEAD // 4)
    for c0 in range(0, n_sub, sub_per_chunk):
        c1 = min(c0 + sub_per_chunk, n_sub)
        acc = _dot(hb, w_ref[:, c0 * LANES:c1 * LANES])
        for s in range(c0, c1):
            a = acc[:, (s - c0) * LANES:(s - c0 + 1) * LANES]
            if rope_flags[s]:
                partner = jnp.where(first_half, pltpu.roll(a, LANES - HEAD // 4, 1), pltpu.roll(a, HEAD // 4, 1))
                a = a * cos + partner * sin
            o_ref[0, :, s * LANES:(s + 1) * LANES] = a.astype(o_ref.dtype)


def _norm_proj(xa, mod, g, w, cos_t, sin_t, rope_flags, out_dtype, n_lat_tiles, name):
    B, T, _ = xa.shape
    N = w.shape[1]
    ctx_row = B
    return pl.pallas_call(
        functools.partial(_norm_proj_kernel, rope_flags=rope_flags),
        grid=(B, T // TM),
        in_specs=[pl.BlockSpec((1, TM, D), lambda b, i: (b, i, 0)),
                  pl.BlockSpec((1, 6, D), lambda b, i: (jnp.where(i >= n_lat_tiles, ctx_row, b), 0, 0)),
                  pl.BlockSpec((1, D), lambda b, i: (0, 0)),
                  pl.BlockSpec((D, N), lambda b, i: (0, 0)),
                  pl.BlockSpec((TM, LANES), lambda b, i: (i, 0)),
                  pl.BlockSpec((TM, LANES), lambda b, i: (i, 0))],
        out_specs=pl.BlockSpec((1, TM, N), lambda b, i: (b, i, 0)),
        out_shape=jax.ShapeDtypeStruct((B, T, N), out_dtype),
        compiler_params=_params("arbitrary", "arbitrary"),
        name=name,
    )(xa, mod, g, w, cos_t, sin_t)


def _rope_tables(L, Lc):
    t = jnp.arange(L)
    pos = jnp.stack([(t // GRID_W).astype(F32), (t % GRID_W).astype(F32)], axis=1)
    n_freq = HEAD // 4
    inv = ROPE_BASE ** (-jnp.arange(n_freq, dtype=F32) / n_freq)
    ang = pos[:, :, None] * inv
    cos = jnp.cos(ang)
    sin = jnp.sin(ang)
    cos_h = jnp.concatenate([cos, cos], axis=-1).reshape(L, HEAD)
    sin_h = jnp.concatenate([-sin, sin], axis=-1).reshape(L, HEAD)
    cos_t = jnp.concatenate([jnp.tile(cos_h, (1, LANES // HEAD)), jnp.ones((Lc, LANES), F32)], axis=0)
    sin_t = jnp.concatenate([jnp.tile(sin_h, (1, LANES // HEAD)), jnp.zeros((Lc, LANES), F32)], axis=0)
    return cos_t, sin_t


def _diff_attn_kernel(lam_ref, q_ref, k_ref, v_ref, g_ref, o_ref, s_scr, *, L, Lc, tk, lambda_init):
    i = pl.program_id(2)
    tq = q_ref.shape[1]
    lp = lam_ref[0]
    lam = (jnp.exp(jnp.sum(lp[0:1] * lp[1:2], axis=-1, keepdims=True))
           - jnp.exp(jnp.sum(lp[2:3] * lp[3:4], axis=-1, keepdims=True)) + lambda_init)
    q = q_ref[0]
    lane = lax.broadcasted_iota(jnp.int32, q.shape, 1)
    zero = jnp.zeros_like(q)
    q_maps = (jnp.where(lane < HEAD, q, zero), jnp.where(lane >= HEAD, q, zero))

    def attend(chunk0, n_chunks):
        outs = []
        for qm in q_maps:
            def scores(c, mrun):
                start = pl.multiple_of((chunk0 + c) * tk, tk)
                s = _dot_nt(qm, k_ref[0, pl.ds(start, tk), :])
                s_scr[c] = s
                for j in range(tk // LANES):
                    mrun = jnp.maximum(mrun, s[:, j * LANES:(j + 1) * LANES])
                return mrun

            mrun = lax.fori_loop(0, n_chunks, scores, jnp.full((tq, LANES), -jnp.inf, F32))
            m = jnp.max(mrun, axis=-1, keepdims=True)

            def weighted(c, carry):
                lrun, acc = carry
                start = pl.multiple_of((chunk0 + c) * tk, tk)
                p = jnp.exp(s_scr[c] - m)
                for j in range(tk // LANES):
                    lrun = lrun + p[:, j * LANES:(j + 1) * LANES]
                acc = acc + _dot(p.astype(BF16), v_ref[0, pl.ds(start, tk), :])
                return lrun, acc

            lrun, acc = lax.fori_loop(0, n_chunks, weighted,
                                      (jnp.zeros((tq, LANES), F32), jnp.zeros((tq, DIFF_V), F32)))
            outs.append(acc / jnp.sum(lrun, axis=-1, keepdims=True))
        o = _rms(outs[0] - lam * outs[1])
        o_ref[0] = (o * g_ref[0] * (1.0 - lambda_init)).astype(o_ref.dtype)

    @pl.when(i < L // tq)
    def _():
        attend(0, (L + Lc) // tk)

    @pl.when(i >= L // tq)
    def _():
        attend(L // tk, Lc // tk)


def _diff_attn(p_attn, diff_lambda, subln_g, L, Lc, lambda_init, l):
    B, T, _ = p_attn.shape
    tk = TM
    nh = N_DIFF_HEADS
    return pl.pallas_call(
        functools.partial(_diff_attn_kernel, L=L, Lc=Lc, tk=tk, lambda_init=lambda_init),
        grid=(B, nh, T // TM),
        in_specs=[pl.BlockSpec((1, 4, HEAD), lambda b, h, i: (l, 0, 0)),
                  pl.BlockSpec((1, TM, DIFF_V), lambda b, h, i: (b, i, h)),
                  pl.BlockSpec((1, T, DIFF_V), lambda b, h, i: (b, 0, nh + h)),
                  pl.BlockSpec((1, T, DIFF_V), lambda b, h, i: (b, 0, 2 * nh + h)),
                  pl.BlockSpec((1, 1, DIFF_V), lambda b, h, i: (l, 0, 0))],
        out_specs=pl.BlockSpec((1, TM, DIFF_V), lambda b, h, i: (b, i, h)),
        out_shape=jax.ShapeDtypeStruct((B, T, D), BF16),
        scratch_shapes=[pltpu.VMEM((T // tk, TM, tk), F32)],
        compiler_params=_params("arbitrary", "arbitrary", "arbitrary"),
        name="diff_attn",
    )(diff_lambda, p_attn, p_attn, p_attn, subln_g.reshape(-1, 1, DIFF_V))


def _swa_kernel(sink_ref, q_ref, k_ref, v_ref, o_ref, *, L, Lc, l):
    i = pl.program_id(1)
    tq = q_ref.shape[1]
    kvw = SWA_KV * HEAD
    win = tq + 2 * TM
    lane_head = lax.broadcasted_iota(jnp.int32, (tq, kvw), 1) // HEAD
    k_ctx = k_ref[0, L:L + Lc, :]
    v_ctx = v_ref[0, L:L + Lc, :]

    def run(latent):
        if latent:
            start = pl.multiple_of(jnp.clip(i * tq - TM, 0, L - win), TM)
            k_win = k_ref[0, pl.ds(start, win), :]
            v_win = v_ref[0, pl.ds(start, win), :]
            q_pos = i * tq + lax.broadcasted_iota(jnp.int32, (tq, win), 0)
            k_pos = start + lax.broadcasted_iota(jnp.int32, (tq, win), 1)
            band = jnp.abs(k_pos - q_pos) <= WINDOW
        for g in range(SWA_GROUP):
            qg = q_ref[0, :, g * kvw:(g + 1) * kvw]
            acc = jnp.zeros((tq, kvw), F32)
            for kv in range(SWA_KV):
                qe = jnp.where(lane_head == kv, qg, jnp.zeros_like(qg))
                snk = sink_ref[l, kv * SWA_GROUP + g]
                s_c = _dot_nt(qe, k_ctx)
                m = jnp.maximum(jnp.max(s_c, axis=-1, keepdims=True), snk)
                if latent:
                    s_w = jnp.where(band, _dot_nt(qe, k_win), -jnp.inf)
                    m = jnp.maximum(m, jnp.max(s_w, axis=-1, keepdims=True))
                p_c = jnp.exp(s_c - m)
                den = jnp.sum(p_c, axis=-1, keepdims=True) + jnp.exp(snk - m)
                o = _dot(p_c.astype(BF16), v_ctx)
                if latent:
                    p_w = jnp.exp(s_w - m)
                    den = den + jnp.sum(p_w, axis=-1, keepdims=True)
                    o = o + _dot(p_w.astype(BF16), v_win)
                acc = acc + jnp.where(lane_head == kv, o / den, 0.0)
            o_ref[0, :, g * kvw:(g + 1) * kvw] = acc.astype(o_ref.dtype)

    @pl.when(i < L // tq)
    def _():
        run(True)

    @pl.when(i >= L // tq)
    def _():
        run(False)


def _swa(p_attn, sink, L, Lc, l):
    B, T, _ = p_attn.shape
    kvw = SWA_KV * HEAD
    return pl.pallas_call(
        functools.partial(_swa_kernel, L=L, Lc=Lc, l=l),
        grid=(B, T // TM),
        in_specs=[pl.BlockSpec(memory_space=pltpu.SMEM),
                  pl.BlockSpec((1, TM, D), lambda b, i: (b, i, 3)),
                  pl.BlockSpec((1, T, kvw), lambda b, i: (b, 0, 4 * D // kvw)),
                  pl.BlockSpec((1, T, kvw), lambda b, i: (b, 0, 4 * D // kvw + 1))],
        out_specs=pl.BlockSpec((1, TM, D), lambda b, i: (b, i, 0)),
        out_shape=jax.ShapeDtypeStruct((B, T, D), BF16),
        compiler_params=_params("arbitrary", "arbitrary"),
        name="swa",
    )(sink, p_attn, p_attn, p_attn)


def _hy_filter_kernel(emb_ref, w1_ref, b1_ref, w2_ref, b2_ref, fr_ref, w3f_ref, w3b_ref, dl_ref, o_ref, *, L):
    rows = TM
    fr = fr_ref[...]

    def raw(i, total):
        r0 = pl.multiple_of(i * rows, rows)
        e = emb_ref[pl.ds(r0, rows), :]
        z = jnp.sin(fr * (_dot(e, w1_ref[...], HIGHEST) + b1_ref[...]))
        z = jnp.sin(fr * (_dot(z, w2_ref[...], HIGHEST) + b2_ref[...]))
        row = r0 + lax.broadcasted_iota(jnp.int32, (rows, 1), 0)
        h = jnp.where(row < L, _dot(z, w3f_ref[...], HIGHEST), _dot(z, w3b_ref[...], HIGHEST))
        h = h * jnp.exp(-e[:, 0:1] * dl_ref[...])
        o_ref[pl.ds(r0, rows), :] = h
        return total + jnp.sum(jnp.abs(h), axis=0, keepdims=True)

    n_chunks = emb_ref.shape[0] // rows
    norm = lax.fori_loop(0, n_chunks, raw, jnp.zeros((1, o_ref.shape[1]), F32)) + EPS

    def scale(i, _):
        r0 = pl.multiple_of(i * rows, rows)
        row = r0 + lax.broadcasted_iota(jnp.int32, (rows, 1), 0)
        o_ref[pl.ds(r0, rows), :] = jnp.where(row == L, 0.0, o_ref[pl.ds(r0, rows), :] / norm)
        return 0

    lax.fori_loop(0, n_chunks, scale, 0)


def _hy_embedding(L):
    n = jnp.arange(2 * L)
    lag = jnp.where(n < L, n, jnp.where(n == L, 0, 2 * L - n))
    t = jnp.linspace(0.0, 1.0, L, dtype=F32)[lag][:, None]
    w = (2.0 * math.pi * jnp.arange(L, dtype=F32) / L)[lag][:, None]
    f = jnp.linspace(1e-4, HY_BANDS - 1, HY_BANDS, dtype=F32)[None, :]
    emb = jnp.concatenate([t, jnp.cos(f * w), -jnp.sin(f * w)], axis=-1)
    return jnp.pad(emb, ((0, 0), (0, LANES - HY_EMB)))


def _hy_filter(L, w1p, b1, w2, b2, fr, w3, deltas):
    ct = 256
    emb = _hy_embedding(L)
    full = lambda shape: pl.BlockSpec(shape, lambda j: (0, 0))
    return pl.pallas_call(
        functools.partial(_hy_filter_kernel, L=L),
        grid=(D // ct,),
        in_specs=[full((2 * L, LANES)), full((LANES, HY_FF)), full((1, HY_FF)), full((HY_FF, HY_FF)),
                  full((1, HY_FF)), full((1, HY_FF)),
                  pl.BlockSpec((HY_FF, ct), lambda j: (0, j)),
                  pl.BlockSpec((HY_FF, ct), lambda j: (0, D // ct + j)),
                  pl.BlockSpec((1, ct), lambda j: (0, j))],
        out_specs=pl.BlockSpec((2 * L, ct), lambda j: (0, j)),
        out_shape=jax.ShapeDtypeStruct((2 * L, D), F32),
        compiler_params=_params("arbitrary"),
        name="hy_filter",
    )(emb, w1p, b1, w2, b2, fr, w3, w3, deltas)


def _short_conv(x, w, b):
    n = x.shape[0]
    row = lax.broadcasted_iota(jnp.int32, x.shape, 0)
    prev = jnp.where(row == 0, 0.0, pltpu.roll(x, 1, 0))
    nxt = jnp.where(row == n - 1, 0.0, pltpu.roll(x, n - 1, 0))
    return prev * w[0:1] + x * w[1:2] + nxt * w[2:3] + b


def _hy_pre_kernel(x0_ref, x1_ref, v_ref, w0_ref, w1_ref, w2_ref, b0_ref, b1_ref, b2_ref, x0c_ref, u_ref):
    x0c_ref[0] = _short_conv(x0_ref[0], w0_ref[...], b0_ref[...])
    u_ref[0] = (_short_conv(x1_ref[0], w1_ref[...], b1_ref[...])
                * _short_conv(v_ref[0], w2_ref[...], b2_ref[...]))


def _hy_pre(p_hg, conv_w, conv_b, L):
    B = p_hg.shape[0]
    ct = LANES
    nb = D // ct
    col = lambda k: pl.BlockSpec((1, L, ct), lambda b, j: (b, 0, k * nb + j))
    wsp = lambda k: pl.BlockSpec((3, ct), lambda b, j: (0, k * nb + j))
    bsp = lambda k: pl.BlockSpec((1, ct), lambda b, j: (0, k * nb + j))
    out = pl.BlockSpec((1, L, ct), lambda b, j: (b, 0, j))
    return pl.pallas_call(
        _hy_pre_kernel,
        grid=(B, nb),
        in_specs=[col(0), col(1), col(2), wsp(0), wsp(1), wsp(2), bsp(0), bsp(1), bsp(2)],
        out_specs=[out, out],
        out_shape=[jax.ShapeDtypeStruct((B, L, D), F32)] * 2,
        compiler_params=_params("arbitrary", "arbitrary"),
        name="hy_pre",
    )(p_hg, p_hg, p_hg, conv_w, conv_w, conv_w, conv_b, conv_b, conv_b)


def _lmat_kernel(m_ref, x_ref, o_ref, *, precision):
    x = x_ref[0]
    if precision is None:
        x = x.astype(BF16)
    o_ref[0] = _dot(m_ref[...], x, precision).astype(o_ref.dtype)


def _lmat(mat, x, out_dtype, precision, wc, name):
    Bn, K, W = x.shape
    M = mat.shape[0]
    return pl.pallas_call(
        functools.partial(_lmat_kernel, precision=precision),
        grid=(Bn, W // wc),
        in_specs=[pl.BlockSpec((M, K), lambda b, j: (0, 0)),
                  pl.BlockSpec((1, K, wc), lambda b, j: (b, 0, j))],
        out_specs=pl.BlockSpec((1, M, wc), lambda b, j: (b, 0, j)),
        out_shape=jax.ShapeDtypeStruct((Bn, M, W), out_dtype),
        compiler_params=_params("arbitrary", "arbitrary"),
        name=name,
    )(mat, x)


def _cmul(xr, xi, kr, ki):
    return xr * kr - xi * ki, xr * ki + xi * kr


def _hy_spec_kernel(a_ref, m_ref, o_ref):
    a = jnp.concatenate([a_ref[0, 0], a_ref[1, 0]], axis=0)
    o_ref[0] = _dot(m_ref[0], a, HIGHEST)


def _hy_spectrum(a, m_tab):
    n1 = a.shape[1]
    n2 = DFT_N2
    return pl.pallas_call(
        _hy_spec_kernel,
        grid=(n1,),
        in_specs=[pl.BlockSpec((2, 1, n2, D), lambda k: (0, k, 0, 0)),
                  pl.BlockSpec((1, 2 * n2, 2 * n2), lambda k: (k, 0, 0))],
        out_specs=pl.BlockSpec((1, 2 * n2, D), lambda k: (k, 0, 0)),
        out_shape=jax.ShapeDtypeStruct((n1, 2 * n2, D), F32),
        compiler_params=_params("arbitrary"),
        name="hy_spectrum",
    )(a, m_tab)


def _hy_mid_kernel(a_ref, m_ref, mt_ref, kf_ref, o_ref):
    n2 = DFT_N2
    kr = kf_ref[0, :n2]
    ki = kf_ref[0, n2:]

    def body(b, _):
        a = jnp.concatenate([a_ref[b, 0, 0], a_ref[b, 1, 0]], axis=0)
        x = _dot(m_ref[0], a)
        yr, yi = _cmul(x[:n2], x[n2:], kr, ki)
        y = jnp.concatenate([yr, yi], axis=0).astype(BF16)
        z = _dot(mt_ref[0], y)
        o_ref[b, 0, 0] = z[:n2].astype(o_ref.dtype)
        o_ref[b, 1, 0] = z[n2:].astype(o_ref.dtype)
        return 0

    lax.fori_loop(0, a_ref.shape[0], body, 0)


def _hy_mid(a, m_bf, mt_bf, kf):
    B, _, n1, n2, _ = a.shape
    blk = pl.BlockSpec((B, 2, 1, n2, D), lambda k: (0, 0, k, 0, 0))
    tab = pl.BlockSpec((1, 2 * n2, 2 * n2), lambda k: (k, 0, 0))
    return pl.pallas_call(
        _hy_mid_kernel,
        grid=(n1,),
        in_specs=[blk, tab, tab, pl.BlockSpec((1, 2 * n2, D), lambda k: (k, 0, 0))],
        out_specs=blk,
        out_shape=jax.ShapeDtypeStruct(a.shape, BF16),
        compiler_params=_params("arbitrary"),
        name="hy_mid",
    )(a, m_bf, mt_bf, kf)


def _hy_post_kernel(g_ref, z_ref, x0_ref, u_ref, bias_ref, o_ref):
    conv = _dot(g_ref[...], z_ref[0])
    o_ref[0] = (x0_ref[0] * (conv + u_ref[0] * bias_ref[...])).astype(o_ref.dtype)


def _hy_post(g_mat, z, x0c, u, bias_t, wc):
    B, K, W = z.shape
    R = g_mat.shape[0]
    row = pl.BlockSpec((1, R, wc), lambda b, j: (b, 0, j))
    return pl.pallas_call(
        _hy_post_kernel,
        grid=(B, W // wc),
        in_specs=[pl.BlockSpec((R, K), lambda b, j: (0, 0)),
                  pl.BlockSpec((1, K, wc), lambda b, j: (b, 0, j)),
                  row, row, pl.BlockSpec((1, wc), lambda b, j: (0, 0))],
        out_specs=row,
        out_shape=jax.ShapeDtypeStruct((B, R, W), BF16),
        compiler_params=_params("arbitrary", "arbitrary"),
        name="hy_post",
    )(g_mat, z, x0c, u, bias_t)


def _hy_ctx_kernel(x0_ref, x1_ref, v_ref, w0_ref, w1_ref, w2_ref, b0_ref, b1_ref, b2_ref,
                   f_ref, kf_ref, g_ref, bias_ref, o_ref):
    x0 = _short_conv(x0_ref[0], w0_ref[...], b0_ref[...])
    u = _short_conv(x1_ref[0], w1_ref[...], b1_ref[...]) * _short_conv(v_ref[0], w2_ref[...], b2_ref[...])
    x = _dot(f_ref[...], u, HIGHEST)
    nf = x.shape[0] // 2
    yr, yi = _cmul(x[:nf], x[nf:], kf_ref[:nf], kf_ref[nf:])
    conv = _dot(g_ref[...], jnp.concatenate([yr, yi], axis=0), HIGHEST)
    o_ref[0] = (x0 * (conv + u * bias_ref[...])).astype(o_ref.dtype)


def _hy_ctx(p_hg, conv_w, conv_b, f_c, kf_c, g_c, bias, L, Lc):
    B = p_hg.shape[0]
    ct = 256
    nb = D // ct
    rb = L // Lc
    col = lambda k: pl.BlockSpec((1, Lc, ct), lambda b, j: (b, rb, k * nb + j))
    wsp = lambda k: pl.BlockSpec((3, ct), lambda b, j: (0, k * nb + j))
    bsp = lambda k: pl.BlockSpec((1, ct), lambda b, j: (0, k * nb + j))
    return pl.pallas_call(
        _hy_ctx_kernel,
        grid=(B, nb),
        in_specs=[col(0), col(1), col(2), wsp(0), wsp(1), wsp(2), bsp(0), bsp(1), bsp(2),
                  pl.BlockSpec(f_c.shape, lambda b, j: (0, 0)),
                  pl.BlockSpec((kf_c.shape[0], ct), lambda b, j: (0, j)),
                  pl.BlockSpec(g_c.shape, lambda b, j: (0, 0)),
                  pl.BlockSpec((1, ct), lambda b, j: (0, j))],
        out_specs=pl.BlockSpec((1, Lc, ct), lambda b, j: (b, 0, j)),
        out_shape=jax.ShapeDtypeStruct((B, Lc, D), BF16),
        compiler_params=_params("arbitrary", "arbitrary"),
        name="hy_ctx",
    )(p_hg, p_hg, p_hg, conv_w, conv_w, conv_w, conv_b, conv_b, conv_b, f_c, kf_c, g_c, bias)


def _dft_tables(L, Lc):
    n = 2 * L
    n2 = DFT_N2
    n1 = n // n2
    two_pi = 2.0 * math.pi

    def cs(idx, period):
        ang = (idx % period).astype(F32) * (two_pi / period)
        return jnp.cos(ang), jnp.sin(ang)

    k1 = jnp.arange(n1)[:, None]
    c, s = cs(k1 * jnp.arange(n1)[None, :], n1)
    f1_full = jnp.concatenate([c, -s], axis=0)
    f1_half = f1_full[:, :n1 // 2]
    kk1 = jnp.arange(n1)[:, None, None]
    kk2 = jnp.arange(n2)[None, :, None]
    nn2 = jnp.arange(n2)[None, None, :]
    c, s = cs(nn2 * kk1 + n1 * nn2 * kk2, n)
    ci = -s
    m_tab = jnp.concatenate([jnp.concatenate([c, -ci], axis=2), jnp.concatenate([ci, c], axis=2)], axis=1)
    c, s = cs(jnp.arange(n1 // 2)[:, None] * jnp.arange(n1)[None, :], n1)
    g_mat = jnp.concatenate([c, -s], axis=1) / n
    nc = 2 * Lc
    kc = jnp.arange(nc)[:, None]
    c, s = cs(kc * jnp.arange(nc)[None, :], nc)
    fc_full = jnp.concatenate([c, -s], axis=0)
    c, s = cs(jnp.arange(Lc)[:, None] * jnp.arange(nc)[None, :], nc)
    g_c = jnp.concatenate([c, -s], axis=1) / nc
    return dict(f1_full=f1_full, f1_half=f1_half.astype(BF16), m_tab=m_tab, m_bf=m_tab.astype(BF16),
                mt_bf=jnp.swapaxes(m_tab, 1, 2).astype(BF16), g_mat=g_mat.astype(BF16),
                fc_full=fc_full, fc_half=fc_full[:, :Lc], g_c=g_c)


def _hyena(p_hg, hp, tabs, L, Lc):
    B = p_hg.shape[0]
    n2 = DFT_N2
    n1 = 2 * L // n2
    wc = min(16 * D, n2 * D)
    deltas = jnp.abs(jnp.linspace(math.log(HY_TARGET) / HY_FAST_DECAY, math.log(HY_TARGET) / HY_SLOW_DECAY,
                                  D, dtype=F32))[None, :]
    filt = functools.partial(_hy_filter, w1p=hp['w1p'], b1=hp['b1'], w2=hp['w2'], b2=hp['b2'], fr=hp['fr'],
                             w3=hp['w3'], deltas=deltas)
    k_lat = filt(L)
    a_f = _lmat(tabs['f1_full'], k_lat.reshape(1, n1, n2 * D), F32, HIGHEST, wc, "hy_filt_s1")
    kf = _hy_spectrum(a_f.reshape(2, n1, n2, D), tabs['m_tab'])
    k_ctx = filt(Lc)
    kf_c = _lmat(tabs['fc_full'], k_ctx[None], F32, HIGHEST, D, "hy_filt_ctx")[0]
    x0c, u = _hy_pre(p_hg, hp['conv_w'], hp['conv_b'], L)
    a = _lmat(tabs['f1_half'], u.reshape(B, n1 // 2, n2 * D), BF16, None, wc, "hy_s1")
    z = _hy_mid(a.reshape(B, 2, n1, n2, D), tabs['m_bf'], tabs['mt_bf'], kf)
    bias_t = jnp.tile(hp['bias'], (1, wc // D))
    y = _hy_post(tabs['g_mat'], z.reshape(B, 2 * n1, n2 * D), x0c.reshape(B, n1 // 2, n2 * D),
                 u.reshape(B, n1 // 2, n2 * D), bias_t, wc).reshape(B, L, D)
    y_c = _hy_ctx(p_hg, hp['conv_w'], hp['conv_b'], tabs['fc_half'], kf_c, tabs['g_c'], hp['bias'], L, Lc)
    return jnp.concatenate([y, y_c], axis=1)


def _merge_kernel(x_ref, mod_ref, yd_ref, yh_ref, ys_ref, gd_ref, gh_ref, gs_ref, wb_ref, wo_ref, g2_ref,
                  rw_ref, xo_ref, h2_ref, lg_ref):
    merged = None
    for k, (y_ref, gt_ref) in enumerate(((yd_ref, gd_ref), (yh_ref, gh_ref), (ys_ref, gs_ref))):
        term = jax.nn.sigmoid(gt_ref[0]) * _dot(y_ref[0], wb_ref[k])
        merged = term if merged is None else merged + term
    y = _dot(merged.astype(BF16), wo_ref[...])
    x = x_ref[0] + mod_ref[0, 2:3, :] * y
    xo_ref[0] = x
    h2 = _rms(x) * g2_ref[...]
    h2 = h2 * (1.0 + mod_ref[0, 4:5, :]) + mod_ref[0, 3:4, :]
    h2_ref[0] = h2.astype(h2_ref.dtype)
    lg_ref[0] = _dot_nt(rw_ref[...], h2, HIGHEST)


def _merge(xa, mod, y_diff, y_hy, y_swa, p_hg, w_branch, w_out, g2, rw_t, n_lat_tiles):
    B, T, _ = xa.shape
    ctx_row = B
    tile = lambda c: pl.BlockSpec((1, TM, D), lambda b, i: (b, i, c))
    return pl.pallas_call(
        _merge_kernel,
        grid=(B, T // TM),
        in_specs=[tile(0),
                  pl.BlockSpec((1, 6, D), lambda b, i: (jnp.where(i >= n_lat_tiles, ctx_row, b), 0, 0)),
                  tile(0), tile(0), tile(0), tile(3), tile(4), tile(5),
                  pl.BlockSpec((3, D, D), lambda b, i: (0, 0, 0)),
                  pl.BlockSpec((D, D), lambda b, i: (0, 0)),
                  pl.BlockSpec((1, D), lambda b, i: (0, 0)),
                  pl.BlockSpec((N_EXPERTS, D), lambda b, i: (0, 0))],
        out_specs=[tile(0), tile(0), pl.BlockSpec((1, N_EXPERTS, TM), lambda b, i: (b, 0, i))],
        out_shape=[jax.ShapeDtypeStruct((B, T, D), F32), jax.ShapeDtypeStruct((B, T, D), BF16),
                   jax.ShapeDtypeStruct((B, N_EXPERTS, T), F32)],
        compiler_params=_params("arbitrary", "arbitrary"),
        name="merge",
    )(xa, mod, y_diff, y_hy, y_swa, p_hg, p_hg, p_hg, w_branch, w_out, g2, rw_t)


def _router_kernel(lg_ref, slot_ref, gate_ref, *, segments):
    lg = lg_ref[0]
    e = jnp.exp(lg - jnp.max(lg, axis=0, keepdims=True))
    aff = e / jnp.sum(e, axis=0, keepdims=True)
    r = lax.broadcasted_iota(jnp.int32, (LANES, LANES), 0)
    c = lax.broadcasted_iota(jnp.int32, (LANES, LANES), 1)
    before = (r < c).astype(BF16)

    def prefix_count(mask, n):
        run = jnp.zeros((N_EXPERTS, 1), F32)
        outs = []
        for j in range(n // LANES):
            blk = mask[:, j * LANES:(j + 1) * LANES].astype(F32).astype(BF16)
            outs.append(_dot(blk, before) + run)
            run = run + jnp.sum(blk.astype(F32), axis=1, keepdims=True)
        return jnp.concatenate(outs, axis=1)

    for start, n, cap, slot_base in segments:
        a = aff[:, start:start + n]
        bits = pltpu.bitcast(a, jnp.int32)

        def search(_, bounds):
            lo, hi = bounds
            mid = lo + ((hi - lo + 1) >> 1)
            cnt = jnp.sum((bits >= mid).astype(F32), axis=1, keepdims=True)
            ok = cnt >= cap
            return jnp.where(ok, mid, lo), jnp.where(ok, hi, mid - 1)

        lo0 = jnp.zeros((N_EXPERTS, 1), jnp.int32)
        hi0 = jnp.full((N_EXPERTS, 1), 0x7F800000, jnp.int32)
        thr, _ = lax.fori_loop(0, 32, search, (lo0, hi0))
        above = bits > thr
        tie = bits == thr
        need = cap - jnp.sum(above.astype(F32), axis=1, keepdims=True)
        sel = above | (tie & (prefix_count(tie, n) < need))
        pos = prefix_count(sel, n)
        slot_ref[0, :, start:start + n] = jnp.where(sel, pos + slot_base, -1.0)
        gate_ref[0, :, start:start + n] = jnp.where(sel, a, 0.0)


def _router(logits_t, segments):
    B, E, T = logits_t.shape
    blk = pl.BlockSpec((1, E, T), lambda b: (b, 0, 0))
    return pl.pallas_call(
        functools.partial(_router_kernel, segments=segments),
        grid=(B,),
        in_specs=[blk],
        out_specs=[blk, blk],
        out_shape=[jax.ShapeDtypeStruct((B, E, T), F32)] * 2,
        compiler_params=_params("arbitrary"),
        name="router",
    )(logits_t)


def _moe_ffn_kernel(slot_ref, gate_ref, h2_ref, w1_ref, w3_ref, w2_ref, o_ref, *, n_slots, halves):
    T = h2_ref.shape[1]
    tc = T // halves
    slot_id = lax.broadcasted_iota(jnp.int32, (n_slots, tc), 0).astype(F32)
    xg = jnp.zeros((n_slots, D), F32)
    g = jnp.zeros((n_slots, 1), F32)
    for c in range(halves):
        hit = slot_ref[0, 0, :, c * tc:(c + 1) * tc] == slot_id
        xg = xg + _dot(jnp.where(hit, 1.0, 0.0).astype(BF16), h2_ref[0, c * tc:(c + 1) * tc, :])
        g = g + jnp.sum(jnp.where(hit, gate_ref[0, 0, :, c * tc:(c + 1) * tc], 0.0), axis=1, keepdims=True)
    xb = xg.astype(BF16)
    a = _dot(xb, w1_ref[0, 0])
    b = _dot(xb, w3_ref[0, 0])
    hidden = (a * jax.nn.sigmoid(a) * b).astype(BF16)
    o_ref[0, 0] = (_dot(hidden, w2_ref[0, 0]) * g).astype(o_ref.dtype)


def _moe_ffn(slot, gate, h2, w1, w3, w2, l, n_slots):
    B, E, T = slot.shape
    row = pl.BlockSpec((1, 1, 1, T), lambda b, e: (b, e, 0, 0))
    wsp = pl.BlockSpec((1, 1, D, D), lambda b, e: (l, e, 0, 0))
    return pl.pallas_call(
        functools.partial(_moe_ffn_kernel, n_slots=n_slots, halves=2),
        grid=(B, E),
        in_specs=[row, row, pl.BlockSpec((1, T, D), lambda b, e: (b, 0, 0)), wsp, wsp, wsp],
        out_specs=pl.BlockSpec((1, 1, n_slots, D), lambda b, e: (b, e, 0, 0)),
        out_shape=jax.ShapeDtypeStruct((B, E, n_slots, D), BF16),
        compiler_params=_params("arbitrary", "arbitrary"),
        name="moe_ffn",
    )(slot.reshape(B, E, 1, T), gate.reshape(B, E, 1, T), h2, w1, w3, w2)


def _moe_combine_kernel(x_ref, mod_ref, slot_ref, y_ref, o_ref, *, n_slots):
    tm = x_ref.shape[1]
    slot_id = lax.broadcasted_iota(jnp.int32, (tm, n_slots), 1).astype(F32)
    acc = jnp.zeros((tm, D), F32)
    for e in range(N_EXPERTS):
        hit = slot_ref[0, :, e:e + 1] == slot_id
        acc = acc + _dot(jnp.where(hit, 1.0, 0.0).astype(BF16), y_ref[0, e])
    o_ref[0] = x_ref[0] + mod_ref[0, 5:6, :] * acc


def _moe_combine(xa, mod, slot_t, y, n_lat_tiles, n_slots):
    B, T, _ = xa.shape
    ctx_row = B
    tile = pl.BlockSpec((1, TM, D), lambda b, i: (b, i, 0))
    return pl.pallas_call(
        functools.partial(_moe_combine_kernel, n_slots=n_slots),
        grid=(B, T // TM),
        in_specs=[tile,
                  pl.BlockSpec((1, 6, D), lambda b, i: (jnp.where(i >= n_lat_tiles, ctx_row, b), 0, 0)),
                  pl.BlockSpec((1, TM, N_EXPERTS), lambda b, i: (b, i, 0)),
                  pl.BlockSpec((1, N_EXPERTS, n_slots, D), lambda b, i: (b, 0, 0, 0))],
        out_specs=tile,
        out_shape=jax.ShapeDtypeStruct((B, T, D), F32),
        compiler_params=_params("arbitrary", "arbitrary"),
        name="moe_combine",
    )(xa, mod, slot_t, y)


def _final_kernel(x_ref, g_ref, o_ref):
    o_ref[0] = _rms(x_ref[0]) * g_ref[...]


def _final_norm(xa, g, L):
    B = xa.shape[0]
    tile = pl.BlockSpec((1, TM, D), lambda b, i: (b, i, 0))
    return pl.pallas_call(
        _final_kernel,
        grid=(B, L // TM),
        in_specs=[tile, pl.BlockSpec((1, D), lambda b, i: (0, 0))],
        out_specs=tile,
        out_shape=jax.ShapeDtypeStruct((B, L, D), F32),
        compiler_params=_params("arbitrary", "arbitrary"),
        name="final_norm",
    )(xa, g)


def _swa_head_perm():
    return [kv * SWA_GROUP + g for g in range(SWA_GROUP) for kv in range(SWA_KV)]


def _split_w_in(w_in):
    depth = w_in.shape[0]
    o = [0, D, 2 * D, 3 * D, 6 * D, 7 * D, 7 * D + SWA_KV * HEAD, 7 * D + 2 * SWA_KV * HEAD, 10 * D + 2 * SWA_KV * HEAD]
    dq, dk, dv, hy, sq, sk, sv, gt = (w_in[:, :, o[k]:o[k + 1]] for k in range(8))
    scale = HEAD ** -0.5
    sq = sq.reshape(depth, D, SWA_KV * SWA_GROUP, HEAD)[:, :, jnp.array(_swa_head_perm())].reshape(depth, D, D)
    w_attn = jnp.concatenate([dq * scale, dk, dv, sq * scale, sk, sv], axis=-1).astype(BF16)
    w_hg = jnp.concatenate([hy, gt], axis=-1).astype(BF16)
    return w_attn, w_hg


def kernel(x, c, ctx, c_ctx, w_mod, b_mod, norm1_g, norm2_g, w_in, diff_lambda, diff_subln_g, hy_conv_w,
           hy_conv_b, hy_ff_w1, hy_ff_b1, hy_ff_w2, hy_ff_b2, hy_ff_w3, hy_sin_freq, hy_bias, swa_sink,
           w_branch, w_out, router_w, moe_w1, moe_w3, moe_w2, final_g):
    B, L, _ = x.shape
    Lc = ctx.shape[1]
    depth = w_mod.shape[0]
    T = L + Lc
    n_lat_tiles = L // TM
    cap_l = CAPACITY_FACTOR * L // N_EXPERTS
    cap_c = CAPACITY_FACTOR * Lc // N_EXPERTS
    n_slots = cap_l + cap_c
    segments = ((0, L, cap_l, 0), (L, Lc, cap_c, cap_l))

    xa = jnp.concatenate([x, ctx], axis=1)
    mods = _mods(c, c_ctx, w_mod, b_mod)
    cos_t, sin_t = _rope_tables(L, Lc)
    tabs = _dft_tables(L, Lc)
    w_attn, w_hg = _split_w_in(w_in)
    rope_flags = tuple([True] * 16 + [False] * 8 + [True] * 8 + [True] * 2 + [False] * 2)
    perm = jnp.array(_swa_head_perm())
    wb = w_branch.at[:, 2].set(
        w_branch[:, 2].reshape(depth, SWA_KV * SWA_GROUP, HEAD, D)[:, perm].reshape(depth, D, D)).astype(BF16)
    wo = w_out.astype(BF16)
    w1, w3, w2 = moe_w1.astype(BF16), moe_w3.astype(BF16), moe_w2.astype(BF16)
    rw_t = jnp.swapaxes(router_w, 1, 2)
    w1p = jnp.pad(hy_ff_w1, ((0, 0), (0, LANES - HY_EMB), (0, 0)))

    for l in range(depth):
        lambda_init = 0.8 - 0.6 * math.exp(-0.3 * l)
        mod = mods[l]
        p_attn = _norm_proj(xa, mod, norm1_g[l][None], w_attn[l], cos_t, sin_t, rope_flags, BF16,
                            n_lat_tiles, "proj_attn")
        p_hg = _norm_proj(xa, mod, norm1_g[l][None], w_hg[l], cos_t, sin_t, (False,) * (N_HG // LANES), F32,
                          n_lat_tiles, "proj_hg")
        y_diff = _diff_attn(p_attn, diff_lambda, diff_subln_g, L, Lc, lambda_init, l)
        y_swa = _swa(p_attn, swa_sink, L, Lc, l)
        hp = dict(w1p=w1p[l], b1=hy_ff_b1[l][None], w2=hy_ff_w2[l], b2=hy_ff_b2[l][None],
                  fr=hy_sin_freq[l][None], w3=hy_ff_w3[l], conv_w=hy_conv_w[l], conv_b=hy_conv_b[l][None],
                  bias=hy_bias[l][None])
        y_hy = _hyena(p_hg, hp, tabs, L, Lc)
        xa, h2, logits_t = _merge(xa, mod, y_diff, y_hy, y_swa, p_hg, wb[l], wo[l], norm2_g[l][None],
                                  rw_t[l], n_lat_tiles)
        slot, gate = _router(logits_t, segments)
        y_e = _moe_ffn(slot, gate, h2, w1, w3, w2, l, n_slots)
        xa = _moe_combine(xa, mod, jnp.swapaxes(slot, 1, 2), y_e, n_lat_tiles, n_slots)
    return _final_norm(xa, final_g[None], L)
```

```python
import functools
import math

import jax
import jax.numpy as jnp
from jax import lax
from jax.experimental import pallas as pl
from jax.experimental.pallas import tpu as pltpu

F32 = jnp.float32
BF16 = jnp.bfloat16
HIGHEST = lax.Precision.HIGHEST

D = 1024
HEAD = 64
GRID_W = 64
ROPE_BASE = 10000.0
EPS = 1e-6
N_DIFF_HEADS = D // (2 * HEAD)
DIFF_V = 2 * HEAD
SWA_KV = 4
SWA_GROUP = 4
WINDOW = 128
N_EXPERTS = 16
CAPACITY_FACTOR = 2
HY_BANDS = 16
HY_EMB = 1 + 2 * HY_BANDS
HY_FF = 64
HY_FAST_DECAY = 0.3
HY_SLOW_DECAY = 1.5
HY_TARGET = 1e-2

LANES = 128
TM = 256
DFT_N2 = 128
VMEM_LIMIT = 56 * 1024 * 1024

N_ATTN = 3 * D + D + 2 * SWA_KV * HEAD
N_HG = 3 * D + 3 * D


def _params(*sem):
    return pltpu.CompilerParams(dimension_semantics=sem, vmem_limit_bytes=VMEM_LIMIT)


def _dot(a, b, precision=None):
    return jnp.dot(a, b, preferred_element_type=F32, precision=precision)


def _dot_nt(a, b, precision=None):
    return lax.dot_general(a, b, (((1,), (1,)), ((), ())), preferred_element_type=F32, precision=precision)


def _rms(x):
    return x * lax.rsqrt(jnp.mean(x * x, axis=-1, keepdims=True) + EPS)


def _mod_kernel(c_ref, w_ref, b_ref, o_ref):
    c = c_ref[...]
    sc = c * jax.nn.sigmoid(c)
    o_ref[0] = _dot(sc, w_ref[0], HIGHEST) + b_ref[0]


def _mods(c, c_ctx, w_mod, b_mod):
    B = c.shape[0]
    depth = w_mod.shape[0]
    rows = -(-(B + 1) // 8) * 8
    cc = jnp.zeros((rows, D), F32).at[:B].set(c).at[B].set(c_ctx)
    out = pl.pallas_call(
        _mod_kernel,
        grid=(depth, 6),
        in_specs=[pl.BlockSpec((rows, D), lambda l, j: (0, 0)),
                  pl.BlockSpec((1, D, D), lambda l, j: (l, 0, j)),
                  pl.BlockSpec((1, 1, D), lambda l, j: (l, 0, j))],
        out_specs=pl.BlockSpec((1, rows, D), lambda l, j: (l, 0, j)),
        out_shape=jax.ShapeDtypeStruct((depth, rows, 6 * D), F32),
        compiler_params=_params("arbitrary", "arbitrary"),
        name="mods",
    )(cc, w_mod, b_mod.reshape(depth, 1, 6 * D))
    return out.reshape(depth, rows, 6, D)


def _norm_proj_kernel(x_ref, mod_ref, g_ref, w_ref, cos_ref, sin_ref, o_ref, *, rope_flags):
    x = x_ref[0]
    h = _rms(x) * g_ref[...]
    h = h * (1.0 + mod_ref[0, 1:2, :]) + mod_ref[0, 0:1, :]
    hb = h.astype(BF16)
    n_sub = len(rope_flags)
    sub_per_chunk = 4
    if any(rope_flags):
        cos = cos_ref[...]
        sin = sin_ref[...]
        lane = lax.broadcasted_iota(jnp.int32, cos.shape, 1)
        first_half = (lane % (HEAD // 2)) < (HEAD // 4)
    for c0 in range(0, n_sub, sub_per_chunk):
        c1 = min(c0 + sub_per_chunk, n_sub)
        acc = _dot(hb, w_ref[:, c0 * LANES:c1 * LANES])
        for s in range(c0, c1):
            a = acc[:, (s - c0) * LANES:(s - c0 + 1) * LANES]
            if rope_flags[s]:
                partner = jnp.where(first_half, pltpu.roll(a, LANES - HEAD // 4, 1), pltpu.roll(a, HEAD // 4, 1))
                a = a * cos + partner * sin
            o_ref[0, :, s * LANES:(s + 1) * LANES] = a.astype(o_ref.dtype)


def _norm_proj(xa, mod, g, w, cos_t, sin_t, rope_flags, out_dtype, n_lat_tiles, name):
    B, T, _ = xa.shape
    N = w.shape[1]
    ctx_row = B
    return pl.pallas_call(
        functools.partial(_norm_proj_kernel, rope_flags=rope_flags),
        grid=(B, T // TM),
        in_specs=[pl.BlockSpec((1, TM, D), lambda b, i: (b, i, 0)),
                  pl.BlockSpec((1, 6, D), lambda b, i: (jnp.where(i >= n_lat_tiles, ctx_row, b), 0, 0)),
                  pl.BlockSpec((1, D), lambda b, i: (0, 0)),
                  pl.BlockSpec((D, N), lambda b, i: (0, 0)),
                  pl.BlockSpec((TM, LANES), lambda b, i: (i, 0)),
                  pl.BlockSpec((TM, LANES), lambda b, i: (i, 0))],
        out_specs=pl.BlockSpec((1, TM, N), lambda b, i: (b, i, 0)),
        out_shape=jax.ShapeDtypeStruct((B, T, N), out_dtype),
        compiler_params=_params("arbitrary", "arbitrary"),
        name=name,
    )(xa, mod, g, w, cos_t, sin_t)


def _rope_tables(L, Lc):
    t = jnp.arange(L)
    pos = jnp.stack([(t // GRID_W).astype(F32), (t % GRID_W).astype(F32)], axis=1)
    n_freq = HEAD // 4
    inv = ROPE_BASE ** (-jnp.arange(n_freq, dtype=F32) / n_freq)
    ang = pos[:, :, None] * inv
    cos = jnp.cos(ang)
    sin = jnp.sin(ang)
    cos_h = jnp.concatenate([cos, cos], axis=-1).reshape(L, HEAD)
    sin_h = jnp.concatenate([-sin, sin], axis=-1).reshape(L, HEAD)
    cos_t = jnp.concatenate([jnp.tile(cos_h, (1, LANES // HEAD)), jnp.ones((Lc, LANES), F32)], axis=0)
    sin_t = jnp.concatenate([jnp.tile(sin_h, (1, LANES // HEAD)), jnp.zeros((Lc, LANES), F32)], axis=0)
    return cos_t, sin_t


def _lane_block_reduce(x, op):
    out = x[:, :LANES]
    for j in range(1, x.shape[1] // LANES):
        out = op(out, x[:, j * LANES:(j + 1) * LANES])
    return out


def _diff_attn_kernel(lam_ref, q_ref, k_ref, v_ref, g_ref, o_ref, s_lat, s_ctx, *, L, Lc, tg, lambda_init):
    i = pl.program_id(2)
    tq = q_ref.shape[1]
    n_groups = L // tg
    lp = lam_ref[0]
    lam = (jnp.exp(jnp.sum(lp[0:1] * lp[1:2], axis=-1, keepdims=True))
           - jnp.exp(jnp.sum(lp[2:3] * lp[3:4], axis=-1, keepdims=True)) + lambda_init)
    q = q_ref[0]
    lane = lax.broadcasted_iota(jnp.int32, q.shape, 1)
    zero = jnp.zeros_like(q)
    qq = jnp.concatenate([jnp.where(lane < HEAD, q, zero), jnp.where(lane >= HEAD, q, zero)], axis=0)

    s_c = _dot_nt(qq, k_ref[0, L:L + Lc, :])
    s_ctx[...] = s_c
    m_ctx = _lane_block_reduce(s_c, jnp.maximum)

    def finish(m_lanes, latent):
        m = jnp.max(m_lanes, axis=-1, keepdims=True)
        p = jnp.exp(s_ctx[...] - m)
        lrun = _lane_block_reduce(p, jnp.add)
        acc = _dot(p.astype(BF16), v_ref[0, L:L + Lc, :])
        if latent:
            def weighted(gi, carry):
                lrun, acc = carry
                start = pl.multiple_of(gi * tg, tg)
                p = jnp.exp(s_lat[gi] - m)
                lrun = lrun + _lane_block_reduce(p, jnp.add)
                acc = acc + _dot(p.astype(BF16), v_ref[0, pl.ds(start, tg), :])
                return lrun, acc

            lrun, acc = lax.fori_loop(0, n_groups, weighted, (lrun, acc))
        o = acc / jnp.sum(lrun, axis=-1, keepdims=True)
        o = _rms(o[:tq] - lam * o[tq:])
        o_ref[0] = (o * g_ref[0] * (1.0 - lambda_init)).astype(o_ref.dtype)

    @pl.when(i < L // tq)
    def _():
        def scores(gi, mrun):
            start = pl.multiple_of(gi * tg, tg)
            s = _dot_nt(qq, k_ref[0, pl.ds(start, tg), :])
            s_lat[gi] = s
            return jnp.maximum(mrun, _lane_block_reduce(s, jnp.maximum))

        finish(lax.fori_loop(0, n_groups, scores, m_ctx), True)

    @pl.when(i >= L // tq)
    def _():
        finish(m_ctx, False)


def _diff_attn(p_attn, diff_lambda, subln_g, L, Lc, lambda_init, l):
    B, T, _ = p_attn.shape
    tg = 4 * TM
    nh = N_DIFF_HEADS
    return pl.pallas_call(
        functools.partial(_diff_attn_kernel, L=L, Lc=Lc, tg=tg, lambda_init=lambda_init),
        grid=(B, nh, T // TM),
        in_specs=[pl.BlockSpec((1, 4, HEAD), lambda b, h, i: (l, 0, 0)),
                  pl.BlockSpec((1, TM, DIFF_V), lambda b, h, i: (b, i, h)),
                  pl.BlockSpec((1, T, DIFF_V), lambda b, h, i: (b, 0, nh + h)),
                  pl.BlockSpec((1, T, DIFF_V), lambda b, h, i: (b, 0, 2 * nh + h)),
                  pl.BlockSpec((1, 1, DIFF_V), lambda b, h, i: (l, 0, 0))],
        out_specs=pl.BlockSpec((1, TM, DIFF_V), lambda b, h, i: (b, i, h)),
        out_shape=jax.ShapeDtypeStruct((B, T, D), BF16),
        scratch_shapes=[pltpu.VMEM((L // tg, 2 * TM, tg), F32), pltpu.VMEM((2 * TM, Lc), F32)],
        compiler_params=_params("arbitrary", "arbitrary", "arbitrary"),
        name="diff_attn",
    )(diff_lambda, p_attn, p_attn, p_attn, subln_g.reshape(-1, 1, DIFF_V))


def _swa_kernel(sink_ref, q_ref, k_ref, v_ref, o_ref, *, L, Lc, l):
    i = pl.program_id(1)
    tq = q_ref.shape[1]
    kvw = SWA_KV * HEAD
    win = tq + 2 * TM
    lane_head = lax.broadcasted_iota(jnp.int32, (tq, kvw), 1) // HEAD
    k_ctx = k_ref[0, L:L + Lc, :]
    v_ctx = v_ref[0, L:L + Lc, :]

    def run(latent):
        if latent:
            start = pl.multiple_of(jnp.clip(i * tq - TM, 0, L - win), TM)
            k_win = k_ref[0, pl.ds(start, win), :]
            v_win = v_ref[0, pl.ds(start, win), :]
            q_pos = i * tq + lax.broadcasted_iota(jnp.int32, (tq, win), 0)
            k_pos = start + lax.broadcasted_iota(jnp.int32, (tq, win), 1)
            band = jnp.abs(k_pos - q_pos) <= WINDOW
        for g in range(SWA_GROUP):
            qg = q_ref[0, :, g * kvw:(g + 1) * kvw]
            acc = jnp.zeros((tq, kvw), F32)
            for kv in range(SWA_KV):
                qe = jnp.where(lane_head == kv, qg, jnp.zeros_like(qg))
                snk = sink_ref[l, kv * SWA_GROUP + g]
                s_c = _dot_nt(qe, k_ctx)
                m = jnp.maximum(jnp.max(s_c, axis=-1, keepdims=True), snk)
                if latent:
                    s_w = jnp.where(band, _dot_nt(qe, k_win), -jnp.inf)
                    m = jnp.maximum(m, jnp.max(s_w, axis=-1, keepdims=True))
                p_c = jnp.exp(s_c - m)
                den = jnp.sum(p_c, axis=-1, keepdims=True) + jnp.exp(snk - m)
                o = _dot(p_c.astype(BF16), v_ctx)
                if latent:
                    p_w = jnp.exp(s_w - m)
                    den = den + jnp.sum(p_w, axis=-1, keepdims=True)
                    o = o + _dot(p_w.astype(BF16), v_win)
                acc = acc + jnp.where(lane_head == kv, o / den, 0.0)
            o_ref[0, :, g * kvw:(g + 1) * kvw] = acc.astype(o_ref.dtype)

    @pl.when(i < L // tq)
    def _():
        run(True)

    @pl.when(i >= L // tq)
    def _():
        run(False)


def _swa(p_attn, sink, L, Lc, l):
    B, T, _ = p_attn.shape
    kvw = SWA_KV * HEAD
    return pl.pallas_call(
        functools.partial(_swa_kernel, L=L, Lc=Lc, l=l),
        grid=(B, T // TM),
        in_specs=[pl.BlockSpec(memory_space=pltpu.SMEM),
                  pl.BlockSpec((1, TM, D), lambda b, i: (b, i, 3)),
                  pl.BlockSpec((1, T, kvw), lambda b, i: (b, 0, 4 * D // kvw)),
                  pl.BlockSpec((1, T, kvw), lambda b, i: (b, 0, 4 * D // kvw + 1))],
        out_specs=pl.BlockSpec((1, TM, D), lambda b, i: (b, i, 0)),
        out_shape=jax.ShapeDtypeStruct((B, T, D), BF16),
        compiler_params=_params("arbitrary", "arbitrary"),
        name="swa",
    )(sink, p_attn, p_attn, p_attn)


def _hy_filter_kernel(emb_ref, w1_ref, b1_ref, w2_ref, b2_ref, fr_ref, w3f_ref, w3b_ref, dl_ref, o_ref, *, L):
    rows = TM
    fr = fr_ref[...]

    def raw(i, total):
        r0 = pl.multiple_of(i * rows, rows)
        e = emb_ref[pl.ds(r0, rows), :]
        z = jnp.sin(fr * (_dot(e, w1_ref[...], HIGHEST) + b1_ref[...]))
        z = jnp.sin(fr * (_dot(z, w2_ref[...], HIGHEST) + b2_ref[...]))
        row = r0 + lax.broadcasted_iota(jnp.int32, (rows, 1), 0)
        h = jnp.where(row < L, _dot(z, w3f_ref[...], HIGHEST), _dot(z, w3b_ref[...], HIGHEST))
        h = h * jnp.exp(-e[:, 0:1] * dl_ref[...])
        o_ref[pl.ds(r0, rows), :] = h
        return total + jnp.sum(jnp.abs(h), axis=0, keepdims=True)

    n_chunks = emb_ref.shape[0] // rows
    norm = lax.fori_loop(0, n_chunks, raw, jnp.zeros((1, o_ref.shape[1]), F32)) + EPS

    def scale(i, _):
        r0 = pl.multiple_of(i * rows, rows)
        row = r0 + lax.broadcasted_iota(jnp.int32, (rows, 1), 0)
        o_ref[pl.ds(r0, rows), :] = jnp.where(row == L, 0.0, o_ref[pl.ds(r0, rows), :] / norm)
        return 0

    lax.fori_loop(0, n_chunks, scale, 0)


def _hy_embedding(L):
    n = jnp.arange(2 * L)
    lag = jnp.where(n < L, n, jnp.where(n == L, 0, 2 * L - n))
    t = jnp.linspace(0.0, 1.0, L, dtype=F32)[lag][:, None]
    w = (2.0 * math.pi * jnp.arange(L, dtype=F32) / L)[lag][:, None]
    f = jnp.linspace(1e-4, HY_BANDS - 1, HY_BANDS, dtype=F32)[None, :]
    emb = jnp.concatenate([t, jnp.cos(f * w), -jnp.sin(f * w)], axis=-1)
    return jnp.pad(emb, ((0, 0), (0, LANES - HY_EMB)))


def _hy_filter(L, w1p, b1, w2, b2, fr, w3, deltas):
    ct = 256
    emb = _hy_embedding(L)
    full = lambda shape: pl.BlockSpec(shape, lambda j: (0, 0))
    return pl.pallas_call(
        functools.partial(_hy_filter_kernel, L=L),
        grid=(D // ct,),
        in_specs=[full((2 * L, LANES)), full((LANES, HY_FF)), full((1, HY_FF)), full((HY_FF, HY_FF)),
                  full((1, HY_FF)), full((1, HY_FF)),
                  pl.BlockSpec((HY_FF, ct), lambda j: (0, j)),
                  pl.BlockSpec((HY_FF, ct), lambda j: (0, D // ct + j)),
                  pl.BlockSpec((1, ct), lambda j: (0, j))],
        out_specs=pl.BlockSpec((2 * L, ct), lambda j: (0, j)),
        out_shape=jax.ShapeDtypeStruct((2 * L, D), F32),
        compiler_params=_params("arbitrary"),
        name="hy_filter",
    )(emb, w1p, b1, w2, b2, fr, w3, w3, deltas)


def _short_conv(x, w, b):
    n = x.shape[0]
    row = lax.broadcasted_iota(jnp.int32, x.shape, 0)
    prev = jnp.where(row == 0, 0.0, pltpu.roll(x, 1, 0))
    nxt = jnp.where(row == n - 1, 0.0, pltpu.roll(x, n - 1, 0))
    return prev * w[0:1] + x * w[1:2] + nxt * w[2:3] + b


def _hy_pre_kernel(x0_ref, x1_ref, v_ref, w0_ref, w1_ref, w2_ref, b0_ref, b1_ref, b2_ref, x0c_ref, u_ref, ub_ref):
    x0c_ref[0] = _short_conv(x0_ref[0], w0_ref[...], b0_ref[...])
    u = _short_conv(x1_ref[0], w1_ref[...], b1_ref[...]) * _short_conv(v_ref[0], w2_ref[...], b2_ref[...])
    u_ref[0] = u
    ub_ref[0] = u.astype(ub_ref.dtype)


def _hy_pre(p_hg, conv_w, conv_b, L):
    B = p_hg.shape[0]
    ct = LANES
    nb = D // ct
    col = lambda k: pl.BlockSpec((1, L, ct), lambda b, j: (b, 0, k * nb + j))
    wsp = lambda k: pl.BlockSpec((3, ct), lambda b, j: (0, k * nb + j))
    bsp = lambda k: pl.BlockSpec((1, ct), lambda b, j: (0, k * nb + j))
    out = pl.BlockSpec((1, L, ct), lambda b, j: (b, 0, j))
    return pl.pallas_call(
        _hy_pre_kernel,
        grid=(B, nb),
        in_specs=[col(0), col(1), col(2), wsp(0), wsp(1), wsp(2), bsp(0), bsp(1), bsp(2)],
        out_specs=[out, out, out],
        out_shape=[jax.ShapeDtypeStruct((B, L, D), F32)] * 2 + [jax.ShapeDtypeStruct((B, L, D), BF16)],
        compiler_params=_params("arbitrary", "arbitrary"),
        name="hy_pre",
    )(p_hg, p_hg, p_hg, conv_w, conv_w, conv_w, conv_b, conv_b, conv_b)


def _lmat_kernel(m_ref, x_ref, o_ref, *, precision):
    x = x_ref[0]
    if precision is None:
        x = x.astype(BF16)
    o_ref[0] = _dot(m_ref[...], x, precision).astype(o_ref.dtype)


def _lmat(mat, x, out_dtype, precision, wc, name):
    Bn, K, W = x.shape
    M = mat.shape[0]
    return pl.pallas_call(
        functools.partial(_lmat_kernel, precision=precision),
        grid=(Bn, W // wc),
        in_specs=[pl.BlockSpec((M, K), lambda b, j: (0, 0)),
                  pl.BlockSpec((1, K, wc), lambda b, j: (b, 0, j))],
        out_specs=pl.BlockSpec((1, M, wc), lambda b, j: (b, 0, j)),
        out_shape=jax.ShapeDtypeStruct((Bn, M, W), out_dtype),
        compiler_params=_params("arbitrary", "arbitrary"),
        name=name,
    )(mat, x)


def _cmul(xr, xi, kr, ki):
    return xr * kr - xi * ki, xr * ki + xi * kr


def _hy_spec_kernel(a_ref, m_ref, o_ref):
    a = jnp.concatenate([a_ref[0, 0], a_ref[1, 0]], axis=0)
    o_ref[0] = _dot(m_ref[0], a, HIGHEST)


def _hy_spectrum(a, m_tab):
    n1 = a.shape[1]
    n2 = DFT_N2
    return pl.pallas_call(
        _hy_spec_kernel,
        grid=(n1,),
        in_specs=[pl.BlockSpec((2, 1, n2, D), lambda k: (0, k, 0, 0)),
                  pl.BlockSpec((1, 2 * n2, 2 * n2), lambda k: (k, 0, 0))],
        out_specs=pl.BlockSpec((1, 2 * n2, D), lambda k: (k, 0, 0)),
        out_shape=jax.ShapeDtypeStruct((n1, 2 * n2, D), F32),
        compiler_params=_params("arbitrary"),
        name="hy_spectrum",
    )(a, m_tab)


def _hy_mid_kernel(a_ref, m_ref, mt_ref, kf_ref, o_ref):
    n2 = DFT_N2
    kr = kf_ref[0, :n2]
    ki = kf_ref[0, n2:]

    def body(b, _):
        a = jnp.concatenate([a_ref[b, 0, 0], a_ref[b, 1, 0]], axis=0)
        x = _dot(m_ref[0], a)
        yr, yi = _cmul(x[:n2], x[n2:], kr, ki)
        y = jnp.concatenate([yr, yi], axis=0).astype(BF16)
        z = _dot(mt_ref[0], y)
        o_ref[b, 0, 0] = z[:n2].astype(o_ref.dtype)
        o_ref[b, 1, 0] = z[n2:].astype(o_ref.dtype)
        return 0

    lax.fori_loop(0, a_ref.shape[0], body, 0)


def _hy_mid(a, m_bf, mt_bf, kf):
    B, _, n1, n2, _ = a.shape
    blk = pl.BlockSpec((B, 2, 1, n2, D), lambda k: (0, 0, k, 0, 0))
    tab = pl.BlockSpec((1, 2 * n2, 2 * n2), lambda k: (k, 0, 0))
    return pl.pallas_call(
        _hy_mid_kernel,
        grid=(n1,),
        in_specs=[blk, tab, tab, pl.BlockSpec((1, 2 * n2, D), lambda k: (k, 0, 0))],
        out_specs=blk,
        out_shape=jax.ShapeDtypeStruct(a.shape, BF16),
        compiler_params=_params("arbitrary"),
        name="hy_mid",
    )(a, m_bf, mt_bf, kf)


def _hy_output(x0c, conv, u, bias):
    return x0c * (conv + u * bias)


def _hy_ctx_kernel(x0_ref, x1_ref, v_ref, w0_ref, w1_ref, w2_ref, b0_ref, b1_ref, b2_ref,
                   f_ref, kf_ref, g_ref, bias_ref, o_ref):
    x0 = _short_conv(x0_ref[0], w0_ref[...], b0_ref[...])
    u = _short_conv(x1_ref[0], w1_ref[...], b1_ref[...]) * _short_conv(v_ref[0], w2_ref[...], b2_ref[...])
    x = _dot(f_ref[...], u, HIGHEST)
    nf = x.shape[0] // 2
    yr, yi = _cmul(x[:nf], x[nf:], kf_ref[:nf], kf_ref[nf:])
    conv = _dot(g_ref[...], jnp.concatenate([yr, yi], axis=0), HIGHEST)
    o_ref[0] = _hy_output(x0, conv, u, bias_ref[...]).astype(o_ref.dtype)


def _hy_ctx(p_hg, conv_w, conv_b, f_c, kf_c, g_c, bias, L, Lc):
    B = p_hg.shape[0]
    ct = 256
    nb = D // ct
    rb = L // Lc
    col = lambda k: pl.BlockSpec((1, Lc, ct), lambda b, j: (b, rb, k * nb + j))
    wsp = lambda k: pl.BlockSpec((3, ct), lambda b, j: (0, k * nb + j))
    bsp = lambda k: pl.BlockSpec((1, ct), lambda b, j: (0, k * nb + j))
    return pl.pallas_call(
        _hy_ctx_kernel,
        grid=(B, nb),
        in_specs=[col(0), col(1), col(2), wsp(0), wsp(1), wsp(2), bsp(0), bsp(1), bsp(2),
                  pl.BlockSpec(f_c.shape, lambda b, j: (0, 0)),
                  pl.BlockSpec((kf_c.shape[0], ct), lambda b, j: (0, j)),
                  pl.BlockSpec(g_c.shape, lambda b, j: (0, 0)),
                  pl.BlockSpec((1, ct), lambda b, j: (0, j))],
        out_specs=pl.BlockSpec((1, Lc, ct), lambda b, j: (b, 0, j)),
        out_shape=jax.ShapeDtypeStruct((B, Lc, D), BF16),
        compiler_params=_params("arbitrary", "arbitrary"),
        name="hy_ctx",
    )(p_hg, p_hg, p_hg, conv_w, conv_w, conv_w, conv_b, conv_b, conv_b, f_c, kf_c, g_c, bias)


def _dft_tables(L, Lc):
    n = 2 * L
    n2 = DFT_N2
    n1 = n // n2
    two_pi = 2.0 * math.pi

    def cs(idx, period):
        ang = (idx % period).astype(F32) * (two_pi / period)
        return jnp.cos(ang), jnp.sin(ang)

    k1 = jnp.arange(n1)[:, None]
    c, s = cs(k1 * jnp.arange(n1)[None, :], n1)
    f1_full = jnp.concatenate([c, -s], axis=0)
    f1_half = f1_full[:, :n1 // 2]
    kk1 = jnp.arange(n1)[:, None, None]
    kk2 = jnp.arange(n2)[None, :, None]
    nn2 = jnp.arange(n2)[None, None, :]
    c, s = cs(nn2 * kk1 + n1 * nn2 * kk2, n)
    ci = -s
    m_tab = jnp.concatenate([jnp.concatenate([c, -ci], axis=2), jnp.concatenate([ci, c], axis=2)], axis=1)
    c, s = cs(jnp.arange(n1 // 2)[:, None] * jnp.arange(n1)[None, :], n1)
    g_mat = jnp.concatenate([c, -s], axis=1) / n
    nc = 2 * Lc
    kc = jnp.arange(nc)[:, None]
    c, s = cs(kc * jnp.arange(nc)[None, :], nc)
    fc_full = jnp.concatenate([c, -s], axis=0)
    c, s = cs(jnp.arange(Lc)[:, None] * jnp.arange(nc)[None, :], nc)
    g_c = jnp.concatenate([c, -s], axis=1) / nc
    return dict(f1_full=f1_full, f1_half=f1_half.astype(BF16), m_tab=m_tab, m_bf=m_tab.astype(BF16),
                mt_bf=jnp.swapaxes(m_tab, 1, 2).astype(BF16), g_mat=g_mat.astype(BF16),
                fc_full=fc_full, fc_half=fc_full[:, :Lc], g_c=g_c)


def _hyena(p_hg, hp, tabs, L, Lc):
    B = p_hg.shape[0]
    n2 = DFT_N2
    n1 = 2 * L // n2
    wc = min(16 * D, n2 * D)
    deltas = jnp.abs(jnp.linspace(math.log(HY_TARGET) / HY_FAST_DECAY, math.log(HY_TARGET) / HY_SLOW_DECAY,
                                  D, dtype=F32))[None, :]
    filt = functools.partial(_hy_filter, w1p=hp['w1p'], b1=hp['b1'], w2=hp['w2'], b2=hp['b2'], fr=hp['fr'],
                             w3=hp['w3'], deltas=deltas)
    k_lat = filt(L)
    a_f = _lmat(tabs['f1_full'], k_lat.reshape(1, n1, n2 * D), F32, HIGHEST, wc, "hy_filt_s1")
    kf = _hy_spectrum(a_f.reshape(2, n1, n2, D), tabs['m_tab'])
    k_ctx = filt(Lc)
    kf_c = _lmat(tabs['fc_full'], k_ctx[None], F32, HIGHEST, D, "hy_filt_ctx")[0]
    x0c, u, u_bf = _hy_pre(p_hg, hp['conv_w'], hp['conv_b'], L)
    a = _lmat(tabs['f1_half'], u_bf.reshape(B, n1 // 2, n2 * D), BF16, None, wc, "hy_s1")
    z = _hy_mid(a.reshape(B, 2, n1, n2, D), tabs['m_bf'], tabs['mt_bf'], kf)
    conv = _lmat(tabs['g_mat'], z.reshape(B, 2 * n1, n2 * D), BF16, None, wc, "hy_s3").reshape(B, L, D)
    y_c = _hy_ctx(p_hg, hp['conv_w'], hp['conv_b'], tabs['fc_half'], kf_c, tabs['g_c'], hp['bias'], L, Lc)
    return conv, x0c, u, y_c


def _merge_kernel(x_ref, mod_ref, yd_ref, cv_ref, x0_ref, u_ref, yc_ref, hb_ref, ys_ref, gd_ref, gh_ref, gs_ref,
                  wb_ref, wo_ref, g2_ref, rw_ref, xo_ref, h2_ref, lg_ref, *, n_lat_tiles):
    y_hy = jnp.where(pl.program_id(1) < n_lat_tiles,
                     _hy_output(x0_ref[0], cv_ref[0].astype(F32), u_ref[0], hb_ref[...]),
                     yc_ref[0].astype(F32)).astype(BF16)
    merged = None
    for k, (y, gt_ref) in enumerate(((yd_ref[0], gd_ref), (y_hy, gh_ref), (ys_ref[0], gs_ref))):
        term = jax.nn.sigmoid(gt_ref[0]) * _dot(y, wb_ref[k])
        merged = term if merged is None else merged + term
    y = _dot(merged.astype(BF16), wo_ref[...])
    x = x_ref[0] + mod_ref[0, 2:3, :] * y
    xo_ref[0] = x
    h2 = _rms(x) * g2_ref[...]
    h2 = h2 * (1.0 + mod_ref[0, 4:5, :]) + mod_ref[0, 3:4, :]
    h2_ref[0] = h2.astype(h2_ref.dtype)
    lg_ref[0] = _dot_nt(rw_ref[...], h2, HIGHEST)


def _merge(xa, mod, y_diff, hy, hy_bias, y_swa, p_hg, w_branch, w_out, g2, rw_t, n_lat_tiles):
    B, T, _ = xa.shape
    ctx_row = B
    conv, x0c, u, y_hy_c = hy
    tile = lambda c: pl.BlockSpec((1, TM, D), lambda b, i: (b, i, c))
    lat = pl.BlockSpec((1, TM, D), lambda b, i: (b, jnp.minimum(i, n_lat_tiles - 1), 0))
    return pl.pallas_call(
        functools.partial(_merge_kernel, n_lat_tiles=n_lat_tiles),
        grid=(B, T // TM),
        in_specs=[tile(0),
                  pl.BlockSpec((1, 6, D), lambda b, i: (jnp.where(i >= n_lat_tiles, ctx_row, b), 0, 0)),
                  tile(0), lat, lat, lat,
                  pl.BlockSpec((1, y_hy_c.shape[1], D), lambda b, i: (b, 0, 0)),
                  pl.BlockSpec((1, D), lambda b, i: (0, 0)),
                  tile(0), tile(3), tile(4), tile(5),
                  pl.BlockSpec((3, D, D), lambda b, i: (0, 0, 0)),
                  pl.BlockSpec((D, D), lambda b, i: (0, 0)),
                  pl.BlockSpec((1, D), lambda b, i: (0, 0)),
                  pl.BlockSpec((N_EXPERTS, D), lambda b, i: (0, 0))],
        out_specs=[tile(0), tile(0), pl.BlockSpec((1, N_EXPERTS, TM), lambda b, i: (b, 0, i))],
        out_shape=[jax.ShapeDtypeStruct((B, T, D), F32), jax.ShapeDtypeStruct((B, T, D), BF16),
                   jax.ShapeDtypeStruct((B, N_EXPERTS, T), F32)],
        compiler_params=_params("arbitrary", "arbitrary"),
        name="merge",
    )(xa, mod, y_diff, conv, x0c, u, y_hy_c, hy_bias, y_swa, p_hg, p_hg, p_hg, w_branch, w_out, g2, rw_t)


def _router_kernel(lg_ref, slot_ref, gate_ref, *, segments):
    lg = lg_ref[0]
    e = jnp.exp(lg - jnp.max(lg, axis=0, keepdims=True))
    aff = e / jnp.sum(e, axis=0, keepdims=True)
    r = lax.broadcasted_iota(jnp.int32, (LANES, LANES), 0)
    c = lax.broadcasted_iota(jnp.int32, (LANES, LANES), 1)
    before = (r < c).astype(BF16)

    def prefix_count(mask, n):
        run = jnp.zeros((N_EXPERTS, 1), F32)
        outs = []
        for j in range(n // LANES):
            blk = mask[:, j * LANES:(j + 1) * LANES].astype(F32).astype(BF16)
            outs.append(_dot(blk, before) + run)
            run = run + jnp.sum(blk.astype(F32), axis=1, keepdims=True)
        return jnp.concatenate(outs, axis=1)

    for start, n, cap, slot_base in segments:
        a = aff[:, start:start + n]
        bits = pltpu.bitcast(a, jnp.int32)

        def search(_, bounds):
            lo, hi = bounds
            mid = lo + ((hi - lo + 1) >> 1)
            cnt = jnp.sum((bits >= mid).astype(F32), axis=1, keepdims=True)
            ok = cnt >= cap
            return jnp.where(ok, mid, lo), jnp.where(ok, hi, mid - 1)

        lo0 = jnp.zeros((N_EXPERTS, 1), jnp.int32)
        hi0 = jnp.full((N_EXPERTS, 1), 0x7F800000, jnp.int32)
        thr, _ = lax.fori_loop(0, 32, search, (lo0, hi0))
        above = bits > thr
        tie = bits == thr
        need = cap - jnp.sum(above.astype(F32), axis=1, keepdims=True)
        sel = above | (tie & (prefix_count(tie, n) < need))
        pos = prefix_count(sel, n)
        slot_ref[0, :, start:start + n] = jnp.where(sel, pos + slot_base, -1.0)
        gate_ref[0, :, start:start + n] = jnp.where(sel, a, 0.0)


def _router(logits_t, segments):
    B, E, T = logits_t.shape
    blk = pl.BlockSpec((1, E, T), lambda b: (b, 0, 0))
    return pl.pallas_call(
        functools.partial(_router_kernel, segments=segments),
        grid=(B,),
        in_specs=[blk],
        out_specs=[blk, blk],
        out_shape=[jax.ShapeDtypeStruct((B, E, T), F32)] * 2,
        compiler_params=_params("arbitrary"),
        name="router",
    )(logits_t)


def _moe_ffn_kernel(slot_ref, gate_ref, h2_ref, w1_ref, w3_ref, w2_ref, o_ref, *, n_slots, halves):
    T = h2_ref.shape[1]
    tc = T // halves
    slot_id = lax.broadcasted_iota(jnp.int32, (n_slots, tc), 0).astype(F32)
    xg = jnp.zeros((n_slots, D), F32)
    g = jnp.zeros((n_slots, 1), F32)
    for c in range(halves):
        hit = slot_ref[0, 0, :, c * tc:(c + 1) * tc] == slot_id
        xg = xg + _dot(jnp.where(hit, 1.0, 0.0).astype(BF16), h2_ref[0, c * tc:(c + 1) * tc, :])
        g = g + jnp.sum(jnp.where(hit, gate_ref[0, 0, :, c * tc:(c + 1) * tc], 0.0), axis=1, keepdims=True)
    xb = xg.astype(BF16)
    a = _dot(xb, w1_ref[0, 0])
    b = _dot(xb, w3_ref[0, 0])
    hidden = (a * jax.nn.sigmoid(a) * b).astype(BF16)
    o_ref[0, 0] = (_dot(hidden, w2_ref[0, 0]) * g).astype(o_ref.dtype)


def _moe_ffn(slot, gate, h2, w1, w3, w2, l, n_slots):
    B, E, T = slot.shape
    row = pl.BlockSpec((1, 1, 1, T), lambda b, e: (b, e, 0, 0))
    wsp = pl.BlockSpec((1, 1, D, D), lambda b, e: (l, e, 0, 0))
    return pl.pallas_call(
        functools.partial(_moe_ffn_kernel, n_slots=n_slots, halves=2),
        grid=(B, E),
        in_specs=[row, row, pl.BlockSpec((1, T, D), lambda b, e: (b, 0, 0)), wsp, wsp, wsp],
        out_specs=pl.BlockSpec((1, 1, n_slots, D), lambda b, e: (b, e, 0, 0)),
        out_shape=jax.ShapeDtypeStruct((B, E, n_slots, D), BF16),
        compiler_params=_params("arbitrary", "arbitrary"),
        name="moe_ffn",
    )(slot.reshape(B, E, 1, T), gate.reshape(B, E, 1, T), h2, w1, w3, w2)


def _moe_combine_kernel(x_ref, mod_ref, slot_ref, y_ref, o_ref, *, n_slots):
    tm = x_ref.shape[1]
    slot_id = lax.broadcasted_iota(jnp.int32, (tm, n_slots), 1).astype(F32)
    acc = jnp.zeros((tm, D), F32)
    for e in range(N_EXPERTS):
        hit = slot_ref[0, :, e:e + 1] == slot_id
        acc = acc + _dot(jnp.where(hit, 1.0, 0.0).astype(BF16), y_ref[0, e])
    o_ref[0] = x_ref[0] + mod_ref[0, 5:6, :] * acc


def _moe_combine(xa, mod, slot_t, y, n_lat_tiles, n_slots):
    B, T, _ = xa.shape
    ctx_row = B
    tile = pl.BlockSpec((1, TM, D), lambda b, i: (b, i, 0))
    return pl.pallas_call(
        functools.partial(_moe_combine_kernel, n_slots=n_slots),
        grid=(B, T // TM),
        in_specs=[tile,
                  pl.BlockSpec((1, 6, D), lambda b, i: (jnp.where(i >= n_lat_tiles, ctx_row, b), 0, 0)),
                  pl.BlockSpec((1, TM, N_EXPERTS), lambda b, i: (b, i, 0)),
                  pl.BlockSpec((1, N_EXPERTS, n_slots, D), lambda b, i: (b, 0, 0, 0))],
        out_specs=tile,
        out_shape=jax.ShapeDtypeStruct((B, T, D), F32),
        compiler_params=_params("arbitrary", "arbitrary"),
        name="moe_combine",
    )(xa, mod, slot_t, y)


def _final_kernel(x_ref, g_ref, o_ref):
    o_ref[0] = _rms(x_ref[0]) * g_ref[...]


def _final_norm(xa, g, L):
    B = xa.shape[0]
    tile = pl.BlockSpec((1, TM, D), lambda b, i: (b, i, 0))
    return pl.pallas_call(
        _final_kernel,
        grid=(B, L // TM),
        in_specs=[tile, pl.BlockSpec((1, D), lambda b, i: (0, 0))],
        out_specs=tile,
        out_shape=jax.ShapeDtypeStruct((B, L, D), F32),
        compiler_params=_params("arbitrary", "arbitrary"),
        name="final_norm",
    )(xa, g)


def _swa_head_perm():
    return [kv * SWA_GROUP + g for g in range(SWA_GROUP) for kv in range(SWA_KV)]


def _split_w_in(w_in):
    depth = w_in.shape[0]
    o = [0, D, 2 * D, 3 * D, 6 * D, 7 * D, 7 * D + SWA_KV * HEAD, 7 * D + 2 * SWA_KV * HEAD, 10 * D + 2 * SWA_KV * HEAD]
    dq, dk, dv, hy, sq, sk, sv, gt = (w_in[:, :, o[k]:o[k + 1]] for k in range(8))
    scale = HEAD ** -0.5
    sq = sq.reshape(depth, D, SWA_KV * SWA_GROUP, HEAD)[:, :, jnp.array(_swa_head_perm())].reshape(depth, D, D)
    w_attn = jnp.concatenate([dq * scale, dk, dv, sq * scale, sk, sv], axis=-1).astype(BF16)
    w_hg = jnp.concatenate([hy, gt], axis=-1).astype(BF16)
    return w_attn, w_hg


def kernel(x, c, ctx, c_ctx, w_mod, b_mod, norm1_g, norm2_g, w_in, diff_lambda, diff_subln_g, hy_conv_w,
           hy_conv_b, hy_ff_w1, hy_ff_b1, hy_ff_w2, hy_ff_b2, hy_ff_w3, hy_sin_freq, hy_bias, swa_sink,
           w_branch, w_out, router_w, moe_w1, moe_w3, moe_w2, final_g):
    B, L, _ = x.shape
    Lc = ctx.shape[1]
    depth = w_mod.shape[0]
    T = L + Lc
    assert Lc == TM and L % (4 * TM) == 0 and x.shape[2] == D
    n_lat_tiles = L // TM
    cap_l = CAPACITY_FACTOR * L // N_EXPERTS
    cap_c = CAPACITY_FACTOR * Lc // N_EXPERTS
    n_slots = cap_l + cap_c
    segments = ((0, L, cap_l, 0), (L, Lc, cap_c, cap_l))

    xa = jnp.concatenate([x, ctx], axis=1)
    mods = _mods(c, c_ctx, w_mod, b_mod)
    cos_t, sin_t = _rope_tables(L, Lc)
    tabs = _dft_tables(L, Lc)
    w_attn, w_hg = _split_w_in(w_in)
    rope_flags = tuple([True] * 16 + [False] * 8 + [True] * 8 + [True] * 2 + [False] * 2)
    perm = jnp.array(_swa_head_perm())
    wb = w_branch.at[:, 2].set(
        w_branch[:, 2].reshape(depth, SWA_KV * SWA_GROUP, HEAD, D)[:, perm].reshape(depth, D, D)).astype(BF16)
    wo = w_out.astype(BF16)
    w1, w3, w2 = moe_w1.astype(BF16), moe_w3.astype(BF16), moe_w2.astype(BF16)
    rw_t = jnp.swapaxes(router_w, 1, 2)
    w1p = jnp.pad(hy_ff_w1, ((0, 0), (0, LANES - HY_EMB), (0, 0)))

    for l in range(depth):
        lambda_init = 0.8 - 0.6 * math.exp(-0.3 * l)
        mod = mods[l]
        p_attn = _norm_proj(xa, mod, norm1_g[l][None], w_attn[l], cos_t, sin_t, rope_flags, BF16,
                            n_lat_tiles, "proj_attn")
        p_hg = _norm_proj(xa, mod, norm1_g[l][None], w_hg[l], cos_t, sin_t, (False,) * (N_HG // LANES), F32,
                          n_lat_tiles, "proj_hg")
        y_diff = _diff_attn(p_attn, diff_lambda, diff_subln_g, L, Lc, lambda_init, l)
        y_swa = _swa(p_attn, swa_sink, L, Lc, l)
        hp = dict(w1p=w1p[l], b1=hy_ff_b1[l][None], w2=hy_ff_w2[l], b2=hy_ff_b2[l][None],
                  fr=hy_sin_freq[l][None], w3=hy_ff_w3[l], conv_w=hy_conv_w[l], conv_b=hy_conv_b[l][None],
                  bias=hy_bias[l][None])
        hy = _hyena(p_hg, hp, tabs, L, Lc)
        xa, h2, logits_t = _merge(xa, mod, y_diff, hy, hp['bias'], y_swa, p_hg, wb[l], wo[l], norm2_g[l][None],
                                  rw_t[l], n_lat_tiles)
        slot, gate = _router(logits_t, segments)
        y_e = _moe_ffn(slot, gate, h2, w1, w3, w2, l, n_slots)
        xa = _moe_combine(xa, mod, jnp.swapaxes(slot, 1, 2), y_e, n_lat_tiles, n_slots)
    return _final_norm(xa, final_g[None], L)
```

```python
import functools
import math

import jax
import jax.numpy as jnp
from jax import lax
from jax.experimental import pallas as pl
from jax.experimental.pallas import tpu as pltpu

F32 = jnp.float32
BF16 = jnp.bfloat16
HIGHEST = lax.Precision.HIGHEST

D = 1024
HEAD = 64
GRID_W = 64
ROPE_BASE = 10000.0
EPS = 1e-6
N_DIFF_HEADS = D // (2 * HEAD)
DIFF_V = 2 * HEAD
SWA_KV = 4
SWA_GROUP = 4
WINDOW = 128
N_EXPERTS = 16
CAPACITY_FACTOR = 2
HY_BANDS = 16
HY_EMB = 1 + 2 * HY_BANDS
HY_FF = 64
HY_FAST_DECAY = 0.3
HY_SLOW_DECAY = 1.5
HY_TARGET = 1e-2

LANES = 128
TM = 256
DFT_N2 = 128
VMEM_LIMIT = 56 * 1024 * 1024

N_ATTN = 3 * D + D + 2 * SWA_KV * HEAD
N_HG = 3 * D + 3 * D


def _params(*sem):
    return pltpu.CompilerParams(dimension_semantics=sem, vmem_limit_bytes=VMEM_LIMIT)


def _dot(a, b, precision=None):
    return jnp.dot(a, b, preferred_element_type=F32, precision=precision)


def _dot_nt(a, b, precision=None):
    return lax.dot_general(a, b, (((1,), (1,)), ((), ())), preferred_element_type=F32, precision=precision)


def _rms(x):
    return x * lax.rsqrt(jnp.mean(x * x, axis=-1, keepdims=True) + EPS)


def _mod_kernel(c_ref, w_ref, b_ref, o_ref):
    c = c_ref[...]
    sc = c * jax.nn.sigmoid(c)
    o_ref[0] = _dot(sc, w_ref[0], HIGHEST) + b_ref[0]


def _mods(c, c_ctx, w_mod, b_mod):
    B = c.shape[0]
    depth = w_mod.shape[0]
    rows = -(-(B + 1) // 8) * 8
    cc = jnp.zeros((rows, D), F32).at[:B].set(c).at[B].set(c_ctx)
    out = pl.pallas_call(
        _mod_kernel,
        grid=(depth, 6),
        in_specs=[pl.BlockSpec((rows, D), lambda l, j: (0, 0)),
                  pl.BlockSpec((1, D, D), lambda l, j: (l, 0, j)),
                  pl.BlockSpec((1, 1, D), lambda l, j: (l, 0, j))],
        out_specs=pl.BlockSpec((1, rows, D), lambda l, j: (l, 0, j)),
        out_shape=jax.ShapeDtypeStruct((depth, rows, 6 * D), F32),
        compiler_params=_params("arbitrary", "arbitrary"),
        name="mods",
    )(cc, w_mod, b_mod.reshape(depth, 1, 6 * D))
    return out.reshape(depth, rows, 6, D)


def _norm_proj_kernel(x_ref, mod_ref, g_ref, w_ref, cos_ref, sin_ref, o_ref, *, rope_flags):
    x = x_ref[0]
    h = _rms(x) * g_ref[...]
    h = h * (1.0 + mod_ref[0, 1:2, :]) + mod_ref[0, 0:1, :]
    hb = h.astype(BF16)
    n_sub = len(rope_flags)
    sub_per_chunk = 4
    if any(rope_flags):
        cos = cos_ref[...]
        sin = sin_ref[...]
        lane = lax.broadcasted_iota(jnp.int32, cos.shape, 1)
        first_half = (lane % (HEAD // 2)) < (HEAD // 4)
    for c0 in range(0, n_sub, sub_per_chunk):
        c1 = min(c0 + sub_per_chunk, n_sub)
        acc = _dot(hb, w_ref[:, c0 * LANES:c1 * LANES])
        for s in range(c0, c1):
            a = acc[:, (s - c0) * LANES:(s - c0 + 1) * LANES]
            if rope_flags[s]:
                partner = jnp.where(first_half, pltpu.roll(a, LANES - HEAD // 4, 1), pltpu.roll(a, HEAD // 4, 1))
                a = a * cos + partner * sin
            o_ref[0, :, s * LANES:(s + 1) * LANES] = a.astype(o_ref.dtype)


def _norm_proj(xa, mod, g, w, cos_t, sin_t, rope_flags, out_dtype, n_lat_tiles, name):
    B, T, _ = xa.shape
    N = w.shape[1]
    ctx_row = B
    return pl.pallas_call(
        functools.partial(_norm_proj_kernel, rope_flags=rope_flags),
        grid=(B, T // TM),
        in_specs=[pl.BlockSpec((1, TM, D), lambda b, i: (b, i, 0)),
                  pl.BlockSpec((1, 6, D), lambda b, i: (jnp.where(i >= n_lat_tiles, ctx_row, b), 0, 0)),
                  pl.BlockSpec((1, D), lambda b, i: (0, 0)),
                  pl.BlockSpec((D, N), lambda b, i: (0, 0)),
                  pl.BlockSpec((TM, LANES), lambda b, i: (i, 0)),
                  pl.BlockSpec((TM, LANES), lambda b, i: (i, 0))],
        out_specs=pl.BlockSpec((1, TM, N), lambda b, i: (b, i, 0)),
        out_shape=jax.ShapeDtypeStruct((B, T, N), out_dtype),
        compiler_params=_params("arbitrary", "arbitrary"),
        name=name,
    )(xa, mod, g, w, cos_t, sin_t)


def _rope_tables(L, Lc):
    t = jnp.arange(L)
    pos = jnp.stack([(t // GRID_W).astype(F32), (t % GRID_W).astype(F32)], axis=1)
    n_freq = HEAD // 4
    inv = ROPE_BASE ** (-jnp.arange(n_freq, dtype=F32) / n_freq)
    ang = pos[:, :, None] * inv
    cos = jnp.cos(ang)
    sin = jnp.sin(ang)
    cos_h = jnp.concatenate([cos, cos], axis=-1).reshape(L, HEAD)
    sin_h = jnp.concatenate([-sin, sin], axis=-1).reshape(L, HEAD)
    cos_t = jnp.concatenate([jnp.tile(cos_h, (1, LANES // HEAD)), jnp.ones((Lc, LANES), F32)], axis=0)
    sin_t = jnp.concatenate([jnp.tile(sin_h, (1, LANES // HEAD)), jnp.zeros((Lc, LANES), F32)], axis=0)
    return cos_t, sin_t


def _lane_block_reduce(x, op):
    out = x[:, :LANES]
    for j in range(1, x.shape[1] // LANES):
        out = op(out, x[:, j * LANES:(j + 1) * LANES])
    return out


def _diff_attn_kernel(lam_ref, q_ref, k_ref, v_ref, g_ref, o_ref, s_lat, s_ctx, m_run, qq_scr, l_scr, acc_scr,
                      *, L, Lc, tg, n_tiles, n_total, lambda_init):
    n = pl.program_id(0)
    tq = q_ref.shape[1]
    n_groups = (L + Lc) // tg
    n_lat = L // tq
    cur = n % 2
    prv = 1 - cur
    score_any = n < n_total
    score_lat = jnp.logical_and(score_any, n % n_tiles < n_lat)
    weigh_any = n >= 1
    weigh_lat = jnp.logical_and(weigh_any, (n + n_tiles - 1) % n_tiles < n_lat)
    rows = (2 * tq, LANES)

    @pl.when(score_any)
    def _():
        q = q_ref[0]
        lane = lax.broadcasted_iota(jnp.int32, q.shape, 1)
        zero = jnp.zeros_like(q)
        qq_scr[...] = jnp.concatenate([jnp.where(lane < HEAD, q, zero), jnp.where(lane >= HEAD, q, zero)], axis=0)

    @pl.when(jnp.logical_and(score_any, jnp.logical_not(score_lat)))
    def _():
        s_c = _dot_nt(qq_scr[...], k_ref[0, L:L + Lc, :])
        s_ctx[...] = s_c
        m_run[cur] = _lane_block_reduce(s_c, jnp.maximum)

    def row_max():
        return jnp.max(m_run[prv], axis=-1, keepdims=True)

    @pl.when(jnp.logical_and(weigh_any, jnp.logical_not(weigh_lat)))
    def _():
        p = jnp.exp2(s_ctx[...] - row_max())
        l_scr[...] = _lane_block_reduce(p, jnp.add)
        acc_scr[...] = _dot(p.astype(BF16), v_ref[0, L:L + Lc, :])

    def scores(gi, mrun):
        start = pl.multiple_of(gi * tg, LANES)
        s = _dot_nt(qq_scr[...], k_ref[0, pl.ds(start, tg), :])
        s_lat[cur, gi] = s
        return jnp.maximum(mrun, _lane_block_reduce(s, jnp.maximum))

    def weighted(gi, lrun, acc, m):
        start = pl.multiple_of(gi * tg, LANES)
        p = jnp.exp2(s_lat[prv, gi] - m)
        return lrun + _lane_block_reduce(p, jnp.add), acc + _dot(p.astype(BF16), v_ref[0, pl.ds(start, tg), :])

    lowest = jnp.full(rows, -jnp.inf, F32)
    nothing = jnp.zeros(rows, F32)

    @pl.when(jnp.logical_and(score_lat, weigh_lat))
    def _():
        m = row_max()

        def both(gi, carry):
            mrun, lrun, acc = carry
            lrun, acc = weighted(gi, lrun, acc, m)
            return scores(gi, mrun), lrun, acc

        m_run[cur], l_scr[...], acc_scr[...] = lax.fori_loop(0, n_groups, both, (lowest, nothing, nothing))

    @pl.when(jnp.logical_and(score_lat, jnp.logical_not(weigh_lat)))
    def _():
        m_run[cur] = lax.fori_loop(0, n_groups, scores, lowest)

    @pl.when(jnp.logical_and(jnp.logical_not(score_lat), weigh_lat))
    def _():
        m = row_max()
        l_scr[...], acc_scr[...] = lax.fori_loop(0, n_groups, lambda gi, c: weighted(gi, c[0], c[1], m),
                                                  (nothing, nothing))

    @pl.when(weigh_any)
    def _():
        lp = lam_ref[0]
        lam = (jnp.exp(jnp.sum(lp[0:1] * lp[1:2], axis=-1, keepdims=True))
               - jnp.exp(jnp.sum(lp[2:3] * lp[3:4], axis=-1, keepdims=True)) + lambda_init)
        o = acc_scr[...] / jnp.sum(l_scr[...], axis=-1, keepdims=True)
        o = _rms(o[:tq] - lam * o[tq:])
        o_ref[0] = (o * g_ref[0] * (1.0 - lambda_init)).astype(o_ref.dtype)


def _diff_attn(p_attn, diff_lambda, subln_g, L, Lc, lambda_init, l):
    B, T, _ = p_attn.shape
    tg = T // 2
    nh = N_DIFF_HEADS
    n_tiles = T // TM
    n_total = B * nh * n_tiles

    def tile(t):
        return t // (nh * n_tiles), (t // n_tiles) % nh, t % n_tiles

    def scored(n):
        return tile(jnp.minimum(n, n_total - 1))

    def weighed(n):
        return tile(jnp.maximum(n - 1, 0))

    def q_map(n):
        b, h, i = scored(n)
        return b, i, h

    def k_map(n):
        b, h, _ = scored(n)
        return b, 0, nh + h

    def v_map(n):
        b, h, _ = weighed(n)
        return b, 0, 2 * nh + h

    def o_map(n):
        b, h, i = weighed(n)
        return b, i, h

    return pl.pallas_call(
        functools.partial(_diff_attn_kernel, L=L, Lc=Lc, tg=tg, n_tiles=n_tiles, n_total=n_total,
                          lambda_init=lambda_init),
        grid=(n_total + 1,),
        in_specs=[pl.BlockSpec((1, 4, HEAD), lambda n: (l, 0, 0)),
                  pl.BlockSpec((1, TM, DIFF_V), q_map),
                  pl.BlockSpec((1, T, DIFF_V), k_map),
                  pl.BlockSpec((1, T, DIFF_V), v_map),
                  pl.BlockSpec((1, 1, DIFF_V), lambda n: (l, 0, 0))],
        out_specs=pl.BlockSpec((1, TM, DIFF_V), o_map),
        out_shape=jax.ShapeDtypeStruct((B, T, D), BF16),
        scratch_shapes=[pltpu.VMEM((2, T // tg, 2 * TM, tg), F32), pltpu.VMEM((2 * TM, Lc), F32),
                        pltpu.VMEM((2, 2 * TM, LANES), F32), pltpu.VMEM((2 * TM, DIFF_V), BF16),
                        pltpu.VMEM((2 * TM, LANES), F32), pltpu.VMEM((2 * TM, DIFF_V), F32)],
        compiler_params=_params("arbitrary"),
        name="diff_attn",
    )(diff_lambda, p_attn, p_attn, p_attn, subln_g.reshape(-1, 1, DIFF_V))


def _swa_kernel(sink_ref, q_ref, k_ref, v_ref, o_ref, *, L, Lc, l):
    i = pl.program_id(1)
    tq = q_ref.shape[1]
    kvw = SWA_KV * HEAD
    win = tq + 2 * TM
    lane_head = lax.broadcasted_iota(jnp.int32, (tq, kvw), 1) // HEAD
    k_ctx = k_ref[0, L:L + Lc, :]
    v_ctx = v_ref[0, L:L + Lc, :]

    def run(latent):
        if latent:
            start = pl.multiple_of(jnp.clip(i * tq - TM, 0, L - win), TM)
            k_win = k_ref[0, pl.ds(start, win), :]
            v_win = v_ref[0, pl.ds(start, win), :]
            q_pos = i * tq + lax.broadcasted_iota(jnp.int32, (tq, win), 0)
            k_pos = start + lax.broadcasted_iota(jnp.int32, (tq, win), 1)
            band = jnp.abs(k_pos - q_pos) <= WINDOW
        for g in range(SWA_GROUP):
            qg = q_ref[0, :, g * kvw:(g + 1) * kvw]
            acc = jnp.zeros((tq, kvw), F32)
            for kv in range(SWA_KV):
                qe = jnp.where(lane_head == kv, qg, jnp.zeros_like(qg))
                snk = sink_ref[l, kv * SWA_GROUP + g]
                s_c = _dot_nt(qe, k_ctx)
                m = jnp.maximum(jnp.max(s_c, axis=-1, keepdims=True), snk)
                if latent:
                    s_w = jnp.where(band, _dot_nt(qe, k_win), -jnp.inf)
                    m = jnp.maximum(m, jnp.max(s_w, axis=-1, keepdims=True))
                p_c = jnp.exp(s_c - m)
                den = jnp.sum(p_c, axis=-1, keepdims=True) + jnp.exp(snk - m)
                o = _dot(p_c.astype(BF16), v_ctx)
                if latent:
                    p_w = jnp.exp(s_w - m)
                    den = den + jnp.sum(p_w, axis=-1, keepdims=True)
                    o = o + _dot(p_w.astype(BF16), v_win)
                acc = acc + jnp.where(lane_head == kv, o / den, 0.0)
            o_ref[0, :, g * kvw:(g + 1) * kvw] = acc.astype(o_ref.dtype)

    @pl.when(i < L // tq)
    def _():
        run(True)

    @pl.when(i >= L // tq)
    def _():
        run(False)


def _swa(p_attn, sink, L, Lc, l):
    B, T, _ = p_attn.shape
    kvw = SWA_KV * HEAD
    return pl.pallas_call(
        functools.partial(_swa_kernel, L=L, Lc=Lc, l=l),
        grid=(B, T // TM),
        in_specs=[pl.BlockSpec(memory_space=pltpu.SMEM),
                  pl.BlockSpec((1, TM, D), lambda b, i: (b, i, 3)),
                  pl.BlockSpec((1, T, kvw), lambda b, i: (b, 0, 4 * D // kvw)),
                  pl.BlockSpec((1, T, kvw), lambda b, i: (b, 0, 4 * D // kvw + 1))],
        out_specs=pl.BlockSpec((1, TM, D), lambda b, i: (b, i, 0)),
        out_shape=jax.ShapeDtypeStruct((B, T, D), BF16),
        compiler_params=_params("arbitrary", "arbitrary"),
        name="swa",
    )(sink, p_attn, p_attn, p_attn)


def _hy_filter_kernel(emb_ref, w1_ref, b1_ref, w2_ref, b2_ref, fr_ref, w3f_ref, w3b_ref, dl_ref, o_ref, *, L):
    rows = TM
    fr = fr_ref[...]

    def raw(i, total):
        r0 = pl.multiple_of(i * rows, rows)
        e = emb_ref[pl.ds(r0, rows), :]
        z = jnp.sin(fr * (_dot(e, w1_ref[...], HIGHEST) + b1_ref[...]))
        z = jnp.sin(fr * (_dot(z, w2_ref[...], HIGHEST) + b2_ref[...]))
        row = r0 + lax.broadcasted_iota(jnp.int32, (rows, 1), 0)
        h = jnp.where(row < L, _dot(z, w3f_ref[...], HIGHEST), _dot(z, w3b_ref[...], HIGHEST))
        h = h * jnp.exp(-e[:, 0:1] * dl_ref[...])
        o_ref[pl.ds(r0, rows), :] = h
        return total + jnp.sum(jnp.abs(h), axis=0, keepdims=True)

    n_chunks = emb_ref.shape[0] // rows
    norm = lax.fori_loop(0, n_chunks, raw, jnp.zeros((1, o_ref.shape[1]), F32)) + EPS

    def scale(i, _):
        r0 = pl.multiple_of(i * rows, rows)
        row = r0 + lax.broadcasted_iota(jnp.int32, (rows, 1), 0)
        o_ref[pl.ds(r0, rows), :] = jnp.where(row == L, 0.0, o_ref[pl.ds(r0, rows), :] / norm)
        return 0

    lax.fori_loop(0, n_chunks, scale, 0)


def _hy_embedding(L):
    n = jnp.arange(2 * L)
    lag = jnp.where(n < L, n, jnp.where(n == L, 0, 2 * L - n))
    t = jnp.linspace(0.0, 1.0, L, dtype=F32)[lag][:, None]
    w = (2.0 * math.pi * jnp.arange(L, dtype=F32) / L)[lag][:, None]
    f = jnp.linspace(1e-4, HY_BANDS - 1, HY_BANDS, dtype=F32)[None, :]
    emb = jnp.concatenate([t, jnp.cos(f * w), -jnp.sin(f * w)], axis=-1)
    return jnp.pad(emb, ((0, 0), (0, LANES - HY_EMB)))


def _hy_filter(L, w1p, b1, w2, b2, fr, w3, deltas):
    ct = 256
    emb = _hy_embedding(L)
    full = lambda shape: pl.BlockSpec(shape, lambda j: (0, 0))
    return pl.pallas_call(
        functools.partial(_hy_filter_kernel, L=L),
        grid=(D // ct,),
        in_specs=[full((2 * L, LANES)), full((LANES, HY_FF)), full((1, HY_FF)), full((HY_FF, HY_FF)),
                  full((1, HY_FF)), full((1, HY_FF)),
                  pl.BlockSpec((HY_FF, ct), lambda j: (0, j)),
                  pl.BlockSpec((HY_FF, ct), lambda j: (0, D // ct + j)),
                  pl.BlockSpec((1, ct), lambda j: (0, j))],
        out_specs=pl.BlockSpec((2 * L, ct), lambda j: (0, j)),
        out_shape=jax.ShapeDtypeStruct((2 * L, D), F32),
        compiler_params=_params("arbitrary"),
        name="hy_filter",
    )(emb, w1p, b1, w2, b2, fr, w3, w3, deltas)


def _short_conv(x, w, b):
    n = x.shape[0]
    row = lax.broadcasted_iota(jnp.int32, x.shape, 0)
    prev = jnp.where(row == 0, 0.0, pltpu.roll(x, 1, 0))
    nxt = jnp.where(row == n - 1, 0.0, pltpu.roll(x, n - 1, 0))
    return prev * w[0:1] + x * w[1:2] + nxt * w[2:3] + b


def _hy_pre_kernel(x0_ref, x1_ref, v_ref, w0_ref, w1_ref, w2_ref, b0_ref, b1_ref, b2_ref, x0c_ref, u_ref, ub_ref):
    x0c_ref[0] = _short_conv(x0_ref[0], w0_ref[...], b0_ref[...])
    u = _short_conv(x1_ref[0], w1_ref[...], b1_ref[...]) * _short_conv(v_ref[0], w2_ref[...], b2_ref[...])
    u_ref[0] = u
    ub_ref[0] = u.astype(ub_ref.dtype)


def _hy_pre(p_hg, conv_w, conv_b, L):
    B = p_hg.shape[0]
    ct = LANES
    nb = D // ct
    col = lambda k: pl.BlockSpec((1, L, ct), lambda b, j: (b, 0, k * nb + j))
    wsp = lambda k: pl.BlockSpec((3, ct), lambda b, j: (0, k * nb + j))
    bsp = lambda k: pl.BlockSpec((1, ct), lambda b, j: (0, k * nb + j))
    out = pl.BlockSpec((1, L, ct), lambda b, j: (b, 0, j))
    return pl.pallas_call(
        _hy_pre_kernel,
        grid=(B, nb),
        in_specs=[col(0), col(1), col(2), wsp(0), wsp(1), wsp(2), bsp(0), bsp(1), bsp(2)],
        out_specs=[out, out, out],
        out_shape=[jax.ShapeDtypeStruct((B, L, D), F32)] * 2 + [jax.ShapeDtypeStruct((B, L, D), BF16)],
        compiler_params=_params("arbitrary", "arbitrary"),
        name="hy_pre",
    )(p_hg, p_hg, p_hg, conv_w, conv_w, conv_w, conv_b, conv_b, conv_b)


def _lmat_kernel(m_ref, x_ref, o_ref, *, precision):
    x = x_ref[0]
    if precision is None:
        x = x.astype(BF16)
    o_ref[0] = _dot(m_ref[...], x, precision).astype(o_ref.dtype)


def _lmat(mat, x, out_dtype, precision, wc, name):
    Bn, K, W = x.shape
    M = mat.shape[0]
    return pl.pallas_call(
        functools.partial(_lmat_kernel, precision=precision),
        grid=(Bn, W // wc),
        in_specs=[pl.BlockSpec((M, K), lambda b, j: (0, 0)),
                  pl.BlockSpec((1, K, wc), lambda b, j: (b, 0, j))],
        out_specs=pl.BlockSpec((1, M, wc), lambda b, j: (b, 0, j)),
        out_shape=jax.ShapeDtypeStruct((Bn, M, W), out_dtype),
        compiler_params=_params("arbitrary", "arbitrary"),
        name=name,
    )(mat, x)


def _cmul(xr, xi, kr, ki):
    return xr * kr - xi * ki, xr * ki + xi * kr


def _hy_spec_kernel(a_ref, m_ref, o_ref):
    a = jnp.concatenate([a_ref[0, 0], a_ref[1, 0]], axis=0)
    o_ref[0] = _dot(m_ref[0], a, HIGHEST)


def _hy_spectrum(a, m_tab):
    n1 = a.shape[1]
    n2 = DFT_N2
    return pl.pallas_call(
        _hy_spec_kernel,
        grid=(n1,),
        in_specs=[pl.BlockSpec((2, 1, n2, D), lambda k: (0, k, 0, 0)),
                  pl.BlockSpec((1, 2 * n2, 2 * n2), lambda k: (k, 0, 0))],
        out_specs=pl.BlockSpec((1, 2 * n2, D), lambda k: (k, 0, 0)),
        out_shape=jax.ShapeDtypeStruct((n1, 2 * n2, D), F32),
        compiler_params=_params("arbitrary"),
        name="hy_spectrum",
    )(a, m_tab)


def _hy_mid_kernel(a_ref, m_ref, mt_ref, kf_ref, o_ref):
    n2 = DFT_N2
    kr = kf_ref[0, :n2]
    ki = kf_ref[0, n2:]

    def body(b, _):
        a = jnp.concatenate([a_ref[b, 0, 0], a_ref[b, 1, 0]], axis=0)
        x = _dot(m_ref[0], a)
        yr, yi = _cmul(x[:n2], x[n2:], kr, ki)
        y = jnp.concatenate([yr, yi], axis=0).astype(BF16)
        z = _dot(mt_ref[0], y)
        o_ref[b, 0, 0] = z[:n2].astype(o_ref.dtype)
        o_ref[b, 1, 0] = z[n2:].astype(o_ref.dtype)
        return 0

    lax.fori_loop(0, a_ref.shape[0], body, 0)


def _hy_mid(a, m_bf, mt_bf, kf):
    B, _, n1, n2, _ = a.shape
    blk = pl.BlockSpec((B, 2, 1, n2, D), lambda k: (0, 0, k, 0, 0))
    tab = pl.BlockSpec((1, 2 * n2, 2 * n2), lambda k: (k, 0, 0))
    return pl.pallas_call(
        _hy_mid_kernel,
        grid=(n1,),
        in_specs=[blk, tab, tab, pl.BlockSpec((1, 2 * n2, D), lambda k: (k, 0, 0))],
        out_specs=blk,
        out_shape=jax.ShapeDtypeStruct(a.shape, BF16),
        compiler_params=_params("arbitrary"),
        name="hy_mid",
    )(a, m_bf, mt_bf, kf)


def _hy_output(x0c, conv, u, bias):
    return x0c * (conv + u * bias)


def _hy_ctx_kernel(x0_ref, x1_ref, v_ref, w0_ref, w1_ref, w2_ref, b0_ref, b1_ref, b2_ref,
                   f_ref, kf_ref, g_ref, bias_ref, o_ref):
    x0 = _short_conv(x0_ref[0], w0_ref[...], b0_ref[...])
    u = _short_conv(x1_ref[0], w1_ref[...], b1_ref[...]) * _short_conv(v_ref[0], w2_ref[...], b2_ref[...])
    x = _dot(f_ref[...], u, HIGHEST)
    nf = x.shape[0] // 2
    yr, yi = _cmul(x[:nf], x[nf:], kf_ref[:nf], kf_ref[nf:])
    conv = _dot(g_ref[...], jnp.concatenate([yr, yi], axis=0), HIGHEST)
    o_ref[0] = _hy_output(x0, conv, u, bias_ref[...]).astype(o_ref.dtype)


def _hy_ctx(p_hg, conv_w, conv_b, f_c, kf_c, g_c, bias, L, Lc):
    B = p_hg.shape[0]
    ct = 256
    nb = D // ct
    rb = L // Lc
    col = lambda k: pl.BlockSpec((1, Lc, ct), lambda b, j: (b, rb, k * nb + j))
    wsp = lambda k: pl.BlockSpec((3, ct), lambda b, j: (0, k * nb + j))
    bsp = lambda k: pl.BlockSpec((1, ct), lambda b, j: (0, k * nb + j))
    return pl.pallas_call(
        _hy_ctx_kernel,
        grid=(B, nb),
        in_specs=[col(0), col(1), col(2), wsp(0), wsp(1), wsp(2), bsp(0), bsp(1), bsp(2),
                  pl.BlockSpec(f_c.shape, lambda b, j: (0, 0)),
                  pl.BlockSpec((kf_c.shape[0], ct), lambda b, j: (0, j)),
                  pl.BlockSpec(g_c.shape, lambda b, j: (0, 0)),
                  pl.BlockSpec((1, ct), lambda b, j: (0, j))],
        out_specs=pl.BlockSpec((1, Lc, ct), lambda b, j: (b, 0, j)),
        out_shape=jax.ShapeDtypeStruct((B, Lc, D), BF16),
        compiler_params=_params("arbitrary", "arbitrary"),
        name="hy_ctx",
    )(p_hg, p_hg, p_hg, conv_w, conv_w, conv_w, conv_b, conv_b, conv_b, f_c, kf_c, g_c, bias)


def _dft_tables(L, Lc):
    n = 2 * L
    n2 = DFT_N2
    n1 = n // n2
    two_pi = 2.0 * math.pi

    def cs(idx, period):
        ang = (idx % period).astype(F32) * (two_pi / period)
        return jnp.cos(ang), jnp.sin(ang)

    k1 = jnp.arange(n1)[:, None]
    c, s = cs(k1 * jnp.arange(n1)[None, :], n1)
    f1_full = jnp.concatenate([c, -s], axis=0)
    f1_half = f1_full[:, :n1 // 2]
    kk1 = jnp.arange(n1)[:, None, None]
    kk2 = jnp.arange(n2)[None, :, None]
    nn2 = jnp.arange(n2)[None, None, :]
    c, s = cs(nn2 * kk1 + n1 * nn2 * kk2, n)
    ci = -s
    m_tab = jnp.concatenate([jnp.concatenate([c, -ci], axis=2), jnp.concatenate([ci, c], axis=2)], axis=1)
    c, s = cs(jnp.arange(n1 // 2)[:, None] * jnp.arange(n1)[None, :], n1)
    g_mat = jnp.concatenate([c, -s], axis=1) / n
    nc = 2 * Lc
    kc = jnp.arange(nc)[:, None]
    c, s = cs(kc * jnp.arange(nc)[None, :], nc)
    fc_full = jnp.concatenate([c, -s], axis=0)
    c, s = cs(jnp.arange(Lc)[:, None] * jnp.arange(nc)[None, :], nc)
    g_c = jnp.concatenate([c, -s], axis=1) / nc
    return dict(f1_full=f1_full, f1_half=f1_half.astype(BF16), m_tab=m_tab, m_bf=m_tab.astype(BF16),
                mt_bf=jnp.swapaxes(m_tab, 1, 2).astype(BF16), g_mat=g_mat.astype(BF16),
                fc_full=fc_full, fc_half=fc_full[:, :Lc], g_c=g_c)


def _hyena(p_hg, hp, tabs, L, Lc):
    B = p_hg.shape[0]
    n2 = DFT_N2
    n1 = 2 * L // n2
    wc = min(16 * D, n2 * D)
    deltas = jnp.abs(jnp.linspace(math.log(HY_TARGET) / HY_FAST_DECAY, math.log(HY_TARGET) / HY_SLOW_DECAY,
                                  D, dtype=F32))[None, :]
    filt = functools.partial(_hy_filter, w1p=hp['w1p'], b1=hp['b1'], w2=hp['w2'], b2=hp['b2'], fr=hp['fr'],
                             w3=hp['w3'], deltas=deltas)
    k_lat = filt(L)
    a_f = _lmat(tabs['f1_full'], k_lat.reshape(1, n1, n2 * D), F32, HIGHEST, wc, "hy_filt_s1")
    kf = _hy_spectrum(a_f.reshape(2, n1, n2, D), tabs['m_tab'])
    k_ctx = filt(Lc)
    kf_c = _lmat(tabs['fc_full'], k_ctx[None], F32, HIGHEST, D, "hy_filt_ctx")[0]
    x0c, u, u_bf = _hy_pre(p_hg, hp['conv_w'], hp['conv_b'], L)
    a = _lmat(tabs['f1_half'], u_bf.reshape(B, n1 // 2, n2 * D), BF16, None, wc, "hy_s1")
    z = _hy_mid(a.reshape(B, 2, n1, n2, D), tabs['m_bf'], tabs['mt_bf'], kf)
    conv = _lmat(tabs['g_mat'], z.reshape(B, 2 * n1, n2 * D), BF16, None, wc, "hy_s3").reshape(B, L, D)
    y_c = _hy_ctx(p_hg, hp['conv_w'], hp['conv_b'], tabs['fc_half'], kf_c, tabs['g_c'], hp['bias'], L, Lc)
    return conv, x0c, u, y_c


def _merge_kernel(x_ref, mod_ref, yd_ref, cv_ref, x0_ref, u_ref, yc_ref, hb_ref, ys_ref, gd_ref, gh_ref, gs_ref,
                  wb_ref, wo_ref, g2_ref, rw_ref, xo_ref, h2_ref, lg_ref, *, n_lat_tiles):
    latent = pl.program_id(1) < n_lat_tiles
    half = x_ref.shape[1] // 2
    for rows in (slice(0, half), slice(half, 2 * half)):
        y_hy = jnp.where(latent,
                         _hy_output(x0_ref[0, rows], cv_ref[0, rows].astype(F32), u_ref[0, rows], hb_ref[...]),
                         yc_ref[0, rows].astype(F32)).astype(BF16)
        merged = None
        for k, (y, gt_ref) in enumerate(((yd_ref[0, rows], gd_ref), (y_hy, gh_ref), (ys_ref[0, rows], gs_ref))):
            term = jax.nn.sigmoid(gt_ref[0, rows]) * _dot(y, wb_ref[k])
            merged = term if merged is None else merged + term
        y = _dot(merged.astype(BF16), wo_ref[...])
        x = x_ref[0, rows] + mod_ref[0, 2:3, :] * y
        xo_ref[0, rows] = x
        h2 = _rms(x) * g2_ref[...]
        h2 = h2 * (1.0 + mod_ref[0, 4:5, :]) + mod_ref[0, 3:4, :]
        h2_ref[0, rows] = h2.astype(h2_ref.dtype)
        lg_ref[0, :, rows] = _dot_nt(rw_ref[...], h2, HIGHEST)


def _merge(xa, mod, y_diff, hy, hy_bias, y_swa, p_hg, w_branch, w_out, g2, rw_t, n_lat_tiles):
    B, T, _ = xa.shape
    ctx_row = B
    conv, x0c, u, y_hy_c = hy
    tile = lambda c: pl.BlockSpec((1, TM, D), lambda b, i: (b, i, c))
    lat = pl.BlockSpec((1, TM, D), lambda b, i: (b, jnp.minimum(i, n_lat_tiles - 1), 0))
    return pl.pallas_call(
        functools.partial(_merge_kernel, n_lat_tiles=n_lat_tiles),
        grid=(B, T // TM),
        in_specs=[tile(0),
                  pl.BlockSpec((1, 6, D), lambda b, i: (jnp.where(i >= n_lat_tiles, ctx_row, b), 0, 0)),
                  tile(0), lat, lat, lat,
                  pl.BlockSpec((1, y_hy_c.shape[1], D), lambda b, i: (b, 0, 0)),
                  pl.BlockSpec((1, D), lambda b, i: (0, 0)),
                  tile(0), tile(3), tile(4), tile(5),
                  pl.BlockSpec((3, D, D), lambda b, i: (0, 0, 0)),
                  pl.BlockSpec((D, D), lambda b, i: (0, 0)),
                  pl.BlockSpec((1, D), lambda b, i: (0, 0)),
                  pl.BlockSpec((N_EXPERTS, D), lambda b, i: (0, 0))],
        out_specs=[tile(0), tile(0), pl.BlockSpec((1, N_EXPERTS, TM), lambda b, i: (b, 0, i))],
        out_shape=[jax.ShapeDtypeStruct((B, T, D), F32), jax.ShapeDtypeStruct((B, T, D), BF16),
                   jax.ShapeDtypeStruct((B, N_EXPERTS, T), F32)],
        compiler_params=_params("arbitrary", "arbitrary"),
        name="merge",
    )(xa, mod, y_diff, conv, x0c, u, y_hy_c, hy_bias, y_swa, p_hg, p_hg, p_hg, w_branch, w_out, g2, rw_t)


def _router_kernel(lg_ref, slot_ref, gate_ref, *, segments):
    lg = lg_ref[0]
    e = jnp.exp(lg - jnp.max(lg, axis=0, keepdims=True))
    aff = e / jnp.sum(e, axis=0, keepdims=True)
    r = lax.broadcasted_iota(jnp.int32, (LANES, LANES), 0)
    c = lax.broadcasted_iota(jnp.int32, (LANES, LANES), 1)
    before = (r < c).astype(BF16)

    def prefix_count(mask, n):
        run = jnp.zeros((N_EXPERTS, 1), F32)
        outs = []
        for j in range(n // LANES):
            blk = mask[:, j * LANES:(j + 1) * LANES].astype(F32).astype(BF16)
            outs.append(_dot(blk, before) + run)
            run = run + jnp.sum(blk.astype(F32), axis=1, keepdims=True)
        return jnp.concatenate(outs, axis=1)

    for start, n, cap, slot_base in segments:
        a = aff[:, start:start + n]
        bits = pltpu.bitcast(a, jnp.int32)

        def search(_, bounds):
            lo, hi = bounds
            mid = lo + ((hi - lo + 1) >> 1)
            cnt = jnp.sum((bits >= mid).astype(F32), axis=1, keepdims=True)
            ok = cnt >= cap
            return jnp.where(ok, mid, lo), jnp.where(ok, hi, mid - 1)

        lo0 = jnp.zeros((N_EXPERTS, 1), jnp.int32)
        hi0 = jnp.full((N_EXPERTS, 1), 0x7F800000, jnp.int32)
        thr, _ = lax.fori_loop(0, 32, search, (lo0, hi0))
        above = bits > thr
        tie = bits == thr
        need = cap - jnp.sum(above.astype(F32), axis=1, keepdims=True)
        sel = above | (tie & (prefix_count(tie, n) < need))
        pos = prefix_count(sel, n)
        slot_ref[0, :, start:start + n] = jnp.where(sel, pos + slot_base, -1.0)
        gate_ref[0, :, start:start + n] = jnp.where(sel, a, 0.0)


def _router(logits_t, segments):
    B, E, T = logits_t.shape
    blk = pl.BlockSpec((1, E, T), lambda b: (b, 0, 0))
    return pl.pallas_call(
        functools.partial(_router_kernel, segments=segments),
        grid=(B,),
        in_specs=[blk],
        out_specs=[blk, blk],
        out_shape=[jax.ShapeDtypeStruct((B, E, T), F32)] * 2,
        compiler_params=_params("arbitrary"),
        name="router",
    )(logits_t)


def _moe_ffn_kernel(slot_ref, gate_ref, h2_ref, w1_ref, w3_ref, w2_ref, o_ref, *, n_slots, halves):
    T = h2_ref.shape[1]
    tc = T // halves
    slot_id = lax.broadcasted_iota(jnp.int32, (n_slots, tc), 0).astype(F32)
    xg = jnp.zeros((n_slots, D), F32)
    g = jnp.zeros((n_slots, 1), F32)
    for c in range(halves):
        hit = slot_ref[0, 0, :, c * tc:(c + 1) * tc] == slot_id
        xg = xg + _dot(jnp.where(hit, 1.0, 0.0).astype(BF16), h2_ref[0, c * tc:(c + 1) * tc, :])
        g = g + jnp.sum(jnp.where(hit, gate_ref[0, 0, :, c * tc:(c + 1) * tc], 0.0), axis=1, keepdims=True)
    xb = xg.astype(BF16)
    a = _dot(xb, w1_ref[0, 0])
    b = _dot(xb, w3_ref[0, 0])
    hidden = (a * jax.nn.sigmoid(a) * b).astype(BF16)
    o_ref[0, 0] = (_dot(hidden, w2_ref[0, 0]) * g).astype(o_ref.dtype)


def _moe_ffn(slot, gate, h2, w1, w3, w2, l, n_slots):
    B, E, T = slot.shape
    row = pl.BlockSpec((1, 1, 1, T), lambda b, e: (b, e, 0, 0))
    wsp = pl.BlockSpec((1, 1, D, D), lambda b, e: (l, e, 0, 0))
    return pl.pallas_call(
        functools.partial(_moe_ffn_kernel, n_slots=n_slots, halves=2),
        grid=(B, E),
        in_specs=[row, row, pl.BlockSpec((1, T, D), lambda b, e: (b, 0, 0)), wsp, wsp, wsp],
        out_specs=pl.BlockSpec((1, 1, n_slots, D), lambda b, e: (b, e, 0, 0)),
        out_shape=jax.ShapeDtypeStruct((B, E, n_slots, D), BF16),
        compiler_params=_params("arbitrary", "arbitrary"),
        name="moe_ffn",
    )(slot.reshape(B, E, 1, T), gate.reshape(B, E, 1, T), h2, w1, w3, w2)


def _moe_combine_kernel(x_ref, mod_ref, slot_ref, y_ref, o_ref, *, n_slots):
    tm = x_ref.shape[1]
    slot_id = lax.broadcasted_iota(jnp.int32, (tm, n_slots), 1).astype(F32)
    acc = jnp.zeros((tm, D), F32)
    for e in range(N_EXPERTS):
        hit = slot_ref[0, :, e:e + 1] == slot_id
        acc = acc + _dot(jnp.where(hit, 1.0, 0.0).astype(BF16), y_ref[0, e])
    o_ref[0] = x_ref[0] + mod_ref[0, 5:6, :] * acc


def _moe_combine(xa, mod, slot_t, y, n_lat_tiles, n_slots):
    B, T, _ = xa.shape
    ctx_row = B
    tile = pl.BlockSpec((1, TM, D), lambda b, i: (b, i, 0))
    return pl.pallas_call(
        functools.partial(_moe_combine_kernel, n_slots=n_slots),
        grid=(B, T // TM),
        in_specs=[tile,
                  pl.BlockSpec((1, 6, D), lambda b, i: (jnp.where(i >= n_lat_tiles, ctx_row, b), 0, 0)),
                  pl.BlockSpec((1, TM, N_EXPERTS), lambda b, i: (b, i, 0)),
                  pl.BlockSpec((1, N_EXPERTS, n_slots, D), lambda b, i: (b, 0, 0, 0))],
        out_specs=tile,
        out_shape=jax.ShapeDtypeStruct((B, T, D), F32),
        compiler_params=_params("arbitrary", "arbitrary"),
        name="moe_combine",
    )(xa, mod, slot_t, y)


def _final_kernel(x_ref, g_ref, o_ref):
    o_ref[0] = _rms(x_ref[0]) * g_ref[...]


def _final_norm(xa, g, L):
    B = xa.shape[0]
    tile = pl.BlockSpec((1, TM, D), lambda b, i: (b, i, 0))
    return pl.pallas_call(
        _final_kernel,
        grid=(B, L // TM),
        in_specs=[tile, pl.BlockSpec((1, D), lambda b, i: (0, 0))],
        out_specs=tile,
        out_shape=jax.ShapeDtypeStruct((B, L, D), F32),
        compiler_params=_params("arbitrary", "arbitrary"),
        name="final_norm",
    )(xa, g)


def _swa_head_perm():
    return [kv * SWA_GROUP + g for g in range(SWA_GROUP) for kv in range(SWA_KV)]


def _split_w_in(w_in):
    depth = w_in.shape[0]
    o = [0, D, 2 * D, 3 * D, 6 * D, 7 * D, 7 * D + SWA_KV * HEAD, 7 * D + 2 * SWA_KV * HEAD, 10 * D + 2 * SWA_KV * HEAD]
    dq, dk, dv, hy, sq, sk, sv, gt = (w_in[:, :, o[k]:o[k + 1]] for k in range(8))
    scale = HEAD ** -0.5
    sq = sq.reshape(depth, D, SWA_KV * SWA_GROUP, HEAD)[:, :, jnp.array(_swa_head_perm())].reshape(depth, D, D)
    w_attn = jnp.concatenate([dq * (scale * math.log2(math.e)), dk, dv, sq * scale, sk, sv], axis=-1).astype(BF16)
    w_hg = jnp.concatenate([hy, gt], axis=-1).astype(BF16)
    return w_attn, w_hg


def kernel(x, c, ctx, c_ctx, w_mod, b_mod, norm1_g, norm2_g, w_in, diff_lambda, diff_subln_g, hy_conv_w,
           hy_conv_b, hy_ff_w1, hy_ff_b1, hy_ff_w2, hy_ff_b2, hy_ff_w3, hy_sin_freq, hy_bias, swa_sink,
           w_branch, w_out, router_w, moe_w1, moe_w3, moe_w2, final_g):
    B, L, _ = x.shape
    Lc = ctx.shape[1]
    depth = w_mod.shape[0]
    T = L + Lc
    assert Lc == TM and L % (4 * TM) == 0 and x.shape[2] == D
    n_lat_tiles = L // TM
    cap_l = CAPACITY_FACTOR * L // N_EXPERTS
    cap_c = CAPACITY_FACTOR * Lc // N_EXPERTS
    n_slots = cap_l + cap_c
    segments = ((0, L, cap_l, 0), (L, Lc, cap_c, cap_l))

    xa = jnp.concatenate([x, ctx], axis=1)
    mods = _mods(c, c_ctx, w_mod, b_mod)
    cos_t, sin_t = _rope_tables(L, Lc)
    tabs = _dft_tables(L, Lc)
    w_attn, w_hg = _split_w_in(w_in)
    rope_flags = tuple([True] * 16 + [False] * 8 + [True] * 8 + [True] * 2 + [False] * 2)
    perm = jnp.array(_swa_head_perm())
    wb = w_branch.at[:, 2].set(
        w_branch[:, 2].reshape(depth, SWA_KV * SWA_GROUP, HEAD, D)[:, perm].reshape(depth, D, D)).astype(BF16)
    wo = w_out.astype(BF16)
    w1, w3, w2 = moe_w1.astype(BF16), moe_w3.astype(BF16), moe_w2.astype(BF16)
    rw_t = jnp.swapaxes(router_w, 1, 2)
    w1p = jnp.pad(hy_ff_w1, ((0, 0), (0, LANES - HY_EMB), (0, 0)))

    for l in range(depth):
        lambda_init = 0.8 - 0.6 * math.exp(-0.3 * l)
        mod = mods[l]
        p_attn = _norm_proj(xa, mod, norm1_g[l][None], w_attn[l], cos_t, sin_t, rope_flags, BF16,
                            n_lat_tiles, "proj_attn")
        p_hg = _norm_proj(xa, mod, norm1_g[l][None], w_hg[l], cos_t, sin_t, (False,) * (N_HG // LANES), F32,
                          n_lat_tiles, "proj_hg")
        y_diff = _diff_attn(p_attn, diff_lambda, diff_subln_g, L, Lc, lambda_init, l)
        y_swa = _swa(p_attn, swa_sink, L, Lc, l)
        hp = dict(w1p=w1p[l], b1=hy_ff_b1[l][None], w2=hy_ff_w2[l], b2=hy_ff_b2[l][None],
                  fr=hy_sin_freq[l][None], w3=hy_ff_w3[l], conv_w=hy_conv_w[l], conv_b=hy_conv_b[l][None],
                  bias=hy_bias[l][None])
        hy = _hyena(p_hg, hp, tabs, L, Lc)
        xa, h2, logits_t = _merge(xa, mod, y_diff, hy, hp['bias'], y_swa, p_hg, wb[l], wo[l], norm2_g[l][None],
                                  rw_t[l], n_lat_tiles)
        slot, gate = _router(logits_t, segments)
        y_e = _moe_ffn(slot, gate, h2, w1, w3, w2, l, n_slots)
        xa = _moe_combine(xa, mod, jnp.swapaxes(slot, 1, 2), y_e, n_lat_tiles, n_slots)
    return _final_norm(xa, final_g[None], L)
```

```python
import functools
import math

import jax
import jax.numpy as jnp
from jax import lax
from jax.experimental import pallas as pl
from jax.experimental.pallas import tpu as pltpu

F32 = jnp.float32
BF16 = jnp.bfloat16
HIGHEST = lax.Precision.HIGHEST

D = 1024
HEAD = 64
GRID_W = 64
ROPE_BASE = 10000.0
EPS = 1e-6
N_DIFF_HEADS = D // (2 * HEAD)
DIFF_V = 2 * HEAD
SWA_KV = 4
SWA_GROUP = 4
WINDOW = 128
N_EXPERTS = 16
CAPACITY_FACTOR = 2
HY_BANDS = 16
HY_EMB = 1 + 2 * HY_BANDS
HY_FF = 64
HY_FAST_DECAY = 0.3
HY_SLOW_DECAY = 1.5
HY_TARGET = 1e-2

LANES = 128
TM = 256
DFT_N2 = 128
VMEM_LIMIT = 56 * 1024 * 1024

N_ATTN = 3 * D + D + 2 * SWA_KV * HEAD
N_HG = 3 * D + 3 * D


def _params(*sem):
    return pltpu.CompilerParams(dimension_semantics=sem, vmem_limit_bytes=VMEM_LIMIT)


def _dot(a, b, precision=None):
    return jnp.dot(a, b, preferred_element_type=F32, precision=precision)


def _dot_nt(a, b, precision=None):
    return lax.dot_general(a, b, (((1,), (1,)), ((), ())), preferred_element_type=F32, precision=precision)


def _rms(x):
    return x * lax.rsqrt(jnp.mean(x * x, axis=-1, keepdims=True) + EPS)


def _mod_kernel(c_ref, w_ref, b_ref, o_ref):
    c = c_ref[...]
    sc = c * jax.nn.sigmoid(c)
    o_ref[0] = _dot(sc, w_ref[0], HIGHEST) + b_ref[0]


def _mods(c, c_ctx, w_mod, b_mod):
    B = c.shape[0]
    depth = w_mod.shape[0]
    rows = -(-(B + 1) // 8) * 8
    cc = jnp.zeros((rows, D), F32).at[:B].set(c).at[B].set(c_ctx)
    out = pl.pallas_call(
        _mod_kernel,
        grid=(depth, 6),
        in_specs=[pl.BlockSpec((rows, D), lambda l, j: (0, 0)),
                  pl.BlockSpec((1, D, D), lambda l, j: (l, 0, j)),
                  pl.BlockSpec((1, 1, D), lambda l, j: (l, 0, j))],
        out_specs=pl.BlockSpec((1, rows, D), lambda l, j: (l, 0, j)),
        out_shape=jax.ShapeDtypeStruct((depth, rows, 6 * D), F32),
        compiler_params=_params("arbitrary", "arbitrary"),
        name="mods",
    )(cc, w_mod, b_mod.reshape(depth, 1, 6 * D))
    return out.reshape(depth, rows, 6, D)


def _norm_proj_kernel(x_ref, mod_ref, g_ref, w_ref, cos_ref, sin_ref, o_ref, *, rope_flags):
    x = x_ref[0]
    h = _rms(x) * g_ref[...]
    h = h * (1.0 + mod_ref[0, 1:2, :]) + mod_ref[0, 0:1, :]
    hb = h.astype(BF16)
    n_sub = len(rope_flags)
    sub_per_chunk = 4
    if any(rope_flags):
        cos = cos_ref[...]
        sin = sin_ref[...]
        lane = lax.broadcasted_iota(jnp.int32, cos.shape, 1)
        first_half = (lane % (HEAD // 2)) < (HEAD // 4)
    for c0 in range(0, n_sub, sub_per_chunk):
        c1 = min(c0 + sub_per_chunk, n_sub)
        acc = _dot(hb, w_ref[:, c0 * LANES:c1 * LANES])
        for s in range(c0, c1):
            a = acc[:, (s - c0) * LANES:(s - c0 + 1) * LANES]
            if rope_flags[s]:
                partner = jnp.where(first_half, pltpu.roll(a, LANES - HEAD // 4, 1), pltpu.roll(a, HEAD // 4, 1))
                a = a * cos + partner * sin
            o_ref[0, :, s * LANES:(s + 1) * LANES] = a.astype(o_ref.dtype)


def _norm_proj(xa, mod, g, w, cos_t, sin_t, rope_flags, out_dtype, n_lat_tiles, name):
    B, T, _ = xa.shape
    N = w.shape[1]
    ctx_row = B
    return pl.pallas_call(
        functools.partial(_norm_proj_kernel, rope_flags=rope_flags),
        grid=(B, T // TM),
        in_specs=[pl.BlockSpec((1, TM, D), lambda b, i: (b, i, 0)),
                  pl.BlockSpec((1, 6, D), lambda b, i: (jnp.where(i >= n_lat_tiles, ctx_row, b), 0, 0)),
                  pl.BlockSpec((1, D), lambda b, i: (0, 0)),
                  pl.BlockSpec((D, N), lambda b, i: (0, 0)),
                  pl.BlockSpec((TM, LANES), lambda b, i: (i, 0)),
                  pl.BlockSpec((TM, LANES), lambda b, i: (i, 0))],
        out_specs=pl.BlockSpec((1, TM, N), lambda b, i: (b, i, 0)),
        out_shape=jax.ShapeDtypeStruct((B, T, N), out_dtype),
        compiler_params=_params("arbitrary", "arbitrary"),
        name=name,
    )(xa, mod, g, w, cos_t, sin_t)


def _rope_tables(L, Lc):
    t = jnp.arange(L)
    pos = jnp.stack([(t // GRID_W).astype(F32), (t % GRID_W).astype(F32)], axis=1)
    n_freq = HEAD // 4
    inv = ROPE_BASE ** (-jnp.arange(n_freq, dtype=F32) / n_freq)
    ang = pos[:, :, None] * inv
    cos = jnp.cos(ang)
    sin = jnp.sin(ang)
    cos_h = jnp.concatenate([cos, cos], axis=-1).reshape(L, HEAD)
    sin_h = jnp.concatenate([-sin, sin], axis=-1).reshape(L, HEAD)
    cos_t = jnp.concatenate([jnp.tile(cos_h, (1, LANES // HEAD)), jnp.ones((Lc, LANES), F32)], axis=0)
    sin_t = jnp.concatenate([jnp.tile(sin_h, (1, LANES // HEAD)), jnp.zeros((Lc, LANES), F32)], axis=0)
    return cos_t, sin_t


def _lane_block_reduce(x, op):
    out = x[:, :LANES]
    for j in range(1, x.shape[1] // LANES):
        out = op(out, x[:, j * LANES:(j + 1) * LANES])
    return out


def _diff_attn_kernel(lam_ref, q_ref, k_ref, v_ref, g_ref, o_ref, s_lat, s_ctx, m_run, qq_scr, l_scr, acc_scr,
                      *, L, Lc, tg, n_tiles, n_total, lambda_init):
    n = pl.program_id(0)
    tq = q_ref.shape[1]
    n_groups = (L + Lc) // tg
    n_lat = L // tq
    cur = n % 2
    prv = 1 - cur
    score_any = n < n_total
    score_lat = jnp.logical_and(score_any, n % n_tiles < n_lat)
    weigh_any = n >= 1
    weigh_lat = jnp.logical_and(weigh_any, (n + n_tiles - 1) % n_tiles < n_lat)
    rows = (2 * tq, LANES)

    @pl.when(score_any)
    def _():
        q = q_ref[0]
        lane = lax.broadcasted_iota(jnp.int32, q.shape, 1)
        zero = jnp.zeros_like(q)
        qq_scr[...] = jnp.concatenate([jnp.where(lane < HEAD, q, zero), jnp.where(lane >= HEAD, q, zero)], axis=0)

    @pl.when(jnp.logical_and(score_any, jnp.logical_not(score_lat)))
    def _():
        s_c = _dot_nt(qq_scr[...], k_ref[0, L:L + Lc, :])
        s_ctx[...] = s_c
        m_run[cur] = _lane_block_reduce(s_c, jnp.maximum)

    def row_max():
        return jnp.max(m_run[prv], axis=-1, keepdims=True)

    @pl.when(jnp.logical_and(weigh_any, jnp.logical_not(weigh_lat)))
    def _():
        p = jnp.exp2(s_ctx[...] - row_max())
        l_scr[...] = _lane_block_reduce(p, jnp.add)
        acc_scr[...] = _dot(p.astype(BF16), v_ref[0, L:L + Lc, :])

    def scores(gi, mrun):
        start = pl.multiple_of(gi * tg, LANES)
        s = _dot_nt(qq_scr[...], k_ref[0, pl.ds(start, tg), :])
        s_lat[cur, gi] = s
        return jnp.maximum(mrun, _lane_block_reduce(s, jnp.maximum))

    def weighted(gi, lrun, acc, m):
        start = pl.multiple_of(gi * tg, LANES)
        p = jnp.exp2(s_lat[prv, gi] - m)
        return lrun + _lane_block_reduce(p, jnp.add), acc + _dot(p.astype(BF16), v_ref[0, pl.ds(start, tg), :])

    lowest = jnp.full(rows, -jnp.inf, F32)
    nothing = jnp.zeros(rows, F32)

    @pl.when(jnp.logical_and(score_lat, weigh_lat))
    def _():
        m = row_max()

        def both(gi, carry):
            mrun, lrun, acc = carry
            lrun, acc = weighted(gi, lrun, acc, m)
            return scores(gi, mrun), lrun, acc

        m_run[cur], l_scr[...], acc_scr[...] = lax.fori_loop(0, n_groups, both, (lowest, nothing, nothing))

    @pl.when(jnp.logical_and(score_lat, jnp.logical_not(weigh_lat)))
    def _():
        m_run[cur] = lax.fori_loop(0, n_groups, scores, lowest)

    @pl.when(jnp.logical_and(jnp.logical_not(score_lat), weigh_lat))
    def _():
        m = row_max()
        l_scr[...], acc_scr[...] = lax.fori_loop(0, n_groups, lambda gi, c: weighted(gi, c[0], c[1], m),
                                                  (nothing, nothing))

    @pl.when(weigh_any)
    def _():
        lp = lam_ref[0]
        lam = (jnp.exp(jnp.sum(lp[0:1] * lp[1:2], axis=-1, keepdims=True))
               - jnp.exp(jnp.sum(lp[2:3] * lp[3:4], axis=-1, keepdims=True)) + lambda_init)
        o = acc_scr[...] / jnp.sum(l_scr[...], axis=-1, keepdims=True)
        o = _rms(o[:tq] - lam * o[tq:])
        o_ref[0] = (o * g_ref[0] * (1.0 - lambda_init)).astype(o_ref.dtype)


def _diff_attn(p_attn, diff_lambda, subln_g, L, Lc, lambda_init, l):
    B, T, _ = p_attn.shape
    tg = T // 2
    nh = N_DIFF_HEADS
    n_tiles = T // TM
    n_total = B * nh * n_tiles

    def tile(t):
        return t // (nh * n_tiles), (t // n_tiles) % nh, t % n_tiles

    def scored(n):
        return tile(jnp.minimum(n, n_total - 1))

    def weighed(n):
        return tile(jnp.maximum(n - 1, 0))

    def q_map(n):
        b, h, i = scored(n)
        return b, i, h

    def k_map(n):
        b, h, _ = scored(n)
        return b, 0, nh + h

    def v_map(n):
        b, h, _ = weighed(n)
        return b, 0, 2 * nh + h

    def o_map(n):
        b, h, i = weighed(n)
        return b, i, h

    return pl.pallas_call(
        functools.partial(_diff_attn_kernel, L=L, Lc=Lc, tg=tg, n_tiles=n_tiles, n_total=n_total,
                          lambda_init=lambda_init),
        grid=(n_total + 1,),
        in_specs=[pl.BlockSpec((1, 4, HEAD), lambda n: (l, 0, 0)),
                  pl.BlockSpec((1, TM, DIFF_V), q_map),
                  pl.BlockSpec((1, T, DIFF_V), k_map),
                  pl.BlockSpec((1, T, DIFF_V), v_map),
                  pl.BlockSpec((1, 1, DIFF_V), lambda n: (l, 0, 0))],
        out_specs=pl.BlockSpec((1, TM, DIFF_V), o_map),
        out_shape=jax.ShapeDtypeStruct((B, T, D), BF16),
        scratch_shapes=[pltpu.VMEM((2, T // tg, 2 * TM, tg), F32), pltpu.VMEM((2 * TM, Lc), F32),
                        pltpu.VMEM((2, 2 * TM, LANES), F32), pltpu.VMEM((2 * TM, DIFF_V), BF16),
                        pltpu.VMEM((2 * TM, LANES), F32), pltpu.VMEM((2 * TM, DIFF_V), F32)],
        compiler_params=_params("arbitrary"),
        name="diff_attn",
    )(diff_lambda, p_attn, p_attn, p_attn, subln_g.reshape(-1, 1, DIFF_V))


def _swa_kernel(sink_ref, q_ref, k_ref, v_ref, o_ref, *, L, Lc, l):
    i = pl.program_id(1)
    tq = q_ref.shape[1]
    kvw = SWA_KV * HEAD
    win = tq + 2 * TM
    lane_head = lax.broadcasted_iota(jnp.int32, (tq, kvw), 1) // HEAD
    k_ctx = k_ref[0, L:L + Lc, :]
    v_ctx = v_ref[0, L:L + Lc, :]

    def run(latent):
        if latent:
            start = pl.multiple_of(jnp.clip(i * tq - TM, 0, L - win), TM)
            k_win = k_ref[0, pl.ds(start, win), :]
            v_win = v_ref[0, pl.ds(start, win), :]
            q_pos = i * tq + lax.broadcasted_iota(jnp.int32, (tq, win), 0)
            k_pos = start + lax.broadcasted_iota(jnp.int32, (tq, win), 1)
            band = jnp.abs(k_pos - q_pos) <= WINDOW
        for g in range(SWA_GROUP):
            qg = q_ref[0, :, g * kvw:(g + 1) * kvw]
            acc = jnp.zeros((tq, kvw), F32)
            for kv in range(SWA_KV):
                qe = jnp.where(lane_head == kv, qg, jnp.zeros_like(qg))
                snk = sink_ref[l, kv * SWA_GROUP + g]
                s_c = _dot_nt(qe, k_ctx)
                m = jnp.maximum(jnp.max(s_c, axis=-1, keepdims=True), snk)
                if latent:
                    s_w = jnp.where(band, _dot_nt(qe, k_win), -jnp.inf)
                    m = jnp.maximum(m, jnp.max(s_w, axis=-1, keepdims=True))
                p_c = jnp.exp(s_c - m)
                den = jnp.sum(p_c, axis=-1, keepdims=True) + jnp.exp(snk - m)
                o = _dot(p_c.astype(BF16), v_ctx)
                if latent:
                    p_w = jnp.exp(s_w - m)
                    den = den + jnp.sum(p_w, axis=-1, keepdims=True)
                    o = o + _dot(p_w.astype(BF16), v_win)
                acc = acc + jnp.where(lane_head == kv, o / den, 0.0)
            o_ref[0, :, g * kvw:(g + 1) * kvw] = acc.astype(o_ref.dtype)

    @pl.when(i < L // tq)
    def _():
        run(True)

    @pl.when(i >= L // tq)
    def _():
        run(False)


def _swa(p_attn, sink, L, Lc, l):
    B, T, _ = p_attn.shape
    kvw = SWA_KV * HEAD
    return pl.pallas_call(
        functools.partial(_swa_kernel, L=L, Lc=Lc, l=l),
        grid=(B, T // TM),
        in_specs=[pl.BlockSpec(memory_space=pltpu.SMEM),
                  pl.BlockSpec((1, TM, D), lambda b, i: (b, i, 3)),
                  pl.BlockSpec((1, T, kvw), lambda b, i: (b, 0, 4 * D // kvw)),
                  pl.BlockSpec((1, T, kvw), lambda b, i: (b, 0, 4 * D // kvw + 1))],
        out_specs=pl.BlockSpec((1, TM, D), lambda b, i: (b, i, 0)),
        out_shape=jax.ShapeDtypeStruct((B, T, D), BF16),
        compiler_params=_params("arbitrary", "arbitrary"),
        name="swa",
    )(sink, p_attn, p_attn, p_attn)


def _hy_filter_kernel(emb_ref, w1_ref, b1_ref, w2_ref, b2_ref, fr_ref, w3f_ref, w3b_ref, dl_ref, o_ref, *, L):
    rows = TM
    fr = fr_ref[...]

    def raw(i, total):
        r0 = pl.multiple_of(i * rows, rows)
        e = emb_ref[pl.ds(r0, rows), :]
        z = jnp.sin(fr * (_dot(e, w1_ref[...], HIGHEST) + b1_ref[...]))
        z = jnp.sin(fr * (_dot(z, w2_ref[...], HIGHEST) + b2_ref[...]))
        row = r0 + lax.broadcasted_iota(jnp.int32, (rows, 1), 0)
        h = jnp.where(row < L, _dot(z, w3f_ref[...], HIGHEST), _dot(z, w3b_ref[...], HIGHEST))
        h = h * jnp.exp(-e[:, 0:1] * dl_ref[...])
        o_ref[pl.ds(r0, rows), :] = h
        return total + jnp.sum(jnp.abs(h), axis=0, keepdims=True)

    n_chunks = emb_ref.shape[0] // rows
    norm = lax.fori_loop(0, n_chunks, raw, jnp.zeros((1, o_ref.shape[1]), F32)) + EPS

    def scale(i, _):
        r0 = pl.multiple_of(i * rows, rows)
        row = r0 + lax.broadcasted_iota(jnp.int32, (rows, 1), 0)
        o_ref[pl.ds(r0, rows), :] = jnp.where(row == L, 0.0, o_ref[pl.ds(r0, rows), :] / norm)
        return 0

    lax.fori_loop(0, n_chunks, scale, 0)


def _hy_embedding(L):
    n = jnp.arange(2 * L)
    lag = jnp.where(n < L, n, jnp.where(n == L, 0, 2 * L - n))
    t = jnp.linspace(0.0, 1.0, L, dtype=F32)[lag][:, None]
    w = (2.0 * math.pi * jnp.arange(L, dtype=F32) / L)[lag][:, None]
    f = jnp.linspace(1e-4, HY_BANDS - 1, HY_BANDS, dtype=F32)[None, :]
    emb = jnp.concatenate([t, jnp.cos(f * w), -jnp.sin(f * w)], axis=-1)
    return jnp.pad(emb, ((0, 0), (0, LANES - HY_EMB)))


def _hy_filter(L, w1p, b1, w2, b2, fr, w3, deltas):
    ct = 256
    emb = _hy_embedding(L)
    full = lambda shape: pl.BlockSpec(shape, lambda j: (0, 0))
    return pl.pallas_call(
        functools.partial(_hy_filter_kernel, L=L),
        grid=(D // ct,),
        in_specs=[full((2 * L, LANES)), full((LANES, HY_FF)), full((1, HY_FF)), full((HY_FF, HY_FF)),
                  full((1, HY_FF)), full((1, HY_FF)),
                  pl.BlockSpec((HY_FF, ct), lambda j: (0, j)),
                  pl.BlockSpec((HY_FF, ct), lambda j: (0, D // ct + j)),
                  pl.BlockSpec((1, ct), lambda j: (0, j))],
        out_specs=pl.BlockSpec((2 * L, ct), lambda j: (0, j)),
        out_shape=jax.ShapeDtypeStruct((2 * L, D), F32),
        compiler_params=_params("arbitrary"),
        name="hy_filter",
    )(emb, w1p, b1, w2, b2, fr, w3, w3, deltas)


def _short_conv(x, w, b):
    n = x.shape[0]
    row = lax.broadcasted_iota(jnp.int32, x.shape, 0)
    prev = jnp.where(row == 0, 0.0, pltpu.roll(x, 1, 0))
    nxt = jnp.where(row == n - 1, 0.0, pltpu.roll(x, n - 1, 0))
    return prev * w[0:1] + x * w[1:2] + nxt * w[2:3] + b


def _hy_pre_kernel(x0_ref, x1_ref, v_ref, w0_ref, w1_ref, w2_ref, b0_ref, b1_ref, b2_ref, x0c_ref, u_ref, ub_ref):
    x0c_ref[0] = _short_conv(x0_ref[0], w0_ref[...], b0_ref[...])
    u = _short_conv(x1_ref[0], w1_ref[...], b1_ref[...]) * _short_conv(v_ref[0], w2_ref[...], b2_ref[...])
    u_ref[0] = u
    ub_ref[0] = u.astype(ub_ref.dtype)


def _hy_pre(p_hg, conv_w, conv_b, L):
    B = p_hg.shape[0]
    ct = LANES
    nb = D // ct
    col = lambda k: pl.BlockSpec((1, L, ct), lambda b, j: (b, 0, k * nb + j))
    wsp = lambda k: pl.BlockSpec((3, ct), lambda b, j: (0, k * nb + j))
    bsp = lambda k: pl.BlockSpec((1, ct), lambda b, j: (0, k * nb + j))
    out = pl.BlockSpec((1, L, ct), lambda b, j: (b, 0, j))
    return pl.pallas_call(
        _hy_pre_kernel,
        grid=(B, nb),
        in_specs=[col(0), col(1), col(2), wsp(0), wsp(1), wsp(2), bsp(0), bsp(1), bsp(2)],
        out_specs=[out, out, out],
        out_shape=[jax.ShapeDtypeStruct((B, L, D), F32)] * 2 + [jax.ShapeDtypeStruct((B, L, D), BF16)],
        compiler_params=_params("arbitrary", "arbitrary"),
        name="hy_pre",
    )(p_hg, p_hg, p_hg, conv_w, conv_w, conv_w, conv_b, conv_b, conv_b)


def _lmat_kernel(m_ref, x_ref, o_ref, *, precision):
    x = x_ref[0]
    if precision is None:
        x = x.astype(BF16)
    o_ref[0] = _dot(m_ref[...], x, precision).astype(o_ref.dtype)


def _lmat(mat, x, out_dtype, precision, wc, name):
    Bn, K, W = x.shape
    M = mat.shape[0]
    return pl.pallas_call(
        functools.partial(_lmat_kernel, precision=precision),
        grid=(Bn, W // wc),
        in_specs=[pl.BlockSpec((M, K), lambda b, j: (0, 0)),
                  pl.BlockSpec((1, K, wc), lambda b, j: (b, 0, j))],
        out_specs=pl.BlockSpec((1, M, wc), lambda b, j: (b, 0, j)),
        out_shape=jax.ShapeDtypeStruct((Bn, M, W), out_dtype),
        compiler_params=_params("arbitrary", "arbitrary"),
        name=name,
    )(mat, x)


def _cmul(xr, xi, kr, ki):
    return xr * kr - xi * ki, xr * ki + xi * kr


def _hy_spec_kernel(a_ref, m_ref, o_ref):
    a = jnp.concatenate([a_ref[0, 0], a_ref[1, 0]], axis=0)
    o_ref[0] = _dot(m_ref[0], a, HIGHEST)


def _hy_spectrum(a, m_tab):
    n1 = a.shape[1]
    n2 = DFT_N2
    return pl.pallas_call(
        _hy_spec_kernel,
        grid=(n1,),
        in_specs=[pl.BlockSpec((2, 1, n2, D), lambda k: (0, k, 0, 0)),
                  pl.BlockSpec((1, 2 * n2, 2 * n2), lambda k: (k, 0, 0))],
        out_specs=pl.BlockSpec((1, 2 * n2, D), lambda k: (k, 0, 0)),
        out_shape=jax.ShapeDtypeStruct((n1, 2 * n2, D), F32),
        compiler_params=_params("arbitrary"),
        name="hy_spectrum",
    )(a, m_tab)


def _hy_mid_kernel(a_ref, m_ref, mt_ref, kf_ref, o_ref):
    n2 = DFT_N2
    kr = kf_ref[0, :n2]
    ki = kf_ref[0, n2:]

    def body(b, _):
        a = jnp.concatenate([a_ref[b, 0, 0], a_ref[b, 1, 0]], axis=0)
        x = _dot(m_ref[0], a)
        yr, yi = _cmul(x[:n2], x[n2:], kr, ki)
        y = jnp.concatenate([yr, yi], axis=0).astype(BF16)
        z = _dot(mt_ref[0], y)
        o_ref[b, 0, 0] = z[:n2].astype(o_ref.dtype)
        o_ref[b, 1, 0] = z[n2:].astype(o_ref.dtype)
        return 0

    lax.fori_loop(0, a_ref.shape[0], body, 0)


def _hy_mid(a, m_bf, mt_bf, kf):
    B, _, n1, n2, _ = a.shape
    blk = pl.BlockSpec((B, 2, 1, n2, D), lambda k: (0, 0, k, 0, 0))
    tab = pl.BlockSpec((1, 2 * n2, 2 * n2), lambda k: (k, 0, 0))
    return pl.pallas_call(
        _hy_mid_kernel,
        grid=(n1,),
        in_specs=[blk, tab, tab, pl.BlockSpec((1, 2 * n2, D), lambda k: (k, 0, 0))],
        out_specs=blk,
        out_shape=jax.ShapeDtypeStruct(a.shape, BF16),
        compiler_params=_params("arbitrary"),
        name="hy_mid",
    )(a, m_bf, mt_bf, kf)


def _hy_output(x0c, conv, u, bias):
    return x0c * (conv + u * bias)


def _hy_ctx_kernel(x0_ref, x1_ref, v_ref, w0_ref, w1_ref, w2_ref, b0_ref, b1_ref, b2_ref,
                   f_ref, kf_ref, g_ref, bias_ref, o_ref):
    x0 = _short_conv(x0_ref[0], w0_ref[...], b0_ref[...])
    u = _short_conv(x1_ref[0], w1_ref[...], b1_ref[...]) * _short_conv(v_ref[0], w2_ref[...], b2_ref[...])
    x = _dot(f_ref[...], u, HIGHEST)
    nf = x.shape[0] // 2
    yr, yi = _cmul(x[:nf], x[nf:], kf_ref[:nf], kf_ref[nf:])
    conv = _dot(g_ref[...], jnp.concatenate([yr, yi], axis=0), HIGHEST)
    o_ref[0] = _hy_output(x0, conv, u, bias_ref[...]).astype(o_ref.dtype)


def _hy_ctx(p_hg, conv_w, conv_b, f_c, kf_c, g_c, bias, L, Lc):
    B = p_hg.shape[0]
    ct = 256
    nb = D // ct
    rb = L // Lc
    col = lambda k: pl.BlockSpec((1, Lc, ct), lambda b, j: (b, rb, k * nb + j))
    wsp = lambda k: pl.BlockSpec((3, ct), lambda b, j: (0, k * nb + j))
    bsp = lambda k: pl.BlockSpec((1, ct), lambda b, j: (0, k * nb + j))
    return pl.pallas_call(
        _hy_ctx_kernel,
        grid=(B, nb),
        in_specs=[col(0), col(1), col(2), wsp(0), wsp(1), wsp(2), bsp(0), bsp(1), bsp(2),
                  pl.BlockSpec(f_c.shape, lambda b, j: (0, 0)),
                  pl.BlockSpec((kf_c.shape[0], ct), lambda b, j: (0, j)),
                  pl.BlockSpec(g_c.shape, lambda b, j: (0, 0)),
                  pl.BlockSpec((1, ct), lambda b, j: (0, j))],
        out_specs=pl.BlockSpec((1, Lc, ct), lambda b, j: (b, 0, j)),
        out_shape=jax.ShapeDtypeStruct((B, Lc, D), BF16),
        compiler_params=_params("arbitrary", "arbitrary"),
        name="hy_ctx",
    )(p_hg, p_hg, p_hg, conv_w, conv_w, conv_w, conv_b, conv_b, conv_b, f_c, kf_c, g_c, bias)


def _dft_tables(L, Lc):
    n = 2 * L
    n2 = DFT_N2
    n1 = n // n2
    two_pi = 2.0 * math.pi

    def cs(idx, period):
        ang = (idx % period).astype(F32) * (two_pi / period)
        return jnp.cos(ang), jnp.sin(ang)

    k1 = jnp.arange(n1)[:, None]
    c, s = cs(k1 * jnp.arange(n1)[None, :], n1)
    f1_full = jnp.concatenate([c, -s], axis=0)
    f1_half = f1_full[:, :n1 // 2]
    kk1 = jnp.arange(n1)[:, None, None]
    kk2 = jnp.arange(n2)[None, :, None]
    nn2 = jnp.arange(n2)[None, None, :]
    c, s = cs(nn2 * kk1 + n1 * nn2 * kk2, n)
    ci = -s
    m_tab = jnp.concatenate([jnp.concatenate([c, -ci], axis=2), jnp.concatenate([ci, c], axis=2)], axis=1)
    c, s = cs(jnp.arange(n1 // 2)[:, None] * jnp.arange(n1)[None, :], n1)
    g_mat = jnp.concatenate([c, -s], axis=1) / n
    nc = 2 * Lc
    kc = jnp.arange(nc)[:, None]
    c, s = cs(kc * jnp.arange(nc)[None, :], nc)
    fc_full = jnp.concatenate([c, -s], axis=0)
    c, s = cs(jnp.arange(Lc)[:, None] * jnp.arange(nc)[None, :], nc)
    g_c = jnp.concatenate([c, -s], axis=1) / nc
    return dict(f1_full=f1_full, f1_half=f1_half.astype(BF16), m_tab=m_tab, m_bf=m_tab.astype(BF16),
                mt_bf=jnp.swapaxes(m_tab, 1, 2).astype(BF16), g_mat=g_mat.astype(BF16),
                fc_full=fc_full, fc_half=fc_full[:, :Lc], g_c=g_c)


def _hyena(p_hg, hp, tabs, L, Lc):
    B = p_hg.shape[0]
    n2 = DFT_N2
    n1 = 2 * L // n2
    wc = min(16 * D, n2 * D)
    deltas = jnp.abs(jnp.linspace(math.log(HY_TARGET) / HY_FAST_DECAY, math.log(HY_TARGET) / HY_SLOW_DECAY,
                                  D, dtype=F32))[None, :]
    filt = functools.partial(_hy_filter, w1p=hp['w1p'], b1=hp['b1'], w2=hp['w2'], b2=hp['b2'], fr=hp['fr'],
                             w3=hp['w3'], deltas=deltas)
    k_lat = filt(L)
    a_f = _lmat(tabs['f1_full'], k_lat.reshape(1, n1, n2 * D), F32, HIGHEST, wc, "hy_filt_s1")
    kf = _hy_spectrum(a_f.reshape(2, n1, n2, D), tabs['m_tab'])
    k_ctx = filt(Lc)
    kf_c = _lmat(tabs['fc_full'], k_ctx[None], F32, HIGHEST, D, "hy_filt_ctx")[0]
    x0c, u, u_bf = _hy_pre(p_hg, hp['conv_w'], hp['conv_b'], L)
    a = _lmat(tabs['f1_half'], u_bf.reshape(B, n1 // 2, n2 * D), BF16, None, wc, "hy_s1")
    z = _hy_mid(a.reshape(B, 2, n1, n2, D), tabs['m_bf'], tabs['mt_bf'], kf)
    conv = _lmat(tabs['g_mat'], z.reshape(B, 2 * n1, n2 * D), BF16, None, wc, "hy_s3").reshape(B, L, D)
    y_c = _hy_ctx(p_hg, hp['conv_w'], hp['conv_b'], tabs['fc_half'], kf_c, tabs['g_c'], hp['bias'], L, Lc)
    return conv, x0c, u, y_c


def _merge_kernel(x_ref, mod_ref, yd_ref, cv_ref, x0_ref, u_ref, yc_ref, hb_ref, ys_ref, gd_ref, gh_ref, gs_ref,
                  wb_ref, wo_ref, g2_ref, rw_ref, xo_ref, h2_ref, lg_ref, *, n_lat_tiles):
    latent = pl.program_id(1) < n_lat_tiles
    half = x_ref.shape[1] // 2
    for rows in (slice(0, half), slice(half, 2 * half)):
        y_hy = jnp.where(latent,
                         _hy_output(x0_ref[0, rows], cv_ref[0, rows].astype(F32), u_ref[0, rows], hb_ref[...]),
                         yc_ref[0, rows].astype(F32)).astype(BF16)
        merged = None
        for k, (y, gt_ref) in enumerate(((yd_ref[0, rows], gd_ref), (y_hy, gh_ref), (ys_ref[0, rows], gs_ref))):
            term = jax.nn.sigmoid(gt_ref[0, rows]) * _dot(y, wb_ref[k])
            merged = term if merged is None else merged + term
        y = _dot(merged.astype(BF16), wo_ref[...])
        x = x_ref[0, rows] + mod_ref[0, 2:3, :] * y
        xo_ref[0, rows] = x
        h2 = _rms(x) * g2_ref[...]
        h2 = h2 * (1.0 + mod_ref[0, 4:5, :]) + mod_ref[0, 3:4, :]
        h2_ref[0, rows] = h2.astype(h2_ref.dtype)
        lg_ref[0, :, rows] = _dot_nt(rw_ref[...], h2, HIGHEST)


def _merge(xa, mod, y_diff, hy, hy_bias, y_swa, p_hg, w_branch, w_out, g2, rw_t, n_lat_tiles):
    B, T, _ = xa.shape
    ctx_row = B
    conv, x0c, u, y_hy_c = hy
    tile = lambda c: pl.BlockSpec((1, TM, D), lambda b, i: (b, i, c))
    lat = pl.BlockSpec((1, TM, D), lambda b, i: (b, jnp.minimum(i, n_lat_tiles - 1), 0))
    return pl.pallas_call(
        functools.partial(_merge_kernel, n_lat_tiles=n_lat_tiles),
        grid=(B, T // TM),
        in_specs=[tile(0),
                  pl.BlockSpec((1, 6, D), lambda b, i: (jnp.where(i >= n_lat_tiles, ctx_row, b), 0, 0)),
                  tile(0), lat, lat, lat,
                  pl.BlockSpec((1, y_hy_c.shape[1], D), lambda b, i: (b, 0, 0)),
                  pl.BlockSpec((1, D), lambda b, i: (0, 0)),
                  tile(0), tile(3), tile(4), tile(5),
                  pl.BlockSpec((3, D, D), lambda b, i: (0, 0, 0)),
                  pl.BlockSpec((D, D), lambda b, i: (0, 0)),
                  pl.BlockSpec((1, D), lambda b, i: (0, 0)),
                  pl.BlockSpec((N_EXPERTS, D), lambda b, i: (0, 0))],
        out_specs=[tile(0), tile(0), pl.BlockSpec((1, N_EXPERTS, TM), lambda b, i: (b, 0, i))],
        out_shape=[jax.ShapeDtypeStruct((B, T, D), F32), jax.ShapeDtypeStruct((B, T, D), BF16),
                   jax.ShapeDtypeStruct((B, N_EXPERTS, T), F32)],
        compiler_params=_params("arbitrary", "arbitrary"),
        name="merge",
    )(xa, mod, y_diff, conv, x0c, u, y_hy_c, hy_bias, y_swa, p_hg, p_hg, p_hg, w_branch, w_out, g2, rw_t)


def _router_kernel(lg_ref, slot_ref, gate_ref, first_ref, *, segments):
    lg = lg_ref[0]
    e = jnp.exp(lg - jnp.max(lg, axis=0, keepdims=True))
    aff = e / jnp.sum(e, axis=0, keepdims=True)
    r = lax.broadcasted_iota(jnp.int32, (LANES, LANES), 0)
    c = lax.broadcasted_iota(jnp.int32, (LANES, LANES), 1)
    before = (r < c).astype(BF16)

    def prefix_count(mask, n):
        run = jnp.zeros((N_EXPERTS, 1), F32)
        outs = []
        for j in range(n // LANES):
            blk = mask[:, j * LANES:(j + 1) * LANES].astype(F32).astype(BF16)
            outs.append(_dot(blk, before) + run)
            run = run + jnp.sum(blk.astype(F32), axis=1, keepdims=True)
        return jnp.concatenate(outs, axis=1)

    for start, n, cap, slot_base in segments:
        a = aff[:, start:start + n]
        bits = pltpu.bitcast(a, jnp.int32)

        def search(_, bounds):
            lo, hi = bounds
            mid = lo + ((hi - lo + 1) >> 1)
            cnt = jnp.sum((bits >= mid).astype(F32), axis=1, keepdims=True)
            ok = cnt >= cap
            return jnp.where(ok, mid, lo), jnp.where(ok, hi, mid - 1)

        lo0 = jnp.zeros((N_EXPERTS, 1), jnp.int32)
        hi0 = jnp.full((N_EXPERTS, 1), 0x7F800000, jnp.int32)
        thr, _ = lax.fori_loop(0, 32, search, (lo0, hi0))
        above = bits > thr
        tie = bits == thr
        need = cap - jnp.sum(above.astype(F32), axis=1, keepdims=True)
        sel = above | (tie & (prefix_count(tie, n) < need))
        pos = prefix_count(sel, n)
        slot_ref[0, :, start:start + n] = jnp.where(sel, pos + slot_base, -1.0)
        gate_ref[0, :, start:start + n] = jnp.where(sel, a, 0.0)
        if start == 0:
            t_idx = lax.broadcasted_iota(jnp.int32, (n, LANES), 0)
            tile_start = lax.broadcasted_iota(jnp.int32, (n, LANES), 1) * TM
            before_tile = jnp.where(t_idx < tile_start, 1.0, 0.0).astype(BF16)
            first_ref[0] = _dot(jnp.where(sel, 1.0, 0.0).astype(BF16), before_tile)


def _router(logits_t, segments):
    B, E, T = logits_t.shape
    blk = pl.BlockSpec((1, E, T), lambda b: (b, 0, 0))
    return pl.pallas_call(
        functools.partial(_router_kernel, segments=segments),
        grid=(B,),
        in_specs=[blk],
        out_specs=[blk, blk, pl.BlockSpec((1, E, LANES), lambda b: (b, 0, 0))],
        out_shape=[jax.ShapeDtypeStruct((B, E, T), F32)] * 2 + [jax.ShapeDtypeStruct((B, E, LANES), F32)],
        compiler_params=_params("arbitrary"),
        name="router",
    )(logits_t)


def _slot_window(first, align, rows, n_slots):
    return pl.multiple_of(jnp.minimum(first // align * align, n_slots - rows), align)


def _moe_ffn_kernel(first_ref, slot_ref, gate_ref, h2_ref, w1_ref, w3_ref, w2_ref, o_ref, xg_scr, g_scr,
                    *, n_slots, n_tiles):
    base = (pl.program_id(0) * N_EXPERTS + pl.program_id(1)) * n_tiles
    rows = min(TM + 8, n_slots)
    row_id = lax.broadcasted_iota(jnp.int32, (rows, TM), 0)
    xg_scr[...] = jnp.zeros_like(xg_scr)
    g_scr[...] = jnp.zeros_like(g_scr)
    for c in range(n_tiles):
        tok = slice(c * TM, (c + 1) * TM)
        w0 = _slot_window(first_ref[base + c], 8, rows, n_slots)
        hit = slot_ref[0, 0, :, tok] == (row_id + w0).astype(F32)
        win = pl.ds(w0, rows)
        xg_scr[win, :] += _dot(jnp.where(hit, 1.0, 0.0).astype(BF16), h2_ref[0, tok, :])
        g_scr[win, :] += jnp.sum(jnp.where(hit, gate_ref[0, 0, :, tok], 0.0), axis=1, keepdims=True)
    xb = xg_scr[...].astype(BF16)
    a = _dot(xb, w1_ref[0, 0])
    b = _dot(xb, w3_ref[0, 0])
    hidden = (a * jax.nn.sigmoid(a) * b).astype(BF16)
    o_ref[0, 0] = (_dot(hidden, w2_ref[0, 0]) * g_scr[...]).astype(o_ref.dtype)


def _moe_ffn(first, slot, gate, h2, w1, w3, w2, l, n_slots):
    B, E, T = slot.shape
    row = pl.BlockSpec((1, 1, 1, T), lambda b, e: (b, e, 0, 0))
    wsp = pl.BlockSpec((1, 1, D, D), lambda b, e: (l, e, 0, 0))
    return pl.pallas_call(
        functools.partial(_moe_ffn_kernel, n_slots=n_slots, n_tiles=T // TM),
        grid=(B, E),
        in_specs=[pl.BlockSpec(memory_space=pltpu.SMEM), row, row,
                  pl.BlockSpec((1, T, D), lambda b, e: (b, 0, 0)), wsp, wsp, wsp],
        out_specs=pl.BlockSpec((1, 1, n_slots, D), lambda b, e: (b, e, 0, 0)),
        out_shape=jax.ShapeDtypeStruct((B, E, n_slots, D), BF16),
        scratch_shapes=[pltpu.VMEM((n_slots, D), F32), pltpu.VMEM((n_slots, 1), F32)],
        compiler_params=_params("arbitrary", "arbitrary"),
        name="moe_ffn",
    )(first, slot.reshape(B, E, 1, T), gate.reshape(B, E, 1, T), h2, w1, w3, w2)


def _moe_combine_kernel(first_ref, x_ref, mod_ref, slot_ref, y_ref, o_ref, *, n_slots, n_tiles):
    tm = x_ref.shape[1]
    rows = min(tm + 16, n_slots)
    col_id = lax.broadcasted_iota(jnp.int32, (tm, rows), 1)
    acc = jnp.zeros((tm, D), F32)
    for e in range(N_EXPERTS):
        w0 = _slot_window(first_ref[(pl.program_id(0) * N_EXPERTS + e) * n_tiles + pl.program_id(1)], 16, rows,
                          n_slots)
        hit = slot_ref[0, :, e:e + 1] == (col_id + w0).astype(F32)
        acc = acc + _dot(jnp.where(hit, 1.0, 0.0).astype(BF16), y_ref[0, e, pl.ds(w0, rows), :])
    o_ref[0] = x_ref[0] + mod_ref[0, 5:6, :] * acc


def _moe_combine(first, xa, mod, slot_t, y, n_lat_tiles, n_slots):
    B, T, _ = xa.shape
    ctx_row = B
    tile = pl.BlockSpec((1, TM, D), lambda b, i: (b, i, 0))
    return pl.pallas_call(
        functools.partial(_moe_combine_kernel, n_slots=n_slots, n_tiles=T // TM),
        grid=(B, T // TM),
        in_specs=[pl.BlockSpec(memory_space=pltpu.SMEM), tile,
                  pl.BlockSpec((1, 6, D), lambda b, i: (jnp.where(i >= n_lat_tiles, ctx_row, b), 0, 0)),
                  pl.BlockSpec((1, TM, N_EXPERTS), lambda b, i: (b, i, 0)),
                  pl.BlockSpec((1, N_EXPERTS, n_slots, D), lambda b, i: (b, 0, 0, 0))],
        out_specs=tile,
        out_shape=jax.ShapeDtypeStruct((B, T, D), F32),
        compiler_params=_params("arbitrary", "arbitrary"),
        name="moe_combine",
    )(first, xa, mod, slot_t, y)


def _final_kernel(x_ref, g_ref, o_ref):
    o_ref[0] = _rms(x_ref[0]) * g_ref[...]


def _final_norm(xa, g, L):
    B = xa.shape[0]
    tile = pl.BlockSpec((1, TM, D), lambda b, i: (b, i, 0))
    return pl.pallas_call(
        _final_kernel,
        grid=(B, L // TM),
        in_specs=[tile, pl.BlockSpec((1, D), lambda b, i: (0, 0))],
        out_specs=tile,
        out_shape=jax.ShapeDtypeStruct((B, L, D), F32),
        compiler_params=_params("arbitrary", "arbitrary"),
        name="final_norm",
    )(xa, g)


def _swa_head_perm():
    return [kv * SWA_GROUP + g for g in range(SWA_GROUP) for kv in range(SWA_KV)]


def _split_w_in(w_in):
    depth = w_in.shape[0]
    o = [0, D, 2 * D, 3 * D, 6 * D, 7 * D, 7 * D + SWA_KV * HEAD, 7 * D + 2 * SWA_KV * HEAD, 10 * D + 2 * SWA_KV * HEAD]
    dq, dk, dv, hy, sq, sk, sv, gt = (w_in[:, :, o[k]:o[k + 1]] for k in range(8))
    scale = HEAD ** -0.5
    sq = sq.reshape(depth, D, SWA_KV * SWA_GROUP, HEAD)[:, :, jnp.array(_swa_head_perm())].reshape(depth, D, D)
    w_attn = jnp.concatenate([dq * (scale * math.log2(math.e)), dk, dv, sq * scale, sk, sv], axis=-1).astype(BF16)
    w_hg = jnp.concatenate([hy, gt], axis=-1).astype(BF16)
    return w_attn, w_hg


def kernel(x, c, ctx, c_ctx, w_mod, b_mod, norm1_g, norm2_g, w_in, diff_lambda, diff_subln_g, hy_conv_w,
           hy_conv_b, hy_ff_w1, hy_ff_b1, hy_ff_w2, hy_ff_b2, hy_ff_w3, hy_sin_freq, hy_bias, swa_sink,
           w_branch, w_out, router_w, moe_w1, moe_w3, moe_w2, final_g):
    B, L, _ = x.shape
    Lc = ctx.shape[1]
    depth = w_mod.shape[0]
    T = L + Lc
    assert Lc == TM and L % (4 * TM) == 0 and x.shape[2] == D
    n_lat_tiles = L // TM
    cap_l = CAPACITY_FACTOR * L // N_EXPERTS
    cap_c = CAPACITY_FACTOR * Lc // N_EXPERTS
    n_slots = cap_l + cap_c
    segments = ((0, L, cap_l, 0), (L, Lc, cap_c, cap_l))

    xa = jnp.concatenate([x, ctx], axis=1)
    mods = _mods(c, c_ctx, w_mod, b_mod)
    cos_t, sin_t = _rope_tables(L, Lc)
    tabs = _dft_tables(L, Lc)
    w_attn, w_hg = _split_w_in(w_in)
    rope_flags = tuple([True] * 16 + [False] * 8 + [True] * 8 + [True] * 2 + [False] * 2)
    perm = jnp.array(_swa_head_perm())
    wb = w_branch.at[:, 2].set(
        w_branch[:, 2].reshape(depth, SWA_KV * SWA_GROUP, HEAD, D)[:, perm].reshape(depth, D, D)).astype(BF16)
    wo = w_out.astype(BF16)
    w1, w3, w2 = moe_w1.astype(BF16), moe_w3.astype(BF16), moe_w2.astype(BF16)
    rw_t = jnp.swapaxes(router_w, 1, 2)
    w1p = jnp.pad(hy_ff_w1, ((0, 0), (0, LANES - HY_EMB), (0, 0)))

    for l in range(depth):
        lambda_init = 0.8 - 0.6 * math.exp(-0.3 * l)
        mod = mods[l]
        p_attn = _norm_proj(xa, mod, norm1_g[l][None], w_attn[l], cos_t, sin_t, rope_flags, BF16,
                            n_lat_tiles, "proj_attn")
        p_hg = _norm_proj(xa, mod, norm1_g[l][None], w_hg[l], cos_t, sin_t, (False,) * (N_HG // LANES), F32,
                          n_lat_tiles, "proj_hg")
        y_diff = _diff_attn(p_attn, diff_lambda, diff_subln_g, L, Lc, lambda_init, l)
        y_swa = _swa(p_attn, swa_sink, L, Lc, l)
        hp = dict(w1p=w1p[l], b1=hy_ff_b1[l][None], w2=hy_ff_w2[l], b2=hy_ff_b2[l][None],
                  fr=hy_sin_freq[l][None], w3=hy_ff_w3[l], conv_w=hy_conv_w[l], conv_b=hy_conv_b[l][None],
                  bias=hy_bias[l][None])
        hy = _hyena(p_hg, hp, tabs, L, Lc)
        xa, h2, logits_t = _merge(xa, mod, y_diff, hy, hp['bias'], y_swa, p_hg, wb[l], wo[l], norm2_g[l][None],
                                  rw_t[l], n_lat_tiles)
        slot, gate, first = _router(logits_t, segments)
        first = jnp.concatenate([first[:, :, :n_lat_tiles], jnp.full((B, N_EXPERTS, 1), cap_l, F32)], axis=2)
        first = first.astype(jnp.int32).reshape(-1)
        y_e = _moe_ffn(first, slot, gate, h2, w1, w3, w2, l, n_slots)
        xa = _moe_combine(first, xa, mod, jnp.swapaxes(slot, 1, 2), y_e, n_lat_tiles, n_slots)
    return _final_norm(xa, final_g[None], L)
```

```python
import functools
import math

import jax
import jax.numpy as jnp
from jax import lax
from jax.experimental import pallas as pl
from jax.experimental.pallas import tpu as pltpu

F32 = jnp.float32
BF16 = jnp.bfloat16
HIGHEST = lax.Precision.HIGHEST

D = 1024
HEAD = 64
GRID_W = 64
ROPE_BASE = 10000.0
EPS = 1e-6
N_DIFF_HEADS = D // (2 * HEAD)
DIFF_V = 2 * HEAD
SWA_KV = 4
SWA_GROUP = 4
WINDOW = 128
N_EXPERTS = 16
CAPACITY_FACTOR = 2
HY_BANDS = 16
HY_EMB = 1 + 2 * HY_BANDS
HY_FF = 64
HY_FAST_DECAY = 0.3
HY_SLOW_DECAY = 1.5
HY_TARGET = 1e-2

LANES = 128
TM = 256
DFT_N2 = 128
VMEM_LIMIT = 56 * 1024 * 1024

N_ATTN = 3 * D + D + 2 * SWA_KV * HEAD
N_HG = 3 * D + 3 * D


def _params(*sem):
    return pltpu.CompilerParams(dimension_semantics=sem, vmem_limit_bytes=VMEM_LIMIT)


def _dot(a, b, precision=None):
    return jnp.dot(a, b, preferred_element_type=F32, precision=precision)


def _dot_nt(a, b, precision=None):
    return lax.dot_general(a, b, (((1,), (1,)), ((), ())), preferred_element_type=F32, precision=precision)


def _rms(x):
    return x * lax.rsqrt(jnp.mean(x * x, axis=-1, keepdims=True) + EPS)


def _mod_kernel(c_ref, w_ref, b_ref, o_ref):
    c = c_ref[...]
    sc = c * jax.nn.sigmoid(c)
    o_ref[0] = _dot(sc, w_ref[0], HIGHEST) + b_ref[0]


def _mods(c, c_ctx, w_mod, b_mod):
    B = c.shape[0]
    depth = w_mod.shape[0]
    rows = -(-(B + 1) // 8) * 8
    cc = jnp.zeros((rows, D), F32).at[:B].set(c).at[B].set(c_ctx)
    out = pl.pallas_call(
        _mod_kernel,
        grid=(depth, 6),
        in_specs=[pl.BlockSpec((rows, D), lambda l, j: (0, 0)),
                  pl.BlockSpec((1, D, D), lambda l, j: (l, 0, j)),
                  pl.BlockSpec((1, 1, D), lambda l, j: (l, 0, j))],
        out_specs=pl.BlockSpec((1, rows, D), lambda l, j: (l, 0, j)),
        out_shape=jax.ShapeDtypeStruct((depth, rows, 6 * D), F32),
        compiler_params=_params("arbitrary", "arbitrary"),
        name="mods",
    )(cc, w_mod, b_mod.reshape(depth, 1, 6 * D))
    return out.reshape(depth, rows, 6, D)


def _norm_proj_kernel(x_ref, mod_ref, g_ref, w_ref, cos_ref, sin_ref, o_ref, *, rope_flags):
    x = x_ref[0]
    h = _rms(x) * g_ref[...]
    h = h * (1.0 + mod_ref[0, 1:2, :]) + mod_ref[0, 0:1, :]
    hb = h.astype(BF16)
    n_sub = len(rope_flags)
    sub_per_chunk = 4
    if any(rope_flags):
        cos = cos_ref[...]
        sin = sin_ref[...]
        lane = lax.broadcasted_iota(jnp.int32, cos.shape, 1)
        first_half = (lane % (HEAD // 2)) < (HEAD // 4)
    for c0 in range(0, n_sub, sub_per_chunk):
        c1 = min(c0 + sub_per_chunk, n_sub)
        acc = _dot(hb, w_ref[:, c0 * LANES:c1 * LANES])
        for s in range(c0, c1):
            a = acc[:, (s - c0) * LANES:(s - c0 + 1) * LANES]
            if rope_flags[s]:
                partner = jnp.where(first_half, pltpu.roll(a, LANES - HEAD // 4, 1), pltpu.roll(a, HEAD // 4, 1))
                a = a * cos + partner * sin
            o_ref[0, :, s * LANES:(s + 1) * LANES] = a.astype(o_ref.dtype)


def _norm_proj(xa, mod, g, w, cos_t, sin_t, rope_flags, out_dtype, n_lat_tiles, name):
    B, T, _ = xa.shape
    N = w.shape[1]
    ctx_row = B
    return pl.pallas_call(
        functools.partial(_norm_proj_kernel, rope_flags=rope_flags),
        grid=(B, T // TM),
        in_specs=[pl.BlockSpec((1, TM, D), lambda b, i: (b, i, 0)),
                  pl.BlockSpec((1, 6, D), lambda b, i: (jnp.where(i >= n_lat_tiles, ctx_row, b), 0, 0)),
                  pl.BlockSpec((1, D), lambda b, i: (0, 0)),
                  pl.BlockSpec((D, N), lambda b, i: (0, 0)),
                  pl.BlockSpec((TM, LANES), lambda b, i: (i, 0)),
                  pl.BlockSpec((TM, LANES), lambda b, i: (i, 0))],
        out_specs=pl.BlockSpec((1, TM, N), lambda b, i: (b, i, 0)),
        out_shape=jax.ShapeDtypeStruct((B, T, N), out_dtype),
        compiler_params=_params("arbitrary", "arbitrary"),
        name=name,
    )(xa, mod, g, w, cos_t, sin_t)


def _rope_tables(L, Lc):
    t = jnp.arange(L)
    pos = jnp.stack([(t // GRID_W).astype(F32), (t % GRID_W).astype(F32)], axis=1)
    n_freq = HEAD // 4
    inv = ROPE_BASE ** (-jnp.arange(n_freq, dtype=F32) / n_freq)
    ang = pos[:, :, None] * inv
    cos = jnp.cos(ang)
    sin = jnp.sin(ang)
    cos_h = jnp.concatenate([cos, cos], axis=-1).reshape(L, HEAD)
    sin_h = jnp.concatenate([-sin, sin], axis=-1).reshape(L, HEAD)
    cos_t = jnp.concatenate([jnp.tile(cos_h, (1, LANES // HEAD)), jnp.ones((Lc, LANES), F32)], axis=0)
    sin_t = jnp.concatenate([jnp.tile(sin_h, (1, LANES // HEAD)), jnp.zeros((Lc, LANES), F32)], axis=0)
    return cos_t, sin_t


def _lane_block_reduce(x, op):
    out = x[:, :LANES]
    for j in range(1, x.shape[1] // LANES):
        out = op(out, x[:, j * LANES:(j + 1) * LANES])
    return out


def _diff_attn_kernel(lam_ref, q_ref, k_ref, v_ref, g_ref, o_ref, s_lat, s_ctx, m_run, qq_scr, l_scr, acc_scr,
                      *, L, Lc, tg, n_tiles, n_total, lambda_init):
    n = pl.program_id(0)
    tq = q_ref.shape[1]
    n_groups = (L + Lc) // tg
    n_lat = L // tq
    cur = n % 2
    prv = 1 - cur
    score_any = n < n_total
    score_lat = jnp.logical_and(score_any, n % n_tiles < n_lat)
    weigh_any = n >= 1
    weigh_lat = jnp.logical_and(weigh_any, (n + n_tiles - 1) % n_tiles < n_lat)
    rows = (2 * tq, LANES)

    @pl.when(score_any)
    def _():
        q = q_ref[0]
        lane = lax.broadcasted_iota(jnp.int32, q.shape, 1)
        zero = jnp.zeros_like(q)
        qq_scr[...] = jnp.concatenate([jnp.where(lane < HEAD, q, zero), jnp.where(lane >= HEAD, q, zero)], axis=0)

    @pl.when(jnp.logical_and(score_any, jnp.logical_not(score_lat)))
    def _():
        s_c = _dot_nt(qq_scr[...], k_ref[0, L:L + Lc, :])
        s_ctx[...] = s_c
        m_run[cur] = _lane_block_reduce(s_c, jnp.maximum)

    def row_max():
        return jnp.max(m_run[prv], axis=-1, keepdims=True)

    @pl.when(jnp.logical_and(weigh_any, jnp.logical_not(weigh_lat)))
    def _():
        p = jnp.exp2(s_ctx[...] - row_max())
        l_scr[...] = _lane_block_reduce(p, jnp.add)
        acc_scr[...] = _dot(p.astype(BF16), v_ref[0, L:L + Lc, :])

    def scores(gi, mrun):
        start = pl.multiple_of(gi * tg, LANES)
        s = _dot_nt(qq_scr[...], k_ref[0, pl.ds(start, tg), :])
        s_lat[cur, gi] = s
        return jnp.maximum(mrun, _lane_block_reduce(s, jnp.maximum))

    def weighted(gi, lrun, acc, m):
        start = pl.multiple_of(gi * tg, LANES)
        p = jnp.exp2(s_lat[prv, gi] - m)
        return lrun + _lane_block_reduce(p, jnp.add), acc + _dot(p.astype(BF16), v_ref[0, pl.ds(start, tg), :])

    lowest = jnp.full(rows, -jnp.inf, F32)
    nothing = jnp.zeros(rows, F32)

    @pl.when(jnp.logical_and(score_lat, weigh_lat))
    def _():
        m = row_max()

        def both(gi, carry):
            mrun, lrun, acc = carry
            lrun, acc = weighted(gi, lrun, acc, m)
            return scores(gi, mrun), lrun, acc

        m_run[cur], l_scr[...], acc_scr[...] = lax.fori_loop(0, n_groups, both, (lowest, nothing, nothing))

    @pl.when(jnp.logical_and(score_lat, jnp.logical_not(weigh_lat)))
    def _():
        m_run[cur] = lax.fori_loop(0, n_groups, scores, lowest)

    @pl.when(jnp.logical_and(jnp.logical_not(score_lat), weigh_lat))
    def _():
        m = row_max()
        l_scr[...], acc_scr[...] = lax.fori_loop(0, n_groups, lambda gi, c: weighted(gi, c[0], c[1], m),
                                                  (nothing, nothing))

    @pl.when(weigh_any)
    def _():
        lp = lam_ref[0]
        lam = (jnp.exp(jnp.sum(lp[0:1] * lp[1:2], axis=-1, keepdims=True))
               - jnp.exp(jnp.sum(lp[2:3] * lp[3:4], axis=-1, keepdims=True)) + lambda_init)
        o = acc_scr[...] / jnp.sum(l_scr[...], axis=-1, keepdims=True)
        o = _rms(o[:tq] - lam * o[tq:])
        o_ref[0] = (o * g_ref[0] * (1.0 - lambda_init)).astype(o_ref.dtype)


def _diff_attn(p_attn, diff_lambda, subln_g, L, Lc, lambda_init, l):
    B, T, _ = p_attn.shape
    tg = T // 2
    nh = N_DIFF_HEADS
    n_tiles = T // TM
    n_total = B * nh * n_tiles

    def tile(t):
        return t // (nh * n_tiles), (t // n_tiles) % nh, t % n_tiles

    def scored(n):
        return tile(jnp.minimum(n, n_total - 1))

    def weighed(n):
        return tile(jnp.maximum(n - 1, 0))

    def q_map(n):
        b, h, i = scored(n)
        return b, i, h

    def k_map(n):
        b, h, _ = scored(n)
        return b, 0, nh + h

    def v_map(n):
        b, h, _ = weighed(n)
        return b, 0, 2 * nh + h

    def o_map(n):
        b, h, i = weighed(n)
        return b, i, h

    return pl.pallas_call(
        functools.partial(_diff_attn_kernel, L=L, Lc=Lc, tg=tg, n_tiles=n_tiles, n_total=n_total,
                          lambda_init=lambda_init),
        grid=(n_total + 1,),
        in_specs=[pl.BlockSpec((1, 4, HEAD), lambda n: (l, 0, 0)),
                  pl.BlockSpec((1, TM, DIFF_V), q_map),
                  pl.BlockSpec((1, T, DIFF_V), k_map),
                  pl.BlockSpec((1, T, DIFF_V), v_map),
                  pl.BlockSpec((1, 1, DIFF_V), lambda n: (l, 0, 0))],
        out_specs=pl.BlockSpec((1, TM, DIFF_V), o_map),
        out_shape=jax.ShapeDtypeStruct((B, T, D), BF16),
        scratch_shapes=[pltpu.VMEM((2, T // tg, 2 * TM, tg), F32), pltpu.VMEM((2 * TM, Lc), F32),
                        pltpu.VMEM((2, 2 * TM, LANES), F32), pltpu.VMEM((2 * TM, DIFF_V), BF16),
                        pltpu.VMEM((2 * TM, LANES), F32), pltpu.VMEM((2 * TM, DIFF_V), F32)],
        compiler_params=_params("arbitrary"),
        name="diff_attn",
    )(diff_lambda, p_attn, p_attn, p_attn, subln_g.reshape(-1, 1, DIFF_V))


def _swa_kernel(sink_ref, q_ref, k_ref, v_ref, o_ref, *, L, Lc, l):
    i = pl.program_id(1)
    tq = q_ref.shape[1]
    kvw = SWA_KV * HEAD
    win = tq + 2 * WINDOW
    lane_head = lax.broadcasted_iota(jnp.int32, (tq, kvw), 1) // HEAD
    k_ctx = k_ref[0, L:L + Lc, :]
    v_ctx = v_ref[0, L:L + Lc, :]

    def run(latent):
        if latent:
            start = pl.multiple_of(jnp.clip(i * tq - WINDOW, 0, L - win), WINDOW)
            k_win = k_ref[0, pl.ds(start, win), :]
            v_win = v_ref[0, pl.ds(start, win), :]
            q_pos = i * tq + lax.broadcasted_iota(jnp.int32, (tq, win), 0)
            k_pos = start + lax.broadcasted_iota(jnp.int32, (tq, win), 1)
            band_bias = jnp.where(jnp.abs(k_pos - q_pos) <= WINDOW, 0.0, -jnp.inf)
        for g in range(SWA_GROUP):
            qg = q_ref[0, :, g * kvw:(g + 1) * kvw]
            acc = jnp.zeros((tq, kvw), F32)
            for kv in range(SWA_KV):
                qe = jnp.where(lane_head == kv, qg, jnp.zeros_like(qg))
                snk = sink_ref[l, kv * SWA_GROUP + g] * math.log2(math.e)
                s_c = _dot_nt(qe, k_ctx)
                m = jnp.maximum(jnp.max(s_c, axis=-1, keepdims=True), snk)
                if latent:
                    s_w = _dot_nt(qe, k_win) + band_bias
                    m = jnp.maximum(m, jnp.max(s_w, axis=-1, keepdims=True))
                p_c = jnp.exp2(s_c - m)
                den = jnp.sum(p_c, axis=-1, keepdims=True) + jnp.exp2(snk - m)
                o = _dot(p_c.astype(BF16), v_ctx)
                if latent:
                    p_w = jnp.exp2(s_w - m)
                    den = den + jnp.sum(p_w, axis=-1, keepdims=True)
                    o = o + _dot(p_w.astype(BF16), v_win)
                acc = acc + jnp.where(lane_head == kv, o / den, 0.0)
            o_ref[0, :, g * kvw:(g + 1) * kvw] = acc.astype(o_ref.dtype)

    @pl.when(i < L // tq)
    def _():
        run(True)

    @pl.when(i >= L // tq)
    def _():
        run(False)


def _swa(p_attn, sink, L, Lc, l):
    B, T, _ = p_attn.shape
    kvw = SWA_KV * HEAD
    return pl.pallas_call(
        functools.partial(_swa_kernel, L=L, Lc=Lc, l=l),
        grid=(B, T // TM),
        in_specs=[pl.BlockSpec(memory_space=pltpu.SMEM),
                  pl.BlockSpec((1, TM, D), lambda b, i: (b, i, 3)),
                  pl.BlockSpec((1, T, kvw), lambda b, i: (b, 0, 4 * D // kvw)),
                  pl.BlockSpec((1, T, kvw), lambda b, i: (b, 0, 4 * D // kvw + 1))],
        out_specs=pl.BlockSpec((1, TM, D), lambda b, i: (b, i, 0)),
        out_shape=jax.ShapeDtypeStruct((B, T, D), BF16),
        compiler_params=_params("arbitrary", "arbitrary"),
        name="swa",
    )(sink, p_attn, p_attn, p_attn)


def _hy_filter_kernel(emb_ref, w1_ref, b1_ref, w2_ref, b2_ref, fr_ref, w3f_ref, w3b_ref, dl_ref, o_ref, *, L):
    rows = TM
    fr = fr_ref[...]

    def raw(i, total):
        r0 = pl.multiple_of(i * rows, rows)
        e = emb_ref[pl.ds(r0, rows), :]
        z = jnp.sin(fr * (_dot(e, w1_ref[...], HIGHEST) + b1_ref[...]))
        z = jnp.sin(fr * (_dot(z, w2_ref[...], HIGHEST) + b2_ref[...]))
        row = r0 + lax.broadcasted_iota(jnp.int32, (rows, 1), 0)
        h = jnp.where(row < L, _dot(z, w3f_ref[...], HIGHEST), _dot(z, w3b_ref[...], HIGHEST))
        h = h * jnp.exp(-e[:, 0:1] * dl_ref[...])
        o_ref[pl.ds(r0, rows), :] = h
        return total + jnp.sum(jnp.abs(h), axis=0, keepdims=True)

    n_chunks = emb_ref.shape[0] // rows
    norm = lax.fori_loop(0, n_chunks, raw, jnp.zeros((1, o_ref.shape[1]), F32)) + EPS

    def scale(i, _):
        r0 = pl.multiple_of(i * rows, rows)
        row = r0 + lax.broadcasted_iota(jnp.int32, (rows, 1), 0)
        o_ref[pl.ds(r0, rows), :] = jnp.where(row == L, 0.0, o_ref[pl.ds(r0, rows), :] / norm)
        return 0

    lax.fori_loop(0, n_chunks, scale, 0)


def _hy_embedding(L):
    n = jnp.arange(2 * L)
    lag = jnp.where(n < L, n, jnp.where(n == L, 0, 2 * L - n))
    t = jnp.linspace(0.0, 1.0, L, dtype=F32)[lag][:, None]
    w = (2.0 * math.pi * jnp.arange(L, dtype=F32) / L)[lag][:, None]
    f = jnp.linspace(1e-4, HY_BANDS - 1, HY_BANDS, dtype=F32)[None, :]
    emb = jnp.concatenate([t, jnp.cos(f * w), -jnp.sin(f * w)], axis=-1)
    return jnp.pad(emb, ((0, 0), (0, LANES - HY_EMB)))


def _hy_filter(L, w1p, b1, w2, b2, fr, w3, deltas):
    ct = 256
    emb = _hy_embedding(L)
    full = lambda shape: pl.BlockSpec(shape, lambda j: (0, 0))
    return pl.pallas_call(
        functools.partial(_hy_filter_kernel, L=L),
        grid=(D // ct,),
        in_specs=[full((2 * L, LANES)), full((LANES, HY_FF)), full((1, HY_FF)), full((HY_FF, HY_FF)),
                  full((1, HY_FF)), full((1, HY_FF)),
                  pl.BlockSpec((HY_FF, ct), lambda j: (0, j)),
                  pl.BlockSpec((HY_FF, ct), lambda j: (0, D // ct + j)),
                  pl.BlockSpec((1, ct), lambda j: (0, j))],
        out_specs=pl.BlockSpec((2 * L, ct), lambda j: (0, j)),
        out_shape=jax.ShapeDtypeStruct((2 * L, D), F32),
        compiler_params=_params("arbitrary"),
        name="hy_filter",
    )(emb, w1p, b1, w2, b2, fr, w3, w3, deltas)


def _short_conv(x, w, b):
    n = x.shape[0]
    row = lax.broadcasted_iota(jnp.int32, x.shape, 0)
    prev = jnp.where(row == 0, 0.0, pltpu.roll(x, 1, 0))
    nxt = jnp.where(row == n - 1, 0.0, pltpu.roll(x, n - 1, 0))
    return prev * w[0:1] + x * w[1:2] + nxt * w[2:3] + b


def _hy_pre_kernel(x0_ref, x1_ref, v_ref, w0_ref, w1_ref, w2_ref, b0_ref, b1_ref, b2_ref, x0c_ref, u_ref, ub_ref):
    x0c_ref[0] = _short_conv(x0_ref[0], w0_ref[...], b0_ref[...])
    u = _short_conv(x1_ref[0], w1_ref[...], b1_ref[...]) * _short_conv(v_ref[0], w2_ref[...], b2_ref[...])
    u_ref[0] = u
    ub_ref[0] = u.astype(ub_ref.dtype)


def _hy_pre(p_hg, conv_w, conv_b, L):
    B = p_hg.shape[0]
    ct = LANES
    nb = D // ct
    col = lambda k: pl.BlockSpec((1, L, ct), lambda b, j: (b, 0, k * nb + j))
    wsp = lambda k: pl.BlockSpec((3, ct), lambda b, j: (0, k * nb + j))
    bsp = lambda k: pl.BlockSpec((1, ct), lambda b, j: (0, k * nb + j))
    out = pl.BlockSpec((1, L, ct), lambda b, j: (b, 0, j))
    return pl.pallas_call(
        _hy_pre_kernel,
        grid=(B, nb),
        in_specs=[col(0), col(1), col(2), wsp(0), wsp(1), wsp(2), bsp(0), bsp(1), bsp(2)],
        out_specs=[out, out, out],
        out_shape=[jax.ShapeDtypeStruct((B, L, D), F32)] * 2 + [jax.ShapeDtypeStruct((B, L, D), BF16)],
        compiler_params=_params("arbitrary", "arbitrary"),
        name="hy_pre",
    )(p_hg, p_hg, p_hg, conv_w, conv_w, conv_w, conv_b, conv_b, conv_b)


def _lmat_kernel(m_ref, x_ref, o_ref, *, precision):
    x = x_ref[0]
    if precision is None:
        x = x.astype(BF16)
    o_ref[0] = _dot(m_ref[...], x, precision).astype(o_ref.dtype)


def _lmat(mat, x, out_dtype, precision, wc, name):
    Bn, K, W = x.shape
    M = mat.shape[0]
    return pl.pallas_call(
        functools.partial(_lmat_kernel, precision=precision),
        grid=(Bn, W // wc),
        in_specs=[pl.BlockSpec((M, K), lambda b, j: (0, 0)),
                  pl.BlockSpec((1, K, wc), lambda b, j: (b, 0, j))],
        out_specs=pl.BlockSpec((1, M, wc), lambda b, j: (b, 0, j)),
        out_shape=jax.ShapeDtypeStruct((Bn, M, W), out_dtype),
        compiler_params=_params("arbitrary", "arbitrary"),
        name=name,
    )(mat, x)


def _cmul(xr, xi, kr, ki):
    return xr * kr - xi * ki, xr * ki + xi * kr


def _hy_spec_kernel(a_ref, m_ref, o_ref):
    a = jnp.concatenate([a_ref[0, 0], a_ref[1, 0]], axis=0)
    o_ref[0] = _dot(m_ref[0], a, HIGHEST)


def _hy_spectrum(a, m_tab):
    n1 = a.shape[1]
    n2 = DFT_N2
    return pl.pallas_call(
        _hy_spec_kernel,
        grid=(n1,),
        in_specs=[pl.BlockSpec((2, 1, n2, D), lambda k: (0, k, 0, 0)),
                  pl.BlockSpec((1, 2 * n2, 2 * n2), lambda k: (k, 0, 0))],
        out_specs=pl.BlockSpec((1, 2 * n2, D), lambda k: (k, 0, 0)),
        out_shape=jax.ShapeDtypeStruct((n1, 2 * n2, D), F32),
        compiler_params=_params("arbitrary"),
        name="hy_spectrum",
    )(a, m_tab)


def _hy_mid_kernel(a_ref, m_ref, mt_ref, kf_ref, o_ref):
    n2 = DFT_N2
    kr = kf_ref[0, :n2]
    ki = kf_ref[0, n2:]

    def body(b, _):
        a = jnp.concatenate([a_ref[b, 0, 0], a_ref[b, 1, 0]], axis=0)
        x = _dot(m_ref[0], a)
        yr, yi = _cmul(x[:n2], x[n2:], kr, ki)
        y = jnp.concatenate([yr, yi], axis=0).astype(BF16)
        z = _dot(mt_ref[0], y)
        o_ref[b, 0, 0] = z[:n2].astype(o_ref.dtype)
        o_ref[b, 1, 0] = z[n2:].astype(o_ref.dtype)
        return 0

    lax.fori_loop(0, a_ref.shape[0], body, 0)


def _hy_mid(a, m_bf, mt_bf, kf):
    B, _, n1, n2, _ = a.shape
    blk = pl.BlockSpec((B, 2, 1, n2, D), lambda k: (0, 0, k, 0, 0))
    tab = pl.BlockSpec((1, 2 * n2, 2 * n2), lambda k: (k, 0, 0))
    return pl.pallas_call(
        _hy_mid_kernel,
        grid=(n1,),
        in_specs=[blk, tab, tab, pl.BlockSpec((1, 2 * n2, D), lambda k: (k, 0, 0))],
        out_specs=blk,
        out_shape=jax.ShapeDtypeStruct(a.shape, BF16),
        compiler_params=_params("arbitrary"),
        name="hy_mid",
    )(a, m_bf, mt_bf, kf)


def _hy_output(x0c, conv, u, bias):
    return x0c * (conv + u * bias)


def _hy_ctx_kernel(x0_ref, x1_ref, v_ref, w0_ref, w1_ref, w2_ref, b0_ref, b1_ref, b2_ref,
                   f_ref, kf_ref, g_ref, bias_ref, o_ref):
    x0 = _short_conv(x0_ref[0], w0_ref[...], b0_ref[...])
    u = _short_conv(x1_ref[0], w1_ref[...], b1_ref[...]) * _short_conv(v_ref[0], w2_ref[...], b2_ref[...])
    x = _dot(f_ref[...], u, HIGHEST)
    nf = x.shape[0] // 2
    yr, yi = _cmul(x[:nf], x[nf:], kf_ref[:nf], kf_ref[nf:])
    conv = _dot(g_ref[...], jnp.concatenate([yr, yi], axis=0), HIGHEST)
    o_ref[0] = _hy_output(x0, conv, u, bias_ref[...]).astype(o_ref.dtype)


def _hy_ctx(p_hg, conv_w, conv_b, f_c, kf_c, g_c, bias, L, Lc):
    B = p_hg.shape[0]
    ct = 256
    nb = D // ct
    rb = L // Lc
    col = lambda k: pl.BlockSpec((1, Lc, ct), lambda b, j: (b, rb, k * nb + j))
    wsp = lambda k: pl.BlockSpec((3, ct), lambda b, j: (0, k * nb + j))
    bsp = lambda k: pl.BlockSpec((1, ct), lambda b, j: (0, k * nb + j))
    return pl.pallas_call(
        _hy_ctx_kernel,
        grid=(B, nb),
        in_specs=[col(0), col(1), col(2), wsp(0), wsp(1), wsp(2), bsp(0), bsp(1), bsp(2),
                  pl.BlockSpec(f_c.shape, lambda b, j: (0, 0)),
                  pl.BlockSpec((kf_c.shape[0], ct), lambda b, j: (0, j)),
                  pl.BlockSpec(g_c.shape, lambda b, j: (0, 0)),
                  pl.BlockSpec((1, ct), lambda b, j: (0, j))],
        out_specs=pl.BlockSpec((1, Lc, ct), lambda b, j: (b, 0, j)),
        out_shape=jax.ShapeDtypeStruct((B, Lc, D), BF16),
        compiler_params=_params("arbitrary", "arbitrary"),
        name="hy_ctx",
    )(p_hg, p_hg, p_hg, conv_w, conv_w, conv_w, conv_b, conv_b, conv_b, f_c, kf_c, g_c, bias)


def _dft_tables(L, Lc):
    n = 2 * L
    n2 = DFT_N2
    n1 = n // n2
    two_pi = 2.0 * math.pi

    def cs(idx, period):
        ang = (idx % period).astype(F32) * (two_pi / period)
        return jnp.cos(ang), jnp.sin(ang)

    k1 = jnp.arange(n1)[:, None]
    c, s = cs(k1 * jnp.arange(n1)[None, :], n1)
    f1_full = jnp.concatenate([c, -s], axis=0)
    f1_half = f1_full[:, :n1 // 2]
    kk1 = jnp.arange(n1)[:, None, None]
    kk2 = jnp.arange(n2)[None, :, None]
    nn2 = jnp.arange(n2)[None, None, :]
    c, s = cs(nn2 * kk1 + n1 * nn2 * kk2, n)
    ci = -s
    m_tab = jnp.concatenate([jnp.concatenate([c, -ci], axis=2), jnp.concatenate([ci, c], axis=2)], axis=1)
    c, s = cs(jnp.arange(n1 // 2)[:, None] * jnp.arange(n1)[None, :], n1)
    g_mat = jnp.concatenate([c, -s], axis=1) / n
    nc = 2 * Lc
    kc = jnp.arange(nc)[:, None]
    c, s = cs(kc * jnp.arange(nc)[None, :], nc)
    fc_full = jnp.concatenate([c, -s], axis=0)
    c, s = cs(jnp.arange(Lc)[:, None] * jnp.arange(nc)[None, :], nc)
    g_c = jnp.concatenate([c, -s], axis=1) / nc
    return dict(f1_full=f1_full, f1_half=f1_half.astype(BF16), m_tab=m_tab, m_bf=m_tab.astype(BF16),
                mt_bf=jnp.swapaxes(m_tab, 1, 2).astype(BF16), g_mat=g_mat.astype(BF16),
                fc_full=fc_full, fc_half=fc_full[:, :Lc], g_c=g_c)


def _hyena(p_hg, hp, tabs, L, Lc):
    B = p_hg.shape[0]
    n2 = DFT_N2
    n1 = 2 * L // n2
    wc = min(16 * D, n2 * D)
    deltas = jnp.abs(jnp.linspace(math.log(HY_TARGET) / HY_FAST_DECAY, math.log(HY_TARGET) / HY_SLOW_DECAY,
                                  D, dtype=F32))[None, :]
    filt = functools.partial(_hy_filter, w1p=hp['w1p'], b1=hp['b1'], w2=hp['w2'], b2=hp['b2'], fr=hp['fr'],
                             w3=hp['w3'], deltas=deltas)
    k_lat = filt(L)
    a_f = _lmat(tabs['f1_full'], k_lat.reshape(1, n1, n2 * D), F32, HIGHEST, wc, "hy_filt_s1")
    kf = _hy_spectrum(a_f.reshape(2, n1, n2, D), tabs['m_tab'])
    k_ctx = filt(Lc)
    kf_c = _lmat(tabs['fc_full'], k_ctx[None], F32, HIGHEST, D, "hy_filt_ctx")[0]
    x0c, u, u_bf = _hy_pre(p_hg, hp['conv_w'], hp['conv_b'], L)
    a = _lmat(tabs['f1_half'], u_bf.reshape(B, n1 // 2, n2 * D), BF16, None, wc, "hy_s1")
    z = _hy_mid(a.reshape(B, 2, n1, n2, D), tabs['m_bf'], tabs['mt_bf'], kf)
    conv = _lmat(tabs['g_mat'], z.reshape(B, 2 * n1, n2 * D), BF16, None, wc, "hy_s3").reshape(B, L, D)
    y_c = _hy_ctx(p_hg, hp['conv_w'], hp['conv_b'], tabs['fc_half'], kf_c, tabs['g_c'], hp['bias'], L, Lc)
    return conv, x0c, u, y_c


def _merge_kernel(x_ref, mod_ref, yd_ref, cv_ref, x0_ref, u_ref, yc_ref, hb_ref, ys_ref, gd_ref, gh_ref, gs_ref,
                  wb_ref, wo_ref, g2_ref, rw_ref, xo_ref, h2_ref, lg_ref, *, n_lat_tiles):
    latent = pl.program_id(1) < n_lat_tiles
    half = x_ref.shape[1] // 2
    for rows in (slice(0, half), slice(half, 2 * half)):
        y_hy = jnp.where(latent,
                         _hy_output(x0_ref[0, rows], cv_ref[0, rows].astype(F32), u_ref[0, rows], hb_ref[...]),
                         yc_ref[0, rows].astype(F32)).astype(BF16)
        merged = None
        for k, (y, gt_ref) in enumerate(((yd_ref[0, rows], gd_ref), (y_hy, gh_ref), (ys_ref[0, rows], gs_ref))):
            term = jax.nn.sigmoid(gt_ref[0, rows]) * _dot(y, wb_ref[k])
            merged = term if merged is None else merged + term
        y = _dot(merged.astype(BF16), wo_ref[...])
        x = x_ref[0, rows] + mod_ref[0, 2:3, :] * y
        xo_ref[0, rows] = x
        h2 = _rms(x) * g2_ref[...]
        h2 = h2 * (1.0 + mod_ref[0, 4:5, :]) + mod_ref[0, 3:4, :]
        h2_ref[0, rows] = h2.astype(h2_ref.dtype)
        lg_ref[0, :, rows] = _dot(h2, rw_ref[...], HIGHEST).T[:N_EXPERTS]


def _merge(xa, mod, y_diff, hy, hy_bias, y_swa, p_hg, w_branch, w_out, g2, rw_pad, n_lat_tiles):
    B, T, _ = xa.shape
    ctx_row = B
    conv, x0c, u, y_hy_c = hy
    tile = lambda c: pl.BlockSpec((1, TM, D), lambda b, i: (b, i, c))
    lat = pl.BlockSpec((1, TM, D), lambda b, i: (b, jnp.minimum(i, n_lat_tiles - 1), 0))
    return pl.pallas_call(
        functools.partial(_merge_kernel, n_lat_tiles=n_lat_tiles),
        grid=(B, T // TM),
        in_specs=[tile(0),
                  pl.BlockSpec((1, 6, D), lambda b, i: (jnp.where(i >= n_lat_tiles, ctx_row, b), 0, 0)),
                  tile(0), lat, lat, lat,
                  pl.BlockSpec((1, y_hy_c.shape[1], D), lambda b, i: (b, 0, 0)),
                  pl.BlockSpec((1, D), lambda b, i: (0, 0)),
                  tile(0), tile(3), tile(4), tile(5),
                  pl.BlockSpec((3, D, D), lambda b, i: (0, 0, 0)),
                  pl.BlockSpec((D, D), lambda b, i: (0, 0)),
                  pl.BlockSpec((1, D), lambda b, i: (0, 0)),
                  pl.BlockSpec((D, LANES), lambda b, i: (0, 0))],
        out_specs=[tile(0), tile(0), pl.BlockSpec((1, N_EXPERTS, TM), lambda b, i: (b, 0, i))],
        out_shape=[jax.ShapeDtypeStruct((B, T, D), F32), jax.ShapeDtypeStruct((B, T, D), BF16),
                   jax.ShapeDtypeStruct((B, N_EXPERTS, T), F32)],
        compiler_params=_params("arbitrary", "arbitrary"),
        name="merge",
    )(xa, mod, y_diff, conv, x0c, u, y_hy_c, hy_bias, y_swa, p_hg, p_hg, p_hg, w_branch, w_out, g2, rw_pad)


def _router_kernel(lg_ref, slot_ref, gate_ref, first_ref, *, segments):
    lg = lg_ref[0]
    e = jnp.exp(lg - jnp.max(lg, axis=0, keepdims=True))
    aff = e / jnp.sum(e, axis=0, keepdims=True)
    r = lax.broadcasted_iota(jnp.int32, (LANES, LANES), 0)
    c = lax.broadcasted_iota(jnp.int32, (LANES, LANES), 1)
    before = (r < c).astype(BF16)

    def prefix_count(mask, n):
        run = jnp.zeros((N_EXPERTS, 1), F32)
        outs = []
        for j in range(n // LANES):
            blk = mask[:, j * LANES:(j + 1) * LANES].astype(F32).astype(BF16)
            outs.append(_dot(blk, before) + run)
            run = run + jnp.sum(blk.astype(F32), axis=1, keepdims=True)
        return jnp.concatenate(outs, axis=1)

    for start, n, cap, slot_base in segments:
        a = aff[:, start:start + n]
        bits = pltpu.bitcast(a, jnp.int32)

        def search(_, bounds):
            lo, hi = bounds
            mid = lo + ((hi - lo + 1) >> 1)
            cnt = jnp.sum((bits >= mid).astype(F32), axis=1, keepdims=True)
            ok = cnt >= cap
            return jnp.where(ok, mid, lo), jnp.where(ok, hi, mid - 1)

        lo0 = jnp.zeros((N_EXPERTS, 1), jnp.int32)
        hi0 = jnp.full((N_EXPERTS, 1), 0x7F800000, jnp.int32)
        thr, _ = lax.fori_loop(0, 32, search, (lo0, hi0))
        above = bits > thr
        tie = bits == thr
        need = cap - jnp.sum(above.astype(F32), axis=1, keepdims=True)
        sel = above | (tie & (prefix_count(tie, n) < need))
        pos = prefix_count(sel, n)
        slot_ref[0, :, start:start + n] = jnp.where(sel, pos + slot_base, -1.0)
        gate_ref[0, :, start:start + n] = jnp.where(sel, a, 0.0)
        if start == 0:
            t_idx = lax.broadcasted_iota(jnp.int32, (n, LANES), 0)
            tile_start = lax.broadcasted_iota(jnp.int32, (n, LANES), 1) * TM
            before_tile = jnp.where(t_idx < tile_start, 1.0, 0.0).astype(BF16)
            first_ref[0] = _dot(jnp.where(sel, 1.0, 0.0).astype(BF16), before_tile)


def _router(logits_t, segments):
    B, E, T = logits_t.shape
    blk = pl.BlockSpec((1, E, T), lambda b: (b, 0, 0))
    return pl.pallas_call(
        functools.partial(_router_kernel, segments=segments),
        grid=(B,),
        in_specs=[blk],
        out_specs=[blk, blk, pl.BlockSpec((1, E, LANES), lambda b: (b, 0, 0))],
        out_shape=[jax.ShapeDtypeStruct((B, E, T), F32)] * 2 + [jax.ShapeDtypeStruct((B, E, LANES), F32)],
        compiler_params=_params("arbitrary"),
        name="router",
    )(logits_t)


def _slot_window(first, align, rows, n_slots):
    return pl.multiple_of(jnp.minimum(first // align * align, n_slots - rows), align)


def _moe_ffn_kernel(first_ref, slot_ref, gate_ref, h2_ref, w1_ref, w3_ref, w2_ref, o_ref, xg_scr, g_scr,
                    *, n_slots, n_tiles):
    base = (pl.program_id(0) * N_EXPERTS + pl.program_id(1)) * n_tiles
    rows = min(TM + 8, n_slots)
    row_id = lax.broadcasted_iota(jnp.int32, (rows, TM), 0)
    xg_scr[...] = jnp.zeros_like(xg_scr)
    g_scr[...] = jnp.zeros_like(g_scr)
    for c in range(n_tiles):
        tok = slice(c * TM, (c + 1) * TM)
        w0 = _slot_window(first_ref[base + c], 8, rows, n_slots)
        hit = slot_ref[0, 0, :, tok] == (row_id + w0).astype(F32)
        win = pl.ds(w0, rows)
        xg_scr[win, :] += _dot(jnp.where(hit, 1.0, 0.0).astype(BF16), h2_ref[0, tok, :])
        g_scr[win, :] += jnp.sum(jnp.where(hit, gate_ref[0, 0, :, tok], 0.0), axis=1, keepdims=True)
    xb = xg_scr[...].astype(BF16)
    a = _dot(xb, w1_ref[0, 0])
    b = _dot(xb, w3_ref[0, 0])
    hidden = (a * jax.nn.sigmoid(a) * b).astype(BF16)
    o_ref[0, 0] = (_dot(hidden, w2_ref[0, 0]) * g_scr[...]).astype(o_ref.dtype)


def _moe_ffn(first, slot, gate, h2, w1, w3, w2, l, n_slots):
    B, E, T = slot.shape
    row = pl.BlockSpec((1, 1, 1, T), lambda b, e: (b, e, 0, 0))
    wsp = pl.BlockSpec((1, 1, D, D), lambda b, e: (l, e, 0, 0))
    return pl.pallas_call(
        functools.partial(_moe_ffn_kernel, n_slots=n_slots, n_tiles=T // TM),
        grid=(B, E),
        in_specs=[pl.BlockSpec(memory_space=pltpu.SMEM), row, row,
                  pl.BlockSpec((1, T, D), lambda b, e: (b, 0, 0)), wsp, wsp, wsp],
        out_specs=pl.BlockSpec((1, 1, n_slots, D), lambda b, e: (b, e, 0, 0)),
        out_shape=jax.ShapeDtypeStruct((B, E, n_slots, D), BF16),
        scratch_shapes=[pltpu.VMEM((n_slots, D), F32), pltpu.VMEM((n_slots, 1), F32)],
        compiler_params=_params("arbitrary", "arbitrary"),
        name="moe_ffn",
    )(first, slot.reshape(B, E, 1, T), gate.reshape(B, E, 1, T), h2, w1, w3, w2)


def _moe_combine_kernel(first_ref, x_ref, mod_ref, slot_ref, y_ref, o_ref, *, n_slots, n_tiles):
    tm = x_ref.shape[1]
    rows = min(tm + 16, n_slots)
    col_id = lax.broadcasted_iota(jnp.int32, (tm, rows), 1)
    acc = jnp.zeros((tm, D), F32)
    for e in range(N_EXPERTS):
        w0 = _slot_window(first_ref[(pl.program_id(0) * N_EXPERTS + e) * n_tiles + pl.program_id(1)], 16, rows,
                          n_slots)
        hit = slot_ref[0, :, e:e + 1] == (col_id + w0).astype(F32)
        acc = acc + _dot(jnp.where(hit, 1.0, 0.0).astype(BF16), y_ref[0, e, pl.ds(w0, rows), :])
    o_ref[0] = x_ref[0] + mod_ref[0, 5:6, :] * acc


def _moe_combine(first, xa, mod, slot_t, y, n_lat_tiles, n_slots):
    B, T, _ = xa.shape
    ctx_row = B
    tile = pl.BlockSpec((1, TM, D), lambda b, i: (b, i, 0))
    return pl.pallas_call(
        functools.partial(_moe_combine_kernel, n_slots=n_slots, n_tiles=T // TM),
        grid=(B, T // TM),
        in_specs=[pl.BlockSpec(memory_space=pltpu.SMEM), tile,
                  pl.BlockSpec((1, 6, D), lambda b, i: (jnp.where(i >= n_lat_tiles, ctx_row, b), 0, 0)),
                  pl.BlockSpec((1, TM, N_EXPERTS), lambda b, i: (b, i, 0)),
                  pl.BlockSpec((1, N_EXPERTS, n_slots, D), lambda b, i: (b, 0, 0, 0))],
        out_specs=tile,
        out_shape=jax.ShapeDtypeStruct((B, T, D), F32),
        compiler_params=_params("arbitrary", "arbitrary"),
        name="moe_combine",
    )(first, xa, mod, slot_t, y)


def _final_kernel(x_ref, g_ref, o_ref):
    o_ref[0] = _rms(x_ref[0]) * g_ref[...]


def _final_norm(xa, g, L):
    B = xa.shape[0]
    tile = pl.BlockSpec((1, TM, D), lambda b, i: (b, i, 0))
    return pl.pallas_call(
        _final_kernel,
        grid=(B, L // TM),
        in_specs=[tile, pl.BlockSpec((1, D), lambda b, i: (0, 0))],
        out_specs=tile,
        out_shape=jax.ShapeDtypeStruct((B, L, D), F32),
        compiler_params=_params("arbitrary", "arbitrary"),
        name="final_norm",
    )(xa, g)


def _swa_head_perm():
    return [kv * SWA_GROUP + g for g in range(SWA_GROUP) for kv in range(SWA_KV)]


def _split_w_in(w_in):
    depth = w_in.shape[0]
    o = [0, D, 2 * D, 3 * D, 6 * D, 7 * D, 7 * D + SWA_KV * HEAD, 7 * D + 2 * SWA_KV * HEAD, 10 * D + 2 * SWA_KV * HEAD]
    dq, dk, dv, hy, sq, sk, sv, gt = (w_in[:, :, o[k]:o[k + 1]] for k in range(8))
    scale = HEAD ** -0.5 * math.log2(math.e)
    sq = sq.reshape(depth, D, SWA_KV * SWA_GROUP, HEAD)[:, :, jnp.array(_swa_head_perm())].reshape(depth, D, D)
    w_attn = jnp.concatenate([dq * scale, dk, dv, sq * scale, sk, sv], axis=-1).astype(BF16)
    w_hg = jnp.concatenate([hy, gt], axis=-1).astype(BF16)
    return w_attn, w_hg


def kernel(x, c, ctx, c_ctx, w_mod, b_mod, norm1_g, norm2_g, w_in, diff_lambda, diff_subln_g, hy_conv_w,
           hy_conv_b, hy_ff_w1, hy_ff_b1, hy_ff_w2, hy_ff_b2, hy_ff_w3, hy_sin_freq, hy_bias, swa_sink,
           w_branch, w_out, router_w, moe_w1, moe_w3, moe_w2, final_g):
    B, L, _ = x.shape
    Lc = ctx.shape[1]
    depth = w_mod.shape[0]
    T = L + Lc
    assert Lc == TM and L % (4 * TM) == 0 and x.shape[2] == D
    n_lat_tiles = L // TM
    cap_l = CAPACITY_FACTOR * L // N_EXPERTS
    cap_c = CAPACITY_FACTOR * Lc // N_EXPERTS
    n_slots = cap_l + cap_c
    segments = ((0, L, cap_l, 0), (L, Lc, cap_c, cap_l))

    xa = jnp.concatenate([x, ctx], axis=1)
    mods = _mods(c, c_ctx, w_mod, b_mod)
    cos_t, sin_t = _rope_tables(L, Lc)
    tabs = _dft_tables(L, Lc)
    w_attn, w_hg = _split_w_in(w_in)
    rope_flags = tuple([True] * 16 + [False] * 8 + [True] * 8 + [True] * 2 + [False] * 2)
    perm = jnp.array(_swa_head_perm())
    wb = w_branch.at[:, 2].set(
        w_branch[:, 2].reshape(depth, SWA_KV * SWA_GROUP, HEAD, D)[:, perm].reshape(depth, D, D)).astype(BF16)
    wo = w_out.astype(BF16)
    w1, w3, w2 = moe_w1.astype(BF16), moe_w3.astype(BF16), moe_w2.astype(BF16)
    rw_pad = jnp.pad(router_w, ((0, 0), (0, 0), (0, LANES - N_EXPERTS)))
    w1p = jnp.pad(hy_ff_w1, ((0, 0), (0, LANES - HY_EMB), (0, 0)))

    for l in range(depth):
        lambda_init = 0.8 - 0.6 * math.exp(-0.3 * l)
        mod = mods[l]
        p_attn = _norm_proj(xa, mod, norm1_g[l][None], w_attn[l], cos_t, sin_t, rope_flags, BF16,
                            n_lat_tiles, "proj_attn")
        p_hg = _norm_proj(xa, mod, norm1_g[l][None], w_hg[l], cos_t, sin_t, (False,) * (N_HG // LANES), F32,
                          n_lat_tiles, "proj_hg")
        y_diff = _diff_attn(p_attn, diff_lambda, diff_subln_g, L, Lc, lambda_init, l)
        y_swa = _swa(p_attn, swa_sink, L, Lc, l)
        hp = dict(w1p=w1p[l], b1=hy_ff_b1[l][None], w2=hy_ff_w2[l], b2=hy_ff_b2[l][None],
                  fr=hy_sin_freq[l][None], w3=hy_ff_w3[l], conv_w=hy_conv_w[l], conv_b=hy_conv_b[l][None],
                  bias=hy_bias[l][None])
        hy = _hyena(p_hg, hp, tabs, L, Lc)
        xa, h2, logits_t = _merge(xa, mod, y_diff, hy, hp['bias'], y_swa, p_hg, wb[l], wo[l], norm2_g[l][None],
                                  rw_pad[l], n_lat_tiles)
        slot, gate, first = _router(logits_t, segments)
        first = jnp.concatenate([first[:, :, :n_lat_tiles], jnp.full((B, N_EXPERTS, 1), cap_l, F32)], axis=2)
        first = first.astype(jnp.int32).reshape(-1)
        y_e = _moe_ffn(first, slot, gate, h2, w1, w3, w2, l, n_slots)
        xa = _moe_combine(first, xa, mod, jnp.swapaxes(slot, 1, 2), y_e, n_lat_tiles, n_slots)
    return _final_norm(xa, final_g[None], L)
```

```python
import functools
import math

import jax
import jax.numpy as jnp
from jax import lax
from jax.experimental import pallas as pl
from jax.experimental.pallas import tpu as pltpu

F32 = jnp.float32
BF16 = jnp.bfloat16
HIGHEST = lax.Precision.HIGHEST

D = 1024
HEAD = 64
GRID_W = 64
ROPE_BASE = 10000.0
EPS = 1e-6
N_DIFF_HEADS = D // (2 * HEAD)
DIFF_V = 2 * HEAD
SWA_KV = 4
SWA_GROUP = 4
WINDOW = 128
N_EXPERTS = 16
CAPACITY_FACTOR = 2
HY_BANDS = 16
HY_EMB = 1 + 2 * HY_BANDS
HY_FF = 64
HY_FAST_DECAY = 0.3
HY_SLOW_DECAY = 1.5
HY_TARGET = 1e-2

LANES = 128
TM = 256
DFT_N2 = 128
VMEM_LIMIT = 56 * 1024 * 1024

N_ATTN = 3 * D + D + 2 * SWA_KV * HEAD
N_HG = 3 * D + 3 * D


def _params(*sem):
    return pltpu.CompilerParams(dimension_semantics=sem, vmem_limit_bytes=VMEM_LIMIT)


def _dot(a, b, precision=None):
    return jnp.dot(a, b, preferred_element_type=F32, precision=precision)


def _dot_nt(a, b, precision=None):
    return lax.dot_general(a, b, (((1,), (1,)), ((), ())), preferred_element_type=F32, precision=precision)


def _rms(x):
    return x * lax.rsqrt(jnp.mean(x * x, axis=-1, keepdims=True) + EPS)


def _mod_kernel(c_ref, w_ref, b_ref, o_ref):
    c = c_ref[...]
    sc = c * jax.nn.sigmoid(c)
    o_ref[0] = _dot(sc, w_ref[0], HIGHEST) + b_ref[0]


def _mods(c, c_ctx, w_mod, b_mod):
    B = c.shape[0]
    depth = w_mod.shape[0]
    rows = -(-(B + 1) // 8) * 8
    cc = jnp.zeros((rows, D), F32).at[:B].set(c).at[B].set(c_ctx)
    out = pl.pallas_call(
        _mod_kernel,
        grid=(depth, 6),
        in_specs=[pl.BlockSpec((rows, D), lambda l, j: (0, 0)),
                  pl.BlockSpec((1, D, D), lambda l, j: (l, 0, j)),
                  pl.BlockSpec((1, 1, D), lambda l, j: (l, 0, j))],
        out_specs=pl.BlockSpec((1, rows, D), lambda l, j: (l, 0, j)),
        out_shape=jax.ShapeDtypeStruct((depth, rows, 6 * D), F32),
        compiler_params=_params("arbitrary", "arbitrary"),
        name="mods",
    )(cc, w_mod, b_mod.reshape(depth, 1, 6 * D))
    return out.reshape(depth, rows, 6, D)


def _norm_proj_kernel(x_ref, mod_ref, g_ref, w_ref, cos_ref, sin_ref, o_ref, *, rope_flags):
    x = x_ref[0]
    h = _rms(x) * g_ref[...]
    h = h * (1.0 + mod_ref[0, 1:2, :]) + mod_ref[0, 0:1, :]
    hb = h.astype(BF16)
    n_sub = len(rope_flags)
    sub_per_chunk = 4
    if any(rope_flags):
        cos = cos_ref[...]
        sin = sin_ref[...]
        lane = lax.broadcasted_iota(jnp.int32, cos.shape, 1)
        first_half = (lane % (HEAD // 2)) < (HEAD // 4)
    for c0 in range(0, n_sub, sub_per_chunk):
        c1 = min(c0 + sub_per_chunk, n_sub)
        acc = _dot(hb, w_ref[:, c0 * LANES:c1 * LANES])
        for s in range(c0, c1):
            a = acc[:, (s - c0) * LANES:(s - c0 + 1) * LANES]
            if rope_flags[s]:
                partner = jnp.where(first_half, pltpu.roll(a, LANES - HEAD // 4, 1), pltpu.roll(a, HEAD // 4, 1))
                a = a * cos + partner * sin
            o_ref[0, :, s * LANES:(s + 1) * LANES] = a.astype(o_ref.dtype)


def _norm_proj(xa, mod, g, w, cos_t, sin_t, rope_flags, out_dtype, n_lat_tiles, name):
    B, T, _ = xa.shape
    N = w.shape[1]
    ctx_row = B
    return pl.pallas_call(
        functools.partial(_norm_proj_kernel, rope_flags=rope_flags),
        grid=(B, T // TM),
        in_specs=[pl.BlockSpec((1, TM, D), lambda b, i: (b, i, 0)),
                  pl.BlockSpec((1, 6, D), lambda b, i: (jnp.where(i >= n_lat_tiles, ctx_row, b), 0, 0)),
                  pl.BlockSpec((1, D), lambda b, i: (0, 0)),
                  pl.BlockSpec((D, N), lambda b, i: (0, 0)),
                  pl.BlockSpec((TM, LANES), lambda b, i: (i, 0)),
                  pl.BlockSpec((TM, LANES), lambda b, i: (i, 0))],
        out_specs=pl.BlockSpec((1, TM, N), lambda b, i: (b, i, 0)),
        out_shape=jax.ShapeDtypeStruct((B, T, N), out_dtype),
        compiler_params=_params("arbitrary", "arbitrary"),
        name=name,
    )(xa, mod, g, w, cos_t, sin_t)


def _rope_tables(L, Lc):
    t = jnp.arange(L)
    pos = jnp.stack([(t // GRID_W).astype(F32), (t % GRID_W).astype(F32)], axis=1)
    n_freq = HEAD // 4
    inv = ROPE_BASE ** (-jnp.arange(n_freq, dtype=F32) / n_freq)
    ang = pos[:, :, None] * inv
    cos = jnp.cos(ang)
    sin = jnp.sin(ang)
    cos_h = jnp.concatenate([cos, cos], axis=-1).reshape(L, HEAD)
    sin_h = jnp.concatenate([-sin, sin], axis=-1).reshape(L, HEAD)
    cos_t = jnp.concatenate([jnp.tile(cos_h, (1, LANES // HEAD)), jnp.ones((Lc, LANES), F32)], axis=0)
    sin_t = jnp.concatenate([jnp.tile(sin_h, (1, LANES // HEAD)), jnp.zeros((Lc, LANES), F32)], axis=0)
    return cos_t, sin_t


def _lane_block_reduce(x, op):
    out = x[:, :LANES]
    for j in range(1, x.shape[1] // LANES):
        out = op(out, x[:, j * LANES:(j + 1) * LANES])
    return out


def _diff_attn_kernel(lam_ref, q_ref, k_ref, v_ref, g_ref, o_ref, s_lat, s_ctx, m_run, qq_scr, l_scr, acc_scr,
                      *, L, Lc, tg, n_tiles, n_total, lambda_init):
    n = pl.program_id(0)
    tq = q_ref.shape[1]
    n_groups = (L + Lc) // tg
    n_lat = L // tq
    cur = n % 2
    prv = 1 - cur
    score_any = n < n_total
    score_lat = jnp.logical_and(score_any, n % n_tiles < n_lat)
    weigh_any = n >= 1
    weigh_lat = jnp.logical_and(weigh_any, (n + n_tiles - 1) % n_tiles < n_lat)
    rows = (2 * tq, LANES)

    @pl.when(score_any)
    def _():
        q = q_ref[0]
        lane = lax.broadcasted_iota(jnp.int32, q.shape, 1)
        zero = jnp.zeros_like(q)
        qq_scr[...] = jnp.concatenate([jnp.where(lane < HEAD, q, zero), jnp.where(lane >= HEAD, q, zero)], axis=0)

    @pl.when(jnp.logical_and(score_any, jnp.logical_not(score_lat)))
    def _():
        s_c = _dot_nt(qq_scr[...], k_ref[0, L:L + Lc, :])
        s_ctx[...] = s_c
        m_run[cur] = _lane_block_reduce(s_c, jnp.maximum)

    def row_max():
        return jnp.max(m_run[prv], axis=-1, keepdims=True)

    @pl.when(jnp.logical_and(weigh_any, jnp.logical_not(weigh_lat)))
    def _():
        p = jnp.exp2(s_ctx[...] - row_max())
        l_scr[...] = _lane_block_reduce(p, jnp.add)
        acc_scr[...] = _dot(p.astype(BF16), v_ref[0, L:L + Lc, :])

    def scores(gi, mrun):
        start = pl.multiple_of(gi * tg, LANES)
        s = _dot_nt(qq_scr[...], k_ref[0, pl.ds(start, tg), :])
        s_lat[cur, gi] = s
        return jnp.maximum(mrun, _lane_block_reduce(s, jnp.maximum))

    def weighted(gi, lrun, acc, m):
        start = pl.multiple_of(gi * tg, LANES)
        p = jnp.exp2(s_lat[prv, gi] - m)
        return lrun + _lane_block_reduce(p, jnp.add), acc + _dot(p.astype(BF16), v_ref[0, pl.ds(start, tg), :])

    lowest = jnp.full(rows, -jnp.inf, F32)
    nothing = jnp.zeros(rows, F32)

    @pl.when(jnp.logical_and(score_lat, weigh_lat))
    def _():
        m = row_max()

        def both(gi, carry):
            mrun, lrun, acc = carry
            lrun, acc = weighted(gi, lrun, acc, m)
            return scores(gi, mrun), lrun, acc

        m_run[cur], l_scr[...], acc_scr[...] = lax.fori_loop(0, n_groups, both, (lowest, nothing, nothing))

    @pl.when(jnp.logical_and(score_lat, jnp.logical_not(weigh_lat)))
    def _():
        m_run[cur] = lax.fori_loop(0, n_groups, scores, lowest)

    @pl.when(jnp.logical_and(jnp.logical_not(score_lat), weigh_lat))
    def _():
        m = row_max()
        l_scr[...], acc_scr[...] = lax.fori_loop(0, n_groups, lambda gi, c: weighted(gi, c[0], c[1], m),
                                                  (nothing, nothing))

    @pl.when(weigh_any)
    def _():
        lp = lam_ref[0]
        lam = (jnp.exp(jnp.sum(lp[0:1] * lp[1:2], axis=-1, keepdims=True))
               - jnp.exp(jnp.sum(lp[2:3] * lp[3:4], axis=-1, keepdims=True)) + lambda_init)
        o = acc_scr[...] / jnp.sum(l_scr[...], axis=-1, keepdims=True)
        o = _rms(o[:tq] - lam * o[tq:])
        o_ref[0] = (o * g_ref[0] * (1.0 - lambda_init)).astype(o_ref.dtype)


def _diff_attn(p_attn, diff_lambda, subln_g, L, Lc, lambda_init, l):
    B, T, _ = p_attn.shape
    tg = T // 2
    nh = N_DIFF_HEADS
    n_tiles = T // TM
    n_total = B * nh * n_tiles

    def tile(t):
        return t // (nh * n_tiles), (t // n_tiles) % nh, t % n_tiles

    def scored(n):
        return tile(jnp.minimum(n, n_total - 1))

    def weighed(n):
        return tile(jnp.maximum(n - 1, 0))

    def q_map(n):
        b, h, i = scored(n)
        return b, i, h

    def k_map(n):
        b, h, _ = scored(n)
        return b, 0, nh + h

    def v_map(n):
        b, h, _ = weighed(n)
        return b, 0, 2 * nh + h

    def o_map(n):
        b, h, i = weighed(n)
        return b, i, h

    return pl.pallas_call(
        functools.partial(_diff_attn_kernel, L=L, Lc=Lc, tg=tg, n_tiles=n_tiles, n_total=n_total,
                          lambda_init=lambda_init),
        grid=(n_total + 1,),
        in_specs=[pl.BlockSpec((1, 4, HEAD), lambda n: (l, 0, 0)),
                  pl.BlockSpec((1, TM, DIFF_V), q_map),
                  pl.BlockSpec((1, T, DIFF_V), k_map),
                  pl.BlockSpec((1, T, DIFF_V), v_map),
                  pl.BlockSpec((1, 1, DIFF_V), lambda n: (l, 0, 0))],
        out_specs=pl.BlockSpec((1, TM, DIFF_V), o_map),
        out_shape=jax.ShapeDtypeStruct((B, T, D), BF16),
        scratch_shapes=[pltpu.VMEM((2, T // tg, 2 * TM, tg), F32), pltpu.VMEM((2 * TM, Lc), F32),
                        pltpu.VMEM((2, 2 * TM, LANES), F32), pltpu.VMEM((2 * TM, DIFF_V), BF16),
                        pltpu.VMEM((2 * TM, LANES), F32), pltpu.VMEM((2 * TM, DIFF_V), F32)],
        compiler_params=_params("arbitrary"),
        name="diff_attn",
    )(diff_lambda, p_attn, p_attn, p_attn, subln_g.reshape(-1, 1, DIFF_V))


def _swa_kernel(sink_ref, q_ref, k_ref, v_ref, o_ref, *, L, Lc, l):
    i = pl.program_id(1)
    tq = q_ref.shape[1]
    kvw = SWA_KV * HEAD
    win = tq + 2 * WINDOW
    lane_head = lax.broadcasted_iota(jnp.int32, (tq, kvw), 1) // HEAD
    k_ctx = k_ref[0, L:L + Lc, :]
    v_ctx = v_ref[0, L:L + Lc, :]

    def run(latent):
        if latent:
            start = pl.multiple_of(jnp.clip(i * tq - WINDOW, 0, L - win), WINDOW)
            k_win = k_ref[0, pl.ds(start, win), :]
            v_win = v_ref[0, pl.ds(start, win), :]
            q_pos = i * tq + lax.broadcasted_iota(jnp.int32, (tq, win), 0)
            k_pos = start + lax.broadcasted_iota(jnp.int32, (tq, win), 1)
            band_bias = jnp.where(jnp.abs(k_pos - q_pos) <= WINDOW, 0.0, -jnp.inf)
        for g in range(SWA_GROUP):
            qg = q_ref[0, :, g * kvw:(g + 1) * kvw]
            acc = jnp.zeros((tq, kvw), F32)
            for kv in range(SWA_KV):
                qe = jnp.where(lane_head == kv, qg, jnp.zeros_like(qg))
                snk = sink_ref[l, kv * SWA_GROUP + g] * math.log2(math.e)
                s_c = _dot_nt(qe, k_ctx)
                m = jnp.maximum(jnp.max(s_c, axis=-1, keepdims=True), snk)
                if latent:
                    s_w = _dot_nt(qe, k_win) + band_bias
                    m = jnp.maximum(m, jnp.max(s_w, axis=-1, keepdims=True))
                p_c = jnp.exp2(s_c - m)
                den = jnp.sum(p_c, axis=-1, keepdims=True) + jnp.exp2(snk - m)
                o = _dot(p_c.astype(BF16), v_ctx)
                if latent:
                    p_w = jnp.exp2(s_w - m)
                    den = den + jnp.sum(p_w, axis=-1, keepdims=True)
                    o = o + _dot(p_w.astype(BF16), v_win)
                acc = acc + jnp.where(lane_head == kv, o / den, 0.0)
            o_ref[0, :, g * kvw:(g + 1) * kvw] = acc.astype(o_ref.dtype)

    @pl.when(i < L // tq)
    def _():
        run(True)

    @pl.when(i >= L // tq)
    def _():
        run(False)


def _swa(p_attn, sink, L, Lc, l):
    B, T, _ = p_attn.shape
    kvw = SWA_KV * HEAD
    return pl.pallas_call(
        functools.partial(_swa_kernel, L=L, Lc=Lc, l=l),
        grid=(B, T // TM),
        in_specs=[pl.BlockSpec(memory_space=pltpu.SMEM),
                  pl.BlockSpec((1, TM, D), lambda b, i: (b, i, 3)),
                  pl.BlockSpec((1, T, kvw), lambda b, i: (b, 0, 4 * D // kvw)),
                  pl.BlockSpec((1, T, kvw), lambda b, i: (b, 0, 4 * D // kvw + 1))],
        out_specs=pl.BlockSpec((1, TM, D), lambda b, i: (b, i, 0)),
        out_shape=jax.ShapeDtypeStruct((B, T, D), BF16),
        compiler_params=_params("arbitrary", "arbitrary"),
        name="swa",
    )(sink, p_attn, p_attn, p_attn)


def _hy_filter_kernel(emb_ref, w1_ref, b1_ref, w2_ref, b2_ref, fr_ref, w3f_ref, w3b_ref, dl_ref, o_ref, *, L):
    rows = TM
    fr = fr_ref[...]

    def raw(i, total):
        r0 = pl.multiple_of(i * rows, rows)
        e = emb_ref[pl.ds(r0, rows), :]
        z = jnp.sin(fr * (_dot(e, w1_ref[...], HIGHEST) + b1_ref[...]))
        z = jnp.sin(fr * (_dot(z, w2_ref[...], HIGHEST) + b2_ref[...]))
        row = r0 + lax.broadcasted_iota(jnp.int32, (rows, 1), 0)
        h = jnp.where(row < L, _dot(z, w3f_ref[...], HIGHEST), _dot(z, w3b_ref[...], HIGHEST))
        h = h * jnp.exp(-e[:, 0:1] * dl_ref[...])
        o_ref[pl.ds(r0, rows), :] = h
        return total + jnp.sum(jnp.abs(h), axis=0, keepdims=True)

    n_chunks = emb_ref.shape[0] // rows
    norm = lax.fori_loop(0, n_chunks, raw, jnp.zeros((1, o_ref.shape[1]), F32)) + EPS

    def scale(i, _):
        r0 = pl.multiple_of(i * rows, rows)
        row = r0 + lax.broadcasted_iota(jnp.int32, (rows, 1), 0)
        o_ref[pl.ds(r0, rows), :] = jnp.where(row == L, 0.0, o_ref[pl.ds(r0, rows), :] / norm)
        return 0

    lax.fori_loop(0, n_chunks, scale, 0)


def _hy_embedding(L):
    n = jnp.arange(2 * L)
    lag = jnp.where(n < L, n, jnp.where(n == L, 0, 2 * L - n))
    t = jnp.linspace(0.0, 1.0, L, dtype=F32)[lag][:, None]
    w = (2.0 * math.pi * jnp.arange(L, dtype=F32) / L)[lag][:, None]
    f = jnp.linspace(1e-4, HY_BANDS - 1, HY_BANDS, dtype=F32)[None, :]
    emb = jnp.concatenate([t, jnp.cos(f * w), -jnp.sin(f * w)], axis=-1)
    return jnp.pad(emb, ((0, 0), (0, LANES - HY_EMB)))


def _hy_filter(L, w1p, b1, w2, b2, fr, w3, deltas):
    ct = 256
    emb = _hy_embedding(L)
    full = lambda shape: pl.BlockSpec(shape, lambda j: (0, 0))
    return pl.pallas_call(
        functools.partial(_hy_filter_kernel, L=L),
        grid=(D // ct,),
        in_specs=[full((2 * L, LANES)), full((LANES, HY_FF)), full((1, HY_FF)), full((HY_FF, HY_FF)),
                  full((1, HY_FF)), full((1, HY_FF)),
                  pl.BlockSpec((HY_FF, ct), lambda j: (0, j)),
                  pl.BlockSpec((HY_FF, ct), lambda j: (0, D // ct + j)),
                  pl.BlockSpec((1, ct), lambda j: (0, j))],
        out_specs=pl.BlockSpec((2 * L, ct), lambda j: (0, j)),
        out_shape=jax.ShapeDtypeStruct((2 * L, D), F32),
        compiler_params=_params("arbitrary"),
        name="hy_filter",
    )(emb, w1p, b1, w2, b2, fr, w3, w3, deltas)


def _short_conv(x, w, b):
    n = x.shape[0]
    row = lax.broadcasted_iota(jnp.int32, x.shape, 0)
    prev = jnp.where(row == 0, 0.0, pltpu.roll(x, 1, 0))
    nxt = jnp.where(row == n - 1, 0.0, pltpu.roll(x, n - 1, 0))
    return prev * w[0:1] + x * w[1:2] + nxt * w[2:3] + b


def _hy_pre_kernel(x0_ref, x1_ref, v_ref, w0_ref, w1_ref, w2_ref, b0_ref, b1_ref, b2_ref, x0c_ref, u_ref):
    x0c_ref[0] = _short_conv(x0_ref[0], w0_ref[...], b0_ref[...]).astype(x0c_ref.dtype)
    u = _short_conv(x1_ref[0], w1_ref[...], b1_ref[...]) * _short_conv(v_ref[0], w2_ref[...], b2_ref[...])
    u_ref[0] = u.astype(u_ref.dtype)


def _hy_pre(p_hg, conv_w, conv_b, L):
    B = p_hg.shape[0]
    ct = LANES
    nb = D // ct
    col = lambda k: pl.BlockSpec((1, L, ct), lambda b, j: (b, 0, k * nb + j))
    wsp = lambda k: pl.BlockSpec((3, ct), lambda b, j: (0, k * nb + j))
    bsp = lambda k: pl.BlockSpec((1, ct), lambda b, j: (0, k * nb + j))
    out = pl.BlockSpec((1, L, ct), lambda b, j: (b, 0, j))
    return pl.pallas_call(
        _hy_pre_kernel,
        grid=(B, nb),
        in_specs=[col(0), col(1), col(2), wsp(0), wsp(1), wsp(2), bsp(0), bsp(1), bsp(2)],
        out_specs=[out, out],
        out_shape=[jax.ShapeDtypeStruct((B, L, D), BF16)] * 2,
        compiler_params=_params("arbitrary", "arbitrary"),
        name="hy_pre",
    )(p_hg, p_hg, p_hg, conv_w, conv_w, conv_w, conv_b, conv_b, conv_b)


def _lmat_kernel(m_ref, x_ref, o_ref, *, precision):
    x = x_ref[0]
    if precision is None:
        x = x.astype(BF16)
    o_ref[0] = _dot(m_ref[...], x, precision).astype(o_ref.dtype)


def _lmat(mat, x, out_dtype, precision, wc, name):
    Bn, K, W = x.shape
    M = mat.shape[0]
    return pl.pallas_call(
        functools.partial(_lmat_kernel, precision=precision),
        grid=(Bn, W // wc),
        in_specs=[pl.BlockSpec((M, K), lambda b, j: (0, 0)),
                  pl.BlockSpec((1, K, wc), lambda b, j: (b, 0, j))],
        out_specs=pl.BlockSpec((1, M, wc), lambda b, j: (b, 0, j)),
        out_shape=jax.ShapeDtypeStruct((Bn, M, W), out_dtype),
        compiler_params=_params("arbitrary", "arbitrary"),
        name=name,
    )(mat, x)


def _hy_s1_kernel(m_ref, x_ref, o_ref, *, precision):
    k, nb = x_ref.shape[1], x_ref.shape[2]
    res = _dot(m_ref[...], x_ref[0].reshape(k, nb * D), precision)
    o_ref[0] = res.reshape(2, res.shape[0] // 2, nb, D).astype(o_ref.dtype)


def _hy_s1(mat, x, nb, out_dtype=BF16, precision=None, name="hy_s1"):
    Bn, K, n2, _ = x.shape
    M = mat.shape[0]
    return pl.pallas_call(
        functools.partial(_hy_s1_kernel, precision=precision),
        grid=(Bn, n2 // nb),
        in_specs=[pl.BlockSpec((M, K), lambda b, j: (0, 0)),
                  pl.BlockSpec((1, K, nb, D), lambda b, j: (b, 0, j, 0))],
        out_specs=pl.BlockSpec((1, 2, M // 2, nb, D), lambda b, j: (b, 0, 0, j, 0)),
        out_shape=jax.ShapeDtypeStruct((Bn, 2, M // 2, n2, D), out_dtype),
        compiler_params=_params("arbitrary", "arbitrary"),
        name=name,
    )(mat, x)


def _hy_s3_kernel(g_ref, z_ref, o_ref):
    two, n1, nb, _ = z_ref.shape[1:]
    conv = _dot(g_ref[...], z_ref[0].reshape(two * n1, nb * D))
    o_ref[0] = conv.reshape(conv.shape[0], nb, D).astype(o_ref.dtype)


def _hy_s3(g_mat, z, nb):
    Bn, _, n1, n2, _ = z.shape
    R = g_mat.shape[0]
    return pl.pallas_call(
        _hy_s3_kernel,
        grid=(Bn, n2 // nb),
        in_specs=[pl.BlockSpec((R, 2 * n1), lambda b, j: (0, 0)),
                  pl.BlockSpec((1, 2, n1, nb, D), lambda b, j: (b, 0, 0, j, 0))],
        out_specs=pl.BlockSpec((1, R, nb, D), lambda b, j: (b, 0, j, 0)),
        out_shape=jax.ShapeDtypeStruct((Bn, R, n2, D), BF16),
        compiler_params=_params("arbitrary", "arbitrary"),
        name="hy_s3",
    )(g_mat, z)


def _cmul(xr, xi, kr, ki):
    return xr * kr - xi * ki, xr * ki + xi * kr


def _hy_spec_kernel(a_ref, m_ref, o_ref):
    a = jnp.concatenate([a_ref[0, 0], a_ref[1, 0]], axis=0)
    o_ref[0] = _dot(m_ref[0], a, HIGHEST)


def _hy_spectrum(a, m_tab):
    n1 = a.shape[1]
    n2 = DFT_N2
    return pl.pallas_call(
        _hy_spec_kernel,
        grid=(n1,),
        in_specs=[pl.BlockSpec((2, 1, n2, D), lambda k: (0, k, 0, 0)),
                  pl.BlockSpec((1, 2 * n2, 2 * n2), lambda k: (k, 0, 0))],
        out_specs=pl.BlockSpec((1, 2 * n2, D), lambda k: (k, 0, 0)),
        out_shape=jax.ShapeDtypeStruct((n1, 2 * n2, D), F32),
        compiler_params=_params("arbitrary"),
        name="hy_spectrum",
    )(a, m_tab)


def _hy_mid_kernel(a_ref, m_ref, mt_ref, kf_ref, o_ref):
    n2 = DFT_N2
    kr = kf_ref[0, :n2]
    ki = kf_ref[0, n2:]

    def body(b, _):
        a = jnp.concatenate([a_ref[b, 0, 0], a_ref[b, 1, 0]], axis=0)
        x = _dot(m_ref[0], a)
        yr, yi = _cmul(x[:n2], x[n2:], kr, ki)
        y = jnp.concatenate([yr, yi], axis=0).astype(BF16)
        z = _dot(mt_ref[0], y)
        o_ref[b, 0, 0] = z[:n2].astype(o_ref.dtype)
        o_ref[b, 1, 0] = z[n2:].astype(o_ref.dtype)
        return 0

    lax.fori_loop(0, a_ref.shape[0], body, 0)


def _hy_mid(a, m_bf, mt_bf, kf):
    B, _, n1, n2, _ = a.shape
    blk = pl.BlockSpec((B, 2, 1, n2, D), lambda k: (0, 0, k, 0, 0))
    tab = pl.BlockSpec((1, 2 * n2, 2 * n2), lambda k: (k, 0, 0))
    return pl.pallas_call(
        _hy_mid_kernel,
        grid=(n1,),
        in_specs=[blk, tab, tab, pl.BlockSpec((1, 2 * n2, D), lambda k: (k, 0, 0))],
        out_specs=blk,
        out_shape=jax.ShapeDtypeStruct(a.shape, BF16),
        compiler_params=_params("arbitrary"),
        name="hy_mid",
    )(a, m_bf, mt_bf, kf)


def _hy_output(x0c, conv, u, bias):
    return x0c * (conv + u * bias)


def _hy_ctx_kernel(x0_ref, x1_ref, v_ref, w0_ref, w1_ref, w2_ref, b0_ref, b1_ref, b2_ref,
                   f_ref, kf_ref, g_ref, bias_ref, o_ref):
    x0 = _short_conv(x0_ref[0], w0_ref[...], b0_ref[...])
    u = _short_conv(x1_ref[0], w1_ref[...], b1_ref[...]) * _short_conv(v_ref[0], w2_ref[...], b2_ref[...])
    x = _dot(f_ref[...], u, HIGHEST)
    nf = x.shape[0] // 2
    yr, yi = _cmul(x[:nf], x[nf:], kf_ref[:nf], kf_ref[nf:])
    conv = _dot(g_ref[...], jnp.concatenate([yr, yi], axis=0), HIGHEST)
    o_ref[0] = _hy_output(x0, conv, u, bias_ref[...]).astype(o_ref.dtype)


def _hy_ctx(p_hg, conv_w, conv_b, f_c, kf_c, g_c, bias, L, Lc):
    B = p_hg.shape[0]
    ct = 256
    nb = D // ct
    rb = L // Lc
    col = lambda k: pl.BlockSpec((1, Lc, ct), lambda b, j: (b, rb, k * nb + j))
    wsp = lambda k: pl.BlockSpec((3, ct), lambda b, j: (0, k * nb + j))
    bsp = lambda k: pl.BlockSpec((1, ct), lambda b, j: (0, k * nb + j))
    return pl.pallas_call(
        _hy_ctx_kernel,
        grid=(B, nb),
        in_specs=[col(0), col(1), col(2), wsp(0), wsp(1), wsp(2), bsp(0), bsp(1), bsp(2),
                  pl.BlockSpec(f_c.shape, lambda b, j: (0, 0)),
                  pl.BlockSpec((kf_c.shape[0], ct), lambda b, j: (0, j)),
                  pl.BlockSpec(g_c.shape, lambda b, j: (0, 0)),
                  pl.BlockSpec((1, ct), lambda b, j: (0, j))],
        out_specs=pl.BlockSpec((1, Lc, ct), lambda b, j: (b, 0, j)),
        out_shape=jax.ShapeDtypeStruct((B, Lc, D), BF16),
        compiler_params=_params("arbitrary", "arbitrary"),
        name="hy_ctx",
    )(p_hg, p_hg, p_hg, conv_w, conv_w, conv_w, conv_b, conv_b, conv_b, f_c, kf_c, g_c, bias)


def _dft_tables(L, Lc):
    n = 2 * L
    n2 = DFT_N2
    n1 = n // n2
    two_pi = 2.0 * math.pi

    def cs(idx, period):
        ang = (idx % period).astype(F32) * (two_pi / period)
        return jnp.cos(ang), jnp.sin(ang)

    k1 = jnp.arange(n1)[:, None]
    c, s = cs(k1 * jnp.arange(n1)[None, :], n1)
    f1_full = jnp.concatenate([c, -s], axis=0)
    f1_half = f1_full[:, :n1 // 2]
    kk1 = jnp.arange(n1)[:, None, None]
    kk2 = jnp.arange(n2)[None, :, None]
    nn2 = jnp.arange(n2)[None, None, :]
    c, s = cs(nn2 * kk1 + n1 * nn2 * kk2, n)
    ci = -s
    m_tab = jnp.concatenate([jnp.concatenate([c, -ci], axis=2), jnp.concatenate([ci, c], axis=2)], axis=1)
    c, s = cs(jnp.arange(n1 // 2)[:, None] * jnp.arange(n1)[None, :], n1)
    g_mat = jnp.concatenate([c, -s], axis=1) / n
    nc = 2 * Lc
    kc = jnp.arange(nc)[:, None]
    c, s = cs(kc * jnp.arange(nc)[None, :], nc)
    fc_full = jnp.concatenate([c, -s], axis=0)
    c, s = cs(jnp.arange(Lc)[:, None] * jnp.arange(nc)[None, :], nc)
    g_c = jnp.concatenate([c, -s], axis=1) / nc
    return dict(f1_full=f1_full, f1_half=f1_half.astype(BF16), m_tab=m_tab, m_bf=m_tab.astype(BF16),
                mt_bf=jnp.swapaxes(m_tab, 1, 2).astype(BF16), g_mat=g_mat.astype(BF16),
                fc_full=fc_full, fc_half=fc_full[:, :Lc], g_c=g_c)


def _hyena(p_hg, hp, tabs, L, Lc):
    B = p_hg.shape[0]
    n2 = DFT_N2
    n1 = 2 * L // n2
    wc = min(16 * D, n2 * D)
    deltas = jnp.abs(jnp.linspace(math.log(HY_TARGET) / HY_FAST_DECAY, math.log(HY_TARGET) / HY_SLOW_DECAY,
                                  D, dtype=F32))[None, :]
    filt = functools.partial(_hy_filter, w1p=hp['w1p'], b1=hp['b1'], w2=hp['w2'], b2=hp['b2'], fr=hp['fr'],
                             w3=hp['w3'], deltas=deltas)
    k_lat = filt(L)
    a_f = _hy_s1(tabs['f1_full'], k_lat.reshape(1, n1, n2, D), wc // D, F32, HIGHEST, "hy_filt_s1")
    kf = _hy_spectrum(a_f[0], tabs['m_tab'])
    k_ctx = filt(Lc)
    kf_c = _lmat(tabs['fc_full'], k_ctx[None], F32, HIGHEST, D, "hy_filt_ctx")[0]
    x0c, u = _hy_pre(p_hg, hp['conv_w'], hp['conv_b'], L)
    a = _hy_s1(tabs['f1_half'], u.reshape(B, n1 // 2, n2, D), wc // D)
    z = _hy_mid(a, tabs['m_bf'], tabs['mt_bf'], kf)
    conv = _hy_s3(tabs['g_mat'], z, wc // D).reshape(B, L, D)
    y_c = _hy_ctx(p_hg, hp['conv_w'], hp['conv_b'], tabs['fc_half'], kf_c, tabs['g_c'], hp['bias'], L, Lc)
    return conv, x0c, u, y_c


def _merge_kernel(x_ref, mod_ref, yd_ref, cv_ref, x0_ref, u_ref, yc_ref, hb_ref, ys_ref, gd_ref, gh_ref, gs_ref,
                  wb_ref, wo_ref, g2_ref, rw_ref, xo_ref, h2_ref, lg_ref, *, n_lat_tiles):
    latent = pl.program_id(1) < n_lat_tiles
    half = x_ref.shape[1] // 2
    for rows in (slice(0, half), slice(half, 2 * half)):
        y_hy = jnp.where(latent,
                         _hy_output(x0_ref[0, rows].astype(F32), cv_ref[0, rows].astype(F32),
                                    u_ref[0, rows].astype(F32), hb_ref[...]),
                         yc_ref[0, rows].astype(F32)).astype(BF16)
        merged = None
        for k, (y, gt_ref) in enumerate(((yd_ref[0, rows], gd_ref), (y_hy, gh_ref), (ys_ref[0, rows], gs_ref))):
            term = jax.nn.sigmoid(gt_ref[0, rows]) * _dot(y, wb_ref[k])
            merged = term if merged is None else merged + term
        y = _dot(merged.astype(BF16), wo_ref[...])
        x = x_ref[0, rows] + mod_ref[0, 2:3, :] * y
        xo_ref[0, rows] = x
        h2 = _rms(x) * g2_ref[...]
        h2 = h2 * (1.0 + mod_ref[0, 4:5, :]) + mod_ref[0, 3:4, :]
        h2_ref[0, rows] = h2.astype(h2_ref.dtype)
        lg_ref[0, :, rows] = _dot(h2, rw_ref[...], HIGHEST).T[:N_EXPERTS]


def _merge(xa, mod, y_diff, hy, hy_bias, y_swa, p_hg, w_branch, w_out, g2, rw_pad, n_lat_tiles):
    B, T, _ = xa.shape
    ctx_row = B
    conv, x0c, u, y_hy_c = hy
    tile = lambda c: pl.BlockSpec((1, TM, D), lambda b, i: (b, i, c))
    lat = pl.BlockSpec((1, TM, D), lambda b, i: (b, jnp.minimum(i, n_lat_tiles - 1), 0))
    return pl.pallas_call(
        functools.partial(_merge_kernel, n_lat_tiles=n_lat_tiles),
        grid=(B, T // TM),
        in_specs=[tile(0),
                  pl.BlockSpec((1, 6, D), lambda b, i: (jnp.where(i >= n_lat_tiles, ctx_row, b), 0, 0)),
                  tile(0), lat, lat, lat,
                  pl.BlockSpec((1, y_hy_c.shape[1], D), lambda b, i: (b, 0, 0)),
                  pl.BlockSpec((1, D), lambda b, i: (0, 0)),
                  tile(0), tile(3), tile(4), tile(5),
                  pl.BlockSpec((3, D, D), lambda b, i: (0, 0, 0)),
                  pl.BlockSpec((D, D), lambda b, i: (0, 0)),
                  pl.BlockSpec((1, D), lambda b, i: (0, 0)),
                  pl.BlockSpec((D, LANES), lambda b, i: (0, 0))],
        out_specs=[tile(0), tile(0), pl.BlockSpec((1, N_EXPERTS, TM), lambda b, i: (b, 0, i))],
        out_shape=[jax.ShapeDtypeStruct((B, T, D), F32), jax.ShapeDtypeStruct((B, T, D), BF16),
                   jax.ShapeDtypeStruct((B, N_EXPERTS, T), F32)],
        compiler_params=_params("arbitrary", "arbitrary"),
        name="merge",
    )(xa, mod, y_diff, conv, x0c, u, y_hy_c, hy_bias, y_swa, p_hg, p_hg, p_hg, w_branch, w_out, g2, rw_pad)


def _router_kernel(lg_ref, slot_ref, gate_ref, first_ref, *, segments):
    lg = lg_ref[0]
    e = jnp.exp(lg - jnp.max(lg, axis=0, keepdims=True))
    aff = e / jnp.sum(e, axis=0, keepdims=True)
    r = lax.broadcasted_iota(jnp.int32, (LANES, LANES), 0)
    c = lax.broadcasted_iota(jnp.int32, (LANES, LANES), 1)
    before = (r < c).astype(BF16)

    def prefix_count(mask, n):
        run = jnp.zeros((N_EXPERTS, 1), F32)
        outs = []
        for j in range(n // LANES):
            blk = mask[:, j * LANES:(j + 1) * LANES].astype(F32).astype(BF16)
            outs.append(_dot(blk, before) + run)
            run = run + jnp.sum(blk.astype(F32), axis=1, keepdims=True)
        return jnp.concatenate(outs, axis=1)

    for start, n, cap, slot_base in segments:
        a = aff[:, start:start + n]
        bits = pltpu.bitcast(a, jnp.int32)

        def search(_, bounds):
            lo, hi = bounds
            mid = lo + ((hi - lo + 1) >> 1)
            cnt = jnp.sum((bits >= mid).astype(F32), axis=1, keepdims=True)
            ok = cnt >= cap
            return jnp.where(ok, mid, lo), jnp.where(ok, hi, mid - 1)

        lo0 = jnp.zeros((N_EXPERTS, 1), jnp.int32)
        hi0 = jnp.full((N_EXPERTS, 1), 0x7F800000, jnp.int32)
        thr, _ = lax.fori_loop(0, 32, search, (lo0, hi0))
        above = bits > thr
        tie = bits == thr
        need = cap - jnp.sum(above.astype(F32), axis=1, keepdims=True)
        sel = above | (tie & (prefix_count(tie, n) < need))
        pos = prefix_count(sel, n)
        slot_ref[0, :, start:start + n] = jnp.where(sel, pos + slot_base, -1.0)
        gate_ref[0, :, start:start + n] = jnp.where(sel, a, 0.0)
        if start == 0:
            t_idx = lax.broadcasted_iota(jnp.int32, (n, LANES), 0)
            tile_start = lax.broadcasted_iota(jnp.int32, (n, LANES), 1) * TM
            before_tile = jnp.where(t_idx < tile_start, 1.0, 0.0).astype(BF16)
            first_ref[0] = _dot(jnp.where(sel, 1.0, 0.0).astype(BF16), before_tile)


def _router(logits_t, segments):
    B, E, T = logits_t.shape
    blk = pl.BlockSpec((1, E, T), lambda b: (b, 0, 0))
    return pl.pallas_call(
        functools.partial(_router_kernel, segments=segments),
        grid=(B,),
        in_specs=[blk],
        out_specs=[blk, blk, pl.BlockSpec((1, E, LANES), lambda b: (b, 0, 0))],
        out_shape=[jax.ShapeDtypeStruct((B, E, T), F32)] * 2 + [jax.ShapeDtypeStruct((B, E, LANES), F32)],
        compiler_params=_params("arbitrary"),
        name="router",
    )(logits_t)


def _slot_window(first, align, rows, n_slots):
    return pl.multiple_of(jnp.minimum(first // align * align, n_slots - rows), align)


def _moe_ffn_kernel(first_ref, slot_ref, gate_ref, h2_ref, w1_ref, w3_ref, w2_ref, o_ref, xg_scr, g_scr,
                    *, n_slots, n_tiles):
    base = (pl.program_id(0) * N_EXPERTS + pl.program_id(1)) * n_tiles
    rows = min(TM + 8, n_slots)
    row_id = lax.broadcasted_iota(jnp.int32, (rows, TM), 0)
    xg_scr[...] = jnp.zeros_like(xg_scr)
    g_scr[...] = jnp.zeros_like(g_scr)
    for c in range(n_tiles):
        tok = slice(c * TM, (c + 1) * TM)
        w0 = _slot_window(first_ref[base + c], 8, rows, n_slots)
        hit = slot_ref[0, 0, :, tok] == (row_id + w0).astype(F32)
        win = pl.ds(w0, rows)
        xg_scr[win, :] += _dot(jnp.where(hit, 1.0, 0.0).astype(BF16), h2_ref[0, tok, :])
        g_scr[win, :] += jnp.sum(jnp.where(hit, gate_ref[0, 0, :, tok], 0.0), axis=1, keepdims=True)
    xb = xg_scr[...].astype(BF16)
    a = _dot(xb, w1_ref[0, 0])
    b = _dot(xb, w3_ref[0, 0])
    hidden = (a * jax.nn.sigmoid(a) * b).astype(BF16)
    o_ref[0, 0] = (_dot(hidden, w2_ref[0, 0]) * g_scr[...]).astype(o_ref.dtype)


def _moe_ffn(first, slot, gate, h2, w1, w3, w2, l, n_slots):
    B, E, T = slot.shape
    row = pl.BlockSpec((1, 1, 1, T), lambda b, e: (b, e, 0, 0))
    wsp = pl.BlockSpec((1, 1, D, D), lambda b, e: (l, e, 0, 0))
    return pl.pallas_call(
        functools.partial(_moe_ffn_kernel, n_slots=n_slots, n_tiles=T // TM),
        grid=(B, E),
        in_specs=[pl.BlockSpec(memory_space=pltpu.SMEM), row, row,
                  pl.BlockSpec((1, T, D), lambda b, e: (b, 0, 0)), wsp, wsp, wsp],
        out_specs=pl.BlockSpec((1, 1, n_slots, D), lambda b, e: (b, e, 0, 0)),
        out_shape=jax.ShapeDtypeStruct((B, E, n_slots, D), BF16),
        scratch_shapes=[pltpu.VMEM((n_slots, D), F32), pltpu.VMEM((n_slots, 1), F32)],
        compiler_params=_params("arbitrary", "arbitrary"),
        name="moe_ffn",
    )(first, slot.reshape(B, E, 1, T), gate.reshape(B, E, 1, T), h2, w1, w3, w2)


def _moe_combine_kernel(first_ref, x_ref, mod_ref, slot_ref, y_ref, o_ref, *, n_slots, n_tiles):
    tm = x_ref.shape[1]
    rows = min(tm + 16, n_slots)
    col_id = lax.broadcasted_iota(jnp.int32, (tm, rows), 1)
    acc = jnp.zeros((tm, D), F32)
    for e in range(N_EXPERTS):
        w0 = _slot_window(first_ref[(pl.program_id(0) * N_EXPERTS + e) * n_tiles + pl.program_id(1)], 16, rows,
                          n_slots)
        hit = slot_ref[0, :, e:e + 1] == (col_id + w0).astype(F32)
        acc = acc + _dot(jnp.where(hit, 1.0, 0.0).astype(BF16), y_ref[0, e, pl.ds(w0, rows), :])
    o_ref[0] = x_ref[0] + mod_ref[0, 5:6, :] * acc


def _moe_combine(first, xa, mod, slot_t, y, n_lat_tiles, n_slots):
    B, T, _ = xa.shape
    ctx_row = B
    tile = pl.BlockSpec((1, TM, D), lambda b, i: (b, i, 0))
    return pl.pallas_call(
        functools.partial(_moe_combine_kernel, n_slots=n_slots, n_tiles=T // TM),
        grid=(B, T // TM),
        in_specs=[pl.BlockSpec(memory_space=pltpu.SMEM), tile,
                  pl.BlockSpec((1, 6, D), lambda b, i: (jnp.where(i >= n_lat_tiles, ctx_row, b), 0, 0)),
                  pl.BlockSpec((1, TM, N_EXPERTS), lambda b, i: (b, i, 0)),
                  pl.BlockSpec((1, N_EXPERTS, n_slots, D), lambda b, i: (b, 0, 0, 0))],
        out_specs=tile,
        out_shape=jax.ShapeDtypeStruct((B, T, D), F32),
        compiler_params=_params("arbitrary", "arbitrary"),
        name="moe_combine",
    )(first, xa, mod, slot_t, y)


def _final_kernel(x_ref, g_ref, o_ref):
    o_ref[0] = _rms(x_ref[0]) * g_ref[...]


def _final_norm(xa, g, L):
    B = xa.shape[0]
    tile = pl.BlockSpec((1, TM, D), lambda b, i: (b, i, 0))
    return pl.pallas_call(
        _final_kernel,
        grid=(B, L // TM),
        in_specs=[tile, pl.BlockSpec((1, D), lambda b, i: (0, 0))],
        out_specs=tile,
        out_shape=jax.ShapeDtypeStruct((B, L, D), F32),
        compiler_params=_params("arbitrary", "arbitrary"),
        name="final_norm",
    )(xa, g)


def _swa_head_perm():
    return [kv * SWA_GROUP + g for g in range(SWA_GROUP) for kv in range(SWA_KV)]


def _split_w_in(w_in):
    depth = w_in.shape[0]
    o = [0, D, 2 * D, 3 * D, 6 * D, 7 * D, 7 * D + SWA_KV * HEAD, 7 * D + 2 * SWA_KV * HEAD, 10 * D + 2 * SWA_KV * HEAD]
    dq, dk, dv, hy, sq, sk, sv, gt = (w_in[:, :, o[k]:o[k + 1]] for k in range(8))
    scale = HEAD ** -0.5 * math.log2(math.e)
    sq = sq.reshape(depth, D, SWA_KV * SWA_GROUP, HEAD)[:, :, jnp.array(_swa_head_perm())].reshape(depth, D, D)
    w_attn = jnp.concatenate([dq * scale, dk, dv, sq * scale, sk, sv], axis=-1).astype(BF16)
    w_hg = jnp.concatenate([hy, gt], axis=-1).astype(BF16)
    return w_attn, w_hg


def kernel(x, c, ctx, c_ctx, w_mod, b_mod, norm1_g, norm2_g, w_in, diff_lambda, diff_subln_g, hy_conv_w,
           hy_conv_b, hy_ff_w1, hy_ff_b1, hy_ff_w2, hy_ff_b2, hy_ff_w3, hy_sin_freq, hy_bias, swa_sink,
           w_branch, w_out, router_w, moe_w1, moe_w3, moe_w2, final_g):
    B, L, _ = x.shape
    Lc = ctx.shape[1]
    depth = w_mod.shape[0]
    T = L + Lc
    assert Lc == TM and L % (4 * TM) == 0 and x.shape[2] == D
    n_lat_tiles = L // TM
    cap_l = CAPACITY_FACTOR * L // N_EXPERTS
    cap_c = CAPACITY_FACTOR * Lc // N_EXPERTS
    n_slots = cap_l + cap_c
    segments = ((0, L, cap_l, 0), (L, Lc, cap_c, cap_l))

    xa = jnp.concatenate([x, ctx], axis=1)
    mods = _mods(c, c_ctx, w_mod, b_mod)
    cos_t, sin_t = _rope_tables(L, Lc)
    tabs = _dft_tables(L, Lc)
    w_attn, w_hg = _split_w_in(w_in)
    rope_flags = tuple([True] * 16 + [False] * 8 + [True] * 8 + [True] * 2 + [False] * 2)
    perm = jnp.array(_swa_head_perm())
    wb = w_branch.at[:, 2].set(
        w_branch[:, 2].reshape(depth, SWA_KV * SWA_GROUP, HEAD, D)[:, perm].reshape(depth, D, D)).astype(BF16)
    wo = w_out.astype(BF16)
    w1, w3, w2 = moe_w1.astype(BF16), moe_w3.astype(BF16), moe_w2.astype(BF16)
    rw_pad = jnp.pad(router_w, ((0, 0), (0, 0), (0, LANES - N_EXPERTS)))
    w1p = jnp.pad(hy_ff_w1, ((0, 0), (0, LANES - HY_EMB), (0, 0)))

    for l in range(depth):
        lambda_init = 0.8 - 0.6 * math.exp(-0.3 * l)
        mod = mods[l]
        p_attn = _norm_proj(xa, mod, norm1_g[l][None], w_attn[l], cos_t, sin_t, rope_flags, BF16,
                            n_lat_tiles, "proj_attn")
        p_hg = _norm_proj(xa, mod, norm1_g[l][None], w_hg[l], cos_t, sin_t, (False,) * (N_HG // LANES), F32,
                          n_lat_tiles, "proj_hg")
        y_diff = _diff_attn(p_attn, diff_lambda, diff_subln_g, L, Lc, lambda_init, l)
        y_swa = _swa(p_attn, swa_sink, L, Lc, l)
        hp = dict(w1p=w1p[l], b1=hy_ff_b1[l][None], w2=hy_ff_w2[l], b2=hy_ff_b2[l][None],
                  fr=hy_sin_freq[l][None], w3=hy_ff_w3[l], conv_w=hy_conv_w[l], conv_b=hy_conv_b[l][None],
                  bias=hy_bias[l][None])
        hy = _hyena(p_hg, hp, tabs, L, Lc)
        xa, h2, logits_t = _merge(xa, mod, y_diff, hy, hp['bias'], y_swa, p_hg, wb[l], wo[l], norm2_g[l][None],
                                  rw_pad[l], n_lat_tiles)
        slot, gate, first = _router(logits_t, segments)
        first = jnp.concatenate([first[:, :, :n_lat_tiles], jnp.full((B, N_EXPERTS, 1), cap_l, F32)], axis=2)
        first = first.astype(jnp.int32).reshape(-1)
        y_e = _moe_ffn(first, slot, gate, h2, w1, w3, w2, l, n_slots)
        xa = _moe_combine(first, xa, mod, jnp.swapaxes(slot, 1, 2), y_e, n_lat_tiles, n_slots)
    return _final_norm(xa, final_g[None], L)
```

```python
import functools
import math

import jax
import jax.numpy as jnp
from jax import lax
from jax.experimental import pallas as pl
from jax.experimental.pallas import tpu as pltpu

F32 = jnp.float32
BF16 = jnp.bfloat16
HIGHEST = lax.Precision.HIGHEST

D = 1024
HEAD = 64
GRID_W = 64
ROPE_BASE = 10000.0
EPS = 1e-6
N_DIFF_HEADS = D // (2 * HEAD)
DIFF_V = 2 * HEAD
SWA_KV = 4
SWA_GROUP = 4
WINDOW = 128
N_EXPERTS = 16
CAPACITY_FACTOR = 2
HY_BANDS = 16
HY_EMB = 1 + 2 * HY_BANDS
HY_FF = 64
HY_FAST_DECAY = 0.3
HY_SLOW_DECAY = 1.5
HY_TARGET = 1e-2

LANES = 128
TM = 256
DFT_N2 = 128
VMEM_LIMIT = 56 * 1024 * 1024

N_ATTN = 3 * D + D + 2 * SWA_KV * HEAD
N_HG = 3 * D + 3 * D


def _params(*sem):
    return pltpu.CompilerParams(dimension_semantics=sem, vmem_limit_bytes=VMEM_LIMIT)


def _dot(a, b, precision=None):
    return jnp.dot(a, b, preferred_element_type=F32, precision=precision)


def _dot_nt(a, b, precision=None):
    return lax.dot_general(a, b, (((1,), (1,)), ((), ())), preferred_element_type=F32, precision=precision)


def _rms(x):
    return x * lax.rsqrt(jnp.mean(x * x, axis=-1, keepdims=True) + EPS)


def _mod_kernel(c_ref, w_ref, b_ref, o_ref):
    c = c_ref[...]
    sc = c * jax.nn.sigmoid(c)
    o_ref[0] = _dot(sc, w_ref[0], HIGHEST) + b_ref[0]


def _mods(c, c_ctx, w_mod, b_mod):
    B = c.shape[0]
    depth = w_mod.shape[0]
    rows = -(-(B + 1) // 8) * 8
    cc = jnp.zeros((rows, D), F32).at[:B].set(c).at[B].set(c_ctx)
    out = pl.pallas_call(
        _mod_kernel,
        grid=(depth, 6),
        in_specs=[pl.BlockSpec((rows, D), lambda l, j: (0, 0)),
                  pl.BlockSpec((1, D, D), lambda l, j: (l, 0, j)),
                  pl.BlockSpec((1, 1, D), lambda l, j: (l, 0, j))],
        out_specs=pl.BlockSpec((1, rows, D), lambda l, j: (l, 0, j)),
        out_shape=jax.ShapeDtypeStruct((depth, rows, 6 * D), F32),
        compiler_params=_params("arbitrary", "arbitrary"),
        name="mods",
    )(cc, w_mod, b_mod.reshape(depth, 1, 6 * D))
    return out.reshape(depth, rows, 6, D)


def _norm_proj_kernel(x_ref, mod_ref, g_ref, w_ref, cos_ref, sin_ref, o_ref, *, rope_flags):
    x = x_ref[0]
    h = _rms(x) * g_ref[...]
    h = h * (1.0 + mod_ref[0, 1:2, :]) + mod_ref[0, 0:1, :]
    hb = h.astype(BF16)
    n_sub = len(rope_flags)
    sub_per_chunk = 4
    if any(rope_flags):
        cos = cos_ref[...]
        sin = sin_ref[...]
        lane = lax.broadcasted_iota(jnp.int32, cos.shape, 1)
        first_half = (lane % (HEAD // 2)) < (HEAD // 4)
    for c0 in range(0, n_sub, sub_per_chunk):
        c1 = min(c0 + sub_per_chunk, n_sub)
        acc = _dot(hb, w_ref[:, c0 * LANES:c1 * LANES])
        for s in range(c0, c1):
            a = acc[:, (s - c0) * LANES:(s - c0 + 1) * LANES]
            if rope_flags[s]:
                partner = jnp.where(first_half, pltpu.roll(a, LANES - HEAD // 4, 1), pltpu.roll(a, HEAD // 4, 1))
                a = a * cos + partner * sin
            o_ref[0, :, s * LANES:(s + 1) * LANES] = a.astype(o_ref.dtype)


def _norm_proj(xa, mod, g, w, cos_t, sin_t, rope_flags, out_dtype, n_lat_tiles, name):
    B, T, _ = xa.shape
    N = w.shape[1]
    ctx_row = B
    return pl.pallas_call(
        functools.partial(_norm_proj_kernel, rope_flags=rope_flags),
        grid=(B, T // TM),
        in_specs=[pl.BlockSpec((1, TM, D), lambda b, i: (b, i, 0)),
                  pl.BlockSpec((1, 6, D), lambda b, i: (jnp.where(i >= n_lat_tiles, ctx_row, b), 0, 0)),
                  pl.BlockSpec((1, D), lambda b, i: (0, 0)),
                  pl.BlockSpec((D, N), lambda b, i: (0, 0)),
                  pl.BlockSpec((TM, LANES), lambda b, i: (i, 0)),
                  pl.BlockSpec((TM, LANES), lambda b, i: (i, 0))],
        out_specs=pl.BlockSpec((1, TM, N), lambda b, i: (b, i, 0)),
        out_shape=jax.ShapeDtypeStruct((B, T, N), out_dtype),
        compiler_params=_params("arbitrary", "arbitrary"),
        name=name,
    )(xa, mod, g, w, cos_t, sin_t)


def _rope_tables(L, Lc):
    t = jnp.arange(L)
    pos = jnp.stack([(t // GRID_W).astype(F32), (t % GRID_W).astype(F32)], axis=1)
    n_freq = HEAD // 4
    inv = ROPE_BASE ** (-jnp.arange(n_freq, dtype=F32) / n_freq)
    ang = pos[:, :, None] * inv
    cos = jnp.cos(ang)
    sin = jnp.sin(ang)
    cos_h = jnp.concatenate([cos, cos], axis=-1).reshape(L, HEAD)
    sin_h = jnp.concatenate([-sin, sin], axis=-1).reshape(L, HEAD)
    cos_t = jnp.concatenate([jnp.tile(cos_h, (1, LANES // HEAD)), jnp.ones((Lc, LANES), F32)], axis=0)
    sin_t = jnp.concatenate([jnp.tile(sin_h, (1, LANES // HEAD)), jnp.zeros((Lc, LANES), F32)], axis=0)
    return cos_t, sin_t


def _lane_block_reduce(x, op):
    out = x[:, :LANES]
    for j in range(1, x.shape[1] // LANES):
        out = op(out, x[:, j * LANES:(j + 1) * LANES])
    return out


def _diff_attn_kernel(lam_ref, q_ref, k_ref, v_ref, g_ref, o_ref, s_lat, s_ctx, m_run, qq_scr, l_scr, acc_scr,
                      *, L, Lc, tg, lambda_init):
    tq = TM
    n_groups = (L + Lc) // tg
    n_lat = L // tq
    n_tiles = (L + Lc) // tq
    rows = (2 * tq, LANES)
    lowest = jnp.full(rows, -jnp.inf, F32)
    nothing = jnp.zeros(rows, F32)
    lp = lam_ref[0]
    lam = (jnp.exp(jnp.sum(lp[0:1] * lp[1:2], axis=-1, keepdims=True))
           - jnp.exp(jnp.sum(lp[2:3] * lp[3:4], axis=-1, keepdims=True)) + lambda_init)

    def stage(t, _):
        cur = t % 2
        prv = 1 - cur
        score_any = t < n_tiles
        score_lat = t < n_lat
        weigh_any = t >= 1
        weigh_lat = jnp.logical_and(weigh_any, t <= n_lat)

        @pl.when(score_any)
        def _():
            q = q_ref[0, pl.ds(pl.multiple_of(t * tq, tq), tq), :]
            lane = lax.broadcasted_iota(jnp.int32, q.shape, 1)
            zero = jnp.zeros_like(q)
            qq_scr[...] = jnp.concatenate([jnp.where(lane < HEAD, q, zero), jnp.where(lane >= HEAD, q, zero)],
                                          axis=0)

        @pl.when(jnp.logical_and(score_any, jnp.logical_not(score_lat)))
        def _():
            s_c = _dot_nt(qq_scr[...], k_ref[0, L:L + Lc, :])
            s_ctx[...] = s_c
            m_run[cur] = _lane_block_reduce(s_c, jnp.maximum)

        def row_max():
            return jnp.max(m_run[prv], axis=-1, keepdims=True)

        @pl.when(jnp.logical_and(weigh_any, jnp.logical_not(weigh_lat)))
        def _():
            p = jnp.exp2(s_ctx[...] - row_max())
            l_scr[...] = _lane_block_reduce(p, jnp.add)
            acc_scr[...] = _dot(p.astype(BF16), v_ref[0, L:L + Lc, :])

        def scores(gi, mrun):
            start = pl.multiple_of(gi * tg, LANES)
            s = _dot_nt(qq_scr[...], k_ref[0, pl.ds(start, tg), :])
            s_lat[cur, gi] = s
            return jnp.maximum(mrun, _lane_block_reduce(s, jnp.maximum))

        def weighted(gi, lrun, acc, m):
            start = pl.multiple_of(gi * tg, LANES)
            p = jnp.exp2(s_lat[prv, gi] - m)
            return (lrun + _lane_block_reduce(p, jnp.add),
                    acc + _dot(p.astype(BF16), v_ref[0, pl.ds(start, tg), :]))

        @pl.when(jnp.logical_and(score_lat, weigh_lat))
        def _():
            m = row_max()

            def both(gi, carry):
                mrun, lrun, acc = carry
                lrun, acc = weighted(gi, lrun, acc, m)
                return scores(gi, mrun), lrun, acc

            m_run[cur], l_scr[...], acc_scr[...] = lax.fori_loop(0, n_groups, both, (lowest, nothing, nothing))

        @pl.when(jnp.logical_and(score_lat, jnp.logical_not(weigh_lat)))
        def _():
            m_run[cur] = lax.fori_loop(0, n_groups, scores, lowest)

        @pl.when(jnp.logical_and(jnp.logical_not(score_lat), weigh_lat))
        def _():
            m = row_max()
            l_scr[...], acc_scr[...] = lax.fori_loop(0, n_groups, lambda gi, c: weighted(gi, c[0], c[1], m),
                                                      (nothing, nothing))

        @pl.when(weigh_any)
        def _():
            o = acc_scr[...] / jnp.sum(l_scr[...], axis=-1, keepdims=True)
            o = _rms(o[:tq] - lam * o[tq:])
            out_rows = pl.ds(pl.multiple_of((t - 1) * tq, tq), tq)
            o_ref[0, out_rows, :] = (o * g_ref[0] * (1.0 - lambda_init)).astype(o_ref.dtype)

        return 0

    lax.fori_loop(0, n_tiles + 1, stage, 0)


def _diff_attn(p_attn, diff_lambda, subln_g, L, Lc, lambda_init, l):
    B, T, _ = p_attn.shape
    tg = T // 2
    nh = N_DIFF_HEADS
    head = lambda c0: pl.BlockSpec((1, T, DIFF_V), lambda b, h: (b, 0, c0 + h))
    return pl.pallas_call(
        functools.partial(_diff_attn_kernel, L=L, Lc=Lc, tg=tg, lambda_init=lambda_init),
        grid=(B, nh),
        in_specs=[pl.BlockSpec((1, 4, HEAD), lambda b, h: (l, 0, 0)),
                  head(0), head(nh), head(2 * nh),
                  pl.BlockSpec((1, 1, DIFF_V), lambda b, h: (l, 0, 0))],
        out_specs=head(0),
        out_shape=jax.ShapeDtypeStruct((B, T, D), BF16),
        scratch_shapes=[pltpu.VMEM((2, T // tg, 2 * TM, tg), F32), pltpu.VMEM((2 * TM, Lc), F32),
                        pltpu.VMEM((2, 2 * TM, LANES), F32), pltpu.VMEM((2 * TM, DIFF_V), BF16),
                        pltpu.VMEM((2 * TM, LANES), F32), pltpu.VMEM((2 * TM, DIFF_V), F32)],
        compiler_params=_params("arbitrary", "arbitrary"),
        name="diff_attn",
    )(diff_lambda, p_attn, p_attn, p_attn, subln_g.reshape(-1, 1, DIFF_V))


def _swa_kernel(sink_ref, q_ref, k_ref, v_ref, o_ref, *, L, Lc, l):
    i = pl.program_id(1)
    tq = q_ref.shape[1]
    kvw = SWA_KV * HEAD
    win = tq + 2 * WINDOW
    lane_head = lax.broadcasted_iota(jnp.int32, (tq, kvw), 1) // HEAD
    k_ctx = k_ref[0, L:L + Lc, :]
    v_ctx = v_ref[0, L:L + Lc, :]

    def run(latent):
        if latent:
            start = pl.multiple_of(jnp.clip(i * tq - WINDOW, 0, L - win), WINDOW)
            k_win = k_ref[0, pl.ds(start, win), :]
            v_win = v_ref[0, pl.ds(start, win), :]
            q_pos = i * tq + lax.broadcasted_iota(jnp.int32, (tq, win), 0)
            k_pos = start + lax.broadcasted_iota(jnp.int32, (tq, win), 1)
            band_bias = jnp.where(jnp.abs(k_pos - q_pos) <= WINDOW, 0.0, -jnp.inf)
        for g in range(SWA_GROUP):
            qg = q_ref[0, :, g * kvw:(g + 1) * kvw]
            acc = jnp.zeros((tq, kvw), F32)
            for kv in range(SWA_KV):
                qe = jnp.where(lane_head == kv, qg, jnp.zeros_like(qg))
                snk = sink_ref[l, kv * SWA_GROUP + g] * math.log2(math.e)
                s_c = _dot_nt(qe, k_ctx)
                m = jnp.maximum(jnp.max(s_c, axis=-1, keepdims=True), snk)
                if latent:
                    s_w = _dot_nt(qe, k_win) + band_bias
                    m = jnp.maximum(m, jnp.max(s_w, axis=-1, keepdims=True))
                p_c = jnp.exp2(s_c - m)
                den = jnp.sum(p_c, axis=-1, keepdims=True) + jnp.exp2(snk - m)
                o = _dot(p_c.astype(BF16), v_ctx)
                if latent:
                    p_w = jnp.exp2(s_w - m)
                    den = den + jnp.sum(p_w, axis=-1, keepdims=True)
                    o = o + _dot(p_w.astype(BF16), v_win)
                acc = acc + jnp.where(lane_head == kv, o / den, 0.0)
            o_ref[0, :, g * kvw:(g + 1) * kvw] = acc.astype(o_ref.dtype)

    @pl.when(i < L // tq)
    def _():
        run(True)

    @pl.when(i >= L // tq)
    def _():
        run(False)


def _swa(p_attn, sink, L, Lc, l):
    B, T, _ = p_attn.shape
    kvw = SWA_KV * HEAD
    return pl.pallas_call(
        functools.partial(_swa_kernel, L=L, Lc=Lc, l=l),
        grid=(B, T // TM),
        in_specs=[pl.BlockSpec(memory_space=pltpu.SMEM),
                  pl.BlockSpec((1, TM, D), lambda b, i: (b, i, 3)),
                  pl.BlockSpec((1, T, kvw), lambda b, i: (b, 0, 4 * D // kvw)),
                  pl.BlockSpec((1, T, kvw), lambda b, i: (b, 0, 4 * D // kvw + 1))],
        out_specs=pl.BlockSpec((1, TM, D), lambda b, i: (b, i, 0)),
        out_shape=jax.ShapeDtypeStruct((B, T, D), BF16),
        compiler_params=_params("arbitrary", "arbitrary"),
        name="swa",
    )(sink, p_attn, p_attn, p_attn)


def _hy_filter_kernel(emb_ref, w1_ref, b1_ref, w2_ref, b2_ref, fr_ref, w3f_ref, w3b_ref, dl_ref, o_ref, *, L):
    rows = TM
    fr = fr_ref[...]

    def raw(i, total):
        r0 = pl.multiple_of(i * rows, rows)
        e = emb_ref[pl.ds(r0, rows), :]
        z = jnp.sin(fr * (_dot(e, w1_ref[...], HIGHEST) + b1_ref[...]))
        z = jnp.sin(fr * (_dot(z, w2_ref[...], HIGHEST) + b2_ref[...]))
        row = r0 + lax.broadcasted_iota(jnp.int32, (rows, 1), 0)
        h = jnp.where(row < L, _dot(z, w3f_ref[...], HIGHEST), _dot(z, w3b_ref[...], HIGHEST))
        h = h * jnp.exp(-e[:, 0:1] * dl_ref[...])
        o_ref[pl.ds(r0, rows), :] = h
        return total + jnp.sum(jnp.abs(h), axis=0, keepdims=True)

    n_chunks = emb_ref.shape[0] // rows
    norm = lax.fori_loop(0, n_chunks, raw, jnp.zeros((1, o_ref.shape[1]), F32)) + EPS

    def scale(i, _):
        r0 = pl.multiple_of(i * rows, rows)
        row = r0 + lax.broadcasted_iota(jnp.int32, (rows, 1), 0)
        o_ref[pl.ds(r0, rows), :] = jnp.where(row == L, 0.0, o_ref[pl.ds(r0, rows), :] / norm)
        return 0

    lax.fori_loop(0, n_chunks, scale, 0)


def _hy_embedding(L):
    n = jnp.arange(2 * L)
    lag = jnp.where(n < L, n, jnp.where(n == L, 0, 2 * L - n))
    t = jnp.linspace(0.0, 1.0, L, dtype=F32)[lag][:, None]
    w = (2.0 * math.pi * jnp.arange(L, dtype=F32) / L)[lag][:, None]
    f = jnp.linspace(1e-4, HY_BANDS - 1, HY_BANDS, dtype=F32)[None, :]
    emb = jnp.concatenate([t, jnp.cos(f * w), -jnp.sin(f * w)], axis=-1)
    return jnp.pad(emb, ((0, 0), (0, LANES - HY_EMB)))


def _hy_filter(L, w1p, b1, w2, b2, fr, w3, deltas):
    ct = 512
    emb = _hy_embedding(L)
    full = lambda shape: pl.BlockSpec(shape, lambda j: (0, 0))
    return pl.pallas_call(
        functools.partial(_hy_filter_kernel, L=L),
        grid=(D // ct,),
        in_specs=[full((2 * L, LANES)), full((LANES, HY_FF)), full((1, HY_FF)), full((HY_FF, HY_FF)),
                  full((1, HY_FF)), full((1, HY_FF)),
                  pl.BlockSpec((HY_FF, ct), lambda j: (0, j)),
                  pl.BlockSpec((HY_FF, ct), lambda j: (0, D // ct + j)),
                  pl.BlockSpec((1, ct), lambda j: (0, j))],
        out_specs=pl.BlockSpec((2 * L, ct), lambda j: (0, j)),
        out_shape=jax.ShapeDtypeStruct((2 * L, D), F32),
        compiler_params=_params("arbitrary"),
        name="hy_filter",
    )(emb, w1p, b1, w2, b2, fr, w3, w3, deltas)


def _short_conv(x, w, b):
    n = x.shape[0]
    row = lax.broadcasted_iota(jnp.int32, x.shape, 0)
    prev = jnp.where(row == 0, 0.0, pltpu.roll(x, 1, 0))
    nxt = jnp.where(row == n - 1, 0.0, pltpu.roll(x, n - 1, 0))
    return prev * w[0:1] + x * w[1:2] + nxt * w[2:3] + b


def _hy_pre_kernel(x0_ref, x1_ref, v_ref, w0_ref, w1_ref, w2_ref, b0_ref, b1_ref, b2_ref, x0c_ref, u_ref):
    x0c_ref[0] = _short_conv(x0_ref[0], w0_ref[...], b0_ref[...]).astype(x0c_ref.dtype)
    u = _short_conv(x1_ref[0], w1_ref[...], b1_ref[...]) * _short_conv(v_ref[0], w2_ref[...], b2_ref[...])
    u_ref[0] = u.astype(u_ref.dtype)


def _hy_pre(p_hg, conv_w, conv_b, L):
    B = p_hg.shape[0]
    ct = LANES
    nb = D // ct
    col = lambda k: pl.BlockSpec((1, L, ct), lambda b, j: (b, 0, k * nb + j))
    wsp = lambda k: pl.BlockSpec((3, ct), lambda b, j: (0, k * nb + j))
    bsp = lambda k: pl.BlockSpec((1, ct), lambda b, j: (0, k * nb + j))
    out = pl.BlockSpec((1, L, ct), lambda b, j: (b, 0, j))
    return pl.pallas_call(
        _hy_pre_kernel,
        grid=(B, nb),
        in_specs=[col(0), col(1), col(2), wsp(0), wsp(1), wsp(2), bsp(0), bsp(1), bsp(2)],
        out_specs=[out, out],
        out_shape=[jax.ShapeDtypeStruct((B, L, D), BF16)] * 2,
        compiler_params=_params("arbitrary", "arbitrary"),
        name="hy_pre",
    )(p_hg, p_hg, p_hg, conv_w, conv_w, conv_w, conv_b, conv_b, conv_b)


def _lmat_kernel(m_ref, x_ref, o_ref, *, precision):
    x = x_ref[0]
    if precision is None:
        x = x.astype(BF16)
    o_ref[0] = _dot(m_ref[...], x, precision).astype(o_ref.dtype)


def _lmat(mat, x, out_dtype, precision, wc, name):
    Bn, K, W = x.shape
    M = mat.shape[0]
    return pl.pallas_call(
        functools.partial(_lmat_kernel, precision=precision),
        grid=(Bn, W // wc),
        in_specs=[pl.BlockSpec((M, K), lambda b, j: (0, 0)),
                  pl.BlockSpec((1, K, wc), lambda b, j: (b, 0, j))],
        out_specs=pl.BlockSpec((1, M, wc), lambda b, j: (b, 0, j)),
        out_shape=jax.ShapeDtypeStruct((Bn, M, W), out_dtype),
        compiler_params=_params("arbitrary", "arbitrary"),
        name=name,
    )(mat, x)


def _hy_s1_kernel(m_ref, x_ref, o_ref, *, precision):
    k, nb = x_ref.shape[1], x_ref.shape[2]
    res = _dot(m_ref[...], x_ref[0].reshape(k, nb * D), precision)
    o_ref[0] = res.reshape(2, res.shape[0] // 2, nb, D).astype(o_ref.dtype)


def _hy_s1(mat, x, nb, out_dtype=BF16, precision=None, name="hy_s1"):
    Bn, K, n2, _ = x.shape
    M = mat.shape[0]
    return pl.pallas_call(
        functools.partial(_hy_s1_kernel, precision=precision),
        grid=(Bn, n2 // nb),
        in_specs=[pl.BlockSpec((M, K), lambda b, j: (0, 0)),
                  pl.BlockSpec((1, K, nb, D), lambda b, j: (b, 0, j, 0))],
        out_specs=pl.BlockSpec((1, 2, M // 2, nb, D), lambda b, j: (b, 0, 0, j, 0)),
        out_shape=jax.ShapeDtypeStruct((Bn, 2, M // 2, n2, D), out_dtype),
        compiler_params=_params("arbitrary", "arbitrary"),
        name=name,
    )(mat, x)


def _hy_s3_kernel(g_ref, z_ref, o_ref):
    two, n1, nb, _ = z_ref.shape[1:]
    conv = _dot(g_ref[...], z_ref[0].reshape(two * n1, nb * D))
    o_ref[0] = conv.reshape(conv.shape[0], nb, D).astype(o_ref.dtype)


def _hy_s3(g_mat, z, nb):
    Bn, _, n1, n2, _ = z.shape
    R = g_mat.shape[0]
    return pl.pallas_call(
        _hy_s3_kernel,
        grid=(Bn, n2 // nb),
        in_specs=[pl.BlockSpec((R, 2 * n1), lambda b, j: (0, 0)),
                  pl.BlockSpec((1, 2, n1, nb, D), lambda b, j: (b, 0, 0, j, 0))],
        out_specs=pl.BlockSpec((1, R, nb, D), lambda b, j: (b, 0, j, 0)),
        out_shape=jax.ShapeDtypeStruct((Bn, R, n2, D), BF16),
        compiler_params=_params("arbitrary", "arbitrary"),
        name="hy_s3",
    )(g_mat, z)


def _cmul(xr, xi, kr, ki):
    return xr * kr - xi * ki, xr * ki + xi * kr


def _hy_spec_kernel(a_ref, m_ref, o_ref):
    a = jnp.concatenate([a_ref[0, 0], a_ref[1, 0]], axis=0)
    o_ref[0] = _dot(m_ref[0], a, HIGHEST)


def _hy_spectrum(a, m_tab):
    n1 = a.shape[1]
    n2 = DFT_N2
    return pl.pallas_call(
        _hy_spec_kernel,
        grid=(n1,),
        in_specs=[pl.BlockSpec((2, 1, n2, D), lambda k: (0, k, 0, 0)),
                  pl.BlockSpec((1, 2 * n2, 2 * n2), lambda k: (k, 0, 0))],
        out_specs=pl.BlockSpec((1, 2 * n2, D), lambda k: (k, 0, 0)),
        out_shape=jax.ShapeDtypeStruct((n1, 2 * n2, D), F32),
        compiler_params=_params("arbitrary"),
        name="hy_spectrum",
    )(a, m_tab)


def _hy_mid_kernel(a_ref, m_ref, mt_ref, kf_ref, o_ref):
    n2 = DFT_N2
    kr = kf_ref[0, :n2]
    ki = kf_ref[0, n2:]

    def body(b, _):
        a = jnp.concatenate([a_ref[b, 0, 0], a_ref[b, 1, 0]], axis=0)
        x = _dot(m_ref[0], a)
        yr, yi = _cmul(x[:n2], x[n2:], kr, ki)
        y = jnp.concatenate([yr, yi], axis=0).astype(BF16)
        z = _dot(mt_ref[0], y)
        o_ref[b, 0, 0] = z[:n2].astype(o_ref.dtype)
        o_ref[b, 1, 0] = z[n2:].astype(o_ref.dtype)
        return 0

    lax.fori_loop(0, a_ref.shape[0], body, 0)


def _hy_mid(a, m_bf, mt_bf, kf):
    B, _, n1, n2, _ = a.shape
    blk = pl.BlockSpec((B, 2, 1, n2, D), lambda k: (0, 0, k, 0, 0))
    tab = pl.BlockSpec((1, 2 * n2, 2 * n2), lambda k: (k, 0, 0))
    return pl.pallas_call(
        _hy_mid_kernel,
        grid=(n1,),
        in_specs=[blk, tab, tab, pl.BlockSpec((1, 2 * n2, D), lambda k: (k, 0, 0))],
        out_specs=blk,
        out_shape=jax.ShapeDtypeStruct(a.shape, BF16),
        compiler_params=_params("arbitrary"),
        name="hy_mid",
    )(a, m_bf, mt_bf, kf)


def _hy_output(x0c, conv, u, bias):
    return x0c * (conv + u * bias)


def _hy_ctx_kernel(x0_ref, x1_ref, v_ref, w0_ref, w1_ref, w2_ref, b0_ref, b1_ref, b2_ref,
                   f_ref, kf_ref, g_ref, bias_ref, o_ref):
    x0 = _short_conv(x0_ref[0], w0_ref[...], b0_ref[...])
    u = _short_conv(x1_ref[0], w1_ref[...], b1_ref[...]) * _short_conv(v_ref[0], w2_ref[...], b2_ref[...])
    x = _dot(f_ref[...], u, HIGHEST)
    nf = x.shape[0] // 2
    yr, yi = _cmul(x[:nf], x[nf:], kf_ref[:nf], kf_ref[nf:])
    conv = _dot(g_ref[...], jnp.concatenate([yr, yi], axis=0), HIGHEST)
    o_ref[0] = _hy_output(x0, conv, u, bias_ref[...]).astype(o_ref.dtype)


def _hy_ctx(p_hg, conv_w, conv_b, f_c, kf_c, g_c, bias, L, Lc):
    B = p_hg.shape[0]
    ct = 256
    nb = D // ct
    rb = L // Lc
    col = lambda k: pl.BlockSpec((1, Lc, ct), lambda b, j: (b, rb, k * nb + j))
    wsp = lambda k: pl.BlockSpec((3, ct), lambda b, j: (0, k * nb + j))
    bsp = lambda k: pl.BlockSpec((1, ct), lambda b, j: (0, k * nb + j))
    return pl.pallas_call(
        _hy_ctx_kernel,
        grid=(B, nb),
        in_specs=[col(0), col(1), col(2), wsp(0), wsp(1), wsp(2), bsp(0), bsp(1), bsp(2),
                  pl.BlockSpec(f_c.shape, lambda b, j: (0, 0)),
                  pl.BlockSpec((kf_c.shape[0], ct), lambda b, j: (0, j)),
                  pl.BlockSpec(g_c.shape, lambda b, j: (0, 0)),
                  pl.BlockSpec((1, ct), lambda b, j: (0, j))],
        out_specs=pl.BlockSpec((1, Lc, ct), lambda b, j: (b, 0, j)),
        out_shape=jax.ShapeDtypeStruct((B, Lc, D), BF16),
        compiler_params=_params("arbitrary", "arbitrary"),
        name="hy_ctx",
    )(p_hg, p_hg, p_hg, conv_w, conv_w, conv_w, conv_b, conv_b, conv_b, f_c, kf_c, g_c, bias)


def _dft_tables(L, Lc):
    n = 2 * L
    n2 = DFT_N2
    n1 = n // n2
    two_pi = 2.0 * math.pi

    def cs(idx, period):
        ang = (idx % period).astype(F32) * (two_pi / period)
        return jnp.cos(ang), jnp.sin(ang)

    k1 = jnp.arange(n1)[:, None]
    c, s = cs(k1 * jnp.arange(n1)[None, :], n1)
    f1_full = jnp.concatenate([c, -s], axis=0)
    f1_half = f1_full[:, :n1 // 2]
    kk1 = jnp.arange(n1)[:, None, None]
    kk2 = jnp.arange(n2)[None, :, None]
    nn2 = jnp.arange(n2)[None, None, :]
    c, s = cs(nn2 * kk1 + n1 * nn2 * kk2, n)
    ci = -s
    m_tab = jnp.concatenate([jnp.concatenate([c, -ci], axis=2), jnp.concatenate([ci, c], axis=2)], axis=1)
    c, s = cs(jnp.arange(n1 // 2)[:, None] * jnp.arange(n1)[None, :], n1)
    g_mat = jnp.concatenate([c, -s], axis=1) / n
    nc = 2 * Lc
    kc = jnp.arange(nc)[:, None]
    c, s = cs(kc * jnp.arange(nc)[None, :], nc)
    fc_full = jnp.concatenate([c, -s], axis=0)
    c, s = cs(jnp.arange(Lc)[:, None] * jnp.arange(nc)[None, :], nc)
    g_c = jnp.concatenate([c, -s], axis=1) / nc
    return dict(f1_full=f1_full, f1_half=f1_half.astype(BF16), m_tab=m_tab, m_bf=m_tab.astype(BF16),
                mt_bf=jnp.swapaxes(m_tab, 1, 2).astype(BF16), g_mat=g_mat.astype(BF16),
                fc_full=fc_full, fc_half=fc_full[:, :Lc], g_c=g_c)


def _hyena(p_hg, hp, tabs, L, Lc):
    B = p_hg.shape[0]
    n2 = DFT_N2
    n1 = 2 * L // n2
    wc = min(16 * D, n2 * D)
    deltas = jnp.abs(jnp.linspace(math.log(HY_TARGET) / HY_FAST_DECAY, math.log(HY_TARGET) / HY_SLOW_DECAY,
                                  D, dtype=F32))[None, :]
    filt = functools.partial(_hy_filter, w1p=hp['w1p'], b1=hp['b1'], w2=hp['w2'], b2=hp['b2'], fr=hp['fr'],
                             w3=hp['w3'], deltas=deltas)
    k_lat = filt(L)
    a_f = _hy_s1(tabs['f1_full'], k_lat.reshape(1, n1, n2, D), wc // D, F32, HIGHEST, "hy_filt_s1")
    kf = _hy_spectrum(a_f[0], tabs['m_tab'])
    k_ctx = filt(Lc)
    kf_c = _lmat(tabs['fc_full'], k_ctx[None], F32, HIGHEST, D, "hy_filt_ctx")[0]
    x0c, u = _hy_pre(p_hg, hp['conv_w'], hp['conv_b'], L)
    a = _hy_s1(tabs['f1_half'], u.reshape(B, n1 // 2, n2, D), wc // D)
    z = _hy_mid(a, tabs['m_bf'], tabs['mt_bf'], kf)
    conv = _hy_s3(tabs['g_mat'], z, wc // D).reshape(B, L, D)
    y_c = _hy_ctx(p_hg, hp['conv_w'], hp['conv_b'], tabs['fc_half'], kf_c, tabs['g_c'], hp['bias'], L, Lc)
    return conv, x0c, u, y_c


def _merge_kernel(x_ref, mod_ref, yd_ref, cv_ref, x0_ref, u_ref, yc_ref, hb_ref, ys_ref, gd_ref, gh_ref, gs_ref,
                  wb_ref, wo_ref, g2_ref, rw_ref, xo_ref, h2_ref, lg_ref, *, n_lat_tiles):
    latent = pl.program_id(1) < n_lat_tiles
    half = x_ref.shape[1] // 2
    for rows in (slice(0, half), slice(half, 2 * half)):
        y_hy = jnp.where(latent,
                         _hy_output(x0_ref[0, rows].astype(F32), cv_ref[0, rows].astype(F32),
                                    u_ref[0, rows].astype(F32), hb_ref[...]),
                         yc_ref[0, rows].astype(F32)).astype(BF16)
        merged = None
        for k, (y, gt_ref) in enumerate(((yd_ref[0, rows], gd_ref), (y_hy, gh_ref), (ys_ref[0, rows], gs_ref))):
            term = jax.nn.sigmoid(gt_ref[0, rows]) * _dot(y, wb_ref[k])
            merged = term if merged is None else merged + term
        y = _dot(merged.astype(BF16), wo_ref[...])
        x = x_ref[0, rows] + mod_ref[0, 2:3, :] * y
        xo_ref[0, rows] = x
        h2 = _rms(x) * g2_ref[...]
        h2 = h2 * (1.0 + mod_ref[0, 4:5, :]) + mod_ref[0, 3:4, :]
        h2_ref[0, rows] = h2.astype(h2_ref.dtype)
        lg_ref[0, :, rows] = _dot(h2, rw_ref[...], HIGHEST).T[:N_EXPERTS]


def _merge(xa, mod, y_diff, hy, hy_bias, y_swa, p_hg, w_branch, w_out, g2, rw_pad, n_lat_tiles):
    B, T, _ = xa.shape
    ctx_row = B
    conv, x0c, u, y_hy_c = hy
    tile = lambda c: pl.BlockSpec((1, TM, D), lambda b, i: (b, i, c))
    lat = pl.BlockSpec((1, TM, D), lambda b, i: (b, jnp.minimum(i, n_lat_tiles - 1), 0))
    return pl.pallas_call(
        functools.partial(_merge_kernel, n_lat_tiles=n_lat_tiles),
        grid=(B, T // TM),
        in_specs=[tile(0),
                  pl.BlockSpec((1, 6, D), lambda b, i: (jnp.where(i >= n_lat_tiles, ctx_row, b), 0, 0)),
                  tile(0), lat, lat, lat,
                  pl.BlockSpec((1, y_hy_c.shape[1], D), lambda b, i: (b, 0, 0)),
                  pl.BlockSpec((1, D), lambda b, i: (0, 0)),
                  tile(0), tile(3), tile(4), tile(5),
                  pl.BlockSpec((3, D, D), lambda b, i: (0, 0, 0)),
                  pl.BlockSpec((D, D), lambda b, i: (0, 0)),
                  pl.BlockSpec((1, D), lambda b, i: (0, 0)),
                  pl.BlockSpec((D, LANES), lambda b, i: (0, 0))],
        out_specs=[tile(0), tile(0), pl.BlockSpec((1, N_EXPERTS, TM), lambda b, i: (b, 0, i))],
        out_shape=[jax.ShapeDtypeStruct((B, T, D), F32), jax.ShapeDtypeStruct((B, T, D), BF16),
                   jax.ShapeDtypeStruct((B, N_EXPERTS, T), F32)],
        compiler_params=_params("arbitrary", "arbitrary"),
        name="merge",
    )(xa, mod, y_diff, conv, x0c, u, y_hy_c, hy_bias, y_swa, p_hg, p_hg, p_hg, w_branch, w_out, g2, rw_pad)


def _router_kernel(lg_ref, slot_ref, gate_ref, first_ref, *, segments):
    lg = lg_ref[0]
    e = jnp.exp(lg - jnp.max(lg, axis=0, keepdims=True))
    aff = e / jnp.sum(e, axis=0, keepdims=True)
    r = lax.broadcasted_iota(jnp.int32, (LANES, LANES), 0)
    c = lax.broadcasted_iota(jnp.int32, (LANES, LANES), 1)
    before = (r < c).astype(BF16)

    def prefix_count(mask, n):
        run = jnp.zeros((N_EXPERTS, 1), F32)
        outs = []
        for j in range(n // LANES):
            blk = mask[:, j * LANES:(j + 1) * LANES].astype(F32).astype(BF16)
            outs.append(_dot(blk, before) + run)
            run = run + jnp.sum(blk.astype(F32), axis=1, keepdims=True)
        return jnp.concatenate(outs, axis=1)

    for start, n, cap, slot_base in segments:
        a = aff[:, start:start + n]
        bits = pltpu.bitcast(a, jnp.int32)

        def search(_, bounds):
            lo, hi = bounds
            mid = lo + ((hi - lo + 1) >> 1)
            cnt = jnp.sum((bits >= mid).astype(F32), axis=1, keepdims=True)
            ok = cnt >= cap
            return jnp.where(ok, mid, lo), jnp.where(ok, hi, mid - 1)

        lo0 = jnp.zeros((N_EXPERTS, 1), jnp.int32)
        hi0 = jnp.full((N_EXPERTS, 1), 0x7F800000, jnp.int32)
        thr, _ = lax.fori_loop(0, 32, search, (lo0, hi0))
        above = bits > thr
        tie = bits == thr
        need = cap - jnp.sum(above.astype(F32), axis=1, keepdims=True)
        sel = above | (tie & (prefix_count(tie, n) < need))
        pos = prefix_count(sel, n)
        slot_ref[0, :, start:start + n] = jnp.where(sel, pos + slot_base, -1.0)
        gate_ref[0, :, start:start + n] = jnp.where(sel, a, 0.0)
        if start == 0:
            t_idx = lax.broadcasted_iota(jnp.int32, (n, LANES), 0)
            tile_start = lax.broadcasted_iota(jnp.int32, (n, LANES), 1) * TM
            before_tile = jnp.where(t_idx < tile_start, 1.0, 0.0).astype(BF16)
            first_ref[0] = _dot(jnp.where(sel, 1.0, 0.0).astype(BF16), before_tile)


def _router(logits_t, segments):
    B, E, T = logits_t.shape
    blk = pl.BlockSpec((1, E, T), lambda b: (b, 0, 0))
    return pl.pallas_call(
        functools.partial(_router_kernel, segments=segments),
        grid=(B,),
        in_specs=[blk],
        out_specs=[blk, blk, pl.BlockSpec((1, E, LANES), lambda b: (b, 0, 0))],
        out_shape=[jax.ShapeDtypeStruct((B, E, T), F32)] * 2 + [jax.ShapeDtypeStruct((B, E, LANES), F32)],
        compiler_params=_params("arbitrary"),
        name="router",
    )(logits_t)


def _slot_window(first, align, rows, n_slots):
    return pl.multiple_of(jnp.minimum(first // align * align, n_slots - rows), align)


def _moe_ffn_kernel(first_ref, slot_ref, gate_ref, h2_ref, w1_ref, w3_ref, w2_ref, o_ref, xg_scr, g_scr,
                    *, n_slots, n_tiles):
    base = (pl.program_id(0) * N_EXPERTS + pl.program_id(1)) * n_tiles
    rows = min(TM + 8, n_slots)
    row_id = lax.broadcasted_iota(jnp.int32, (rows, TM), 0)
    xg_scr[...] = jnp.zeros_like(xg_scr)
    g_scr[...] = jnp.zeros_like(g_scr)
    for c in range(n_tiles):
        tok = slice(c * TM, (c + 1) * TM)
        w0 = _slot_window(first_ref[base + c], 8, rows, n_slots)
        hit = slot_ref[0, 0, :, tok] == (row_id + w0).astype(F32)
        win = pl.ds(w0, rows)
        xg_scr[win, :] += _dot(jnp.where(hit, 1.0, 0.0).astype(BF16), h2_ref[0, tok, :])
        g_scr[win, :] += jnp.sum(jnp.where(hit, gate_ref[0, 0, :, tok], 0.0), axis=1, keepdims=True)
    xb = xg_scr[...].astype(BF16)
    a = _dot(xb, w1_ref[0, 0])
    b = _dot(xb, w3_ref[0, 0])
    hidden = (a * jax.nn.sigmoid(a) * b).astype(BF16)
    o_ref[0, 0] = (_dot(hidden, w2_ref[0, 0]) * g_scr[...]).astype(o_ref.dtype)


def _moe_ffn(first, slot, gate, h2, w1, w3, w2, l, n_slots):
    B, E, T = slot.shape
    row = pl.BlockSpec((1, 1, 1, T), lambda b, e: (b, e, 0, 0))
    wsp = pl.BlockSpec((1, 1, D, D), lambda b, e: (l, e, 0, 0))
    return pl.pallas_call(
        functools.partial(_moe_ffn_kernel, n_slots=n_slots, n_tiles=T // TM),
        grid=(B, E),
        in_specs=[pl.BlockSpec(memory_space=pltpu.SMEM), row, row,
                  pl.BlockSpec((1, T, D), lambda b, e: (b, 0, 0)), wsp, wsp, wsp],
        out_specs=pl.BlockSpec((1, 1, n_slots, D), lambda b, e: (b, e, 0, 0)),
        out_shape=jax.ShapeDtypeStruct((B, E, n_slots, D), BF16),
        scratch_shapes=[pltpu.VMEM((n_slots, D), F32), pltpu.VMEM((n_slots, 1), F32)],
        compiler_params=_params("arbitrary", "arbitrary"),
        name="moe_ffn",
    )(first, slot.reshape(B, E, 1, T), gate.reshape(B, E, 1, T), h2, w1, w3, w2)


def _moe_combine_kernel(first_ref, x_ref, mod_ref, slot_ref, y_ref, o_ref, *, n_slots, n_tiles):
    tm = x_ref.shape[1]
    rows = min(tm + 16, n_slots)
    col_id = lax.broadcasted_iota(jnp.int32, (tm, rows), 1)
    acc = jnp.zeros((tm, D), F32)
    for e in range(N_EXPERTS):
        w0 = _slot_window(first_ref[(pl.program_id(0) * N_EXPERTS + e) * n_tiles + pl.program_id(1)], 16, rows,
                          n_slots)
        hit = slot_ref[0, :, e:e + 1] == (col_id + w0).astype(F32)
        acc = acc + _dot(jnp.where(hit, 1.0, 0.0).astype(BF16), y_ref[0, e, pl.ds(w0, rows), :])
    o_ref[0] = x_ref[0] + mod_ref[0, 5:6, :] * acc


def _moe_combine(first, xa, mod, slot_t, y, n_lat_tiles, n_slots):
    B, T, _ = xa.shape
    ctx_row = B
    tile = pl.BlockSpec((1, TM, D), lambda b, i: (b, i, 0))
    return pl.pallas_call(
        functools.partial(_moe_combine_kernel, n_slots=n_slots, n_tiles=T // TM),
        grid=(B, T // TM),
        in_specs=[pl.BlockSpec(memory_space=pltpu.SMEM), tile,
                  pl.BlockSpec((1, 6, D), lambda b, i: (jnp.where(i >= n_lat_tiles, ctx_row, b), 0, 0)),
                  pl.BlockSpec((1, TM, N_EXPERTS), lambda b, i: (b, i, 0)),
                  pl.BlockSpec((1, N_EXPERTS, n_slots, D), lambda b, i: (b, 0, 0, 0))],
        out_specs=tile,
        out_shape=jax.ShapeDtypeStruct((B, T, D), F32),
        compiler_params=_params("arbitrary", "arbitrary"),
        name="moe_combine",
    )(first, xa, mod, slot_t, y)


def _final_kernel(x_ref, g_ref, o_ref):
    o_ref[0] = _rms(x_ref[0]) * g_ref[...]


def _final_norm(xa, g, L):
    B = xa.shape[0]
    tile = pl.BlockSpec((1, TM, D), lambda b, i: (b, i, 0))
    return pl.pallas_call(
        _final_kernel,
        grid=(B, L // TM),
        in_specs=[tile, pl.BlockSpec((1, D), lambda b, i: (0, 0))],
        out_specs=tile,
        out_shape=jax.ShapeDtypeStruct((B, L, D), F32),
        compiler_params=_params("arbitrary", "arbitrary"),
        name="final_norm",
    )(xa, g)


def _swa_head_perm():
    return [kv * SWA_GROUP + g for g in range(SWA_GROUP) for kv in range(SWA_KV)]


def _split_w_in(w_in):
    depth = w_in.shape[0]
    o = [0, D, 2 * D, 3 * D, 6 * D, 7 * D, 7 * D + SWA_KV * HEAD, 7 * D + 2 * SWA_KV * HEAD, 10 * D + 2 * SWA_KV * HEAD]
    dq, dk, dv, hy, sq, sk, sv, gt = (w_in[:, :, o[k]:o[k + 1]] for k in range(8))
    scale = HEAD ** -0.5 * math.log2(math.e)
    sq = sq.reshape(depth, D, SWA_KV * SWA_GROUP, HEAD)[:, :, jnp.array(_swa_head_perm())].reshape(depth, D, D)
    w_attn = jnp.concatenate([dq * scale, dk, dv, sq * scale, sk, sv], axis=-1).astype(BF16)
    w_hg = jnp.concatenate([hy, gt], axis=-1).astype(BF16)
    return w_attn, w_hg


def kernel(x, c, ctx, c_ctx, w_mod, b_mod, norm1_g, norm2_g, w_in, diff_lambda, diff_subln_g, hy_conv_w,
           hy_conv_b, hy_ff_w1, hy_ff_b1, hy_ff_w2, hy_ff_b2, hy_ff_w3, hy_sin_freq, hy_bias, swa_sink,
           w_branch, w_out, router_w, moe_w1, moe_w3, moe_w2, final_g):
    B, L, _ = x.shape
    Lc = ctx.shape[1]
    depth = w_mod.shape[0]
    T = L + Lc
    assert Lc == TM and L % (4 * TM) == 0 and x.shape[2] == D
    n_lat_tiles = L // TM
    cap_l = CAPACITY_FACTOR * L // N_EXPERTS
    cap_c = CAPACITY_FACTOR * Lc // N_EXPERTS
    n_slots = cap_l + cap_c
    segments = ((0, L, cap_l, 0), (L, Lc, cap_c, cap_l))

    xa = jnp.concatenate([x, ctx], axis=1)
    mods = _mods(c, c_ctx, w_mod, b_mod)
    cos_t, sin_t = _rope_tables(L, Lc)
    tabs = _dft_tables(L, Lc)
    w_attn, w_hg = _split_w_in(w_in)
    rope_flags = tuple([True] * 16 + [False] * 8 + [True] * 8 + [True] * 2 + [False] * 2)
    perm = jnp.array(_swa_head_perm())
    wb = w_branch.at[:, 2].set(
        w_branch[:, 2].reshape(depth, SWA_KV * SWA_GROUP, HEAD, D)[:, perm].reshape(depth, D, D)).astype(BF16)
    wo = w_out.astype(BF16)
    w1, w3, w2 = moe_w1.astype(BF16), moe_w3.astype(BF16), moe_w2.astype(BF16)
    rw_pad = jnp.pad(router_w, ((0, 0), (0, 0), (0, LANES - N_EXPERTS)))
    w1p = jnp.pad(hy_ff_w1, ((0, 0), (0, LANES - HY_EMB), (0, 0)))

    for l in range(depth):
        lambda_init = 0.8 - 0.6 * math.exp(-0.3 * l)
        mod = mods[l]
        p_attn = _norm_proj(xa, mod, norm1_g[l][None], w_attn[l], cos_t, sin_t, rope_flags, BF16,
                            n_lat_tiles, "proj_attn")
        p_hg = _norm_proj(xa, mod, norm1_g[l][None], w_hg[l], cos_t, sin_t, (False,) * (N_HG // LANES), F32,
                          n_lat_tiles, "proj_hg")
        y_diff = _diff_attn(p_attn, diff_lambda, diff_subln_g, L, Lc, lambda_init, l)
        y_swa = _swa(p_attn, swa_sink, L, Lc, l)
        hp = dict(w1p=w1p[l], b1=hy_ff_b1[l][None], w2=hy_ff_w2[l], b2=hy_ff_b2[l][None],
                  fr=hy_sin_freq[l][None], w3=hy_ff_w3[l], conv_w=hy_conv_w[l], conv_b=hy_conv_b[l][None],
                  bias=hy_bias[l][None])
        hy = _hyena(p_hg, hp, tabs, L, Lc)
        xa, h2, logits_t = _merge(xa, mod, y_diff, hy, hp['bias'], y_swa, p_hg, wb[l], wo[l], norm2_g[l][None],
                                  rw_pad[l], n_lat_tiles)
        slot, gate, first = _router(logits_t, segments)
        first = jnp.concatenate([first[:, :, :n_lat_tiles], jnp.full((B, N_EXPERTS, 1), cap_l, F32)], axis=2)
        first = first.astype(jnp.int32).reshape(-1)
        y_e = _moe_ffn(first, slot, gate, h2, w1, w3, w2, l, n_slots)
        xa = _moe_combine(first, xa, mod, jnp.swapaxes(slot, 1, 2), y_e, n_lat_tiles, n_slots)
    return _final_norm(xa, final_g[None], L)
```

```python
import functools
import math

import jax
import jax.numpy as jnp
from jax import lax
from jax.experimental import pallas as pl
from jax.experimental.pallas import tpu as pltpu

F32 = jnp.float32
BF16 = jnp.bfloat16
HIGHEST = lax.Precision.HIGHEST

D = 1024
HEAD = 64
GRID_W = 64
ROPE_BASE = 10000.0
EPS = 1e-6
N_DIFF_HEADS = D // (2 * HEAD)
DIFF_V = 2 * HEAD
SWA_KV = 4
SWA_GROUP = 4
WINDOW = 128
N_EXPERTS = 16
CAPACITY_FACTOR = 2
HY_BANDS = 16
HY_EMB = 1 + 2 * HY_BANDS
HY_FF = 64
HY_FAST_DECAY = 0.3
HY_SLOW_DECAY = 1.5
HY_TARGET = 1e-2

LANES = 128
TM = 256
DFT_N2 = 128
VMEM_LIMIT = 56 * 1024 * 1024

SPARSE_HITS = 64

N_ATTN = 3 * D + D + 2 * SWA_KV * HEAD
N_HG = 3 * D + 3 * D


def _params(*sem):
    return pltpu.CompilerParams(dimension_semantics=sem, vmem_limit_bytes=VMEM_LIMIT)


def _dot(a, b, precision=None):
    return jnp.dot(a, b, preferred_element_type=F32, precision=precision)


def _dot_nt(a, b, precision=None):
    return lax.dot_general(a, b, (((1,), (1,)), ((), ())), preferred_element_type=F32, precision=precision)


def _rms(x):
    return x * lax.rsqrt(jnp.mean(x * x, axis=-1, keepdims=True) + EPS)


def _mod_kernel(c_ref, w_ref, b_ref, o_ref):
    c = c_ref[...]
    sc = c * jax.nn.sigmoid(c)
    o_ref[0] = _dot(sc, w_ref[0], HIGHEST) + b_ref[0]


def _mods(c, c_ctx, w_mod, b_mod):
    B = c.shape[0]
    depth = w_mod.shape[0]
    rows = -(-(B + 1) // 8) * 8
    cc = jnp.zeros((rows, D), F32).at[:B].set(c).at[B].set(c_ctx)
    out = pl.pallas_call(
        _mod_kernel,
        grid=(depth, 6),
        in_specs=[pl.BlockSpec((rows, D), lambda l, j: (0, 0)),
                  pl.BlockSpec((1, D, D), lambda l, j: (l, 0, j)),
                  pl.BlockSpec((1, 1, D), lambda l, j: (l, 0, j))],
        out_specs=pl.BlockSpec((1, rows, D), lambda l, j: (l, 0, j)),
        out_shape=jax.ShapeDtypeStruct((depth, rows, 6 * D), F32),
        compiler_params=_params("arbitrary", "arbitrary"),
        name="mods",
    )(cc, w_mod, b_mod.reshape(depth, 1, 6 * D))
    return out.reshape(depth, rows, 6, D)


def _norm_proj_kernel(x_ref, mod_ref, g_ref, w_ref, cos_ref, sin_ref, o_ref, *, rope_flags):
    x = x_ref[0]
    h = _rms(x) * g_ref[...]
    h = h * (1.0 + mod_ref[0, 1:2, :]) + mod_ref[0, 0:1, :]
    hb = h.astype(BF16)
    n_sub = len(rope_flags)
    sub_per_chunk = 4
    if any(rope_flags):
        cos = cos_ref[...]
        sin = sin_ref[...]
        lane = lax.broadcasted_iota(jnp.int32, cos.shape, 1)
        first_half = (lane % (HEAD // 2)) < (HEAD // 4)
    for c0 in range(0, n_sub, sub_per_chunk):
        c1 = min(c0 + sub_per_chunk, n_sub)
        acc = _dot(hb, w_ref[:, c0 * LANES:c1 * LANES])
        for s in range(c0, c1):
            a = acc[:, (s - c0) * LANES:(s - c0 + 1) * LANES]
            if rope_flags[s]:
                partner = jnp.where(first_half, pltpu.roll(a, LANES - HEAD // 4, 1), pltpu.roll(a, HEAD // 4, 1))
                a = a * cos + partner * sin
            o_ref[0, :, s * LANES:(s + 1) * LANES] = a.astype(o_ref.dtype)


def _norm_proj(xa, mod, g, w, cos_t, sin_t, rope_flags, out_dtype, n_lat_tiles, name):
    B, T, _ = xa.shape
    N = w.shape[1]
    ctx_row = B
    return pl.pallas_call(
        functools.partial(_norm_proj_kernel, rope_flags=rope_flags),
        grid=(B, T // TM),
        in_specs=[pl.BlockSpec((1, TM, D), lambda b, i: (b, i, 0)),
                  pl.BlockSpec((1, 6, D), lambda b, i: (jnp.where(i >= n_lat_tiles, ctx_row, b), 0, 0)),
                  pl.BlockSpec((1, D), lambda b, i: (0, 0)),
                  pl.BlockSpec((D, N), lambda b, i: (0, 0)),
                  pl.BlockSpec((TM, LANES), lambda b, i: (i, 0)),
                  pl.BlockSpec((TM, LANES), lambda b, i: (i, 0))],
        out_specs=pl.BlockSpec((1, TM, N), lambda b, i: (b, i, 0)),
        out_shape=jax.ShapeDtypeStruct((B, T, N), out_dtype),
        compiler_params=_params("arbitrary", "arbitrary"),
        name=name,
    )(xa, mod, g, w, cos_t, sin_t)


def _rope_tables(L, Lc):
    t = jnp.arange(L)
    pos = jnp.stack([(t // GRID_W).astype(F32), (t % GRID_W).astype(F32)], axis=1)
    n_freq = HEAD // 4
    inv = ROPE_BASE ** (-jnp.arange(n_freq, dtype=F32) / n_freq)
    ang = pos[:, :, None] * inv
    cos = jnp.cos(ang)
    sin = jnp.sin(ang)
    cos_h = jnp.concatenate([cos, cos], axis=-1).reshape(L, HEAD)
    sin_h = jnp.concatenate([-sin, sin], axis=-1).reshape(L, HEAD)
    cos_t = jnp.concatenate([jnp.tile(cos_h, (1, LANES // HEAD)), jnp.ones((Lc, LANES), F32)], axis=0)
    sin_t = jnp.concatenate([jnp.tile(sin_h, (1, LANES // HEAD)), jnp.zeros((Lc, LANES), F32)], axis=0)
    return cos_t, sin_t


def _lane_block_reduce(x, op):
    out = x[:, :LANES]
    for j in range(1, x.shape[1] // LANES):
        out = op(out, x[:, j * LANES:(j + 1) * LANES])
    return out


def _diff_attn_kernel(lam_ref, q_ref, k_ref, v_ref, g_ref, o_ref, s_lat, s_ctx, m_run, qq_scr, l_scr, acc_scr,
                      *, L, Lc, tg, lambda_init):
    tq = TM
    n_groups = (L + Lc) // tg
    n_lat = L // tq
    n_tiles = (L + Lc) // tq
    rows = (2 * tq, LANES)
    lowest = jnp.full(rows, -jnp.inf, F32)
    nothing = jnp.zeros(rows, F32)
    lp = lam_ref[0]
    lam = (jnp.exp(jnp.sum(lp[0:1] * lp[1:2], axis=-1, keepdims=True))
           - jnp.exp(jnp.sum(lp[2:3] * lp[3:4], axis=-1, keepdims=True)) + lambda_init)

    def stage(t, _):
        cur = t % 2
        prv = 1 - cur
        score_any = t < n_tiles
        score_lat = t < n_lat
        weigh_any = t >= 1
        weigh_lat = jnp.logical_and(weigh_any, t <= n_lat)

        @pl.when(score_any)
        def _():
            q = q_ref[0, pl.ds(pl.multiple_of(t * tq, tq), tq), :]
            lane = lax.broadcasted_iota(jnp.int32, q.shape, 1)
            zero = jnp.zeros_like(q)
            qq_scr[...] = jnp.concatenate([jnp.where(lane < HEAD, q, zero), jnp.where(lane >= HEAD, q, zero)],
                                          axis=0)

        @pl.when(jnp.logical_and(score_any, jnp.logical_not(score_lat)))
        def _():
            s_c = _dot_nt(qq_scr[...], k_ref[0, L:L + Lc, :])
            s_ctx[...] = s_c
            m_run[cur] = _lane_block_reduce(s_c, jnp.maximum)

        def row_max():
            return jnp.max(m_run[prv], axis=-1, keepdims=True)

        @pl.when(jnp.logical_and(weigh_any, jnp.logical_not(weigh_lat)))
        def _():
            p = jnp.exp2(s_ctx[...] - row_max())
            l_scr[...] = _lane_block_reduce(p, jnp.add)
            acc_scr[...] = _dot(p.astype(BF16), v_ref[0, L:L + Lc, :])

        def scores(gi, mrun):
            start = pl.multiple_of(gi * tg, LANES)
            s = _dot_nt(qq_scr[...], k_ref[0, pl.ds(start, tg), :])
            s_lat[cur, gi] = s
            return jnp.maximum(mrun, _lane_block_reduce(s, jnp.maximum))

        def weighted(gi, lrun, acc, m):
            start = pl.multiple_of(gi * tg, LANES)
            p = jnp.exp2(s_lat[prv, gi] - m)
            return (lrun + _lane_block_reduce(p, jnp.add),
                    acc + _dot(p.astype(BF16), v_ref[0, pl.ds(start, tg), :]))

        @pl.when(jnp.logical_and(score_lat, weigh_lat))
        def _():
            m = row_max()

            def both(gi, carry):
                mrun, lrun, acc = carry
                lrun, acc = weighted(gi, lrun, acc, m)
                return scores(gi, mrun), lrun, acc

            m_run[cur], l_scr[...], acc_scr[...] = lax.fori_loop(0, n_groups, both, (lowest, nothing, nothing))

        @pl.when(jnp.logical_and(score_lat, jnp.logical_not(weigh_lat)))
        def _():
            m_run[cur] = lax.fori_loop(0, n_groups, scores, lowest)

        @pl.when(jnp.logical_and(jnp.logical_not(score_lat), weigh_lat))
        def _():
            m = row_max()
            l_scr[...], acc_scr[...] = lax.fori_loop(0, n_groups, lambda gi, c: weighted(gi, c[0], c[1], m),
                                                      (nothing, nothing))

        @pl.when(weigh_any)
        def _():
            o = acc_scr[...] / jnp.sum(l_scr[...], axis=-1, keepdims=True)
            o = _rms(o[:tq] - lam * o[tq:])
            out_rows = pl.ds(pl.multiple_of((t - 1) * tq, tq), tq)
            o_ref[0, out_rows, :] = (o * g_ref[0] * (1.0 - lambda_init)).astype(o_ref.dtype)

        return 0

    lax.fori_loop(0, n_tiles + 1, stage, 0)


def _diff_attn(p_attn, diff_lambda, subln_g, L, Lc, lambda_init, l):
    B, T, _ = p_attn.shape
    tg = T // 2
    nh = N_DIFF_HEADS
    head = lambda c0: pl.BlockSpec((1, T, DIFF_V), lambda b, h: (b, 0, c0 + h))
    return pl.pallas_call(
        functools.partial(_diff_attn_kernel, L=L, Lc=Lc, tg=tg, lambda_init=lambda_init),
        grid=(B, nh),
        in_specs=[pl.BlockSpec((1, 4, HEAD), lambda b, h: (l, 0, 0)),
                  head(0), head(nh), head(2 * nh),
                  pl.BlockSpec((1, 1, DIFF_V), lambda b, h: (l, 0, 0))],
        out_specs=head(0),
        out_shape=jax.ShapeDtypeStruct((B, T, D), BF16),
        scratch_shapes=[pltpu.VMEM((2, T // tg, 2 * TM, tg), F32), pltpu.VMEM((2 * TM, Lc), F32),
                        pltpu.VMEM((2, 2 * TM, LANES), F32), pltpu.VMEM((2 * TM, DIFF_V), BF16),
                        pltpu.VMEM((2 * TM, LANES), F32), pltpu.VMEM((2 * TM, DIFF_V), F32)],
        compiler_params=_params("arbitrary", "arbitrary"),
        name="diff_attn",
    )(diff_lambda, p_attn, p_attn, p_attn, subln_g.reshape(-1, 1, DIFF_V))


def _swa_kernel(sink_ref, q_ref, k_ref, v_ref, o_ref, *, L, Lc, l):
    i = pl.program_id(1)
    tq = q_ref.shape[1]
    kvw = SWA_KV * HEAD
    win = tq + 2 * WINDOW
    lane_head = lax.broadcasted_iota(jnp.int32, (tq, kvw), 1) // HEAD
    k_ctx = k_ref[0, L:L + Lc, :]
    v_ctx = v_ref[0, L:L + Lc, :]

    def run(latent):
        if latent:
            start = pl.multiple_of(jnp.clip(i * tq - WINDOW, 0, L - win), WINDOW)
            k_win = k_ref[0, pl.ds(start, win), :]
            v_win = v_ref[0, pl.ds(start, win), :]
            q_pos = i * tq + lax.broadcasted_iota(jnp.int32, (tq, win), 0)
            k_pos = start + lax.broadcasted_iota(jnp.int32, (tq, win), 1)
            band_bias = jnp.where(jnp.abs(k_pos - q_pos) <= WINDOW, 0.0, -jnp.inf)
        for g in range(SWA_GROUP):
            qg = q_ref[0, :, g * kvw:(g + 1) * kvw]
            acc = jnp.zeros((tq, kvw), F32)
            for kv in range(SWA_KV):
                qe = jnp.where(lane_head == kv, qg, jnp.zeros_like(qg))
                snk = sink_ref[l, kv * SWA_GROUP + g] * math.log2(math.e)
                s_c = _dot_nt(qe, k_ctx)
                m = jnp.maximum(jnp.max(s_c, axis=-1, keepdims=True), snk)
                if latent:
                    s_w = _dot_nt(qe, k_win) + band_bias
                    m = jnp.maximum(m, jnp.max(s_w, axis=-1, keepdims=True))
                p_c = jnp.exp2(s_c - m)
                den = jnp.sum(p_c, axis=-1, keepdims=True) + jnp.exp2(snk - m)
                o = _dot(p_c.astype(BF16), v_ctx)
                if latent:
                    p_w = jnp.exp2(s_w - m)
                    den = den + jnp.sum(p_w, axis=-1, keepdims=True)
                    o = o + _dot(p_w.astype(BF16), v_win)
                acc = acc + jnp.where(lane_head == kv, o / den, 0.0)
            o_ref[0, :, g * kvw:(g + 1) * kvw] = acc.astype(o_ref.dtype)

    @pl.when(i < L // tq)
    def _():
        run(True)

    @pl.when(i >= L // tq)
    def _():
        run(False)


def _swa(p_attn, sink, L, Lc, l):
    B, T, _ = p_attn.shape
    kvw = SWA_KV * HEAD
    return pl.pallas_call(
        functools.partial(_swa_kernel, L=L, Lc=Lc, l=l),
        grid=(B, T // TM),
        in_specs=[pl.BlockSpec(memory_space=pltpu.SMEM),
                  pl.BlockSpec((1, TM, D), lambda b, i: (b, i, 3)),
                  pl.BlockSpec((1, T, kvw), lambda b, i: (b, 0, 4 * D // kvw)),
                  pl.BlockSpec((1, T, kvw), lambda b, i: (b, 0, 4 * D // kvw + 1))],
        out_specs=pl.BlockSpec((1, TM, D), lambda b, i: (b, i, 0)),
        out_shape=jax.ShapeDtypeStruct((B, T, D), BF16),
        compiler_params=_params("arbitrary", "arbitrary"),
        name="swa",
    )(sink, p_attn, p_attn, p_attn)


def _hy_filter_kernel(emb_ref, w1_ref, b1_ref, w2_ref, b2_ref, fr_ref, w3f_ref, w3b_ref, dl_ref, o_ref, *, L):
    rows = TM
    fr = fr_ref[...]

    def raw(i, total):
        r0 = pl.multiple_of(i * rows, rows)
        e = emb_ref[pl.ds(r0, rows), :]
        z = jnp.sin(fr * (_dot(e, w1_ref[...], HIGHEST) + b1_ref[...]))
        z = jnp.sin(fr * (_dot(z, w2_ref[...], HIGHEST) + b2_ref[...]))
        row = r0 + lax.broadcasted_iota(jnp.int32, (rows, 1), 0)
        h = jnp.where(row < L, _dot(z, w3f_ref[...], HIGHEST), _dot(z, w3b_ref[...], HIGHEST))
        h = h * jnp.exp(-e[:, 0:1] * dl_ref[...])
        o_ref[pl.ds(r0, rows), :] = h
        return total + jnp.sum(jnp.abs(h), axis=0, keepdims=True)

    n_chunks = emb_ref.shape[0] // rows
    norm = lax.fori_loop(0, n_chunks, raw, jnp.zeros((1, o_ref.shape[1]), F32)) + EPS

    def scale(i, _):
        r0 = pl.multiple_of(i * rows, rows)
        row = r0 + lax.broadcasted_iota(jnp.int32, (rows, 1), 0)
        o_ref[pl.ds(r0, rows), :] = jnp.where(row == L, 0.0, o_ref[pl.ds(r0, rows), :] / norm)
        return 0

    lax.fori_loop(0, n_chunks, scale, 0)


def _hy_embedding(L):
    n = jnp.arange(2 * L)
    lag = jnp.where(n < L, n, jnp.where(n == L, 0, 2 * L - n))
    t = jnp.linspace(0.0, 1.0, L, dtype=F32)[lag][:, None]
    w = (2.0 * math.pi * jnp.arange(L, dtype=F32) / L)[lag][:, None]
    f = jnp.linspace(1e-4, HY_BANDS - 1, HY_BANDS, dtype=F32)[None, :]
    emb = jnp.concatenate([t, jnp.cos(f * w), -jnp.sin(f * w)], axis=-1)
    return jnp.pad(emb, ((0, 0), (0, LANES - HY_EMB)))


def _hy_filter(L, w1p, b1, w2, b2, fr, w3, deltas):
    ct = 512
    emb = _hy_embedding(L)
    full = lambda shape: pl.BlockSpec(shape, lambda j: (0, 0))
    return pl.pallas_call(
        functools.partial(_hy_filter_kernel, L=L),
        grid=(D // ct,),
        in_specs=[full((2 * L, LANES)), full((LANES, HY_FF)), full((1, HY_FF)), full((HY_FF, HY_FF)),
                  full((1, HY_FF)), full((1, HY_FF)),
                  pl.BlockSpec((HY_FF, ct), lambda j: (0, j)),
                  pl.BlockSpec((HY_FF, ct), lambda j: (0, D // ct + j)),
                  pl.BlockSpec((1, ct), lambda j: (0, j))],
        out_specs=pl.BlockSpec((2 * L, ct), lambda j: (0, j)),
        out_shape=jax.ShapeDtypeStruct((2 * L, D), F32),
        compiler_params=_params("arbitrary"),
        name="hy_filter",
    )(emb, w1p, b1, w2, b2, fr, w3, w3, deltas)


def _short_conv(x, w, b):
    n = x.shape[0]
    row = lax.broadcasted_iota(jnp.int32, x.shape, 0)
    prev = jnp.where(row == 0, 0.0, pltpu.roll(x, 1, 0))
    nxt = jnp.where(row == n - 1, 0.0, pltpu.roll(x, n - 1, 0))
    return prev * w[0:1] + x * w[1:2] + nxt * w[2:3] + b


def _hy_pre_kernel(x0_ref, x1_ref, v_ref, w0_ref, w1_ref, w2_ref, b0_ref, b1_ref, b2_ref, x0c_ref, u_ref):
    x0c_ref[0] = _short_conv(x0_ref[0], w0_ref[...], b0_ref[...]).astype(x0c_ref.dtype)
    u = _short_conv(x1_ref[0], w1_ref[...], b1_ref[...]) * _short_conv(v_ref[0], w2_ref[...], b2_ref[...])
    u_ref[0] = u.astype(u_ref.dtype)


def _hy_pre(p_hg, conv_w, conv_b, L):
    B = p_hg.shape[0]
    ct = LANES
    nb = D // ct
    col = lambda k: pl.BlockSpec((1, L, ct), lambda b, j: (b, 0, k * nb + j))
    wsp = lambda k: pl.BlockSpec((3, ct), lambda b, j: (0, k * nb + j))
    bsp = lambda k: pl.BlockSpec((1, ct), lambda b, j: (0, k * nb + j))
    out = pl.BlockSpec((1, L, ct), lambda b, j: (b, 0, j))
    return pl.pallas_call(
        _hy_pre_kernel,
        grid=(B, nb),
        in_specs=[col(0), col(1), col(2), wsp(0), wsp(1), wsp(2), bsp(0), bsp(1), bsp(2)],
        out_specs=[out, out],
        out_shape=[jax.ShapeDtypeStruct((B, L, D), BF16)] * 2,
        compiler_params=_params("arbitrary", "arbitrary"),
        name="hy_pre",
    )(p_hg, p_hg, p_hg, conv_w, conv_w, conv_w, conv_b, conv_b, conv_b)


def _lmat_kernel(m_ref, x_ref, o_ref, *, precision):
    x = x_ref[0]
    if precision is None:
        x = x.astype(BF16)
    o_ref[0] = _dot(m_ref[...], x, precision).astype(o_ref.dtype)


def _lmat(mat, x, out_dtype, precision, wc, name):
    Bn, K, W = x.shape
    M = mat.shape[0]
    return pl.pallas_call(
        functools.partial(_lmat_kernel, precision=precision),
        grid=(Bn, W // wc),
        in_specs=[pl.BlockSpec((M, K), lambda b, j: (0, 0)),
                  pl.BlockSpec((1, K, wc), lambda b, j: (b, 0, j))],
        out_specs=pl.BlockSpec((1, M, wc), lambda b, j: (b, 0, j)),
        out_shape=jax.ShapeDtypeStruct((Bn, M, W), out_dtype),
        compiler_params=_params("arbitrary", "arbitrary"),
        name=name,
    )(mat, x)


def _hy_s1_kernel(m_ref, x_ref, o_ref, *, precision):
    k, nb = x_ref.shape[1], x_ref.shape[2]
    res = _dot(m_ref[...], x_ref[0].reshape(k, nb * D), precision)
    o_ref[0] = res.reshape(2, res.shape[0] // 2, nb, D).astype(o_ref.dtype)


def _hy_s1(mat, x, nb, out_dtype=BF16, precision=None, name="hy_s1"):
    Bn, K, n2, _ = x.shape
    M = mat.shape[0]
    return pl.pallas_call(
        functools.partial(_hy_s1_kernel, precision=precision),
        grid=(Bn, n2 // nb),
        in_specs=[pl.BlockSpec((M, K), lambda b, j: (0, 0)),
                  pl.BlockSpec((1, K, nb, D), lambda b, j: (b, 0, j, 0))],
        out_specs=pl.BlockSpec((1, 2, M // 2, nb, D), lambda b, j: (b, 0, 0, j, 0)),
        out_shape=jax.ShapeDtypeStruct((Bn, 2, M // 2, n2, D), out_dtype),
        compiler_params=_params("arbitrary", "arbitrary"),
        name=name,
    )(mat, x)


def _hy_s3_kernel(g_ref, z_ref, o_ref):
    two, n1, nb, _ = z_ref.shape[1:]
    conv = _dot(g_ref[...], z_ref[0].reshape(two * n1, nb * D))
    o_ref[0] = conv.reshape(conv.shape[0], nb, D).astype(o_ref.dtype)


def _hy_s3(g_mat, z, nb):
    Bn, _, n1, n2, _ = z.shape
    R = g_mat.shape[0]
    return pl.pallas_call(
        _hy_s3_kernel,
        grid=(Bn, n2 // nb),
        in_specs=[pl.BlockSpec((R, 2 * n1), lambda b, j: (0, 0)),
                  pl.BlockSpec((1, 2, n1, nb, D), lambda b, j: (b, 0, 0, j, 0))],
        out_specs=pl.BlockSpec((1, R, nb, D), lambda b, j: (b, 0, j, 0)),
        out_shape=jax.ShapeDtypeStruct((Bn, R, n2, D), BF16),
        compiler_params=_params("arbitrary", "arbitrary"),
        name="hy_s3",
    )(g_mat, z)


def _cmul(xr, xi, kr, ki):
    return xr * kr - xi * ki, xr * ki + xi * kr


def _hy_spec_kernel(a_ref, m_ref, o_ref):
    a = jnp.concatenate([a_ref[0, 0], a_ref[1, 0]], axis=0)
    o_ref[0] = _dot(m_ref[0], a, HIGHEST)


def _hy_spectrum(a, m_tab):
    n1 = a.shape[1]
    n2 = DFT_N2
    return pl.pallas_call(
        _hy_spec_kernel,
        grid=(n1,),
        in_specs=[pl.BlockSpec((2, 1, n2, D), lambda k: (0, k, 0, 0)),
                  pl.BlockSpec((1, 2 * n2, 2 * n2), lambda k: (k, 0, 0))],
        out_specs=pl.BlockSpec((1, 2 * n2, D), lambda k: (k, 0, 0)),
        out_shape=jax.ShapeDtypeStruct((n1, 2 * n2, D), F32),
        compiler_params=_params("arbitrary"),
        name="hy_spectrum",
    )(a, m_tab)


def _hy_mid_kernel(a_ref, m_ref, mt_ref, kf_ref, o_ref):
    n2 = DFT_N2
    kr = kf_ref[0, :n2]
    ki = kf_ref[0, n2:]

    def body(b, _):
        a = jnp.concatenate([a_ref[b, 0, 0], a_ref[b, 1, 0]], axis=0)
        x = _dot(m_ref[0], a)
        yr, yi = _cmul(x[:n2], x[n2:], kr, ki)
        y = jnp.concatenate([yr, yi], axis=0).astype(BF16)
        z = _dot(mt_ref[0], y)
        o_ref[b, 0, 0] = z[:n2].astype(o_ref.dtype)
        o_ref[b, 1, 0] = z[n2:].astype(o_ref.dtype)
        return 0

    lax.fori_loop(0, a_ref.shape[0], body, 0)


def _hy_mid(a, m_bf, mt_bf, kf):
    B, _, n1, n2, _ = a.shape
    blk = pl.BlockSpec((B, 2, 1, n2, D), lambda k: (0, 0, k, 0, 0))
    tab = pl.BlockSpec((1, 2 * n2, 2 * n2), lambda k: (k, 0, 0))
    return pl.pallas_call(
        _hy_mid_kernel,
        grid=(n1,),
        in_specs=[blk, tab, tab, pl.BlockSpec((1, 2 * n2, D), lambda k: (k, 0, 0))],
        out_specs=blk,
        out_shape=jax.ShapeDtypeStruct(a.shape, BF16),
        compiler_params=_params("arbitrary"),
        name="hy_mid",
    )(a, m_bf, mt_bf, kf)


def _hy_output(x0c, conv, u, bias):
    return x0c * (conv + u * bias)


def _hy_ctx_kernel(x0_ref, x1_ref, v_ref, w0_ref, w1_ref, w2_ref, b0_ref, b1_ref, b2_ref,
                   f_ref, kf_ref, g_ref, bias_ref, o_ref):
    x0 = _short_conv(x0_ref[0], w0_ref[...], b0_ref[...])
    u = _short_conv(x1_ref[0], w1_ref[...], b1_ref[...]) * _short_conv(v_ref[0], w2_ref[...], b2_ref[...])
    x = _dot(f_ref[...], u, HIGHEST)
    nf = x.shape[0] // 2
    yr, yi = _cmul(x[:nf], x[nf:], kf_ref[:nf], kf_ref[nf:])
    conv = _dot(g_ref[...], jnp.concatenate([yr, yi], axis=0), HIGHEST)
    o_ref[0] = _hy_output(x0, conv, u, bias_ref[...]).astype(o_ref.dtype)


def _hy_ctx(p_hg, conv_w, conv_b, f_c, kf_c, g_c, bias, L, Lc):
    B = p_hg.shape[0]
    ct = 256
    nb = D // ct
    rb = L // Lc
    col = lambda k: pl.BlockSpec((1, Lc, ct), lambda b, j: (b, rb, k * nb + j))
    wsp = lambda k: pl.BlockSpec((3, ct), lambda b, j: (0, k * nb + j))
    bsp = lambda k: pl.BlockSpec((1, ct), lambda b, j: (0, k * nb + j))
    return pl.pallas_call(
        _hy_ctx_kernel,
        grid=(B, nb),
        in_specs=[col(0), col(1), col(2), wsp(0), wsp(1), wsp(2), bsp(0), bsp(1), bsp(2),
                  pl.BlockSpec(f_c.shape, lambda b, j: (0, 0)),
                  pl.BlockSpec((kf_c.shape[0], ct), lambda b, j: (0, j)),
                  pl.BlockSpec(g_c.shape, lambda b, j: (0, 0)),
                  pl.BlockSpec((1, ct), lambda b, j: (0, j))],
        out_specs=pl.BlockSpec((1, Lc, ct), lambda b, j: (b, 0, j)),
        out_shape=jax.ShapeDtypeStruct((B, Lc, D), BF16),
        compiler_params=_params("arbitrary", "arbitrary"),
        name="hy_ctx",
    )(p_hg, p_hg, p_hg, conv_w, conv_w, conv_w, conv_b, conv_b, conv_b, f_c, kf_c, g_c, bias)


def _dft_tables(L, Lc):
    n = 2 * L
    n2 = DFT_N2
    n1 = n // n2
    two_pi = 2.0 * math.pi

    def cs(idx, period):
        ang = (idx % period).astype(F32) * (two_pi / period)
        return jnp.cos(ang), jnp.sin(ang)

    k1 = jnp.arange(n1)[:, None]
    c, s = cs(k1 * jnp.arange(n1)[None, :], n1)
    f1_full = jnp.concatenate([c, -s], axis=0)
    f1_half = f1_full[:, :n1 // 2]
    kk1 = jnp.arange(n1)[:, None, None]
    kk2 = jnp.arange(n2)[None, :, None]
    nn2 = jnp.arange(n2)[None, None, :]
    c, s = cs(nn2 * kk1 + n1 * nn2 * kk2, n)
    ci = -s
    m_tab = jnp.concatenate([jnp.concatenate([c, -ci], axis=2), jnp.concatenate([ci, c], axis=2)], axis=1)
    c, s = cs(jnp.arange(n1 // 2)[:, None] * jnp.arange(n1)[None, :], n1)
    g_mat = jnp.concatenate([c, -s], axis=1) / n
    nc = 2 * Lc
    kc = jnp.arange(nc)[:, None]
    c, s = cs(kc * jnp.arange(nc)[None, :], nc)
    fc_full = jnp.concatenate([c, -s], axis=0)
    c, s = cs(jnp.arange(Lc)[:, None] * jnp.arange(nc)[None, :], nc)
    g_c = jnp.concatenate([c, -s], axis=1) / nc
    return dict(f1_full=f1_full, f1_half=f1_half.astype(BF16), m_tab=m_tab, m_bf=m_tab.astype(BF16),
                mt_bf=jnp.swapaxes(m_tab, 1, 2).astype(BF16), g_mat=g_mat.astype(BF16),
                fc_full=fc_full, fc_half=fc_full[:, :Lc], g_c=g_c)


def _hyena(p_hg, hp, tabs, L, Lc):
    B = p_hg.shape[0]
    n2 = DFT_N2
    n1 = 2 * L // n2
    wc = min(16 * D, n2 * D)
    deltas = jnp.abs(jnp.linspace(math.log(HY_TARGET) / HY_FAST_DECAY, math.log(HY_TARGET) / HY_SLOW_DECAY,
                                  D, dtype=F32))[None, :]
    filt = functools.partial(_hy_filter, w1p=hp['w1p'], b1=hp['b1'], w2=hp['w2'], b2=hp['b2'], fr=hp['fr'],
                             w3=hp['w3'], deltas=deltas)
    k_lat = filt(L)
    a_f = _hy_s1(tabs['f1_full'], k_lat.reshape(1, n1, n2, D), wc // D, F32, HIGHEST, "hy_filt_s1")
    kf = _hy_spectrum(a_f[0], tabs['m_tab'])
    k_ctx = filt(Lc)
    kf_c = _lmat(tabs['fc_full'], k_ctx[None], F32, HIGHEST, D, "hy_filt_ctx")[0]
    x0c, u = _hy_pre(p_hg, hp['conv_w'], hp['conv_b'], L)
    a = _hy_s1(tabs['f1_half'], u.reshape(B, n1 // 2, n2, D), wc // D)
    z = _hy_mid(a, tabs['m_bf'], tabs['mt_bf'], kf)
    conv = _hy_s3(tabs['g_mat'], z, wc // D).reshape(B, L, D)
    y_c = _hy_ctx(p_hg, hp['conv_w'], hp['conv_b'], tabs['fc_half'], kf_c, tabs['g_c'], hp['bias'], L, Lc)
    return conv, x0c, u, y_c


def _merge_kernel(x_ref, mod_ref, yd_ref, cv_ref, x0_ref, u_ref, yc_ref, hb_ref, ys_ref, gd_ref, gh_ref, gs_ref,
                  wb_ref, wo_ref, g2_ref, rw_ref, xo_ref, h2_ref, lg_ref, *, n_lat_tiles):
    latent = pl.program_id(1) < n_lat_tiles
    half = x_ref.shape[1] // 2
    for rows in (slice(0, half), slice(half, 2 * half)):
        y_hy = jnp.where(latent,
                         _hy_output(x0_ref[0, rows].astype(F32), cv_ref[0, rows].astype(F32),
                                    u_ref[0, rows].astype(F32), hb_ref[...]),
                         yc_ref[0, rows].astype(F32)).astype(BF16)
        merged = None
        for k, (y, gt_ref) in enumerate(((yd_ref[0, rows], gd_ref), (y_hy, gh_ref), (ys_ref[0, rows], gs_ref))):
            term = jax.nn.sigmoid(gt_ref[0, rows]) * _dot(y, wb_ref[k])
            merged = term if merged is None else merged + term
        y = _dot(merged.astype(BF16), wo_ref[...])
        x = x_ref[0, rows] + mod_ref[0, 2:3, :] * y
        xo_ref[0, rows] = x
        h2 = _rms(x) * g2_ref[...]
        h2 = h2 * (1.0 + mod_ref[0, 4:5, :]) + mod_ref[0, 3:4, :]
        h2_ref[0, rows] = h2.astype(h2_ref.dtype)
        lg_ref[0, :, rows] = _dot(h2, rw_ref[...], HIGHEST).T[:N_EXPERTS]


def _merge(xa, mod, y_diff, hy, hy_bias, y_swa, p_hg, w_branch, w_out, g2, rw_pad, n_lat_tiles):
    B, T, _ = xa.shape
    ctx_row = B
    conv, x0c, u, y_hy_c = hy
    tile = lambda c: pl.BlockSpec((1, TM, D), lambda b, i: (b, i, c))
    lat = pl.BlockSpec((1, TM, D), lambda b, i: (b, jnp.minimum(i, n_lat_tiles - 1), 0))
    return pl.pallas_call(
        functools.partial(_merge_kernel, n_lat_tiles=n_lat_tiles),
        grid=(B, T // TM),
        in_specs=[tile(0),
                  pl.BlockSpec((1, 6, D), lambda b, i: (jnp.where(i >= n_lat_tiles, ctx_row, b), 0, 0)),
                  tile(0), lat, lat, lat,
                  pl.BlockSpec((1, y_hy_c.shape[1], D), lambda b, i: (b, 0, 0)),
                  pl.BlockSpec((1, D), lambda b, i: (0, 0)),
                  tile(0), tile(3), tile(4), tile(5),
                  pl.BlockSpec((3, D, D), lambda b, i: (0, 0, 0)),
                  pl.BlockSpec((D, D), lambda b, i: (0, 0)),
                  pl.BlockSpec((1, D), lambda b, i: (0, 0)),
                  pl.BlockSpec((D, LANES), lambda b, i: (0, 0))],
        out_specs=[tile(0), tile(0), pl.BlockSpec((1, N_EXPERTS, TM), lambda b, i: (b, 0, i))],
        out_shape=[jax.ShapeDtypeStruct((B, T, D), F32), jax.ShapeDtypeStruct((B, T, D), BF16),
                   jax.ShapeDtypeStruct((B, N_EXPERTS, T), F32)],
        compiler_params=_params("arbitrary", "arbitrary"),
        name="merge",
    )(xa, mod, y_diff, conv, x0c, u, y_hy_c, hy_bias, y_swa, p_hg, p_hg, p_hg, w_branch, w_out, g2, rw_pad)


def _router_kernel(lg_ref, slot_ref, gate_ref, first_ref, *, segments):
    lg = lg_ref[0]
    e = jnp.exp(lg - jnp.max(lg, axis=0, keepdims=True))
    aff = e / jnp.sum(e, axis=0, keepdims=True)
    r = lax.broadcasted_iota(jnp.int32, (LANES, LANES), 0)
    c = lax.broadcasted_iota(jnp.int32, (LANES, LANES), 1)
    before = (r < c).astype(BF16)

    def prefix_count(mask, n):
        run = jnp.zeros((N_EXPERTS, 1), F32)
        outs = []
        for j in range(n // LANES):
            blk = mask[:, j * LANES:(j + 1) * LANES].astype(F32).astype(BF16)
            outs.append(_dot(blk, before) + run)
            run = run + jnp.sum(blk.astype(F32), axis=1, keepdims=True)
        return jnp.concatenate(outs, axis=1)

    for start, n, cap, slot_base in segments:
        a = aff[:, start:start + n]
        bits = pltpu.bitcast(a, jnp.int32)

        def search(_, bounds):
            lo, hi = bounds
            mid = lo + ((hi - lo + 1) >> 1)
            cnt = jnp.sum((bits >= mid).astype(F32), axis=1, keepdims=True)
            ok = cnt >= cap
            return jnp.where(ok, mid, lo), jnp.where(ok, hi, mid - 1)

        lo0 = jnp.zeros((N_EXPERTS, 1), jnp.int32)
        hi0 = jnp.full((N_EXPERTS, 1), 0x7F800000, jnp.int32)
        thr, _ = lax.fori_loop(0, 32, search, (lo0, hi0))
        above = bits > thr
        tie = bits == thr
        need = cap - jnp.sum(above.astype(F32), axis=1, keepdims=True)
        sel = above | (tie & (prefix_count(tie, n) < need))
        pos = prefix_count(sel, n)
        slot_ref[0, :, start:start + n] = jnp.where(sel, pos + slot_base, -1.0)
        gate_ref[0, :, start:start + n] = jnp.where(sel, a, 0.0)
        if start == 0:
            t_idx = lax.broadcasted_iota(jnp.int32, (n, LANES), 0)
            tile_start = lax.broadcasted_iota(jnp.int32, (n, LANES), 1) * TM
            before_tile = jnp.where(t_idx < tile_start, 1.0, 0.0).astype(BF16)
            first_ref[0] = _dot(jnp.where(sel, 1.0, 0.0).astype(BF16), before_tile)


def _router(logits_t, segments):
    B, E, T = logits_t.shape
    blk = pl.BlockSpec((1, E, T), lambda b: (b, 0, 0))
    return pl.pallas_call(
        functools.partial(_router_kernel, segments=segments),
        grid=(B,),
        in_specs=[blk],
        out_specs=[blk, blk, pl.BlockSpec((1, E, LANES), lambda b: (b, 0, 0))],
        out_shape=[jax.ShapeDtypeStruct((B, E, T), F32)] * 2 + [jax.ShapeDtypeStruct((B, E, LANES), F32)],
        compiler_params=_params("arbitrary"),
        name="router",
    )(logits_t)


def _slot_window(first, align, rows, n_slots):
    return pl.multiple_of(jnp.minimum(first // align * align, n_slots - rows), align)


def _moe_ffn_kernel(first_ref, busy_ref, slot_ref, gate_ref, h2_ref, w1_ref, w3_ref, w2_ref, o_ref, xg_scr, g_scr,
                    *, n_slots, n_tiles):
    pair = pl.program_id(0) * N_EXPERTS + pl.program_id(1)
    base = pair * n_tiles
    xg_scr[...] = jnp.zeros_like(xg_scr)
    g_scr[...] = jnp.zeros_like(g_scr)

    def gather(rows):
        row_id = lax.broadcasted_iota(jnp.int32, (rows, TM), 0)
        for c in range(n_tiles):
            tok = slice(c * TM, (c + 1) * TM)
            w0 = _slot_window(first_ref[base + c], 8, rows, n_slots)
            hit = slot_ref[0, 0, :, tok] == (row_id + w0).astype(F32)
            win = pl.ds(w0, rows)
            xg_scr[win, :] += _dot(jnp.where(hit, 1.0, 0.0).astype(BF16), h2_ref[0, tok, :])
            g_scr[win, :] += jnp.sum(jnp.where(hit, gate_ref[0, 0, :, tok], 0.0), axis=1, keepdims=True)

    sparse = busy_ref[pair] <= SPARSE_HITS

    @pl.when(sparse)
    def _():
        gather(min(SPARSE_HITS + 8, n_slots))

    @pl.when(jnp.logical_not(sparse))
    def _():
        gather(min(TM + 8, n_slots))

    xb = xg_scr[...].astype(BF16)
    a = _dot(xb, w1_ref[0, 0])
    b = _dot(xb, w3_ref[0, 0])
    hidden = (a * jax.nn.sigmoid(a) * b).astype(BF16)
    o_ref[0, 0] = (_dot(hidden, w2_ref[0, 0]) * g_scr[...]).astype(o_ref.dtype)


def _moe_ffn(first, busy, slot, gate, h2, w1, w3, w2, l, n_slots):
    B, E, T = slot.shape
    row = pl.BlockSpec((1, 1, 1, T), lambda b, e: (b, e, 0, 0))
    wsp = pl.BlockSpec((1, 1, D, D), lambda b, e: (l, e, 0, 0))
    smem = pl.BlockSpec(memory_space=pltpu.SMEM)
    return pl.pallas_call(
        functools.partial(_moe_ffn_kernel, n_slots=n_slots, n_tiles=T // TM),
        grid=(B, E),
        in_specs=[smem, smem, row, row, pl.BlockSpec((1, T, D), lambda b, e: (b, 0, 0)), wsp, wsp, wsp],
        out_specs=pl.BlockSpec((1, 1, n_slots, D), lambda b, e: (b, e, 0, 0)),
        out_shape=jax.ShapeDtypeStruct((B, E, n_slots, D), BF16),
        scratch_shapes=[pltpu.VMEM((n_slots, D), F32), pltpu.VMEM((n_slots, 1), F32)],
        compiler_params=_params("arbitrary", "arbitrary"),
        name="moe_ffn",
    )(first, busy, slot.reshape(B, E, 1, T), gate.reshape(B, E, 1, T), h2, w1, w3, w2)


def _moe_combine_kernel(first_ref, busy_ref, x_ref, mod_ref, slot_ref, y_ref, o_ref, acc_scr, *, n_slots, n_tiles):
    tm = x_ref.shape[1]

    def combine(rows):
        col_id = lax.broadcasted_iota(jnp.int32, (tm, rows), 1)
        acc = jnp.zeros((tm, D), F32)
        for e in range(N_EXPERTS):
            w0 = _slot_window(first_ref[(pl.program_id(0) * N_EXPERTS + e) * n_tiles + pl.program_id(1)], 16, rows,
                              n_slots)
            hit = slot_ref[0, :, e:e + 1] == (col_id + w0).astype(F32)
            acc = acc + _dot(jnp.where(hit, 1.0, 0.0).astype(BF16), y_ref[0, e, pl.ds(w0, rows), :])
        acc_scr[...] = acc

    sparse = busy_ref[pl.program_id(0) * n_tiles + pl.program_id(1)] <= SPARSE_HITS

    @pl.when(sparse)
    def _():
        combine(min(SPARSE_HITS + 16, n_slots))

    @pl.when(jnp.logical_not(sparse))
    def _():
        combine(min(tm + 16, n_slots))

    o_ref[0] = x_ref[0] + mod_ref[0, 5:6, :] * acc_scr[...]


def _moe_combine(first, busy, xa, mod, slot_t, y, n_lat_tiles, n_slots):
    B, T, _ = xa.shape
    ctx_row = B
    tile = pl.BlockSpec((1, TM, D), lambda b, i: (b, i, 0))
    smem = pl.BlockSpec(memory_space=pltpu.SMEM)
    return pl.pallas_call(
        functools.partial(_moe_combine_kernel, n_slots=n_slots, n_tiles=T // TM),
        grid=(B, T // TM),
        in_specs=[smem, smem, tile,
                  pl.BlockSpec((1, 6, D), lambda b, i: (jnp.where(i >= n_lat_tiles, ctx_row, b), 0, 0)),
                  pl.BlockSpec((1, TM, N_EXPERTS), lambda b, i: (b, i, 0)),
                  pl.BlockSpec((1, N_EXPERTS, n_slots, D), lambda b, i: (b, 0, 0, 0))],
        out_specs=tile,
        out_shape=jax.ShapeDtypeStruct((B, T, D), F32),
        scratch_shapes=[pltpu.VMEM((TM, D), F32)],
        compiler_params=_params("arbitrary", "arbitrary"),
        name="moe_combine",
    )(first, busy, xa, mod, slot_t, y)


def _final_kernel(x_ref, g_ref, o_ref):
    o_ref[0] = _rms(x_ref[0]) * g_ref[...]


def _final_norm(xa, g, L):
    B = xa.shape[0]
    tile = pl.BlockSpec((1, TM, D), lambda b, i: (b, i, 0))
    return pl.pallas_call(
        _final_kernel,
        grid=(B, L // TM),
        in_specs=[tile, pl.BlockSpec((1, D), lambda b, i: (0, 0))],
        out_specs=tile,
        out_shape=jax.ShapeDtypeStruct((B, L, D), F32),
        compiler_params=_params("arbitrary", "arbitrary"),
        name="final_norm",
    )(xa, g)


def _swa_head_perm():
    return [kv * SWA_GROUP + g for g in range(SWA_GROUP) for kv in range(SWA_KV)]


def _split_w_in(w_in):
    depth = w_in.shape[0]
    o = [0, D, 2 * D, 3 * D, 6 * D, 7 * D, 7 * D + SWA_KV * HEAD, 7 * D + 2 * SWA_KV * HEAD, 10 * D + 2 * SWA_KV * HEAD]
    dq, dk, dv, hy, sq, sk, sv, gt = (w_in[:, :, o[k]:o[k + 1]] for k in range(8))
    scale = HEAD ** -0.5 * math.log2(math.e)
    sq = sq.reshape(depth, D, SWA_KV * SWA_GROUP, HEAD)[:, :, jnp.array(_swa_head_perm())].reshape(depth, D, D)
    w_attn = jnp.concatenate([dq * scale, dk, dv, sq * scale, sk, sv], axis=-1).astype(BF16)
    w_hg = jnp.concatenate([hy, gt], axis=-1).astype(BF16)
    return w_attn, w_hg


def kernel(x, c, ctx, c_ctx, w_mod, b_mod, norm1_g, norm2_g, w_in, diff_lambda, diff_subln_g, hy_conv_w,
           hy_conv_b, hy_ff_w1, hy_ff_b1, hy_ff_w2, hy_ff_b2, hy_ff_w3, hy_sin_freq, hy_bias, swa_sink,
           w_branch, w_out, router_w, moe_w1, moe_w3, moe_w2, final_g):
    B, L, _ = x.shape
    Lc = ctx.shape[1]
    depth = w_mod.shape[0]
    T = L + Lc
    assert Lc == TM and L % (4 * TM) == 0 and x.shape[2] == D
    n_lat_tiles = L // TM
    cap_l = CAPACITY_FACTOR * L // N_EXPERTS
    cap_c = CAPACITY_FACTOR * Lc // N_EXPERTS
    n_slots = cap_l + cap_c
    segments = ((0, L, cap_l, 0), (L, Lc, cap_c, cap_l))

    xa = jnp.concatenate([x, ctx], axis=1)
    mods = _mods(c, c_ctx, w_mod, b_mod)
    cos_t, sin_t = _rope_tables(L, Lc)
    tabs = _dft_tables(L, Lc)
    w_attn, w_hg = _split_w_in(w_in)
    rope_flags = tuple([True] * 16 + [False] * 8 + [True] * 8 + [True] * 2 + [False] * 2)
    perm = jnp.array(_swa_head_perm())
    wb = w_branch.at[:, 2].set(
        w_branch[:, 2].reshape(depth, SWA_KV * SWA_GROUP, HEAD, D)[:, perm].reshape(depth, D, D)).astype(BF16)
    wo = w_out.astype(BF16)
    w1, w3, w2 = moe_w1.astype(BF16), moe_w3.astype(BF16), moe_w2.astype(BF16)
    rw_pad = jnp.pad(router_w, ((0, 0), (0, 0), (0, LANES - N_EXPERTS)))
    w1p = jnp.pad(hy_ff_w1, ((0, 0), (0, LANES - HY_EMB), (0, 0)))

    for l in range(depth):
        lambda_init = 0.8 - 0.6 * math.exp(-0.3 * l)
        mod = mods[l]
        p_attn = _norm_proj(xa, mod, norm1_g[l][None], w_attn[l], cos_t, sin_t, rope_flags, BF16,
                            n_lat_tiles, "proj_attn")
        p_hg = _norm_proj(xa, mod, norm1_g[l][None], w_hg[l], cos_t, sin_t, (False,) * (N_HG // LANES), F32,
                          n_lat_tiles, "proj_hg")
        y_diff = _diff_attn(p_attn, diff_lambda, diff_subln_g, L, Lc, lambda_init, l)
        y_swa = _swa(p_attn, swa_sink, L, Lc, l)
        hp = dict(w1p=w1p[l], b1=hy_ff_b1[l][None], w2=hy_ff_w2[l], b2=hy_ff_b2[l][None],
                  fr=hy_sin_freq[l][None], w3=hy_ff_w3[l], conv_w=hy_conv_w[l], conv_b=hy_conv_b[l][None],
                  bias=hy_bias[l][None])
        hy = _hyena(p_hg, hp, tabs, L, Lc)
        xa, h2, logits_t = _merge(xa, mod, y_diff, hy, hp['bias'], y_swa, p_hg, wb[l], wo[l], norm2_g[l][None],
                                  rw_pad[l], n_lat_tiles)
        slot, gate, first = _router(logits_t, segments)
        first = jnp.concatenate([first[:, :, :n_lat_tiles], jnp.full((B, N_EXPERTS, 1), cap_l, F32)], axis=2)
        first = first.astype(jnp.int32)
        ends = jnp.concatenate([first[:, :, 1:n_lat_tiles], jnp.full((B, N_EXPERTS, 1), cap_l, jnp.int32),
                                jnp.full((B, N_EXPERTS, 1), n_slots, jnp.int32)], axis=2)
        hits = ends - first
        first = first.reshape(-1)
        y_e = _moe_ffn(first, jnp.max(hits, axis=2).reshape(-1), slot, gate, h2, w1, w3, w2, l, n_slots)
        xa = _moe_combine(first, jnp.max(hits, axis=1).reshape(-1), xa, mod, jnp.swapaxes(slot, 1, 2), y_e,
                          n_lat_tiles, n_slots)
    return _final_norm(xa, final_g[None], L)
```

```python
import functools
import math

import jax
import jax.numpy as jnp
from jax import lax
from jax.experimental import pallas as pl
from jax.experimental.pallas import tpu as pltpu

F32 = jnp.float32
BF16 = jnp.bfloat16
HIGHEST = lax.Precision.HIGHEST

D = 1024
HEAD = 64
GRID_W = 64
ROPE_BASE = 10000.0
EPS = 1e-6
N_DIFF_HEADS = D // (2 * HEAD)
DIFF_V = 2 * HEAD
SWA_KV = 4
SWA_GROUP = 4
WINDOW = 128
N_EXPERTS = 16
CAPACITY_FACTOR = 2
HY_BANDS = 16
HY_EMB = 1 + 2 * HY_BANDS
HY_FF = 64
HY_FAST_DECAY = 0.3
HY_SLOW_DECAY = 1.5
HY_TARGET = 1e-2

LANES = 128
TM = 256
DFT_N2 = 128
VMEM_LIMIT = 56 * 1024 * 1024

SPARSE_HITS = 64

N_ATTN = 3 * D + D + 2 * SWA_KV * HEAD
N_HG = 3 * D + 3 * D


def _params(*sem):
    return pltpu.CompilerParams(dimension_semantics=sem, vmem_limit_bytes=VMEM_LIMIT)


def _dot(a, b, precision=None):
    return jnp.dot(a, b, preferred_element_type=F32, precision=precision)


def _dot_nt(a, b, precision=None):
    return lax.dot_general(a, b, (((1,), (1,)), ((), ())), preferred_element_type=F32, precision=precision)


def _rms(x):
    return x * lax.rsqrt(jnp.mean(x * x, axis=-1, keepdims=True) + EPS)


def _mod_kernel(c_ref, w_ref, b_ref, o_ref):
    c = c_ref[...]
    sc = c * jax.nn.sigmoid(c)
    o_ref[0] = _dot(sc, w_ref[0], HIGHEST) + b_ref[0]


def _mods(c, c_ctx, w_mod, b_mod):
    B = c.shape[0]
    depth = w_mod.shape[0]
    rows = -(-(B + 1) // 8) * 8
    cc = jnp.zeros((rows, D), F32).at[:B].set(c).at[B].set(c_ctx)
    out = pl.pallas_call(
        _mod_kernel,
        grid=(depth, 6),
        in_specs=[pl.BlockSpec((rows, D), lambda l, j: (0, 0)),
                  pl.BlockSpec((1, D, D), lambda l, j: (l, 0, j)),
                  pl.BlockSpec((1, 1, D), lambda l, j: (l, 0, j))],
        out_specs=pl.BlockSpec((1, rows, D), lambda l, j: (l, 0, j)),
        out_shape=jax.ShapeDtypeStruct((depth, rows, 6 * D), F32),
        compiler_params=_params("arbitrary", "arbitrary"),
        name="mods",
    )(cc, w_mod, b_mod.reshape(depth, 1, 6 * D))
    return out.reshape(depth, rows, 6, D)


def _norm_proj_kernel(x_ref, mod_ref, g_ref, w_ref, cos_ref, sin_ref, o_ref, *, rope_flags):
    x = x_ref[0]
    h = _rms(x) * g_ref[...]
    h = h * (1.0 + mod_ref[0, 1:2, :]) + mod_ref[0, 0:1, :]
    hb = h.astype(BF16)
    n_sub = len(rope_flags)
    sub_per_chunk = 4
    if any(rope_flags):
        cos = cos_ref[...]
        sin = sin_ref[...]
        lane = lax.broadcasted_iota(jnp.int32, cos.shape, 1)
        first_half = (lane % (HEAD // 2)) < (HEAD // 4)
    for c0 in range(0, n_sub, sub_per_chunk):
        c1 = min(c0 + sub_per_chunk, n_sub)
        acc = _dot(hb, w_ref[:, c0 * LANES:c1 * LANES])
        for s in range(c0, c1):
            a = acc[:, (s - c0) * LANES:(s - c0 + 1) * LANES]
            if rope_flags[s]:
                partner = jnp.where(first_half, pltpu.roll(a, LANES - HEAD // 4, 1), pltpu.roll(a, HEAD // 4, 1))
                a = a * cos + partner * sin
            o_ref[0, :, s * LANES:(s + 1) * LANES] = a.astype(o_ref.dtype)


def _norm_proj(xa, mod, g, w, cos_t, sin_t, rope_flags, out_dtype, n_lat_tiles, name):
    B, T, _ = xa.shape
    N = w.shape[1]
    ctx_row = B
    return pl.pallas_call(
        functools.partial(_norm_proj_kernel, rope_flags=rope_flags),
        grid=(B, T // TM),
        in_specs=[pl.BlockSpec((1, TM, D), lambda b, i: (b, i, 0)),
                  pl.BlockSpec((1, 6, D), lambda b, i: (jnp.where(i >= n_lat_tiles, ctx_row, b), 0, 0)),
                  pl.BlockSpec((1, D), lambda b, i: (0, 0)),
                  pl.BlockSpec((D, N), lambda b, i: (0, 0)),
                  pl.BlockSpec((TM, LANES), lambda b, i: (i, 0)),
                  pl.BlockSpec((TM, LANES), lambda b, i: (i, 0))],
        out_specs=pl.BlockSpec((1, TM, N), lambda b, i: (b, i, 0)),
        out_shape=jax.ShapeDtypeStruct((B, T, N), out_dtype),
        compiler_params=_params("arbitrary", "arbitrary"),
        name=name,
    )(xa, mod, g, w, cos_t, sin_t)


def _rope_tables(L, Lc):
    t = jnp.arange(L)
    pos = jnp.stack([(t // GRID_W).astype(F32), (t % GRID_W).astype(F32)], axis=1)
    n_freq = HEAD // 4
    inv = ROPE_BASE ** (-jnp.arange(n_freq, dtype=F32) / n_freq)
    ang = pos[:, :, None] * inv
    cos = jnp.cos(ang)
    sin = jnp.sin(ang)
    cos_h = jnp.concatenate([cos, cos], axis=-1).reshape(L, HEAD)
    sin_h = jnp.concatenate([-sin, sin], axis=-1).reshape(L, HEAD)
    cos_t = jnp.concatenate([jnp.tile(cos_h, (1, LANES // HEAD)), jnp.ones((Lc, LANES), F32)], axis=0)
    sin_t = jnp.concatenate([jnp.tile(sin_h, (1, LANES // HEAD)), jnp.zeros((Lc, LANES), F32)], axis=0)
    return cos_t, sin_t


def _lane_block_reduce(x, op):
    out = x[:, :LANES]
    for j in range(1, x.shape[1] // LANES):
        out = op(out, x[:, j * LANES:(j + 1) * LANES])
    return out


def _diff_attn_kernel(lam_ref, q_ref, k_ref, v_ref, g_ref, o_ref, s_lat, s_ctx, m_run, qq_scr, l_scr, acc_scr,
                      *, L, Lc, tg, lambda_init):
    tq = TM
    n_groups = (L + Lc) // tg
    n_lat = L // tq
    n_tiles = (L + Lc) // tq
    rows = (2 * tq, LANES)
    lowest = jnp.full(rows, -jnp.inf, F32)
    nothing = jnp.zeros(rows, F32)
    lp = lam_ref[0]
    lam = (jnp.exp(jnp.sum(lp[0:1] * lp[1:2], axis=-1, keepdims=True))
           - jnp.exp(jnp.sum(lp[2:3] * lp[3:4], axis=-1, keepdims=True)) + lambda_init)

    def stage(t, _):
        cur = t % 2
        prv = 1 - cur
        score_any = t < n_tiles
        score_lat = t < n_lat
        weigh_any = t >= 1
        weigh_lat = jnp.logical_and(weigh_any, t <= n_lat)

        @pl.when(score_any)
        def _():
            q = q_ref[0, pl.ds(pl.multiple_of(t * tq, tq), tq), :]
            lane = lax.broadcasted_iota(jnp.int32, q.shape, 1)
            zero = jnp.zeros_like(q)
            qq_scr[...] = jnp.concatenate([jnp.where(lane < HEAD, q, zero), jnp.where(lane >= HEAD, q, zero)],
                                          axis=0)

        @pl.when(jnp.logical_and(score_any, jnp.logical_not(score_lat)))
        def _():
            s_c = _dot_nt(qq_scr[...], k_ref[0, L:L + Lc, :])
            s_ctx[...] = s_c
            m_run[cur] = _lane_block_reduce(s_c, jnp.maximum)

        def row_max():
            return jnp.max(m_run[prv], axis=-1, keepdims=True)

        @pl.when(jnp.logical_and(weigh_any, jnp.logical_not(weigh_lat)))
        def _():
            p = jnp.exp2(s_ctx[...] - row_max())
            l_scr[...] = _lane_block_reduce(p, jnp.add)
            acc_scr[...] = _dot(p.astype(BF16), v_ref[0, L:L + Lc, :])

        def scores(gi, mrun):
            start = pl.multiple_of(gi * tg, LANES)
            s = _dot_nt(qq_scr[...], k_ref[0, pl.ds(start, tg), :])
            s_lat[cur, gi] = s
            return jnp.maximum(mrun, _lane_block_reduce(s, jnp.maximum))

        def weighted(gi, lrun, acc, m):
            start = pl.multiple_of(gi * tg, LANES)
            p = jnp.exp2(s_lat[prv, gi] - m)
            return (lrun + _lane_block_reduce(p, jnp.add),
                    acc + _dot(p.astype(BF16), v_ref[0, pl.ds(start, tg), :]))

        @pl.when(jnp.logical_and(score_lat, weigh_lat))
        def _():
            m = row_max()

            def both(gi, carry):
                mrun, lrun, acc = carry
                lrun, acc = weighted(gi, lrun, acc, m)
                return scores(gi, mrun), lrun, acc

            m_run[cur], l_scr[...], acc_scr[...] = lax.fori_loop(0, n_groups, both, (lowest, nothing, nothing))

        @pl.when(jnp.logical_and(score_lat, jnp.logical_not(weigh_lat)))
        def _():
            m_run[cur] = lax.fori_loop(0, n_groups, scores, lowest)

        @pl.when(jnp.logical_and(jnp.logical_not(score_lat), weigh_lat))
        def _():
            m = row_max()
            l_scr[...], acc_scr[...] = lax.fori_loop(0, n_groups, lambda gi, c: weighted(gi, c[0], c[1], m),
                                                      (nothing, nothing))

        @pl.when(weigh_any)
        def _():
            o = acc_scr[...] / jnp.sum(l_scr[...], axis=-1, keepdims=True)
            o = _rms(o[:tq] - lam * o[tq:])
            out_rows = pl.ds(pl.multiple_of((t - 1) * tq, tq), tq)
            o_ref[0, out_rows, :] = (o * g_ref[0] * (1.0 - lambda_init)).astype(o_ref.dtype)

        return 0

    lax.fori_loop(0, n_tiles + 1, stage, 0)


def _diff_attn(p_attn, diff_lambda, subln_g, L, Lc, lambda_init, l):
    B, T, _ = p_attn.shape
    tg = T // 2
    nh = N_DIFF_HEADS
    head = lambda c0: pl.BlockSpec((1, T, DIFF_V), lambda b, h: (b, 0, c0 + h))
    return pl.pallas_call(
        functools.partial(_diff_attn_kernel, L=L, Lc=Lc, tg=tg, lambda_init=lambda_init),
        grid=(B, nh),
        in_specs=[pl.BlockSpec((1, 4, HEAD), lambda b, h: (l, 0, 0)),
                  head(0), head(nh), head(2 * nh),
                  pl.BlockSpec((1, 1, DIFF_V), lambda b, h: (l, 0, 0))],
        out_specs=head(0),
        out_shape=jax.ShapeDtypeStruct((B, T, D), BF16),
        scratch_shapes=[pltpu.VMEM((2, T // tg, 2 * TM, tg), F32), pltpu.VMEM((2 * TM, Lc), F32),
                        pltpu.VMEM((2, 2 * TM, LANES), F32), pltpu.VMEM((2 * TM, DIFF_V), BF16),
                        pltpu.VMEM((2 * TM, LANES), F32), pltpu.VMEM((2 * TM, DIFF_V), F32)],
        compiler_params=_params("arbitrary", "arbitrary"),
        name="diff_attn",
    )(diff_lambda, p_attn, p_attn, p_attn, subln_g.reshape(-1, 1, DIFF_V))


def _swa_kernel(sink_ref, q_ref, k_ref, v_ref, o_ref, *, L, Lc, l):
    i = pl.program_id(1)
    tq = q_ref.shape[1]
    kvw = SWA_KV * HEAD
    win = tq + 2 * WINDOW
    lane_head = lax.broadcasted_iota(jnp.int32, (tq, kvw), 1) // HEAD
    k_ctx = k_ref[0, L:L + Lc, :]
    v_ctx = v_ref[0, L:L + Lc, :]

    def run(latent):
        if latent:
            start = pl.multiple_of(jnp.clip(i * tq - WINDOW, 0, L - win), WINDOW)
            k_win = k_ref[0, pl.ds(start, win), :]
            v_win = v_ref[0, pl.ds(start, win), :]
            q_pos = i * tq + lax.broadcasted_iota(jnp.int32, (tq, win), 0)
            k_pos = start + lax.broadcasted_iota(jnp.int32, (tq, win), 1)
            band_bias = jnp.where(jnp.abs(k_pos - q_pos) <= WINDOW, 0.0, -jnp.inf)
        for g in range(SWA_GROUP):
            qg = q_ref[0, :, g * kvw:(g + 1) * kvw]
            acc = jnp.zeros((tq, kvw), F32)
            for kv in range(SWA_KV):
                qe = jnp.where(lane_head == kv, qg, jnp.zeros_like(qg))
                snk = sink_ref[l, kv * SWA_GROUP + g] * math.log2(math.e)
                s_c = _dot_nt(qe, k_ctx)
                m = jnp.maximum(jnp.max(s_c, axis=-1, keepdims=True), snk)
                if latent:
                    s_w = _dot_nt(qe, k_win) + band_bias
                    m = jnp.maximum(m, jnp.max(s_w, axis=-1, keepdims=True))
                p_c = jnp.exp2(s_c - m)
                den = jnp.sum(p_c, axis=-1, keepdims=True) + jnp.exp2(snk - m)
                o = _dot(p_c.astype(BF16), v_ctx)
                if latent:
                    p_w = jnp.exp2(s_w - m)
                    den = den + jnp.sum(p_w, axis=-1, keepdims=True)
                    o = o + _dot(p_w.astype(BF16), v_win)
                acc = acc + jnp.where(lane_head == kv, o / den, 0.0)
            o_ref[0, :, g * kvw:(g + 1) * kvw] = acc.astype(o_ref.dtype)

    @pl.when(i < L // tq)
    def _():
        run(True)

    @pl.when(i >= L // tq)
    def _():
        run(False)


def _swa(p_attn, sink, L, Lc, l):
    B, T, _ = p_attn.shape
    kvw = SWA_KV * HEAD
    return pl.pallas_call(
        functools.partial(_swa_kernel, L=L, Lc=Lc, l=l),
        grid=(B, T // TM),
        in_specs=[pl.BlockSpec(memory_space=pltpu.SMEM),
                  pl.BlockSpec((1, TM, D), lambda b, i: (b, i, 3)),
                  pl.BlockSpec((1, T, kvw), lambda b, i: (b, 0, 4 * D // kvw)),
                  pl.BlockSpec((1, T, kvw), lambda b, i: (b, 0, 4 * D // kvw + 1))],
        out_specs=pl.BlockSpec((1, TM, D), lambda b, i: (b, i, 0)),
        out_shape=jax.ShapeDtypeStruct((B, T, D), BF16),
        compiler_params=_params("arbitrary", "arbitrary"),
        name="swa",
    )(sink, p_attn, p_attn, p_attn)


def _hy_filter_kernel(emb_ref, w1_ref, b1_ref, w2_ref, b2_ref, fr_ref, w3f_ref, w3b_ref, dl_ref, o_ref, *, L):
    rows = TM
    fr = fr_ref[...]

    def raw(i, total):
        r0 = pl.multiple_of(i * rows, rows)
        e = emb_ref[pl.ds(r0, rows), :]
        z = jnp.sin(fr * (_dot(e, w1_ref[...], HIGHEST) + b1_ref[...]))
        z = jnp.sin(fr * (_dot(z, w2_ref[...], HIGHEST) + b2_ref[...]))
        row = r0 + lax.broadcasted_iota(jnp.int32, (rows, 1), 0)
        h = jnp.where(row < L, _dot(z, w3f_ref[...], HIGHEST), _dot(z, w3b_ref[...], HIGHEST))
        h = h * jnp.exp(-e[:, 0:1] * dl_ref[...])
        o_ref[pl.ds(r0, rows), :] = h
        return total + jnp.sum(jnp.abs(h), axis=0, keepdims=True)

    n_chunks = emb_ref.shape[0] // rows
    norm = lax.fori_loop(0, n_chunks, raw, jnp.zeros((1, o_ref.shape[1]), F32)) + EPS

    def scale(i, _):
        r0 = pl.multiple_of(i * rows, rows)
        row = r0 + lax.broadcasted_iota(jnp.int32, (rows, 1), 0)
        o_ref[pl.ds(r0, rows), :] = jnp.where(row == L, 0.0, o_ref[pl.ds(r0, rows), :] / norm)
        return 0

    lax.fori_loop(0, n_chunks, scale, 0)


def _hy_embedding(L):
    n = jnp.arange(2 * L)
    lag = jnp.where(n < L, n, jnp.where(n == L, 0, 2 * L - n))
    t = jnp.linspace(0.0, 1.0, L, dtype=F32)[lag][:, None]
    w = (2.0 * math.pi * jnp.arange(L, dtype=F32) / L)[lag][:, None]
    f = jnp.linspace(1e-4, HY_BANDS - 1, HY_BANDS, dtype=F32)[None, :]
    emb = jnp.concatenate([t, jnp.cos(f * w), -jnp.sin(f * w)], axis=-1)
    return jnp.pad(emb, ((0, 0), (0, LANES - HY_EMB)))


def _hy_filter(L, w1p, b1, w2, b2, fr, w3, deltas):
    ct = 512
    emb = _hy_embedding(L)
    full = lambda shape: pl.BlockSpec(shape, lambda j: (0, 0))
    return pl.pallas_call(
        functools.partial(_hy_filter_kernel, L=L),
        grid=(D // ct,),
        in_specs=[full((2 * L, LANES)), full((LANES, HY_FF)), full((1, HY_FF)), full((HY_FF, HY_FF)),
                  full((1, HY_FF)), full((1, HY_FF)),
                  pl.BlockSpec((HY_FF, ct), lambda j: (0, j)),
                  pl.BlockSpec((HY_FF, ct), lambda j: (0, D // ct + j)),
                  pl.BlockSpec((1, ct), lambda j: (0, j))],
        out_specs=pl.BlockSpec((2 * L, ct), lambda j: (0, j)),
        out_shape=jax.ShapeDtypeStruct((2 * L, D), F32),
        compiler_params=_params("arbitrary"),
        name="hy_filter",
    )(emb, w1p, b1, w2, b2, fr, w3, w3, deltas)


def _short_conv(x, w, b):
    n = x.shape[0]
    row = lax.broadcasted_iota(jnp.int32, x.shape, 0)
    prev = jnp.where(row == 0, 0.0, pltpu.roll(x, 1, 0))
    nxt = jnp.where(row == n - 1, 0.0, pltpu.roll(x, n - 1, 0))
    return prev * w[0:1] + x * w[1:2] + nxt * w[2:3] + b


def _hy_pre_kernel(x0_ref, x1_ref, v_ref, w0_ref, w1_ref, w2_ref, b0_ref, b1_ref, b2_ref, x0c_ref, u_ref):
    x0c_ref[0] = _short_conv(x0_ref[0], w0_ref[...], b0_ref[...]).astype(x0c_ref.dtype)
    u = _short_conv(x1_ref[0], w1_ref[...], b1_ref[...]) * _short_conv(v_ref[0], w2_ref[...], b2_ref[...])
    u_ref[0] = u.astype(u_ref.dtype)


def _hy_pre(p_hg, conv_w, conv_b, L):
    B = p_hg.shape[0]
    ct = LANES
    nb = D // ct
    col = lambda k: pl.BlockSpec((1, L, ct), lambda b, j: (b, 0, k * nb + j))
    wsp = lambda k: pl.BlockSpec((3, ct), lambda b, j: (0, k * nb + j))
    bsp = lambda k: pl.BlockSpec((1, ct), lambda b, j: (0, k * nb + j))
    out = pl.BlockSpec((1, L, ct), lambda b, j: (b, 0, j))
    return pl.pallas_call(
        _hy_pre_kernel,
        grid=(B, nb),
        in_specs=[col(0), col(1), col(2), wsp(0), wsp(1), wsp(2), bsp(0), bsp(1), bsp(2)],
        out_specs=[out, out],
        out_shape=[jax.ShapeDtypeStruct((B, L, D), BF16)] * 2,
        compiler_params=_params("arbitrary", "arbitrary"),
        name="hy_pre",
    )(p_hg, p_hg, p_hg, conv_w, conv_w, conv_w, conv_b, conv_b, conv_b)


def _lmat_kernel(m_ref, x_ref, o_ref, *, precision):
    x = x_ref[0]
    if precision is None:
        x = x.astype(BF16)
    o_ref[0] = _dot(m_ref[...], x, precision).astype(o_ref.dtype)


def _lmat(mat, x, out_dtype, precision, wc, name):
    Bn, K, W = x.shape
    M = mat.shape[0]
    return pl.pallas_call(
        functools.partial(_lmat_kernel, precision=precision),
        grid=(Bn, W // wc),
        in_specs=[pl.BlockSpec((M, K), lambda b, j: (0, 0)),
                  pl.BlockSpec((1, K, wc), lambda b, j: (b, 0, j))],
        out_specs=pl.BlockSpec((1, M, wc), lambda b, j: (b, 0, j)),
        out_shape=jax.ShapeDtypeStruct((Bn, M, W), out_dtype),
        compiler_params=_params("arbitrary", "arbitrary"),
        name=name,
    )(mat, x)


def _hy_s1_kernel(m_ref, x_ref, o_ref, *, precision):
    k, nb = x_ref.shape[1], x_ref.shape[2]
    res = _dot(m_ref[...], x_ref[0].reshape(k, nb * D), precision)
    o_ref[0] = res.reshape(2, res.shape[0] // 2, nb, D).astype(o_ref.dtype)


def _hy_s1(mat, x, nb, out_dtype=BF16, precision=None, name="hy_s1"):
    Bn, K, n2, _ = x.shape
    M = mat.shape[0]
    return pl.pallas_call(
        functools.partial(_hy_s1_kernel, precision=precision),
        grid=(Bn, n2 // nb),
        in_specs=[pl.BlockSpec((M, K), lambda b, j: (0, 0)),
                  pl.BlockSpec((1, K, nb, D), lambda b, j: (b, 0, j, 0))],
        out_specs=pl.BlockSpec((1, 2, M // 2, nb, D), lambda b, j: (b, 0, 0, j, 0)),
        out_shape=jax.ShapeDtypeStruct((Bn, 2, M // 2, n2, D), out_dtype),
        compiler_params=_params("arbitrary", "arbitrary"),
        name=name,
    )(mat, x)


def _hy_s3_kernel(g_ref, z_ref, o_ref):
    two, n1, nb, _ = z_ref.shape[1:]
    conv = _dot(g_ref[...], z_ref[0].reshape(two * n1, nb * D))
    o_ref[0] = conv.reshape(conv.shape[0], nb, D).astype(o_ref.dtype)


def _hy_s3(g_mat, z, nb):
    Bn, _, n1, n2, _ = z.shape
    R = g_mat.shape[0]
    return pl.pallas_call(
        _hy_s3_kernel,
        grid=(Bn, n2 // nb),
        in_specs=[pl.BlockSpec((R, 2 * n1), lambda b, j: (0, 0)),
                  pl.BlockSpec((1, 2, n1, nb, D), lambda b, j: (b, 0, 0, j, 0))],
        out_specs=pl.BlockSpec((1, R, nb, D), lambda b, j: (b, 0, j, 0)),
        out_shape=jax.ShapeDtypeStruct((Bn, R, n2, D), BF16),
        compiler_params=_params("arbitrary", "arbitrary"),
        name="hy_s3",
    )(g_mat, z)


def _cmul(xr, xi, kr, ki):
    return xr * kr - xi * ki, xr * ki + xi * kr


def _hy_spec_kernel(a_ref, m_ref, o_ref):
    a = jnp.concatenate([a_ref[0, 0], a_ref[1, 0]], axis=0)
    o_ref[0] = _dot(m_ref[0], a, HIGHEST)


def _hy_spectrum(a, m_tab):
    n1 = a.shape[1]
    n2 = DFT_N2
    return pl.pallas_call(
        _hy_spec_kernel,
        grid=(n1,),
        in_specs=[pl.BlockSpec((2, 1, n2, D), lambda k: (0, k, 0, 0)),
                  pl.BlockSpec((1, 2 * n2, 2 * n2), lambda k: (k, 0, 0))],
        out_specs=pl.BlockSpec((1, 2 * n2, D), lambda k: (k, 0, 0)),
        out_shape=jax.ShapeDtypeStruct((n1, 2 * n2, D), F32),
        compiler_params=_params("arbitrary"),
        name="hy_spectrum",
    )(a, m_tab)


def _hy_mid_kernel(a_ref, m_ref, mt_ref, kf_ref, o_ref):
    n2 = DFT_N2
    kr = kf_ref[0, :n2]
    ki = kf_ref[0, n2:]

    def body(b, _):
        a = jnp.concatenate([a_ref[b, 0, 0], a_ref[b, 1, 0]], axis=0)
        x = _dot(m_ref[0], a)
        yr, yi = _cmul(x[:n2], x[n2:], kr, ki)
        y = jnp.concatenate([yr, yi], axis=0).astype(BF16)
        z = _dot(mt_ref[0], y)
        o_ref[b, 0, 0] = z[:n2].astype(o_ref.dtype)
        o_ref[b, 1, 0] = z[n2:].astype(o_ref.dtype)
        return 0

    lax.fori_loop(0, a_ref.shape[0], body, 0)


def _hy_mid(a, m_bf, mt_bf, kf):
    B, _, n1, n2, _ = a.shape
    blk = pl.BlockSpec((B, 2, 1, n2, D), lambda k: (0, 0, k, 0, 0))
    tab = pl.BlockSpec((1, 2 * n2, 2 * n2), lambda k: (k, 0, 0))
    return pl.pallas_call(
        _hy_mid_kernel,
        grid=(n1,),
        in_specs=[blk, tab, tab, pl.BlockSpec((1, 2 * n2, D), lambda k: (k, 0, 0))],
        out_specs=blk,
        out_shape=jax.ShapeDtypeStruct(a.shape, BF16),
        compiler_params=_params("arbitrary"),
        name="hy_mid",
    )(a, m_bf, mt_bf, kf)


def _hy_output(x0c, conv, u, bias):
    return x0c * (conv + u * bias)


def _hy_ctx_kernel(x0_ref, x1_ref, v_ref, w0_ref, w1_ref, w2_ref, b0_ref, b1_ref, b2_ref,
                   f_ref, kf_ref, g_ref, bias_ref, o_ref):
    x0 = _short_conv(x0_ref[0], w0_ref[...], b0_ref[...])
    u = _short_conv(x1_ref[0], w1_ref[...], b1_ref[...]) * _short_conv(v_ref[0], w2_ref[...], b2_ref[...])
    x = _dot(f_ref[...], u.astype(BF16))
    nf = x.shape[0] // 2
    yr, yi = _cmul(x[:nf], x[nf:], kf_ref[:nf], kf_ref[nf:])
    conv = _dot(g_ref[...], jnp.concatenate([yr, yi], axis=0).astype(BF16))
    o_ref[0] = _hy_output(x0, conv, u, bias_ref[...]).astype(o_ref.dtype)


def _hy_ctx(p_hg, conv_w, conv_b, f_c, kf_c, g_c, bias, L, Lc):
    B = p_hg.shape[0]
    ct = 256
    nb = D // ct
    rb = L // Lc
    col = lambda k: pl.BlockSpec((1, Lc, ct), lambda b, j: (b, rb, k * nb + j))
    wsp = lambda k: pl.BlockSpec((3, ct), lambda b, j: (0, k * nb + j))
    bsp = lambda k: pl.BlockSpec((1, ct), lambda b, j: (0, k * nb + j))
    return pl.pallas_call(
        _hy_ctx_kernel,
        grid=(B, nb),
        in_specs=[col(0), col(1), col(2), wsp(0), wsp(1), wsp(2), bsp(0), bsp(1), bsp(2),
                  pl.BlockSpec(f_c.shape, lambda b, j: (0, 0)),
                  pl.BlockSpec((kf_c.shape[0], ct), lambda b, j: (0, j)),
                  pl.BlockSpec(g_c.shape, lambda b, j: (0, 0)),
                  pl.BlockSpec((1, ct), lambda b, j: (0, j))],
        out_specs=pl.BlockSpec((1, Lc, ct), lambda b, j: (b, 0, j)),
        out_shape=jax.ShapeDtypeStruct((B, Lc, D), BF16),
        compiler_params=_params("arbitrary", "arbitrary"),
        name="hy_ctx",
    )(p_hg, p_hg, p_hg, conv_w, conv_w, conv_w, conv_b, conv_b, conv_b, f_c, kf_c, g_c, bias)


def _dft_tables(L, Lc):
    n = 2 * L
    n2 = DFT_N2
    n1 = n // n2
    two_pi = 2.0 * math.pi

    def cs(idx, period):
        ang = (idx % period).astype(F32) * (two_pi / period)
        return jnp.cos(ang), jnp.sin(ang)

    k1 = jnp.arange(n1)[:, None]
    c, s = cs(k1 * jnp.arange(n1)[None, :], n1)
    f1_full = jnp.concatenate([c, -s], axis=0)
    f1_half = f1_full[:, :n1 // 2]
    kk1 = jnp.arange(n1)[:, None, None]
    kk2 = jnp.arange(n2)[None, :, None]
    nn2 = jnp.arange(n2)[None, None, :]
    c, s = cs(nn2 * kk1 + n1 * nn2 * kk2, n)
    ci = -s
    m_tab = jnp.concatenate([jnp.concatenate([c, -ci], axis=2), jnp.concatenate([ci, c], axis=2)], axis=1)
    c, s = cs(jnp.arange(n1 // 2)[:, None] * jnp.arange(n1)[None, :], n1)
    g_mat = jnp.concatenate([c, -s], axis=1) / n
    nc = 2 * Lc
    kc = jnp.arange(nc)[:, None]
    c, s = cs(kc * jnp.arange(nc)[None, :], nc)
    fc_full = jnp.concatenate([c, -s], axis=0)
    c, s = cs(jnp.arange(Lc)[:, None] * jnp.arange(nc)[None, :], nc)
    g_c = jnp.concatenate([c, -s], axis=1) / nc
    return dict(f1_full=f1_full, f1_half=f1_half.astype(BF16), m_tab=m_tab, m_bf=m_tab.astype(BF16),
                mt_bf=jnp.swapaxes(m_tab, 1, 2).astype(BF16), g_mat=g_mat.astype(BF16),
                fc_full=fc_full, fc_half=fc_full[:, :Lc].astype(BF16), g_c=g_c.astype(BF16))


def _hyena(p_hg, hp, tabs, L, Lc):
    B = p_hg.shape[0]
    n2 = DFT_N2
    n1 = 2 * L // n2
    wc = min(16 * D, n2 * D)
    deltas = jnp.abs(jnp.linspace(math.log(HY_TARGET) / HY_FAST_DECAY, math.log(HY_TARGET) / HY_SLOW_DECAY,
                                  D, dtype=F32))[None, :]
    filt = functools.partial(_hy_filter, w1p=hp['w1p'], b1=hp['b1'], w2=hp['w2'], b2=hp['b2'], fr=hp['fr'],
                             w3=hp['w3'], deltas=deltas)
    k_lat = filt(L)
    a_f = _hy_s1(tabs['f1_full'], k_lat.reshape(1, n1, n2, D), wc // D, F32, HIGHEST, "hy_filt_s1")
    kf = _hy_spectrum(a_f[0], tabs['m_tab'])
    k_ctx = filt(Lc)
    kf_c = _lmat(tabs['fc_full'], k_ctx[None], F32, HIGHEST, D, "hy_filt_ctx")[0]
    x0c, u = _hy_pre(p_hg, hp['conv_w'], hp['conv_b'], L)
    a = _hy_s1(tabs['f1_half'], u.reshape(B, n1 // 2, n2, D), wc // D)
    z = _hy_mid(a, tabs['m_bf'], tabs['mt_bf'], kf)
    conv = _hy_s3(tabs['g_mat'], z, wc // D).reshape(B, L, D)
    y_c = _hy_ctx(p_hg, hp['conv_w'], hp['conv_b'], tabs['fc_half'], kf_c, tabs['g_c'], hp['bias'], L, Lc)
    return conv, x0c, u, y_c


def _merge_kernel(x_ref, mod_ref, yd_ref, cv_ref, x0_ref, u_ref, yc_ref, hb_ref, ys_ref, gd_ref, gh_ref, gs_ref,
                  wb_ref, wo_ref, g2_ref, rw_ref, xo_ref, h2_ref, lg_ref, *, n_lat_tiles):
    latent = pl.program_id(1) < n_lat_tiles
    half = x_ref.shape[1] // 2
    for rows in (slice(0, half), slice(half, 2 * half)):
        y_hy = jnp.where(latent,
                         _hy_output(x0_ref[0, rows].astype(F32), cv_ref[0, rows].astype(F32),
                                    u_ref[0, rows].astype(F32), hb_ref[...]),
                         yc_ref[0, rows].astype(F32)).astype(BF16)
        merged = None
        for k, (y, gt_ref) in enumerate(((yd_ref[0, rows], gd_ref), (y_hy, gh_ref), (ys_ref[0, rows], gs_ref))):
            term = jax.nn.sigmoid(gt_ref[0, rows]) * _dot(y, wb_ref[k])
            merged = term if merged is None else merged + term
        y = _dot(merged.astype(BF16), wo_ref[...])
        x = x_ref[0, rows] + mod_ref[0, 2:3, :] * y
        xo_ref[0, rows] = x
        h2 = _rms(x) * g2_ref[...]
        h2 = h2 * (1.0 + mod_ref[0, 4:5, :]) + mod_ref[0, 3:4, :]
        h2_ref[0, rows] = h2.astype(h2_ref.dtype)
        lg_ref[0, :, rows] = _dot(h2, rw_ref[...], HIGHEST).T[:N_EXPERTS]


def _merge(xa, mod, y_diff, hy, hy_bias, y_swa, p_hg, w_branch, w_out, g2, rw_pad, n_lat_tiles):
    B, T, _ = xa.shape
    ctx_row = B
    conv, x0c, u, y_hy_c = hy
    tile = lambda c: pl.BlockSpec((1, TM, D), lambda b, i: (b, i, c))
    lat = pl.BlockSpec((1, TM, D), lambda b, i: (b, jnp.minimum(i, n_lat_tiles - 1), 0))
    return pl.pallas_call(
        functools.partial(_merge_kernel, n_lat_tiles=n_lat_tiles),
        grid=(B, T // TM),
        in_specs=[tile(0),
                  pl.BlockSpec((1, 6, D), lambda b, i: (jnp.where(i >= n_lat_tiles, ctx_row, b), 0, 0)),
                  tile(0), lat, lat, lat,
                  pl.BlockSpec((1, y_hy_c.shape[1], D), lambda b, i: (b, 0, 0)),
                  pl.BlockSpec((1, D), lambda b, i: (0, 0)),
                  tile(0), tile(3), tile(4), tile(5),
                  pl.BlockSpec((3, D, D), lambda b, i: (0, 0, 0)),
                  pl.BlockSpec((D, D), lambda b, i: (0, 0)),
                  pl.BlockSpec((1, D), lambda b, i: (0, 0)),
                  pl.BlockSpec((D, LANES), lambda b, i: (0, 0))],
        out_specs=[tile(0), tile(0), pl.BlockSpec((1, N_EXPERTS, TM), lambda b, i: (b, 0, i))],
        out_shape=[jax.ShapeDtypeStruct((B, T, D), F32), jax.ShapeDtypeStruct((B, T, D), BF16),
                   jax.ShapeDtypeStruct((B, N_EXPERTS, T), F32)],
        compiler_params=_params("arbitrary", "arbitrary"),
        name="merge",
    )(xa, mod, y_diff, conv, x0c, u, y_hy_c, hy_bias, y_swa, p_hg, p_hg, p_hg, w_branch, w_out, g2, rw_pad)


def _router_kernel(lg_ref, slot_ref, gate_ref, first_ref, *, segments):
    lg = lg_ref[0]
    e = jnp.exp(lg - jnp.max(lg, axis=0, keepdims=True))
    aff = e / jnp.sum(e, axis=0, keepdims=True)
    r = lax.broadcasted_iota(jnp.int32, (LANES, LANES), 0)
    c = lax.broadcasted_iota(jnp.int32, (LANES, LANES), 1)
    before = (r < c).astype(BF16)

    def prefix_count(mask, n):
        run = jnp.zeros((N_EXPERTS, 1), F32)
        outs = []
        for j in range(n // LANES):
            blk = mask[:, j * LANES:(j + 1) * LANES].astype(F32).astype(BF16)
            outs.append(_dot(blk, before) + run)
            run = run + jnp.sum(blk.astype(F32), axis=1, keepdims=True)
        return jnp.concatenate(outs, axis=1)

    for start, n, cap, slot_base in segments:
        a = aff[:, start:start + n]
        bits = pltpu.bitcast(a, jnp.int32)

        def search(_, bounds):
            lo, hi = bounds
            mid = lo + ((hi - lo + 1) >> 1)
            cnt = jnp.sum((bits >= mid).astype(F32), axis=1, keepdims=True)
            ok = cnt >= cap
            return jnp.where(ok, mid, lo), jnp.where(ok, hi, mid - 1)

        lo0 = jnp.zeros((N_EXPERTS, 1), jnp.int32)
        hi0 = jnp.full((N_EXPERTS, 1), 0x7F800000, jnp.int32)
        thr, _ = lax.fori_loop(0, 32, search, (lo0, hi0))
        above = bits > thr
        tie = bits == thr
        need = cap - jnp.sum(above.astype(F32), axis=1, keepdims=True)
        sel = above | (tie & (prefix_count(tie, n) < need))
        pos = prefix_count(sel, n)
        slot_ref[0, :, start:start + n] = jnp.where(sel, pos + slot_base, -1.0)
        gate_ref[0, :, start:start + n] = jnp.where(sel, a, 0.0)
        if start == 0:
            t_idx = lax.broadcasted_iota(jnp.int32, (n, LANES), 0)
            tile_start = lax.broadcasted_iota(jnp.int32, (n, LANES), 1) * TM
            before_tile = jnp.where(t_idx < tile_start, 1.0, 0.0).astype(BF16)
            first_ref[0] = _dot(jnp.where(sel, 1.0, 0.0).astype(BF16), before_tile)


def _router(logits_t, segments):
    B, E, T = logits_t.shape
    blk = pl.BlockSpec((1, E, T), lambda b: (b, 0, 0))
    return pl.pallas_call(
        functools.partial(_router_kernel, segments=segments),
        grid=(B,),
        in_specs=[blk],
        out_specs=[blk, blk, pl.BlockSpec((1, E, LANES), lambda b: (b, 0, 0))],
        out_shape=[jax.ShapeDtypeStruct((B, E, T), F32)] * 2 + [jax.ShapeDtypeStruct((B, E, LANES), F32)],
        compiler_params=_params("arbitrary"),
        name="router",
    )(logits_t)


def _slot_window(first, align, rows, n_slots):
    return pl.multiple_of(jnp.minimum(first // align * align, n_slots - rows), align)


def _moe_ffn_kernel(first_ref, busy_ref, slot_ref, gate_ref, h2_ref, w1_ref, w3_ref, w2_ref, o_ref, xg_scr, g_scr,
                    *, n_slots, n_tiles):
    pair = pl.program_id(0) * N_EXPERTS + pl.program_id(1)
    base = pair * n_tiles
    xg_scr[...] = jnp.zeros_like(xg_scr)
    g_scr[...] = jnp.zeros_like(g_scr)

    def gather(rows):
        row_id = lax.broadcasted_iota(jnp.int32, (rows, TM), 0)
        for c in range(n_tiles):
            tok = slice(c * TM, (c + 1) * TM)
            w0 = _slot_window(first_ref[base + c], 8, rows, n_slots)
            hit = slot_ref[0, 0, :, tok] == (row_id + w0).astype(F32)
            win = pl.ds(w0, rows)
            xg_scr[win, :] += _dot(jnp.where(hit, 1.0, 0.0).astype(BF16), h2_ref[0, tok, :])
            g_scr[win, :] += jnp.sum(jnp.where(hit, gate_ref[0, 0, :, tok], 0.0), axis=1, keepdims=True)

    sparse = busy_ref[pair] <= SPARSE_HITS

    @pl.when(sparse)
    def _():
        gather(min(SPARSE_HITS + 8, n_slots))

    @pl.when(jnp.logical_not(sparse))
    def _():
        gather(min(TM + 8, n_slots))

    xb = xg_scr[...].astype(BF16)
    a = _dot(xb, w1_ref[0, 0])
    b = _dot(xb, w3_ref[0, 0])
    hidden = (a * jax.nn.sigmoid(a) * b).astype(BF16)
    o_ref[0, 0] = (_dot(hidden, w2_ref[0, 0]) * g_scr[...]).astype(o_ref.dtype)


def _moe_ffn(first, busy, slot, gate, h2, w1, w3, w2, l, n_slots):
    B, E, T = slot.shape
    row = pl.BlockSpec((1, 1, 1, T), lambda b, e: (b, e, 0, 0))
    wsp = pl.BlockSpec((1, 1, D, D), lambda b, e: (l, e, 0, 0))
    smem = pl.BlockSpec(memory_space=pltpu.SMEM)
    return pl.pallas_call(
        functools.partial(_moe_ffn_kernel, n_slots=n_slots, n_tiles=T // TM),
        grid=(B, E),
        in_specs=[smem, smem, row, row, pl.BlockSpec((1, T, D), lambda b, e: (b, 0, 0)), wsp, wsp, wsp],
        out_specs=pl.BlockSpec((1, 1, n_slots, D), lambda b, e: (b, e, 0, 0)),
        out_shape=jax.ShapeDtypeStruct((B, E, n_slots, D), BF16),
        scratch_shapes=[pltpu.VMEM((n_slots, D), F32), pltpu.VMEM((n_slots, 1), F32)],
        compiler_params=_params("arbitrary", "arbitrary"),
        name="moe_ffn",
    )(first, busy, slot.reshape(B, E, 1, T), gate.reshape(B, E, 1, T), h2, w1, w3, w2)


def _moe_combine_kernel(first_ref, busy_ref, x_ref, mod_ref, slot_ref, y_ref, fg_ref, o_ref, acc_scr,
                        *, n_slots, n_tiles, n_lat_tiles, final):
    tm = x_ref.shape[1]

    def combine(rows):
        col_id = lax.broadcasted_iota(jnp.int32, (tm, rows), 1)
        acc = jnp.zeros((tm, D), F32)
        for e in range(N_EXPERTS):
            w0 = _slot_window(first_ref[(pl.program_id(0) * N_EXPERTS + e) * n_tiles + pl.program_id(1)], 16, rows,
                              n_slots)
            hit = slot_ref[0, :, e:e + 1] == (col_id + w0).astype(F32)
            acc = acc + _dot(jnp.where(hit, 1.0, 0.0).astype(BF16), y_ref[0, e, pl.ds(w0, rows), :])
        acc_scr[...] = acc

    sparse = busy_ref[pl.program_id(0) * n_tiles + pl.program_id(1)] <= SPARSE_HITS

    @pl.when(sparse)
    def _():
        combine(min(SPARSE_HITS + 16, n_slots))

    @pl.when(jnp.logical_not(sparse))
    def _():
        combine(min(tm + 16, n_slots))

    x = x_ref[0] + mod_ref[0, 5:6, :] * acc_scr[...]
    if final:
        @pl.when(pl.program_id(1) < n_lat_tiles)
        def _():
            o_ref[0] = _rms(x) * fg_ref[...]
    else:
        o_ref[0] = x


def _moe_combine(first, busy, xa, mod, slot_t, y, final_g, n_lat_tiles, n_slots, final):
    B, T, _ = xa.shape
    ctx_row = B
    tile = pl.BlockSpec((1, TM, D), lambda b, i: (b, i, 0))
    smem = pl.BlockSpec(memory_space=pltpu.SMEM)
    if final:
        out_spec = pl.BlockSpec((1, TM, D), lambda b, i: (b, jnp.minimum(i, n_lat_tiles - 1), 0))
        out_rows = n_lat_tiles * TM
    else:
        out_spec, out_rows = tile, T
    return pl.pallas_call(
        functools.partial(_moe_combine_kernel, n_slots=n_slots, n_tiles=T // TM, n_lat_tiles=n_lat_tiles,
                          final=final),
        grid=(B, T // TM),
        in_specs=[smem, smem, tile,
                  pl.BlockSpec((1, 6, D), lambda b, i: (jnp.where(i >= n_lat_tiles, ctx_row, b), 0, 0)),
                  pl.BlockSpec((1, TM, N_EXPERTS), lambda b, i: (b, i, 0)),
                  pl.BlockSpec((1, N_EXPERTS, n_slots, D), lambda b, i: (b, 0, 0, 0)),
                  pl.BlockSpec((1, D), lambda b, i: (0, 0))],
        out_specs=out_spec,
        out_shape=jax.ShapeDtypeStruct((B, out_rows, D), F32),
        scratch_shapes=[pltpu.VMEM((TM, D), F32)],
        compiler_params=_params("arbitrary", "arbitrary"),
        name="moe_combine",
    )(first, busy, xa, mod, slot_t, y, final_g)


def _swa_head_perm():
    return [kv * SWA_GROUP + g for g in range(SWA_GROUP) for kv in range(SWA_KV)]


def _split_w_in(w_in):
    depth = w_in.shape[0]
    o = [0, D, 2 * D, 3 * D, 6 * D, 7 * D, 7 * D + SWA_KV * HEAD, 7 * D + 2 * SWA_KV * HEAD, 10 * D + 2 * SWA_KV * HEAD]
    dq, dk, dv, hy, sq, sk, sv, gt = (w_in[:, :, o[k]:o[k + 1]] for k in range(8))
    scale = HEAD ** -0.5 * math.log2(math.e)
    sq = sq.reshape(depth, D, SWA_KV * SWA_GROUP, HEAD)[:, :, jnp.array(_swa_head_perm())].reshape(depth, D, D)
    w_attn = jnp.concatenate([dq * scale, dk, dv, sq * scale, sk, sv], axis=-1).astype(BF16)
    w_hg = jnp.concatenate([hy, gt], axis=-1).astype(BF16)
    return w_attn, w_hg


def kernel(x, c, ctx, c_ctx, w_mod, b_mod, norm1_g, norm2_g, w_in, diff_lambda, diff_subln_g, hy_conv_w,
           hy_conv_b, hy_ff_w1, hy_ff_b1, hy_ff_w2, hy_ff_b2, hy_ff_w3, hy_sin_freq, hy_bias, swa_sink,
           w_branch, w_out, router_w, moe_w1, moe_w3, moe_w2, final_g):
    B, L, _ = x.shape
    Lc = ctx.shape[1]
    depth = w_mod.shape[0]
    T = L + Lc
    assert Lc == TM and L % (4 * TM) == 0 and x.shape[2] == D
    n_lat_tiles = L // TM
    cap_l = CAPACITY_FACTOR * L // N_EXPERTS
    cap_c = CAPACITY_FACTOR * Lc // N_EXPERTS
    n_slots = cap_l + cap_c
    segments = ((0, L, cap_l, 0), (L, Lc, cap_c, cap_l))

    xa = jnp.concatenate([x, ctx], axis=1)
    mods = _mods(c, c_ctx, w_mod, b_mod)
    cos_t, sin_t = _rope_tables(L, Lc)
    tabs = _dft_tables(L, Lc)
    w_attn, w_hg = _split_w_in(w_in)
    rope_flags = tuple([True] * 16 + [False] * 8 + [True] * 8 + [True] * 2 + [False] * 2)
    perm = jnp.array(_swa_head_perm())
    wb = w_branch.at[:, 2].set(
        w_branch[:, 2].reshape(depth, SWA_KV * SWA_GROUP, HEAD, D)[:, perm].reshape(depth, D, D)).astype(BF16)
    wo = w_out.astype(BF16)
    w1, w3, w2 = moe_w1.astype(BF16), moe_w3.astype(BF16), moe_w2.astype(BF16)
    rw_pad = jnp.pad(router_w, ((0, 0), (0, 0), (0, LANES - N_EXPERTS)))
    w1p = jnp.pad(hy_ff_w1, ((0, 0), (0, LANES - HY_EMB), (0, 0)))

    for l in range(depth):
        lambda_init = 0.8 - 0.6 * math.exp(-0.3 * l)
        mod = mods[l]
        p_attn = _norm_proj(xa, mod, norm1_g[l][None], w_attn[l], cos_t, sin_t, rope_flags, BF16,
                            n_lat_tiles, "proj_attn")
        p_hg = _norm_proj(xa, mod, norm1_g[l][None], w_hg[l], cos_t, sin_t, (False,) * (N_HG // LANES), F32,
                          n_lat_tiles, "proj_hg")
        y_diff = _diff_attn(p_attn, diff_lambda, diff_subln_g, L, Lc, lambda_init, l)
        y_swa = _swa(p_attn, swa_sink, L, Lc, l)
        hp = dict(w1p=w1p[l], b1=hy_ff_b1[l][None], w2=hy_ff_w2[l], b2=hy_ff_b2[l][None],
                  fr=hy_sin_freq[l][None], w3=hy_ff_w3[l], conv_w=hy_conv_w[l], conv_b=hy_conv_b[l][None],
                  bias=hy_bias[l][None])
        hy = _hyena(p_hg, hp, tabs, L, Lc)
        xa, h2, logits_t = _merge(xa, mod, y_diff, hy, hp['bias'], y_swa, p_hg, wb[l], wo[l], norm2_g[l][None],
                                  rw_pad[l], n_lat_tiles)
        slot, gate, first = _router(logits_t, segments)
        first = jnp.concatenate([first[:, :, :n_lat_tiles], jnp.full((B, N_EXPERTS, 1), cap_l, F32)], axis=2)
        first = first.astype(jnp.int32)
        ends = jnp.concatenate([first[:, :, 1:n_lat_tiles], jnp.full((B, N_EXPERTS, 1), cap_l, jnp.int32),
                                jnp.full((B, N_EXPERTS, 1), n_slots, jnp.int32)], axis=2)
        hits = ends - first
        first = first.reshape(-1)
        y_e = _moe_ffn(first, jnp.max(hits, axis=2).reshape(-1), slot, gate, h2, w1, w3, w2, l, n_slots)
        xa = _moe_combine(first, jnp.max(hits, axis=1).reshape(-1), xa, mod, jnp.swapaxes(slot, 1, 2), y_e,
                          final_g[None], n_lat_tiles, n_slots, final=l == depth - 1)
    return xa
```

```python
import functools
import math

import jax
import jax.numpy as jnp
from jax import lax
from jax.experimental import pallas as pl
from jax.experimental.pallas import tpu as pltpu

F32 = jnp.float32
BF16 = jnp.bfloat16
HIGHEST = lax.Precision.HIGHEST

D = 1024
HEAD = 64
GRID_W = 64
ROPE_BASE = 10000.0
EPS = 1e-6
N_DIFF_HEADS = D // (2 * HEAD)
DIFF_V = 2 * HEAD
SWA_KV = 4
SWA_GROUP = 4
WINDOW = 128
N_EXPERTS = 16
CAPACITY_FACTOR = 2
HY_BANDS = 16
HY_EMB = 1 + 2 * HY_BANDS
HY_FF = 64
HY_FAST_DECAY = 0.3
HY_SLOW_DECAY = 1.5
HY_TARGET = 1e-2

LANES = 128
TM = 256
DFT_N2 = 128
VMEM_LIMIT = 56 * 1024 * 1024

SPARSE_HITS = 64

N_ATTN = 3 * D + D + 2 * SWA_KV * HEAD
N_HG = 3 * D + 3 * D


def _params(*sem):
    return pltpu.CompilerParams(dimension_semantics=sem, vmem_limit_bytes=VMEM_LIMIT)


def _dot(a, b, precision=None):
    return jnp.dot(a, b, preferred_element_type=F32, precision=precision)


def _dot_nt(a, b, precision=None):
    return lax.dot_general(a, b, (((1,), (1,)), ((), ())), preferred_element_type=F32, precision=precision)


def _rms(x):
    return x * lax.rsqrt(jnp.mean(x * x, axis=-1, keepdims=True) + EPS)


def _mod_kernel(c_ref, w_ref, b_ref, o_ref):
    c = c_ref[...]
    sc = c * jax.nn.sigmoid(c)
    o_ref[0] = _dot(sc, w_ref[0], HIGHEST) + b_ref[0]


def _mods(c, c_ctx, w_mod, b_mod):
    B = c.shape[0]
    depth = w_mod.shape[0]
    rows = -(-(B + 1) // 8) * 8
    cc = jnp.zeros((rows, D), F32).at[:B].set(c).at[B].set(c_ctx)
    out = pl.pallas_call(
        _mod_kernel,
        grid=(depth, 6),
        in_specs=[pl.BlockSpec((rows, D), lambda l, j: (0, 0)),
                  pl.BlockSpec((1, D, D), lambda l, j: (l, 0, j)),
                  pl.BlockSpec((1, 1, D), lambda l, j: (l, 0, j))],
        out_specs=pl.BlockSpec((1, rows, D), lambda l, j: (l, 0, j)),
        out_shape=jax.ShapeDtypeStruct((depth, rows, 6 * D), F32),
        compiler_params=_params("arbitrary", "arbitrary"),
        name="mods",
    )(cc, w_mod, b_mod.reshape(depth, 1, 6 * D))
    return out.reshape(depth, rows, 6, D)


def _norm_proj_kernel(x_ref, mod_ref, g_ref, w_ref, cos_ref, sin_ref, o_ref, *, rope_flags):
    x = x_ref[0]
    h = _rms(x) * g_ref[...]
    h = h * (1.0 + mod_ref[0, 1:2, :]) + mod_ref[0, 0:1, :]
    hb = h.astype(BF16)
    n_sub = len(rope_flags)
    sub_per_chunk = 4
    if any(rope_flags):
        cos = cos_ref[...]
        sin = sin_ref[...]
        lane = lax.broadcasted_iota(jnp.int32, cos.shape, 1)
        first_half = (lane % (HEAD // 2)) < (HEAD // 4)
    for c0 in range(0, n_sub, sub_per_chunk):
        c1 = min(c0 + sub_per_chunk, n_sub)
        acc = _dot(hb, w_ref[:, c0 * LANES:c1 * LANES])
        for s in range(c0, c1):
            a = acc[:, (s - c0) * LANES:(s - c0 + 1) * LANES]
            if rope_flags[s]:
                partner = jnp.where(first_half, pltpu.roll(a, LANES - HEAD // 4, 1), pltpu.roll(a, HEAD // 4, 1))
                a = a * cos + partner * sin
            o_ref[0, :, s * LANES:(s + 1) * LANES] = a.astype(o_ref.dtype)


def _norm_proj(xa, mod, g, w, cos_t, sin_t, rope_flags, out_dtype, n_lat_tiles, name):
    B, T, _ = xa.shape
    N = w.shape[1]
    ctx_row = B
    return pl.pallas_call(
        functools.partial(_norm_proj_kernel, rope_flags=rope_flags),
        grid=(B, T // TM),
        in_specs=[pl.BlockSpec((1, TM, D), lambda b, i: (b, i, 0)),
                  pl.BlockSpec((1, 6, D), lambda b, i: (jnp.where(i >= n_lat_tiles, ctx_row, b), 0, 0)),
                  pl.BlockSpec((1, D), lambda b, i: (0, 0)),
                  pl.BlockSpec((D, N), lambda b, i: (0, 0)),
                  pl.BlockSpec((TM, LANES), lambda b, i: (i, 0)),
                  pl.BlockSpec((TM, LANES), lambda b, i: (i, 0))],
        out_specs=pl.BlockSpec((1, TM, N), lambda b, i: (b, i, 0)),
        out_shape=jax.ShapeDtypeStruct((B, T, N), out_dtype),
        compiler_params=_params("arbitrary", "arbitrary"),
        name=name,
    )(xa, mod, g, w, cos_t, sin_t)


def _rope_tables(L, Lc):
    t = jnp.arange(L)
    pos = jnp.stack([(t // GRID_W).astype(F32), (t % GRID_W).astype(F32)], axis=1)
    n_freq = HEAD // 4
    inv = ROPE_BASE ** (-jnp.arange(n_freq, dtype=F32) / n_freq)
    ang = pos[:, :, None] * inv
    cos = jnp.cos(ang)
    sin = jnp.sin(ang)
    cos_h = jnp.concatenate([cos, cos], axis=-1).reshape(L, HEAD)
    sin_h = jnp.concatenate([-sin, sin], axis=-1).reshape(L, HEAD)
    cos_t = jnp.concatenate([jnp.tile(cos_h, (1, LANES // HEAD)), jnp.ones((Lc, LANES), F32)], axis=0)
    sin_t = jnp.concatenate([jnp.tile(sin_h, (1, LANES // HEAD)), jnp.zeros((Lc, LANES), F32)], axis=0)
    return cos_t, sin_t


def _lane_block_reduce(x, op):
    out = x[:, :LANES]
    for j in range(1, x.shape[1] // LANES):
        out = op(out, x[:, j * LANES:(j + 1) * LANES])
    return out


def _diff_attn_kernel(lam_ref, q_ref, k_ref, v_ref, g_ref, o_ref, s_lat, s_ctx, m_run, qq_scr, l_scr, acc_scr,
                      *, L, Lc, tg, lambda_init):
    tq = TM
    n_groups = (L + Lc) // tg
    n_lat = L // tq
    n_tiles = (L + Lc) // tq
    rows = (2 * tq, LANES)
    lowest = jnp.full(rows, -jnp.inf, F32)
    nothing = jnp.zeros(rows, F32)
    lp = lam_ref[0]
    lam = (jnp.exp(jnp.sum(lp[0:1] * lp[1:2], axis=-1, keepdims=True))
           - jnp.exp(jnp.sum(lp[2:3] * lp[3:4], axis=-1, keepdims=True)) + lambda_init)

    def stage(t, _):
        cur = t % 2
        prv = 1 - cur
        score_any = t < n_tiles
        score_lat = t < n_lat
        weigh_any = t >= 1
        weigh_lat = jnp.logical_and(weigh_any, t <= n_lat)

        @pl.when(score_any)
        def _():
            q = q_ref[0, pl.ds(pl.multiple_of(t * tq, tq), tq), :]
            lane = lax.broadcasted_iota(jnp.int32, q.shape, 1)
            zero = jnp.zeros_like(q)
            qq_scr[...] = jnp.concatenate([jnp.where(lane < HEAD, q, zero), jnp.where(lane >= HEAD, q, zero)],
                                          axis=0)

        @pl.when(jnp.logical_and(score_any, jnp.logical_not(score_lat)))
        def _():
            s_c = _dot_nt(qq_scr[...], k_ref[0, L:L + Lc, :])
            s_ctx[...] = s_c
            m_run[cur] = _lane_block_reduce(s_c, jnp.maximum)

        def row_max():
            return jnp.max(m_run[prv], axis=-1, keepdims=True)

        @pl.when(jnp.logical_and(weigh_any, jnp.logical_not(weigh_lat)))
        def _():
            p = jnp.exp2(s_ctx[...] - row_max())
            l_scr[...] = _lane_block_reduce(p, jnp.add)
            acc_scr[...] = _dot(p.astype(BF16), v_ref[0, L:L + Lc, :])

        def scores(gi, mrun):
            start = pl.multiple_of(gi * tg, LANES)
            s = _dot_nt(qq_scr[...], k_ref[0, pl.ds(start, tg), :])
            s_lat[cur, gi] = s
            return jnp.maximum(mrun, _lane_block_reduce(s, jnp.maximum))

        def weighted(gi, lrun, acc, m):
            start = pl.multiple_of(gi * tg, LANES)
            p = jnp.exp2(s_lat[prv, gi] - m)
            return (lrun + _lane_block_reduce(p, jnp.add),
                    acc + _dot(p.astype(BF16), v_ref[0, pl.ds(start, tg), :]))

        @pl.when(jnp.logical_and(score_lat, weigh_lat))
        def _():
            m = row_max()

            def both(gi, carry):
                mrun, lrun, acc = carry
                lrun, acc = weighted(gi, lrun, acc, m)
                return scores(gi, mrun), lrun, acc

            m_run[cur], l_scr[...], acc_scr[...] = lax.fori_loop(0, n_groups, both, (lowest, nothing, nothing))

        @pl.when(jnp.logical_and(score_lat, jnp.logical_not(weigh_lat)))
        def _():
            m_run[cur] = lax.fori_loop(0, n_groups, scores, lowest)

        @pl.when(jnp.logical_and(jnp.logical_not(score_lat), weigh_lat))
        def _():
            m = row_max()
            l_scr[...], acc_scr[...] = lax.fori_loop(0, n_groups, lambda gi, c: weighted(gi, c[0], c[1], m),
                                                      (nothing, nothing))

        @pl.when(weigh_any)
        def _():
            o = acc_scr[...] / jnp.sum(l_scr[...], axis=-1, keepdims=True)
            o = _rms(o[:tq] - lam * o[tq:])
            out_rows = pl.ds(pl.multiple_of((t - 1) * tq, tq), tq)
            o_ref[0, out_rows, :] = (o * g_ref[0] * (1.0 - lambda_init)).astype(o_ref.dtype)

        return 0

    lax.fori_loop(0, n_tiles + 1, stage, 0)


def _diff_attn(p_attn, diff_lambda, subln_g, L, Lc, lambda_init, l):
    B, T, _ = p_attn.shape
    tg = T // 2
    nh = N_DIFF_HEADS
    head = lambda c0: pl.BlockSpec((1, T, DIFF_V), lambda b, h: (b, 0, c0 + h))
    return pl.pallas_call(
        functools.partial(_diff_attn_kernel, L=L, Lc=Lc, tg=tg, lambda_init=lambda_init),
        grid=(B, nh),
        in_specs=[pl.BlockSpec((1, 4, HEAD), lambda b, h: (l, 0, 0)),
                  head(0), head(nh), head(2 * nh),
                  pl.BlockSpec((1, 1, DIFF_V), lambda b, h: (l, 0, 0))],
        out_specs=head(0),
        out_shape=jax.ShapeDtypeStruct((B, T, D), BF16),
        scratch_shapes=[pltpu.VMEM((2, T // tg, 2 * TM, tg), F32), pltpu.VMEM((2 * TM, Lc), F32),
                        pltpu.VMEM((2, 2 * TM, LANES), F32), pltpu.VMEM((2 * TM, DIFF_V), BF16),
                        pltpu.VMEM((2 * TM, LANES), F32), pltpu.VMEM((2 * TM, DIFF_V), F32)],
        compiler_params=_params("arbitrary", "arbitrary"),
        name="diff_attn",
    )(diff_lambda, p_attn, p_attn, p_attn, subln_g.reshape(-1, 1, DIFF_V))


def _swa_kernel(sink_ref, q_ref, k_ref, v_ref, o_ref, *, L, Lc, l):
    i = pl.program_id(1)
    tq = q_ref.shape[1]
    kvw = SWA_KV * HEAD
    win = tq + 2 * WINDOW
    lane_head = lax.broadcasted_iota(jnp.int32, (tq, kvw), 1) // HEAD
    k_ctx = k_ref[0, L:L + Lc, :]
    v_ctx = v_ref[0, L:L + Lc, :]

    def run(latent):
        if latent:
            start = pl.multiple_of(jnp.clip(i * tq - WINDOW, 0, L - win), WINDOW)
            k_win = k_ref[0, pl.ds(start, win), :]
            v_win = v_ref[0, pl.ds(start, win), :]
            q_pos = i * tq + lax.broadcasted_iota(jnp.int32, (tq, win), 0)
            k_pos = start + lax.broadcasted_iota(jnp.int32, (tq, win), 1)
            band_bias = jnp.where(jnp.abs(k_pos - q_pos) <= WINDOW, 0.0, -jnp.inf)
        for g in range(SWA_GROUP):
            qg = q_ref[0, :, g * kvw:(g + 1) * kvw]
            acc = jnp.zeros((tq, kvw), F32)
            for kv in range(SWA_KV):
                qe = jnp.where(lane_head == kv, qg, jnp.zeros_like(qg))
                snk = sink_ref[l, kv * SWA_GROUP + g] * math.log2(math.e)
                s_c = _dot_nt(qe, k_ctx)
                m = jnp.maximum(jnp.max(s_c, axis=-1, keepdims=True), snk)
                if latent:
                    s_w = _dot_nt(qe, k_win) + band_bias
                    m = jnp.maximum(m, jnp.max(s_w, axis=-1, keepdims=True))
                p_c = jnp.exp2(s_c - m)
                den = jnp.sum(p_c, axis=-1, keepdims=True) + jnp.exp2(snk - m)
                o = _dot(p_c.astype(BF16), v_ctx)
                if latent:
                    p_w = jnp.exp2(s_w - m)
                    den = den + jnp.sum(p_w, axis=-1, keepdims=True)
                    o = o + _dot(p_w.astype(BF16), v_win)
                acc = acc + jnp.where(lane_head == kv, o / den, 0.0)
            o_ref[0, :, g * kvw:(g + 1) * kvw] = acc.astype(o_ref.dtype)

    @pl.when(i < L // tq)
    def _():
        run(True)

    @pl.when(i >= L // tq)
    def _():
        run(False)


def _swa(p_attn, sink, L, Lc, l):
    B, T, _ = p_attn.shape
    kvw = SWA_KV * HEAD
    return pl.pallas_call(
        functools.partial(_swa_kernel, L=L, Lc=Lc, l=l),
        grid=(B, T // TM),
        in_specs=[pl.BlockSpec(memory_space=pltpu.SMEM),
                  pl.BlockSpec((1, TM, D), lambda b, i: (b, i, 3)),
                  pl.BlockSpec((1, T, kvw), lambda b, i: (b, 0, 4 * D // kvw)),
                  pl.BlockSpec((1, T, kvw), lambda b, i: (b, 0, 4 * D // kvw + 1))],
        out_specs=pl.BlockSpec((1, TM, D), lambda b, i: (b, i, 0)),
        out_shape=jax.ShapeDtypeStruct((B, T, D), BF16),
        compiler_params=_params("arbitrary", "arbitrary"),
        name="swa",
    )(sink, p_attn, p_attn, p_attn)


def _hy_filter_kernel(emb_ref, w1_ref, b1_ref, w2_ref, b2_ref, fr_ref, w3f_ref, w3b_ref, dl_ref, o_ref, *, L):
    rows = TM
    fr = fr_ref[...]

    def raw(i, total):
        r0 = pl.multiple_of(i * rows, rows)
        e = emb_ref[pl.ds(r0, rows), :]
        z = jnp.sin(fr * (_dot(e, w1_ref[...], HIGHEST) + b1_ref[...]))
        z = jnp.sin(fr * (_dot(z, w2_ref[...], HIGHEST) + b2_ref[...]))
        row = r0 + lax.broadcasted_iota(jnp.int32, (rows, 1), 0)
        h = jnp.where(row < L, _dot(z, w3f_ref[...], HIGHEST), _dot(z, w3b_ref[...], HIGHEST))
        h = h * jnp.exp(-e[:, 0:1] * dl_ref[...])
        o_ref[pl.ds(r0, rows), :] = h
        return total + jnp.sum(jnp.abs(h), axis=0, keepdims=True)

    n_chunks = emb_ref.shape[0] // rows
    norm = lax.fori_loop(0, n_chunks, raw, jnp.zeros((1, o_ref.shape[1]), F32)) + EPS

    def scale(i, _):
        r0 = pl.multiple_of(i * rows, rows)
        row = r0 + lax.broadcasted_iota(jnp.int32, (rows, 1), 0)
        o_ref[pl.ds(r0, rows), :] = jnp.where(row == L, 0.0, o_ref[pl.ds(r0, rows), :] / norm)
        return 0

    lax.fori_loop(0, n_chunks, scale, 0)


def _hy_embedding(L):
    n = jnp.arange(2 * L)
    lag = jnp.where(n < L, n, jnp.where(n == L, 0, 2 * L - n))
    t = jnp.linspace(0.0, 1.0, L, dtype=F32)[lag][:, None]
    w = (2.0 * math.pi * jnp.arange(L, dtype=F32) / L)[lag][:, None]
    f = jnp.linspace(1e-4, HY_BANDS - 1, HY_BANDS, dtype=F32)[None, :]
    emb = jnp.concatenate([t, jnp.cos(f * w), -jnp.sin(f * w)], axis=-1)
    return jnp.pad(emb, ((0, 0), (0, LANES - HY_EMB)))


def _hy_filter(L, w1p, b1, w2, b2, fr, w3, deltas):
    ct = 512
    emb = _hy_embedding(L)
    full = lambda shape: pl.BlockSpec(shape, lambda j: (0, 0))
    return pl.pallas_call(
        functools.partial(_hy_filter_kernel, L=L),
        grid=(D // ct,),
        in_specs=[full((2 * L, LANES)), full((LANES, HY_FF)), full((1, HY_FF)), full((HY_FF, HY_FF)),
                  full((1, HY_FF)), full((1, HY_FF)),
                  pl.BlockSpec((HY_FF, ct), lambda j: (0, j)),
                  pl.BlockSpec((HY_FF, ct), lambda j: (0, D // ct + j)),
                  pl.BlockSpec((1, ct), lambda j: (0, j))],
        out_specs=pl.BlockSpec((2 * L, ct), lambda j: (0, j)),
        out_shape=jax.ShapeDtypeStruct((2 * L, D), F32),
        compiler_params=_params("arbitrary"),
        name="hy_filter",
    )(emb, w1p, b1, w2, b2, fr, w3, w3, deltas)


def _short_conv(x, w, b):
    n = x.shape[0]
    row = lax.broadcasted_iota(jnp.int32, x.shape, 0)
    prev = jnp.where(row == 0, 0.0, pltpu.roll(x, 1, 0))
    nxt = jnp.where(row == n - 1, 0.0, pltpu.roll(x, n - 1, 0))
    return prev * w[0:1] + x * w[1:2] + nxt * w[2:3] + b


def _hy_pre_kernel(x0_ref, x1_ref, v_ref, w0_ref, w1_ref, w2_ref, b0_ref, b1_ref, b2_ref, x0c_ref, u_ref):
    x0c_ref[0] = _short_conv(x0_ref[0], w0_ref[...], b0_ref[...]).astype(x0c_ref.dtype)
    u = _short_conv(x1_ref[0], w1_ref[...], b1_ref[...]) * _short_conv(v_ref[0], w2_ref[...], b2_ref[...])
    u_ref[0] = u.astype(u_ref.dtype)


def _hy_pre(p_hg, conv_w, conv_b, L):
    B = p_hg.shape[0]
    ct = LANES
    nb = D // ct
    col = lambda k: pl.BlockSpec((1, L, ct), lambda b, j: (b, 0, k * nb + j))
    wsp = lambda k: pl.BlockSpec((3, ct), lambda b, j: (0, k * nb + j))
    bsp = lambda k: pl.BlockSpec((1, ct), lambda b, j: (0, k * nb + j))
    out = pl.BlockSpec((1, L, ct), lambda b, j: (b, 0, j))
    return pl.pallas_call(
        _hy_pre_kernel,
        grid=(B, nb),
        in_specs=[col(0), col(1), col(2), wsp(0), wsp(1), wsp(2), bsp(0), bsp(1), bsp(2)],
        out_specs=[out, out],
        out_shape=[jax.ShapeDtypeStruct((B, L, D), BF16)] * 2,
        compiler_params=_params("arbitrary", "arbitrary"),
        name="hy_pre",
    )(p_hg, p_hg, p_hg, conv_w, conv_w, conv_w, conv_b, conv_b, conv_b)


def _lmat_kernel(m_ref, x_ref, o_ref, *, precision):
    x = x_ref[0]
    if precision is None:
        x = x.astype(BF16)
    o_ref[0] = _dot(m_ref[...], x, precision).astype(o_ref.dtype)


def _lmat(mat, x, out_dtype, precision, wc, name):
    Bn, K, W = x.shape
    M = mat.shape[0]
    return pl.pallas_call(
        functools.partial(_lmat_kernel, precision=precision),
        grid=(Bn, W // wc),
        in_specs=[pl.BlockSpec((M, K), lambda b, j: (0, 0)),
                  pl.BlockSpec((1, K, wc), lambda b, j: (b, 0, j))],
        out_specs=pl.BlockSpec((1, M, wc), lambda b, j: (b, 0, j)),
        out_shape=jax.ShapeDtypeStruct((Bn, M, W), out_dtype),
        compiler_params=_params("arbitrary", "arbitrary"),
        name=name,
    )(mat, x)


def _hy_s1_kernel(m_ref, x_ref, o_ref, *, precision):
    k, nb = x_ref.shape[1], x_ref.shape[2]
    res = _dot(m_ref[...], x_ref[0].reshape(k, nb * D), precision)
    o_ref[0] = res.reshape(2, res.shape[0] // 2, nb, D).astype(o_ref.dtype)


def _hy_s1(mat, x, nb, out_dtype=BF16, precision=None, name="hy_s1"):
    Bn, K, n2, _ = x.shape
    M = mat.shape[0]
    return pl.pallas_call(
        functools.partial(_hy_s1_kernel, precision=precision),
        grid=(Bn, n2 // nb),
        in_specs=[pl.BlockSpec((M, K), lambda b, j: (0, 0)),
                  pl.BlockSpec((1, K, nb, D), lambda b, j: (b, 0, j, 0))],
        out_specs=pl.BlockSpec((1, 2, M // 2, nb, D), lambda b, j: (b, 0, 0, j, 0)),
        out_shape=jax.ShapeDtypeStruct((Bn, 2, M // 2, n2, D), out_dtype),
        compiler_params=_params("arbitrary", "arbitrary"),
        name=name,
    )(mat, x)


def _hy_s3_kernel(g_ref, z_ref, o_ref):
    two, n1, nb, _ = z_ref.shape[1:]
    conv = _dot(g_ref[...], z_ref[0].reshape(two * n1, nb * D))
    o_ref[0] = conv.reshape(conv.shape[0], nb, D).astype(o_ref.dtype)


def _hy_s3(g_mat, z, nb):
    Bn, _, n1, n2, _ = z.shape
    R = g_mat.shape[0]
    return pl.pallas_call(
        _hy_s3_kernel,
        grid=(Bn, n2 // nb),
        in_specs=[pl.BlockSpec((R, 2 * n1), lambda b, j: (0, 0)),
                  pl.BlockSpec((1, 2, n1, nb, D), lambda b, j: (b, 0, 0, j, 0))],
        out_specs=pl.BlockSpec((1, R, nb, D), lambda b, j: (b, 0, j, 0)),
        out_shape=jax.ShapeDtypeStruct((Bn, R, n2, D), BF16),
        compiler_params=_params("arbitrary", "arbitrary"),
        name="hy_s3",
    )(g_mat, z)


def _cmul(xr, xi, kr, ki):
    return xr * kr - xi * ki, xr * ki + xi * kr


def _hy_spec_kernel(a_ref, m_ref, o_ref):
    a = jnp.concatenate([a_ref[0, 0], a_ref[1, 0]], axis=0)
    o_ref[0] = _dot(m_ref[0], a, HIGHEST)


def _hy_spectrum(a, m_tab):
    n1 = a.shape[1]
    n2 = DFT_N2
    return pl.pallas_call(
        _hy_spec_kernel,
        grid=(n1,),
        in_specs=[pl.BlockSpec((2, 1, n2, D), lambda k: (0, k, 0, 0)),
                  pl.BlockSpec((1, 2 * n2, 2 * n2), lambda k: (k, 0, 0))],
        out_specs=pl.BlockSpec((1, 2 * n2, D), lambda k: (k, 0, 0)),
        out_shape=jax.ShapeDtypeStruct((n1, 2 * n2, D), F32),
        compiler_params=_params("arbitrary"),
        name="hy_spectrum",
    )(a, m_tab)


def _hy_mid_kernel(a_ref, m_ref, mt_ref, kf_ref, o_ref):
    n2 = DFT_N2
    kr = kf_ref[0, :n2]
    ki = kf_ref[0, n2:]

    samples = range(a_ref.shape[0])
    a = jnp.concatenate([jnp.concatenate([a_ref[b, 0, 0], a_ref[b, 1, 0]], axis=0) for b in samples], axis=1)
    x = _dot(m_ref[0], a)
    ys = []
    for b in samples:
        yr, yi = _cmul(x[:n2, b * D:(b + 1) * D], x[n2:, b * D:(b + 1) * D], kr, ki)
        ys.append(jnp.concatenate([yr, yi], axis=0).astype(BF16))
    z = _dot(mt_ref[0], jnp.concatenate(ys, axis=1)).astype(o_ref.dtype)
    for b in samples:
        o_ref[b, 0, 0] = z[:n2, b * D:(b + 1) * D]
        o_ref[b, 1, 0] = z[n2:, b * D:(b + 1) * D]


def _hy_mid(a, m_bf, mt_bf, kf):
    B, _, n1, n2, _ = a.shape
    blk = pl.BlockSpec((B, 2, 1, n2, D), lambda k: (0, 0, k, 0, 0))
    tab = pl.BlockSpec((1, 2 * n2, 2 * n2), lambda k: (k, 0, 0))
    return pl.pallas_call(
        _hy_mid_kernel,
        grid=(n1,),
        in_specs=[blk, tab, tab, pl.BlockSpec((1, 2 * n2, D), lambda k: (k, 0, 0))],
        out_specs=blk,
        out_shape=jax.ShapeDtypeStruct(a.shape, BF16),
        compiler_params=_params("arbitrary"),
        name="hy_mid",
    )(a, m_bf, mt_bf, kf)


def _hy_output(x0c, conv, u, bias):
    return x0c * (conv + u * bias)


def _hy_ctx_kernel(x0_ref, x1_ref, v_ref, w0_ref, w1_ref, w2_ref, b0_ref, b1_ref, b2_ref,
                   f_ref, kf_ref, g_ref, bias_ref, o_ref):
    x0 = _short_conv(x0_ref[0], w0_ref[...], b0_ref[...])
    u = _short_conv(x1_ref[0], w1_ref[...], b1_ref[...]) * _short_conv(v_ref[0], w2_ref[...], b2_ref[...])
    x = _dot(f_ref[...], u.astype(BF16))
    nf = x.shape[0] // 2
    yr, yi = _cmul(x[:nf], x[nf:], kf_ref[:nf], kf_ref[nf:])
    conv = _dot(g_ref[...], jnp.concatenate([yr, yi], axis=0).astype(BF16))
    o_ref[0] = _hy_output(x0, conv, u, bias_ref[...]).astype(o_ref.dtype)


def _hy_ctx(p_hg, conv_w, conv_b, f_c, kf_c, g_c, bias, L, Lc):
    B = p_hg.shape[0]
    ct = 256
    nb = D // ct
    rb = L // Lc
    col = lambda k: pl.BlockSpec((1, Lc, ct), lambda b, j: (b, rb, k * nb + j))
    wsp = lambda k: pl.BlockSpec((3, ct), lambda b, j: (0, k * nb + j))
    bsp = lambda k: pl.BlockSpec((1, ct), lambda b, j: (0, k * nb + j))
    return pl.pallas_call(
        _hy_ctx_kernel,
        grid=(B, nb),
        in_specs=[col(0), col(1), col(2), wsp(0), wsp(1), wsp(2), bsp(0), bsp(1), bsp(2),
                  pl.BlockSpec(f_c.shape, lambda b, j: (0, 0)),
                  pl.BlockSpec((kf_c.shape[0], ct), lambda b, j: (0, j)),
                  pl.BlockSpec(g_c.shape, lambda b, j: (0, 0)),
                  pl.BlockSpec((1, ct), lambda b, j: (0, j))],
        out_specs=pl.BlockSpec((1, Lc, ct), lambda b, j: (b, 0, j)),
        out_shape=jax.ShapeDtypeStruct((B, Lc, D), BF16),
        compiler_params=_params("arbitrary", "arbitrary"),
        name="hy_ctx",
    )(p_hg, p_hg, p_hg, conv_w, conv_w, conv_w, conv_b, conv_b, conv_b, f_c, kf_c, g_c, bias)


def _dft_tables(L, Lc):
    n = 2 * L
    n2 = DFT_N2
    n1 = n // n2
    two_pi = 2.0 * math.pi

    def cs(idx, period):
        ang = (idx % period).astype(F32) * (two_pi / period)
        return jnp.cos(ang), jnp.sin(ang)

    k1 = jnp.arange(n1)[:, None]
    c, s = cs(k1 * jnp.arange(n1)[None, :], n1)
    f1_full = jnp.concatenate([c, -s], axis=0)
    f1_half = f1_full[:, :n1 // 2]
    kk1 = jnp.arange(n1)[:, None, None]
    kk2 = jnp.arange(n2)[None, :, None]
    nn2 = jnp.arange(n2)[None, None, :]
    c, s = cs(nn2 * kk1 + n1 * nn2 * kk2, n)
    ci = -s
    m_tab = jnp.concatenate([jnp.concatenate([c, -ci], axis=2), jnp.concatenate([ci, c], axis=2)], axis=1)
    c, s = cs(jnp.arange(n1 // 2)[:, None] * jnp.arange(n1)[None, :], n1)
    g_mat = jnp.concatenate([c, -s], axis=1) / n
    nc = 2 * Lc
    kc = jnp.arange(nc)[:, None]
    c, s = cs(kc * jnp.arange(nc)[None, :], nc)
    fc_full = jnp.concatenate([c, -s], axis=0)
    c, s = cs(jnp.arange(Lc)[:, None] * jnp.arange(nc)[None, :], nc)
    g_c = jnp.concatenate([c, -s], axis=1) / nc
    return dict(f1_full=f1_full, f1_half=f1_half.astype(BF16), m_tab=m_tab, m_bf=m_tab.astype(BF16),
                mt_bf=jnp.swapaxes(m_tab, 1, 2).astype(BF16), g_mat=g_mat.astype(BF16),
                fc_full=fc_full, fc_half=fc_full[:, :Lc].astype(BF16), g_c=g_c.astype(BF16))


def _hyena(p_hg, hp, tabs, L, Lc):
    B = p_hg.shape[0]
    n2 = DFT_N2
    n1 = 2 * L // n2
    wc = min(16 * D, n2 * D)
    deltas = jnp.abs(jnp.linspace(math.log(HY_TARGET) / HY_FAST_DECAY, math.log(HY_TARGET) / HY_SLOW_DECAY,
                                  D, dtype=F32))[None, :]
    filt = functools.partial(_hy_filter, w1p=hp['w1p'], b1=hp['b1'], w2=hp['w2'], b2=hp['b2'], fr=hp['fr'],
                             w3=hp['w3'], deltas=deltas)
    k_lat = filt(L)
    a_f = _hy_s1(tabs['f1_full'], k_lat.reshape(1, n1, n2, D), wc // D, F32, HIGHEST, "hy_filt_s1")
    kf = _hy_spectrum(a_f[0], tabs['m_tab'])
    k_ctx = filt(Lc)
    kf_c = _lmat(tabs['fc_full'], k_ctx[None], F32, HIGHEST, D, "hy_filt_ctx")[0]
    x0c, u = _hy_pre(p_hg, hp['conv_w'], hp['conv_b'], L)
    a = _hy_s1(tabs['f1_half'], u.reshape(B, n1 // 2, n2, D), wc // D)
    z = _hy_mid(a, tabs['m_bf'], tabs['mt_bf'], kf)
    conv = _hy_s3(tabs['g_mat'], z, wc // D).reshape(B, L, D)
    y_c = _hy_ctx(p_hg, hp['conv_w'], hp['conv_b'], tabs['fc_half'], kf_c, tabs['g_c'], hp['bias'], L, Lc)
    return conv, x0c, u, y_c


def _merge_kernel(x_ref, mod_ref, yd_ref, cv_ref, x0_ref, u_ref, yc_ref, hb_ref, ys_ref, gd_ref, gh_ref, gs_ref,
                  wb_ref, wo_ref, g2_ref, rw_ref, xo_ref, h2_ref, lg_ref, *, n_lat_tiles):
    latent = pl.program_id(1) < n_lat_tiles
    half = x_ref.shape[1] // 2
    for rows in (slice(0, half), slice(half, 2 * half)):
        y_hy = jnp.where(latent,
                         _hy_output(x0_ref[0, rows].astype(F32), cv_ref[0, rows].astype(F32),
                                    u_ref[0, rows].astype(F32), hb_ref[...]),
                         yc_ref[0, rows].astype(F32)).astype(BF16)
        merged = None
        for k, (y, gt_ref) in enumerate(((yd_ref[0, rows], gd_ref), (y_hy, gh_ref), (ys_ref[0, rows], gs_ref))):
            term = jax.nn.sigmoid(gt_ref[0, rows]) * _dot(y, wb_ref[k])
            merged = term if merged is None else merged + term
        y = _dot(merged.astype(BF16), wo_ref[...])
        x = x_ref[0, rows] + mod_ref[0, 2:3, :] * y
        xo_ref[0, rows] = x
        h2 = _rms(x) * g2_ref[...]
        h2 = h2 * (1.0 + mod_ref[0, 4:5, :]) + mod_ref[0, 3:4, :]
        h2_ref[0, rows] = h2.astype(h2_ref.dtype)
        lg_ref[0, :, rows] = _dot(h2, rw_ref[...], HIGHEST).T[:N_EXPERTS]


def _merge(xa, mod, y_diff, hy, hy_bias, y_swa, p_hg, w_branch, w_out, g2, rw_pad, n_lat_tiles):
    B, T, _ = xa.shape
    ctx_row = B
    conv, x0c, u, y_hy_c = hy
    tile = lambda c: pl.BlockSpec((1, TM, D), lambda b, i: (b, i, c))
    lat = pl.BlockSpec((1, TM, D), lambda b, i: (b, jnp.minimum(i, n_lat_tiles - 1), 0))
    return pl.pallas_call(
        functools.partial(_merge_kernel, n_lat_tiles=n_lat_tiles),
        grid=(B, T // TM),
        in_specs=[tile(0),
                  pl.BlockSpec((1, 6, D), lambda b, i: (jnp.where(i >= n_lat_tiles, ctx_row, b), 0, 0)),
                  tile(0), lat, lat, lat,
                  pl.BlockSpec((1, y_hy_c.shape[1], D), lambda b, i: (b, 0, 0)),
                  pl.BlockSpec((1, D), lambda b, i: (0, 0)),
                  tile(0), tile(3), tile(4), tile(5),
                  pl.BlockSpec((3, D, D), lambda b, i: (0, 0, 0)),
                  pl.BlockSpec((D, D), lambda b, i: (0, 0)),
                  pl.BlockSpec((1, D), lambda b, i: (0, 0)),
                  pl.BlockSpec((D, LANES), lambda b, i: (0, 0))],
        out_specs=[tile(0), tile(0), pl.BlockSpec((1, N_EXPERTS, TM), lambda b, i: (b, 0, i))],
        out_shape=[jax.ShapeDtypeStruct((B, T, D), F32), jax.ShapeDtypeStruct((B, T, D), BF16),
                   jax.ShapeDtypeStruct((B, N_EXPERTS, T), F32)],
        compiler_params=_params("arbitrary", "arbitrary"),
        name="merge",
    )(xa, mod, y_diff, conv, x0c, u, y_hy_c, hy_bias, y_swa, p_hg, p_hg, p_hg, w_branch, w_out, g2, rw_pad)


def _router_kernel(lg_ref, slot_ref, gate_ref, first_ref, *, segments):
    lg = lg_ref[0]
    e = jnp.exp(lg - jnp.max(lg, axis=0, keepdims=True))
    aff = e / jnp.sum(e, axis=0, keepdims=True)
    r = lax.broadcasted_iota(jnp.int32, (LANES, LANES), 0)
    c = lax.broadcasted_iota(jnp.int32, (LANES, LANES), 1)
    before = (r < c).astype(BF16)

    def prefix_count(mask, n):
        run = jnp.zeros((N_EXPERTS, 1), F32)
        outs = []
        for j in range(n // LANES):
            blk = mask[:, j * LANES:(j + 1) * LANES].astype(F32).astype(BF16)
            outs.append(_dot(blk, before) + run)
            run = run + jnp.sum(blk.astype(F32), axis=1, keepdims=True)
        return jnp.concatenate(outs, axis=1)

    for start, n, cap, slot_base in segments:
        a = aff[:, start:start + n]
        bits = pltpu.bitcast(a, jnp.int32)

        def search(_, bounds):
            lo, hi = bounds
            mid = lo + ((hi - lo + 1) >> 1)
            cnt = jnp.sum((bits >= mid).astype(F32), axis=1, keepdims=True)
            ok = cnt >= cap
            return jnp.where(ok, mid, lo), jnp.where(ok, hi, mid - 1)

        lo0 = jnp.zeros((N_EXPERTS, 1), jnp.int32)
        hi0 = jnp.full((N_EXPERTS, 1), 0x7F800000, jnp.int32)
        thr, _ = lax.fori_loop(0, 32, search, (lo0, hi0))
        above = bits > thr
        tie = bits == thr
        need = cap - jnp.sum(above.astype(F32), axis=1, keepdims=True)
        sel = above | (tie & (prefix_count(tie, n) < need))
        pos = prefix_count(sel, n)
        slot_ref[0, :, start:start + n] = jnp.where(sel, pos + slot_base, -1.0)
        gate_ref[0, :, start:start + n] = jnp.where(sel, a, 0.0)
        if start == 0:
            t_idx = lax.broadcasted_iota(jnp.int32, (n, LANES), 0)
            tile_start = lax.broadcasted_iota(jnp.int32, (n, LANES), 1) * TM
            before_tile = jnp.where(t_idx < tile_start, 1.0, 0.0).astype(BF16)
            first_ref[0] = _dot(jnp.where(sel, 1.0, 0.0).astype(BF16), before_tile)


def _router(logits_t, segments):
    B, E, T = logits_t.shape
    blk = pl.BlockSpec((1, E, T), lambda b: (b, 0, 0))
    return pl.pallas_call(
        functools.partial(_router_kernel, segments=segments),
        grid=(B,),
        in_specs=[blk],
        out_specs=[blk, blk, pl.BlockSpec((1, E, LANES), lambda b: (b, 0, 0))],
        out_shape=[jax.ShapeDtypeStruct((B, E, T), F32)] * 2 + [jax.ShapeDtypeStruct((B, E, LANES), F32)],
        compiler_params=_params("arbitrary"),
        name="router",
    )(logits_t)


def _slot_window(first, align, rows, n_slots):
    return pl.multiple_of(jnp.minimum(first // align * align, n_slots - rows), align)


def _moe_ffn_kernel(first_ref, busy_ref, slot_ref, gate_ref, h2_ref, w1_ref, w3_ref, w2_ref, o_ref, xg_scr, g_scr,
                    *, n_slots, n_tiles):
    pair = pl.program_id(0) * N_EXPERTS + pl.program_id(1)
    base = pair * n_tiles
    xg_scr[...] = jnp.zeros_like(xg_scr)
    g_scr[...] = jnp.zeros_like(g_scr)

    def gather(rows):
        row_id = lax.broadcasted_iota(jnp.int32, (rows, TM), 0)
        for c in range(n_tiles):
            tok = slice(c * TM, (c + 1) * TM)
            w0 = _slot_window(first_ref[base + c], 8, rows, n_slots)
            hit = slot_ref[0, 0, :, tok] == (row_id + w0).astype(F32)
            win = pl.ds(w0, rows)
            xg_scr[win, :] += _dot(jnp.where(hit, 1.0, 0.0).astype(BF16), h2_ref[0, tok, :])
            g_scr[win, :] += jnp.sum(jnp.where(hit, gate_ref[0, 0, :, tok], 0.0), axis=1, keepdims=True)

    sparse = busy_ref[pair] <= SPARSE_HITS

    @pl.when(sparse)
    def _():
        gather(min(SPARSE_HITS + 8, n_slots))

    @pl.when(jnp.logical_not(sparse))
    def _():
        gather(min(TM + 8, n_slots))

    xb = xg_scr[...].astype(BF16)
    a = _dot(xb, w1_ref[0, 0])
    b = _dot(xb, w3_ref[0, 0])
    hidden = (a * jax.nn.sigmoid(a) * b).astype(BF16)
    o_ref[0, 0] = (_dot(hidden, w2_ref[0, 0]) * g_scr[...]).astype(o_ref.dtype)


def _moe_ffn(first, busy, slot, gate, h2, w1, w3, w2, l, n_slots):
    B, E, T = slot.shape
    row = pl.BlockSpec((1, 1, 1, T), lambda b, e: (b, e, 0, 0))
    wsp = pl.BlockSpec((1, 1, D, D), lambda b, e: (l, e, 0, 0))
    smem = pl.BlockSpec(memory_space=pltpu.SMEM)
    return pl.pallas_call(
        functools.partial(_moe_ffn_kernel, n_slots=n_slots, n_tiles=T // TM),
        grid=(B, E),
        in_specs=[smem, smem, row, row, pl.BlockSpec((1, T, D), lambda b, e: (b, 0, 0)), wsp, wsp, wsp],
        out_specs=pl.BlockSpec((1, 1, n_slots, D), lambda b, e: (b, e, 0, 0)),
        out_shape=jax.ShapeDtypeStruct((B, E, n_slots, D), BF16),
        scratch_shapes=[pltpu.VMEM((n_slots, D), F32), pltpu.VMEM((n_slots, 1), F32)],
        compiler_params=_params("arbitrary", "arbitrary"),
        name="moe_ffn",
    )(first, busy, slot.reshape(B, E, 1, T), gate.reshape(B, E, 1, T), h2, w1, w3, w2)


def _moe_combine_kernel(first_ref, busy_ref, x_ref, mod_ref, slot_ref, y_ref, fg_ref, o_ref, acc_scr,
                        *, n_slots, n_tiles, n_lat_tiles, final):
    tm = x_ref.shape[1]

    def combine(rows):
        col_id = lax.broadcasted_iota(jnp.int32, (tm, rows), 1)
        acc = jnp.zeros((tm, D), F32)
        for e in range(N_EXPERTS):
            w0 = _slot_window(first_ref[(pl.program_id(0) * N_EXPERTS + e) * n_tiles + pl.program_id(1)], 16, rows,
                              n_slots)
            hit = slot_ref[0, :, e:e + 1] == (col_id + w0).astype(F32)
            acc = acc + _dot(jnp.where(hit, 1.0, 0.0).astype(BF16), y_ref[0, e, pl.ds(w0, rows), :])
        acc_scr[...] = acc

    sparse = busy_ref[pl.program_id(0) * n_tiles + pl.program_id(1)] <= SPARSE_HITS

    @pl.when(sparse)
    def _():
        combine(min(SPARSE_HITS + 16, n_slots))

    @pl.when(jnp.logical_not(sparse))
    def _():
        combine(min(tm + 16, n_slots))

    x = x_ref[0] + mod_ref[0, 5:6, :] * acc_scr[...]
    if final:
        @pl.when(pl.program_id(1) < n_lat_tiles)
        def _():
            o_ref[0] = _rms(x) * fg_ref[...]
    else:
        o_ref[0] = x


def _moe_combine(first, busy, xa, mod, slot_t, y, final_g, n_lat_tiles, n_slots, final):
    B, T, _ = xa.shape
    ctx_row = B
    tile = pl.BlockSpec((1, TM, D), lambda b, i: (b, i, 0))
    smem = pl.BlockSpec(memory_space=pltpu.SMEM)
    if final:
        out_spec = pl.BlockSpec((1, TM, D), lambda b, i: (b, jnp.minimum(i, n_lat_tiles - 1), 0))
        out_rows = n_lat_tiles * TM
    else:
        out_spec, out_rows = tile, T
    return pl.pallas_call(
        functools.partial(_moe_combine_kernel, n_slots=n_slots, n_tiles=T // TM, n_lat_tiles=n_lat_tiles,
                          final=final),
        grid=(B, T // TM),
        in_specs=[smem, smem, tile,
                  pl.BlockSpec((1, 6, D), lambda b, i: (jnp.where(i >= n_lat_tiles, ctx_row, b), 0, 0)),
                  pl.BlockSpec((1, TM, N_EXPERTS), lambda b, i: (b, i, 0)),
                  pl.BlockSpec((1, N_EXPERTS, n_slots, D), lambda b, i: (b, 0, 0, 0)),
                  pl.BlockSpec((1, D), lambda b, i: (0, 0))],
        out_specs=out_spec,
        out_shape=jax.ShapeDtypeStruct((B, out_rows, D), F32),
        scratch_shapes=[pltpu.VMEM((TM, D), F32)],
        compiler_params=_params("arbitrary", "arbitrary"),
        name="moe_combine",
    )(first, busy, xa, mod, slot_t, y, final_g)


def _swa_head_perm():
    return [kv * SWA_GROUP + g for g in range(SWA_GROUP) for kv in range(SWA_KV)]


def _split_w_in(w_in):
    depth = w_in.shape[0]
    o = [0, D, 2 * D, 3 * D, 6 * D, 7 * D, 7 * D + SWA_KV * HEAD, 7 * D + 2 * SWA_KV * HEAD, 10 * D + 2 * SWA_KV * HEAD]
    dq, dk, dv, hy, sq, sk, sv, gt = (w_in[:, :, o[k]:o[k + 1]] for k in range(8))
    scale = HEAD ** -0.5 * math.log2(math.e)
    sq = sq.reshape(depth, D, SWA_KV * SWA_GROUP, HEAD)[:, :, jnp.array(_swa_head_perm())].reshape(depth, D, D)
    w_attn = jnp.concatenate([dq * scale, dk, dv, sq * scale, sk, sv], axis=-1).astype(BF16)
    w_hg = jnp.concatenate([hy, gt], axis=-1).astype(BF16)
    return w_attn, w_hg


def kernel(x, c, ctx, c_ctx, w_mod, b_mod, norm1_g, norm2_g, w_in, diff_lambda, diff_subln_g, hy_conv_w,
           hy_conv_b, hy_ff_w1, hy_ff_b1, hy_ff_w2, hy_ff_b2, hy_ff_w3, hy_sin_freq, hy_bias, swa_sink,
           w_branch, w_out, router_w, moe_w1, moe_w3, moe_w2, final_g):
    B, L, _ = x.shape
    Lc = ctx.shape[1]
    depth = w_mod.shape[0]
    T = L + Lc
    assert Lc == TM and L % (4 * TM) == 0 and x.shape[2] == D
    n_lat_tiles = L // TM
    cap_l = CAPACITY_FACTOR * L // N_EXPERTS
    cap_c = CAPACITY_FACTOR * Lc // N_EXPERTS
    n_slots = cap_l + cap_c
    segments = ((0, L, cap_l, 0), (L, Lc, cap_c, cap_l))

    xa = jnp.concatenate([x, ctx], axis=1)
    mods = _mods(c, c_ctx, w_mod, b_mod)
    cos_t, sin_t = _rope_tables(L, Lc)
    tabs = _dft_tables(L, Lc)
    w_attn, w_hg = _split_w_in(w_in)
    rope_flags = tuple([True] * 16 + [False] * 8 + [True] * 8 + [True] * 2 + [False] * 2)
    perm = jnp.array(_swa_head_perm())
    wb = w_branch.at[:, 2].set(
        w_branch[:, 2].reshape(depth, SWA_KV * SWA_GROUP, HEAD, D)[:, perm].reshape(depth, D, D)).astype(BF16)
    wo = w_out.astype(BF16)
    w1, w3, w2 = moe_w1.astype(BF16), moe_w3.astype(BF16), moe_w2.astype(BF16)
    rw_pad = jnp.pad(router_w, ((0, 0), (0, 0), (0, LANES - N_EXPERTS)))
    w1p = jnp.pad(hy_ff_w1, ((0, 0), (0, LANES - HY_EMB), (0, 0)))

    for l in range(depth):
        lambda_init = 0.8 - 0.6 * math.exp(-0.3 * l)
        mod = mods[l]
        p_attn = _norm_proj(xa, mod, norm1_g[l][None], w_attn[l], cos_t, sin_t, rope_flags, BF16,
                            n_lat_tiles, "proj_attn")
        p_hg = _norm_proj(xa, mod, norm1_g[l][None], w_hg[l], cos_t, sin_t, (False,) * (N_HG // LANES), F32,
                          n_lat_tiles, "proj_hg")
        y_diff = _diff_attn(p_attn, diff_lambda, diff_subln_g, L, Lc, lambda_init, l)
        y_swa = _swa(p_attn, swa_sink, L, Lc, l)
        hp = dict(w1p=w1p[l], b1=hy_ff_b1[l][None], w2=hy_ff_w2[l], b2=hy_ff_b2[l][None],
                  fr=hy_sin_freq[l][None], w3=hy_ff_w3[l], conv_w=hy_conv_w[l], conv_b=hy_conv_b[l][None],
                  bias=hy_bias[l][None])
        hy = _hyena(p_hg, hp, tabs, L, Lc)
        xa, h2, logits_t = _merge(xa, mod, y_diff, hy, hp['bias'], y_swa, p_hg, wb[l], wo[l], norm2_g[l][None],
                                  rw_pad[l], n_lat_tiles)
        slot, gate, first = _router(logits_t, segments)
        first = jnp.concatenate([first[:, :, :n_lat_tiles], jnp.full((B, N_EXPERTS, 1), cap_l, F32)], axis=2)
        first = first.astype(jnp.int32)
        ends = jnp.concatenate([first[:, :, 1:n_lat_tiles], jnp.full((B, N_EXPERTS, 1), cap_l, jnp.int32),
                                jnp.full((B, N_EXPERTS, 1), n_slots, jnp.int32)], axis=2)
        hits = ends - first
        first = first.reshape(-1)
        y_e = _moe_ffn(first, jnp.max(hits, axis=2).reshape(-1), slot, gate, h2, w1, w3, w2, l, n_slots)
        xa = _moe_combine(first, jnp.max(hits, axis=1).reshape(-1), xa, mod, jnp.swapaxes(slot, 1, 2), y_e,
                          final_g[None], n_lat_tiles, n_slots, final=l == depth - 1)
    return xa
```

```python
import functools
import math

import jax
import jax.numpy as jnp
from jax import lax
from jax.experimental import pallas as pl
from jax.experimental.pallas import tpu as pltpu

F32 = jnp.float32
BF16 = jnp.bfloat16
HIGHEST = lax.Precision.HIGHEST

D = 1024
HEAD = 64
GRID_W = 64
ROPE_BASE = 10000.0
EPS = 1e-6
N_DIFF_HEADS = D // (2 * HEAD)
DIFF_V = 2 * HEAD
SWA_KV = 4
SWA_GROUP = 4
WINDOW = 128
N_EXPERTS = 16
CAPACITY_FACTOR = 2
HY_BANDS = 16
HY_EMB = 1 + 2 * HY_BANDS
HY_FF = 64
HY_FAST_DECAY = 0.3
HY_SLOW_DECAY = 1.5
HY_TARGET = 1e-2

LANES = 128
TM = 256
DFT_N2 = 128
VMEM_LIMIT = 56 * 1024 * 1024

SPARSE_HITS = 64

N_ATTN = 3 * D + D + 2 * SWA_KV * HEAD
N_HG = 3 * D + 3 * D


def _params(*sem):
    return pltpu.CompilerParams(dimension_semantics=sem, vmem_limit_bytes=VMEM_LIMIT)


def _dot(a, b, precision=None):
    return jnp.dot(a, b, preferred_element_type=F32, precision=precision)


def _dot_nt(a, b, precision=None):
    return lax.dot_general(a, b, (((1,), (1,)), ((), ())), preferred_element_type=F32, precision=precision)


def _rms(x):
    return x * lax.rsqrt(jnp.mean(x * x, axis=-1, keepdims=True) + EPS)


def _mod_kernel(c_ref, w_ref, b_ref, o_ref):
    c = c_ref[...]
    sc = c * jax.nn.sigmoid(c)
    o_ref[0] = _dot(sc, w_ref[0], HIGHEST) + b_ref[0]


def _mods(c, c_ctx, w_mod, b_mod):
    B = c.shape[0]
    depth = w_mod.shape[0]
    rows = -(-(B + 1) // 8) * 8
    cc = jnp.zeros((rows, D), F32).at[:B].set(c).at[B].set(c_ctx)
    out = pl.pallas_call(
        _mod_kernel,
        grid=(depth, 6),
        in_specs=[pl.BlockSpec((rows, D), lambda l, j: (0, 0)),
                  pl.BlockSpec((1, D, D), lambda l, j: (l, 0, j)),
                  pl.BlockSpec((1, 1, D), lambda l, j: (l, 0, j))],
        out_specs=pl.BlockSpec((1, rows, D), lambda l, j: (l, 0, j)),
        out_shape=jax.ShapeDtypeStruct((depth, rows, 6 * D), F32),
        compiler_params=_params("arbitrary", "arbitrary"),
        name="mods",
    )(cc, w_mod, b_mod.reshape(depth, 1, 6 * D))
    return out.reshape(depth, rows, 6, D)


def _norm_proj_kernel(x_ref, mod_ref, g_ref, w_ref, cos_ref, sin_ref, o_ref, *, rope_flags):
    x = x_ref[0]
    h = _rms(x) * g_ref[...]
    h = h * (1.0 + mod_ref[0, 1:2, :]) + mod_ref[0, 0:1, :]
    hb = h.astype(BF16)
    n_sub = len(rope_flags)
    sub_per_chunk = 4
    if any(rope_flags):
        cos = cos_ref[...]
        sin = sin_ref[...]
        lane = lax.broadcasted_iota(jnp.int32, cos.shape, 1)
        first_half = (lane % (HEAD // 2)) < (HEAD // 4)
    for c0 in range(0, n_sub, sub_per_chunk):
        c1 = min(c0 + sub_per_chunk, n_sub)
        acc = _dot(hb, w_ref[:, c0 * LANES:c1 * LANES])
        for s in range(c0, c1):
            a = acc[:, (s - c0) * LANES:(s - c0 + 1) * LANES]
            if rope_flags[s]:
                partner = jnp.where(first_half, pltpu.roll(a, LANES - HEAD // 4, 1), pltpu.roll(a, HEAD // 4, 1))
                a = a * cos + partner * sin
            o_ref[0, :, s * LANES:(s + 1) * LANES] = a.astype(o_ref.dtype)


def _norm_proj(xa, mod, g, w, cos_t, sin_t, rope_flags, out_dtype, n_lat_tiles, name):
    B, T, _ = xa.shape
    N = w.shape[1]
    ctx_row = B
    return pl.pallas_call(
        functools.partial(_norm_proj_kernel, rope_flags=rope_flags),
        grid=(B, T // TM),
        in_specs=[pl.BlockSpec((1, TM, D), lambda b, i: (b, i, 0)),
                  pl.BlockSpec((1, 6, D), lambda b, i: (jnp.where(i >= n_lat_tiles, ctx_row, b), 0, 0)),
                  pl.BlockSpec((1, D), lambda b, i: (0, 0)),
                  pl.BlockSpec((D, N), lambda b, i: (0, 0)),
                  pl.BlockSpec((TM, LANES), lambda b, i: (i, 0)),
                  pl.BlockSpec((TM, LANES), lambda b, i: (i, 0))],
        out_specs=pl.BlockSpec((1, TM, N), lambda b, i: (b, i, 0)),
        out_shape=jax.ShapeDtypeStruct((B, T, N), out_dtype),
        compiler_params=_params("arbitrary", "arbitrary"),
        name=name,
    )(xa, mod, g, w, cos_t, sin_t)


def _rope_tables(L, Lc):
    t = jnp.arange(L)
    pos = jnp.stack([(t // GRID_W).astype(F32), (t % GRID_W).astype(F32)], axis=1)
    n_freq = HEAD // 4
    inv = ROPE_BASE ** (-jnp.arange(n_freq, dtype=F32) / n_freq)
    ang = pos[:, :, None] * inv
    cos = jnp.cos(ang)
    sin = jnp.sin(ang)
    cos_h = jnp.concatenate([cos, cos], axis=-1).reshape(L, HEAD)
    sin_h = jnp.concatenate([-sin, sin], axis=-1).reshape(L, HEAD)
    cos_t = jnp.concatenate([jnp.tile(cos_h, (1, LANES // HEAD)), jnp.ones((Lc, LANES), F32)], axis=0)
    sin_t = jnp.concatenate([jnp.tile(sin_h, (1, LANES // HEAD)), jnp.zeros((Lc, LANES), F32)], axis=0)
    return cos_t, sin_t


def _lane_block_reduce(x, op):
    out = x[:, :LANES]
    for j in range(1, x.shape[1] // LANES):
        out = op(out, x[:, j * LANES:(j + 1) * LANES])
    return out


def _diff_attn_kernel(lam_ref, q_ref, k_ref, v_ref, g_ref, o_ref, s_lat, s_ctx, m_run, qq_scr, l_scr, acc_scr,
                      *, L, Lc, tg, lambda_init):
    tq = TM
    n_groups = (L + Lc) // tg
    n_lat = L // tq
    n_tiles = (L + Lc) // tq
    rows = (2 * tq, LANES)
    lowest = jnp.full(rows, -jnp.inf, F32)
    nothing = jnp.zeros(rows, F32)
    lp = lam_ref[0]
    lam = (jnp.exp(jnp.sum(lp[0:1] * lp[1:2], axis=-1, keepdims=True))
           - jnp.exp(jnp.sum(lp[2:3] * lp[3:4], axis=-1, keepdims=True)) + lambda_init)

    def stage(t, _):
        cur = t % 2
        prv = 1 - cur
        score_any = t < n_tiles
        score_lat = t < n_lat
        weigh_any = t >= 1
        weigh_lat = jnp.logical_and(weigh_any, t <= n_lat)

        @pl.when(score_any)
        def _():
            q = q_ref[0, pl.ds(pl.multiple_of(t * tq, tq), tq), :]
            lane = lax.broadcasted_iota(jnp.int32, q.shape, 1)
            zero = jnp.zeros_like(q)
            qq_scr[...] = jnp.concatenate([jnp.where(lane < HEAD, q, zero), jnp.where(lane >= HEAD, q, zero)],
                                          axis=0)

        @pl.when(jnp.logical_and(score_any, jnp.logical_not(score_lat)))
        def _():
            s_c = _dot_nt(qq_scr[...], k_ref[0, L:L + Lc, :])
            s_ctx[...] = s_c
            m_run[cur] = _lane_block_reduce(s_c, jnp.maximum)

        def row_max():
            return jnp.max(m_run[prv], axis=-1, keepdims=True)

        @pl.when(jnp.logical_and(weigh_any, jnp.logical_not(weigh_lat)))
        def _():
            p = jnp.exp2(s_ctx[...] - row_max())
            l_scr[...] = _lane_block_reduce(p, jnp.add)
            acc_scr[...] = _dot(p.astype(BF16), v_ref[0, L:L + Lc, :])

        def scores(gi, mrun):
            start = pl.multiple_of(gi * tg, LANES)
            s = _dot_nt(qq_scr[...], k_ref[0, pl.ds(start, tg), :])
            s_lat[cur, gi] = s
            return jnp.maximum(mrun, _lane_block_reduce(s, jnp.maximum))

        def weighted(gi, lrun, acc, m):
            start = pl.multiple_of(gi * tg, LANES)
            p = jnp.exp2(s_lat[prv, gi] - m)
            return (lrun + _lane_block_reduce(p, jnp.add),
                    acc + _dot(p.astype(BF16), v_ref[0, pl.ds(start, tg), :]))

        @pl.when(jnp.logical_and(score_lat, weigh_lat))
        def _():
            m = row_max()

            def both(gi, carry):
                mrun, lrun, acc = carry
                lrun, acc = weighted(gi, lrun, acc, m)
                return scores(gi, mrun), lrun, acc

            m_run[cur], l_scr[...], acc_scr[...] = lax.fori_loop(0, n_groups, both, (lowest, nothing, nothing))

        @pl.when(jnp.logical_and(score_lat, jnp.logical_not(weigh_lat)))
        def _():
            m_run[cur] = lax.fori_loop(0, n_groups, scores, lowest)

        @pl.when(jnp.logical_and(jnp.logical_not(score_lat), weigh_lat))
        def _():
            m = row_max()
            l_scr[...], acc_scr[...] = lax.fori_loop(0, n_groups, lambda gi, c: weighted(gi, c[0], c[1], m),
                                                      (nothing, nothing))

        @pl.when(weigh_any)
        def _():
            o = acc_scr[...] / jnp.sum(l_scr[...], axis=-1, keepdims=True)
            o = _rms(o[:tq] - lam * o[tq:])
            out_rows = pl.ds(pl.multiple_of((t - 1) * tq, tq), tq)
            o_ref[0, out_rows, :] = (o * g_ref[0] * (1.0 - lambda_init)).astype(o_ref.dtype)

        return 0

    lax.fori_loop(0, n_tiles + 1, stage, 0)


def _diff_attn(p_attn, diff_lambda, subln_g, L, Lc, lambda_init, l):
    B, T, _ = p_attn.shape
    tg = T // 2
    nh = N_DIFF_HEADS
    head = lambda c0: pl.BlockSpec((1, T, DIFF_V), lambda b, h: (b, 0, c0 + h))
    return pl.pallas_call(
        functools.partial(_diff_attn_kernel, L=L, Lc=Lc, tg=tg, lambda_init=lambda_init),
        grid=(B, nh),
        in_specs=[pl.BlockSpec((1, 4, HEAD), lambda b, h: (l, 0, 0)),
                  head(0), head(nh), head(2 * nh),
                  pl.BlockSpec((1, 1, DIFF_V), lambda b, h: (l, 0, 0))],
        out_specs=head(0),
        out_shape=jax.ShapeDtypeStruct((B, T, D), BF16),
        scratch_shapes=[pltpu.VMEM((2, T // tg, 2 * TM, tg), F32), pltpu.VMEM((2 * TM, Lc), F32),
                        pltpu.VMEM((2, 2 * TM, LANES), F32), pltpu.VMEM((2 * TM, DIFF_V), BF16),
                        pltpu.VMEM((2 * TM, LANES), F32), pltpu.VMEM((2 * TM, DIFF_V), F32)],
        compiler_params=_params("arbitrary", "arbitrary"),
        name="diff_attn",
    )(diff_lambda, p_attn, p_attn, p_attn, subln_g.reshape(-1, 1, DIFF_V))


def _swa_kernel(sink_ref, q_ref, k_ref, v_ref, o_ref, *, L, Lc, l):
    i = pl.program_id(1)
    tq = q_ref.shape[1]
    kvw = SWA_KV * HEAD
    win = tq + 2 * WINDOW
    lane_head = lax.broadcasted_iota(jnp.int32, (tq, kvw), 1) // HEAD
    k_ctx = k_ref[0, L:L + Lc, :]
    v_ctx = v_ref[0, L:L + Lc, :]

    def run(latent):
        if latent:
            start = pl.multiple_of(jnp.clip(i * tq - WINDOW, 0, L - win), WINDOW)
            k_win = k_ref[0, pl.ds(start, win), :]
            v_win = v_ref[0, pl.ds(start, win), :]
            q_pos = i * tq + lax.broadcasted_iota(jnp.int32, (tq, win), 0)
            k_pos = start + lax.broadcasted_iota(jnp.int32, (tq, win), 1)
            band_bias = jnp.where(jnp.abs(k_pos - q_pos) <= WINDOW, 0.0, -jnp.inf)
        for g in range(SWA_GROUP):
            qg = q_ref[0, :, g * kvw:(g + 1) * kvw]
            acc = jnp.zeros((tq, kvw), F32)
            for kv in range(SWA_KV):
                qe = jnp.where(lane_head == kv, qg, jnp.zeros_like(qg))
                snk = sink_ref[l, kv * SWA_GROUP + g] * math.log2(math.e)
                s_c = _dot_nt(qe, k_ctx)
                m = jnp.maximum(jnp.max(s_c, axis=-1, keepdims=True), snk)
                if latent:
                    s_w = _dot_nt(qe, k_win) + band_bias
                    m = jnp.maximum(m, jnp.max(s_w, axis=-1, keepdims=True))
                p_c = jnp.exp2(s_c - m)
                den = jnp.sum(p_c, axis=-1, keepdims=True) + jnp.exp2(snk - m)
                o = _dot(p_c.astype(BF16), v_ctx)
                if latent:
                    p_w = jnp.exp2(s_w - m)
                    den = den + jnp.sum(p_w, axis=-1, keepdims=True)
                    o = o + _dot(p_w.astype(BF16), v_win)
                acc = acc + jnp.where(lane_head == kv, o / den, 0.0)
            o_ref[0, :, g * kvw:(g + 1) * kvw] = acc.astype(o_ref.dtype)

    @pl.when(i < L // tq)
    def _():
        run(True)

    @pl.when(i >= L // tq)
    def _():
        run(False)


def _swa(p_attn, sink, L, Lc, l):
    B, T, _ = p_attn.shape
    kvw = SWA_KV * HEAD
    return pl.pallas_call(
        functools.partial(_swa_kernel, L=L, Lc=Lc, l=l),
        grid=(B, T // TM),
        in_specs=[pl.BlockSpec(memory_space=pltpu.SMEM),
                  pl.BlockSpec((1, TM, D), lambda b, i: (b, i, 3)),
                  pl.BlockSpec((1, T, kvw), lambda b, i: (b, 0, 4 * D // kvw)),
                  pl.BlockSpec((1, T, kvw), lambda b, i: (b, 0, 4 * D // kvw + 1))],
        out_specs=pl.BlockSpec((1, TM, D), lambda b, i: (b, i, 0)),
        out_shape=jax.ShapeDtypeStruct((B, T, D), BF16),
        compiler_params=_params("arbitrary", "arbitrary"),
        name="swa",
    )(sink, p_attn, p_attn, p_attn)


def _hy_filter_kernel(emb_ref, w1_ref, b1_ref, w2_ref, b2_ref, fr_ref, w3f_ref, w3b_ref, dl_ref, o_ref, *, L):
    rows = TM
    fr = fr_ref[...]

    def raw(i, total):
        r0 = pl.multiple_of(i * rows, rows)
        e = emb_ref[pl.ds(r0, rows), :]
        z = jnp.sin(fr * (_dot(e, w1_ref[...], HIGHEST) + b1_ref[...]))
        z = jnp.sin(fr * (_dot(z, w2_ref[...], HIGHEST) + b2_ref[...]))
        row = r0 + lax.broadcasted_iota(jnp.int32, (rows, 1), 0)
        h = jnp.where(row < L, _dot(z, w3f_ref[...], HIGHEST), _dot(z, w3b_ref[...], HIGHEST))
        h = h * jnp.exp(-e[:, 0:1] * dl_ref[...])
        o_ref[pl.ds(r0, rows), :] = h
        return total + jnp.sum(jnp.abs(h), axis=0, keepdims=True)

    n_chunks = emb_ref.shape[0] // rows
    norm = lax.fori_loop(0, n_chunks, raw, jnp.zeros((1, o_ref.shape[1]), F32)) + EPS

    def scale(i, _):
        r0 = pl.multiple_of(i * rows, rows)
        row = r0 + lax.broadcasted_iota(jnp.int32, (rows, 1), 0)
        o_ref[pl.ds(r0, rows), :] = jnp.where(row == L, 0.0, o_ref[pl.ds(r0, rows), :] / norm)
        return 0

    lax.fori_loop(0, n_chunks, scale, 0)


def _hy_embedding(L):
    n = jnp.arange(2 * L)
    lag = jnp.where(n < L, n, jnp.where(n == L, 0, 2 * L - n))
    t = jnp.linspace(0.0, 1.0, L, dtype=F32)[lag][:, None]
    w = (2.0 * math.pi * jnp.arange(L, dtype=F32) / L)[lag][:, None]
    f = jnp.linspace(1e-4, HY_BANDS - 1, HY_BANDS, dtype=F32)[None, :]
    emb = jnp.concatenate([t, jnp.cos(f * w), -jnp.sin(f * w)], axis=-1)
    return jnp.pad(emb, ((0, 0), (0, LANES - HY_EMB)))


def _hy_filter(L, w1p, b1, w2, b2, fr, w3, deltas):
    ct = 512
    emb = _hy_embedding(L)
    full = lambda shape: pl.BlockSpec(shape, lambda j: (0, 0))
    return pl.pallas_call(
        functools.partial(_hy_filter_kernel, L=L),
        grid=(D // ct,),
        in_specs=[full((2 * L, LANES)), full((LANES, HY_FF)), full((1, HY_FF)), full((HY_FF, HY_FF)),
                  full((1, HY_FF)), full((1, HY_FF)),
                  pl.BlockSpec((HY_FF, ct), lambda j: (0, j)),
                  pl.BlockSpec((HY_FF, ct), lambda j: (0, D // ct + j)),
                  pl.BlockSpec((1, ct), lambda j: (0, j))],
        out_specs=pl.BlockSpec((2 * L, ct), lambda j: (0, j)),
        out_shape=jax.ShapeDtypeStruct((2 * L, D), F32),
        compiler_params=_params("arbitrary"),
        name="hy_filter",
    )(emb, w1p, b1, w2, b2, fr, w3, w3, deltas)


def _short_conv(x, w, b):
    n = x.shape[0]
    row = lax.broadcasted_iota(jnp.int32, x.shape, 0)
    prev = jnp.where(row == 0, 0.0, pltpu.roll(x, 1, 0))
    nxt = jnp.where(row == n - 1, 0.0, pltpu.roll(x, n - 1, 0))
    return prev * w[0:1] + x * w[1:2] + nxt * w[2:3] + b


def _hy_pre_kernel(x0_ref, x1_ref, v_ref, w0_ref, w1_ref, w2_ref, b0_ref, b1_ref, b2_ref, x0c_ref, u_ref):
    x0c_ref[0] = _short_conv(x0_ref[0], w0_ref[...], b0_ref[...]).astype(x0c_ref.dtype)
    u = _short_conv(x1_ref[0], w1_ref[...], b1_ref[...]) * _short_conv(v_ref[0], w2_ref[...], b2_ref[...])
    u_ref[0] = u.astype(u_ref.dtype)


def _hy_pre(p_hg, conv_w, conv_b, L):
    B = p_hg.shape[0]
    ct = LANES
    nb = D // ct
    col = lambda k: pl.BlockSpec((1, L, ct), lambda b, j: (b, 0, k * nb + j))
    wsp = lambda k: pl.BlockSpec((3, ct), lambda b, j: (0, k * nb + j))
    bsp = lambda k: pl.BlockSpec((1, ct), lambda b, j: (0, k * nb + j))
    out = pl.BlockSpec((1, L, ct), lambda b, j: (b, 0, j))
    return pl.pallas_call(
        _hy_pre_kernel,
        grid=(B, nb),
        in_specs=[col(0), col(1), col(2), wsp(0), wsp(1), wsp(2), bsp(0), bsp(1), bsp(2)],
        out_specs=[out, out],
        out_shape=[jax.ShapeDtypeStruct((B, L, D), BF16)] * 2,
        compiler_params=_params("arbitrary", "arbitrary"),
        name="hy_pre",
    )(p_hg, p_hg, p_hg, conv_w, conv_w, conv_w, conv_b, conv_b, conv_b)


def _lmat_kernel(m_ref, x_ref, o_ref, *, precision):
    x = x_ref[0]
    if precision is None:
        x = x.astype(BF16)
    o_ref[0] = _dot(m_ref[...], x, precision).astype(o_ref.dtype)


def _lmat(mat, x, out_dtype, precision, wc, name):
    Bn, K, W = x.shape
    M = mat.shape[0]
    return pl.pallas_call(
        functools.partial(_lmat_kernel, precision=precision),
        grid=(Bn, W // wc),
        in_specs=[pl.BlockSpec((M, K), lambda b, j: (0, 0)),
                  pl.BlockSpec((1, K, wc), lambda b, j: (b, 0, j))],
        out_specs=pl.BlockSpec((1, M, wc), lambda b, j: (b, 0, j)),
        out_shape=jax.ShapeDtypeStruct((Bn, M, W), out_dtype),
        compiler_params=_params("arbitrary", "arbitrary"),
        name=name,
    )(mat, x)


def _hy_s1_kernel(m_ref, x_ref, o_ref, *, precision):
    k, nb = x_ref.shape[1], x_ref.shape[2]
    res = _dot(m_ref[...], x_ref[0].reshape(k, nb * D), precision)
    o_ref[0] = res.reshape(2, res.shape[0] // 2, nb, D).astype(o_ref.dtype)


def _hy_s1(mat, x, nb, out_dtype=BF16, precision=None, name="hy_s1"):
    Bn, K, n2, _ = x.shape
    M = mat.shape[0]
    return pl.pallas_call(
        functools.partial(_hy_s1_kernel, precision=precision),
        grid=(Bn, n2 // nb),
        in_specs=[pl.BlockSpec((M, K), lambda b, j: (0, 0)),
                  pl.BlockSpec((1, K, nb, D), lambda b, j: (b, 0, j, 0))],
        out_specs=pl.BlockSpec((1, 2, M // 2, nb, D), lambda b, j: (b, 0, 0, j, 0)),
        out_shape=jax.ShapeDtypeStruct((Bn, 2, M // 2, n2, D), out_dtype),
        compiler_params=_params("arbitrary", "arbitrary"),
        name=name,
    )(mat, x)


def _hy_s3_kernel(g_ref, z_ref, o_ref):
    two, n1, nb, _ = z_ref.shape[1:]
    conv = _dot(g_ref[...], z_ref[0].reshape(two * n1, nb * D))
    o_ref[0] = conv.reshape(conv.shape[0], nb, D).astype(o_ref.dtype)


def _hy_s3(g_mat, z, nb):
    Bn, _, n1, n2, _ = z.shape
    R = g_mat.shape[0]
    return pl.pallas_call(
        _hy_s3_kernel,
        grid=(Bn, n2 // nb),
        in_specs=[pl.BlockSpec((R, 2 * n1), lambda b, j: (0, 0)),
                  pl.BlockSpec((1, 2, n1, nb, D), lambda b, j: (b, 0, 0, j, 0))],
        out_specs=pl.BlockSpec((1, R, nb, D), lambda b, j: (b, 0, j, 0)),
        out_shape=jax.ShapeDtypeStruct((Bn, R, n2, D), BF16),
        compiler_params=_params("arbitrary", "arbitrary"),
        name="hy_s3",
    )(g_mat, z)


def _cmul(xr, xi, kr, ki):
    return xr * kr - xi * ki, xr * ki + xi * kr


def _hy_spec_kernel(a_ref, m_ref, o_ref):
    a = jnp.concatenate([a_ref[0, 0], a_ref[1, 0]], axis=0)
    o_ref[0] = _dot(m_ref[0], a, HIGHEST)


def _hy_spectrum(a, m_tab):
    n1 = a.shape[1]
    n2 = DFT_N2
    return pl.pallas_call(
        _hy_spec_kernel,
        grid=(n1,),
        in_specs=[pl.BlockSpec((2, 1, n2, D), lambda k: (0, k, 0, 0)),
                  pl.BlockSpec((1, 2 * n2, 2 * n2), lambda k: (k, 0, 0))],
        out_specs=pl.BlockSpec((1, 2 * n2, D), lambda k: (k, 0, 0)),
        out_shape=jax.ShapeDtypeStruct((n1, 2 * n2, D), F32),
        compiler_params=_params("arbitrary"),
        name="hy_spectrum",
    )(a, m_tab)


def _hy_mid_kernel(a_ref, m_ref, mt_ref, kf_ref, o_ref):
    n2 = DFT_N2
    kr = kf_ref[0, :n2]
    ki = kf_ref[0, n2:]

    samples = range(a_ref.shape[0])
    a = jnp.concatenate([jnp.concatenate([a_ref[b, 0, 0], a_ref[b, 1, 0]], axis=0) for b in samples], axis=1)
    x = _dot(m_ref[0], a)
    ys = []
    for b in samples:
        yr, yi = _cmul(x[:n2, b * D:(b + 1) * D], x[n2:, b * D:(b + 1) * D], kr, ki)
        ys.append(jnp.concatenate([yr, yi], axis=0).astype(BF16))
    z = _dot(mt_ref[0], jnp.concatenate(ys, axis=1)).astype(o_ref.dtype)
    for b in samples:
        o_ref[b, 0, 0] = z[:n2, b * D:(b + 1) * D]
        o_ref[b, 1, 0] = z[n2:, b * D:(b + 1) * D]


def _hy_mid(a, m_bf, mt_bf, kf):
    B, _, n1, n2, _ = a.shape
    blk = pl.BlockSpec((B, 2, 1, n2, D), lambda k: (0, 0, k, 0, 0))
    tab = pl.BlockSpec((1, 2 * n2, 2 * n2), lambda k: (k, 0, 0))
    return pl.pallas_call(
        _hy_mid_kernel,
        grid=(n1,),
        in_specs=[blk, tab, tab, pl.BlockSpec((1, 2 * n2, D), lambda k: (k, 0, 0))],
        out_specs=blk,
        out_shape=jax.ShapeDtypeStruct(a.shape, BF16),
        compiler_params=_params("arbitrary"),
        name="hy_mid",
    )(a, m_bf, mt_bf, kf)


def _hy_output(x0c, conv, u, bias):
    return x0c * (conv + u * bias)


def _hy_ctx_kernel(x0_ref, x1_ref, v_ref, w0_ref, w1_ref, w2_ref, b0_ref, b1_ref, b2_ref,
                   f_ref, kf_ref, g_ref, bias_ref, o_ref):
    x0 = _short_conv(x0_ref[0], w0_ref[...], b0_ref[...])
    u = _short_conv(x1_ref[0], w1_ref[...], b1_ref[...]) * _short_conv(v_ref[0], w2_ref[...], b2_ref[...])
    x = _dot(f_ref[...], u.astype(BF16))
    nf = x.shape[0] // 2
    yr, yi = _cmul(x[:nf], x[nf:], kf_ref[:nf], kf_ref[nf:])
    conv = _dot(g_ref[...], jnp.concatenate([yr, yi], axis=0).astype(BF16))
    o_ref[0] = _hy_output(x0, conv, u, bias_ref[...]).astype(o_ref.dtype)


def _hy_ctx(p_hg, conv_w, conv_b, f_c, kf_c, g_c, bias, L, Lc):
    B = p_hg.shape[0]
    ct = 256
    nb = D // ct
    rb = L // Lc
    col = lambda k: pl.BlockSpec((1, Lc, ct), lambda b, j: (b, rb, k * nb + j))
    wsp = lambda k: pl.BlockSpec((3, ct), lambda b, j: (0, k * nb + j))
    bsp = lambda k: pl.BlockSpec((1, ct), lambda b, j: (0, k * nb + j))
    return pl.pallas_call(
        _hy_ctx_kernel,
        grid=(B, nb),
        in_specs=[col(0), col(1), col(2), wsp(0), wsp(1), wsp(2), bsp(0), bsp(1), bsp(2),
                  pl.BlockSpec(f_c.shape, lambda b, j: (0, 0)),
                  pl.BlockSpec((kf_c.shape[0], ct), lambda b, j: (0, j)),
                  pl.BlockSpec(g_c.shape, lambda b, j: (0, 0)),
                  pl.BlockSpec((1, ct), lambda b, j: (0, j))],
        out_specs=pl.BlockSpec((1, Lc, ct), lambda b, j: (b, 0, j)),
        out_shape=jax.ShapeDtypeStruct((B, Lc, D), BF16),
        compiler_params=_params("arbitrary", "arbitrary"),
        name="hy_ctx",
    )(p_hg, p_hg, p_hg, conv_w, conv_w, conv_w, conv_b, conv_b, conv_b, f_c, kf_c, g_c, bias)


def _dft_tables(L, Lc):
    n = 2 * L
    n2 = DFT_N2
    n1 = n // n2
    two_pi = 2.0 * math.pi

    def cs(idx, period):
        ang = (idx % period).astype(F32) * (two_pi / period)
        return jnp.cos(ang), jnp.sin(ang)

    k1 = jnp.arange(n1)[:, None]
    c, s = cs(k1 * jnp.arange(n1)[None, :], n1)
    f1_full = jnp.concatenate([c, -s], axis=0)
    f1_half = f1_full[:, :n1 // 2]
    kk1 = jnp.arange(n1)[:, None, None]
    kk2 = jnp.arange(n2)[None, :, None]
    nn2 = jnp.arange(n2)[None, None, :]
    c, s = cs(nn2 * kk1 + n1 * nn2 * kk2, n)
    ci = -s
    m_tab = jnp.concatenate([jnp.concatenate([c, -ci], axis=2), jnp.concatenate([ci, c], axis=2)], axis=1)
    c, s = cs(jnp.arange(n1 // 2)[:, None] * jnp.arange(n1)[None, :], n1)
    g_mat = jnp.concatenate([c, -s], axis=1) / n
    nc = 2 * Lc
    kc = jnp.arange(nc)[:, None]
    c, s = cs(kc * jnp.arange(nc)[None, :], nc)
    fc_full = jnp.concatenate([c, -s], axis=0)
    c, s = cs(jnp.arange(Lc)[:, None] * jnp.arange(nc)[None, :], nc)
    g_c = jnp.concatenate([c, -s], axis=1) / nc
    return dict(f1_full=f1_full, f1_half=f1_half.astype(BF16), m_tab=m_tab, m_bf=m_tab.astype(BF16),
                mt_bf=jnp.swapaxes(m_tab, 1, 2).astype(BF16), g_mat=g_mat.astype(BF16),
                fc_full=fc_full, fc_half=fc_full[:, :Lc].astype(BF16), g_c=g_c.astype(BF16))


def _hyena(p_hg, hp, tabs, L, Lc):
    B = p_hg.shape[0]
    n2 = DFT_N2
    n1 = 2 * L // n2
    wc = min(16 * D, n2 * D)
    deltas = jnp.abs(jnp.linspace(math.log(HY_TARGET) / HY_FAST_DECAY, math.log(HY_TARGET) / HY_SLOW_DECAY,
                                  D, dtype=F32))[None, :]
    filt = functools.partial(_hy_filter, w1p=hp['w1p'], b1=hp['b1'], w2=hp['w2'], b2=hp['b2'], fr=hp['fr'],
                             w3=hp['w3'], deltas=deltas)
    k_lat = filt(L)
    a_f = _hy_s1(tabs['f1_full'], k_lat.reshape(1, n1, n2, D), wc // D, F32, HIGHEST, "hy_filt_s1")
    kf = _hy_spectrum(a_f[0], tabs['m_tab'])
    k_ctx = filt(Lc)
    kf_c = _lmat(tabs['fc_full'], k_ctx[None], F32, HIGHEST, D, "hy_filt_ctx")[0]
    x0c, u = _hy_pre(p_hg, hp['conv_w'], hp['conv_b'], L)
    a = _hy_s1(tabs['f1_half'], u.reshape(B, n1 // 2, n2, D), wc // D)
    z = _hy_mid(a, tabs['m_bf'], tabs['mt_bf'], kf)
    conv = _hy_s3(tabs['g_mat'], z, wc // D).reshape(B, L, D)
    y_c = _hy_ctx(p_hg, hp['conv_w'], hp['conv_b'], tabs['fc_half'], kf_c, tabs['g_c'], hp['bias'], L, Lc)
    return conv, x0c, u, y_c


def _merge_kernel(x_ref, mod_ref, yd_ref, cv_ref, x0_ref, u_ref, yc_ref, hb_ref, ys_ref, gd_ref, gh_ref, gs_ref,
                  wb_ref, wo_ref, g2_ref, rw_ref, xo_ref, h2_ref, lg_ref, *, n_lat_tiles):
    latent = pl.program_id(1) < n_lat_tiles
    half = x_ref.shape[1] // 2
    for rows in (slice(0, half), slice(half, 2 * half)):
        y_hy = jnp.where(latent,
                         _hy_output(x0_ref[0, rows].astype(F32), cv_ref[0, rows].astype(F32),
                                    u_ref[0, rows].astype(F32), hb_ref[...]),
                         yc_ref[0, rows].astype(F32)).astype(BF16)
        merged = None
        for k, (y, gt_ref) in enumerate(((yd_ref[0, rows], gd_ref), (y_hy, gh_ref), (ys_ref[0, rows], gs_ref))):
            term = jax.nn.sigmoid(gt_ref[0, rows]) * _dot(y, wb_ref[k])
            merged = term if merged is None else merged + term
        y = _dot(merged.astype(BF16), wo_ref[...])
        x = x_ref[0, rows] + mod_ref[0, 2:3, :] * y
        xo_ref[0, rows] = x
        h2 = _rms(x) * g2_ref[...]
        h2 = h2 * (1.0 + mod_ref[0, 4:5, :]) + mod_ref[0, 3:4, :]
        h2_ref[0, rows] = h2.astype(h2_ref.dtype)
        lg_ref[0, :, rows] = _dot(h2, rw_ref[...], HIGHEST).T[:N_EXPERTS]


def _merge(xa, mod, y_diff, hy, hy_bias, y_swa, p_hg, w_branch, w_out, g2, rw_pad, n_lat_tiles):
    B, T, _ = xa.shape
    ctx_row = B
    conv, x0c, u, y_hy_c = hy
    tile = lambda c: pl.BlockSpec((1, TM, D), lambda b, i: (b, i, c))
    lat = pl.BlockSpec((1, TM, D), lambda b, i: (b, jnp.minimum(i, n_lat_tiles - 1), 0))
    return pl.pallas_call(
        functools.partial(_merge_kernel, n_lat_tiles=n_lat_tiles),
        grid=(B, T // TM),
        in_specs=[tile(0),
                  pl.BlockSpec((1, 6, D), lambda b, i: (jnp.where(i >= n_lat_tiles, ctx_row, b), 0, 0)),
                  tile(0), lat, lat, lat,
                  pl.BlockSpec((1, y_hy_c.shape[1], D), lambda b, i: (b, 0, 0)),
                  pl.BlockSpec((1, D), lambda b, i: (0, 0)),
                  tile(0), tile(3), tile(4), tile(5),
                  pl.BlockSpec((3, D, D), lambda b, i: (0, 0, 0)),
                  pl.BlockSpec((D, D), lambda b, i: (0, 0)),
                  pl.BlockSpec((1, D), lambda b, i: (0, 0)),
                  pl.BlockSpec((D, LANES), lambda b, i: (0, 0))],
        out_specs=[tile(0), tile(0), pl.BlockSpec((1, N_EXPERTS, TM), lambda b, i: (b, 0, i))],
        out_shape=[jax.ShapeDtypeStruct((B, T, D), F32), jax.ShapeDtypeStruct((B, T, D), BF16),
                   jax.ShapeDtypeStruct((B, N_EXPERTS, T), F32)],
        compiler_params=_params("arbitrary", "arbitrary"),
        name="merge",
    )(xa, mod, y_diff, conv, x0c, u, y_hy_c, hy_bias, y_swa, p_hg, p_hg, p_hg, w_branch, w_out, g2, rw_pad)


def _router_kernel(lg_ref, slot_ref, gate_ref, first_ref, *, segments):
    lg = lg_ref[0]
    e = jnp.exp(lg - jnp.max(lg, axis=0, keepdims=True))
    aff = e / jnp.sum(e, axis=0, keepdims=True)
    r = lax.broadcasted_iota(jnp.int32, (LANES, LANES), 0)
    c = lax.broadcasted_iota(jnp.int32, (LANES, LANES), 1)
    before = (r < c).astype(BF16)

    def prefix_count(mask, n):
        run = jnp.zeros((N_EXPERTS, 1), F32)
        outs = []
        for j in range(n // LANES):
            blk = mask[:, j * LANES:(j + 1) * LANES].astype(F32).astype(BF16)
            outs.append(_dot(blk, before) + run)
            run = run + jnp.sum(blk.astype(F32), axis=1, keepdims=True)
        return jnp.concatenate(outs, axis=1)

    for start, n, cap, slot_base in segments:
        a = aff[:, start:start + n]
        bits = pltpu.bitcast(a, jnp.int32)

        def search(_, bounds):
            lo, hi = bounds
            mid = lo + ((hi - lo + 1) >> 1)
            cnt = jnp.sum((bits >= mid).astype(F32), axis=1, keepdims=True)
            ok = cnt >= cap
            return jnp.where(ok, mid, lo), jnp.where(ok, hi, mid - 1)

        lo0 = jnp.zeros((N_EXPERTS, 1), jnp.int32)
        hi0 = jnp.full((N_EXPERTS, 1), 0x7F800000, jnp.int32)
        thr, _ = lax.fori_loop(0, 32, search, (lo0, hi0))
        above = bits > thr
        tie = bits == thr
        need = cap - jnp.sum(above.astype(F32), axis=1, keepdims=True)
        sel = above | (tie & (prefix_count(tie, n) < need))
        pos = prefix_count(sel, n)
        slot_ref[0, :, start:start + n] = jnp.where(sel, pos + slot_base, -1.0)
        gate_ref[0, :, start:start + n] = jnp.where(sel, a, 0.0)
        if start == 0:
            t_idx = lax.broadcasted_iota(jnp.int32, (n, LANES), 0)
            tile_start = lax.broadcasted_iota(jnp.int32, (n, LANES), 1) * TM
            before_tile = jnp.where(t_idx < tile_start, 1.0, 0.0).astype(BF16)
            first_ref[0] = _dot(jnp.where(sel, 1.0, 0.0).astype(BF16), before_tile)


def _router(logits_t, segments):
    B, E, T = logits_t.shape
    blk = pl.BlockSpec((1, E, T), lambda b: (b, 0, 0))
    return pl.pallas_call(
        functools.partial(_router_kernel, segments=segments),
        grid=(B,),
        in_specs=[blk],
        out_specs=[blk, blk, pl.BlockSpec((1, E, LANES), lambda b: (b, 0, 0))],
        out_shape=[jax.ShapeDtypeStruct((B, E, T), F32)] * 2 + [jax.ShapeDtypeStruct((B, E, LANES), F32)],
        compiler_params=_params("arbitrary"),
        name="router",
    )(logits_t)


def _slot_window(first, align, rows, n_slots):
    return pl.multiple_of(jnp.minimum(first // align * align, n_slots - rows), align)


def _moe_ffn_kernel(first_ref, busy_ref, slot_ref, gate_ref, h2_ref, w1_ref, w3_ref, w2_ref, o_ref, xg_scr, g_scr,
                    *, n_slots, n_tiles):
    pair = pl.program_id(0) * N_EXPERTS + pl.program_id(1)
    base = pair * n_tiles
    xg_scr[...] = jnp.zeros_like(xg_scr)
    g_scr[...] = jnp.zeros_like(g_scr)

    def gather(rows):
        row_id = lax.broadcasted_iota(jnp.int32, (rows, TM), 0)
        for c in range(n_tiles):
            tok = slice(c * TM, (c + 1) * TM)
            w0 = _slot_window(first_ref[base + c], 8, rows, n_slots)
            hit = slot_ref[0, 0, :, tok] == (row_id + w0).astype(F32)
            win = pl.ds(w0, rows)
            xg_scr[win, :] += _dot(jnp.where(hit, 1.0, 0.0).astype(BF16), h2_ref[0, tok, :])
            g_scr[win, :] += jnp.sum(jnp.where(hit, gate_ref[0, 0, :, tok], 0.0), axis=1, keepdims=True)

    sparse = busy_ref[pair] <= SPARSE_HITS

    @pl.when(sparse)
    def _():
        gather(min(SPARSE_HITS + 8, n_slots))

    @pl.when(jnp.logical_not(sparse))
    def _():
        gather(min(TM + 8, n_slots))

    xb = xg_scr[...].astype(BF16)
    a = _dot(xb, w1_ref[0, 0].astype(BF16))
    b = _dot(xb, w3_ref[0, 0].astype(BF16))
    hidden = (a * jax.nn.sigmoid(a) * b).astype(BF16)
    o_ref[0, 0] = (_dot(hidden, w2_ref[0, 0].astype(BF16)) * g_scr[...]).astype(o_ref.dtype)


def _moe_ffn(first, busy, slot, gate, h2, w1, w3, w2, l, n_slots):
    B, E, T = slot.shape
    row = pl.BlockSpec((1, 1, 1, T), lambda b, e: (b, e, 0, 0))
    wsp = pl.BlockSpec((1, 1, D, D), lambda b, e: (l, e, 0, 0))
    smem = pl.BlockSpec(memory_space=pltpu.SMEM)
    return pl.pallas_call(
        functools.partial(_moe_ffn_kernel, n_slots=n_slots, n_tiles=T // TM),
        grid=(B, E),
        in_specs=[smem, smem, row, row,
                  pl.BlockSpec((1, T, D), lambda b, e: (b, 0, 0), pipeline_mode=pl.Buffered(1)), wsp, wsp, wsp],
        out_specs=pl.BlockSpec((1, 1, n_slots, D), lambda b, e: (b, e, 0, 0)),
        out_shape=jax.ShapeDtypeStruct((B, E, n_slots, D), BF16),
        scratch_shapes=[pltpu.VMEM((n_slots, D), F32), pltpu.VMEM((n_slots, 1), F32)],
        compiler_params=_params("arbitrary", "arbitrary"),
        name="moe_ffn",
    )(first, busy, slot.reshape(B, E, 1, T), gate.reshape(B, E, 1, T), h2, w1, w3, w2)


def _moe_combine_kernel(first_ref, busy_ref, x_ref, mod_ref, slot_ref, y_ref, fg_ref, o_ref, acc_scr,
                        *, n_slots, n_tiles, n_lat_tiles, final):
    tm = x_ref.shape[1]

    def combine(rows):
        col_id = lax.broadcasted_iota(jnp.int32, (tm, rows), 1)
        acc = jnp.zeros((tm, D), F32)
        for e in range(N_EXPERTS):
            w0 = _slot_window(first_ref[(pl.program_id(0) * N_EXPERTS + e) * n_tiles + pl.program_id(1)], 16, rows,
                              n_slots)
            hit = slot_ref[0, :, e:e + 1] == (col_id + w0).astype(F32)
            acc = acc + _dot(jnp.where(hit, 1.0, 0.0).astype(BF16), y_ref[0, e, pl.ds(w0, rows), :])
        acc_scr[...] = acc

    sparse = busy_ref[pl.program_id(0) * n_tiles + pl.program_id(1)] <= SPARSE_HITS

    @pl.when(sparse)
    def _():
        combine(min(SPARSE_HITS + 16, n_slots))

    @pl.when(jnp.logical_not(sparse))
    def _():
        combine(min(tm + 16, n_slots))

    x = x_ref[0] + mod_ref[0, 5:6, :] * acc_scr[...]
    if final:
        @pl.when(pl.program_id(1) < n_lat_tiles)
        def _():
            o_ref[0] = _rms(x) * fg_ref[...]
    else:
        o_ref[0] = x


def _moe_combine(first, busy, xa, mod, slot_t, y, final_g, n_lat_tiles, n_slots, final):
    B, T, _ = xa.shape
    ctx_row = B
    tile = pl.BlockSpec((1, TM, D), lambda b, i: (b, i, 0))
    smem = pl.BlockSpec(memory_space=pltpu.SMEM)
    if final:
        out_spec = pl.BlockSpec((1, TM, D), lambda b, i: (b, jnp.minimum(i, n_lat_tiles - 1), 0))
        out_rows = n_lat_tiles * TM
    else:
        out_spec, out_rows = tile, T
    return pl.pallas_call(
        functools.partial(_moe_combine_kernel, n_slots=n_slots, n_tiles=T // TM, n_lat_tiles=n_lat_tiles,
                          final=final),
        grid=(B, T // TM),
        in_specs=[smem, smem, tile,
                  pl.BlockSpec((1, 6, D), lambda b, i: (jnp.where(i >= n_lat_tiles, ctx_row, b), 0, 0)),
                  pl.BlockSpec((1, TM, N_EXPERTS), lambda b, i: (b, i, 0)),
                  pl.BlockSpec((1, N_EXPERTS, n_slots, D), lambda b, i: (b, 0, 0, 0)),
                  pl.BlockSpec((1, D), lambda b, i: (0, 0))],
        out_specs=out_spec,
        out_shape=jax.ShapeDtypeStruct((B, out_rows, D), F32),
        scratch_shapes=[pltpu.VMEM((TM, D), F32)],
        compiler_params=_params("arbitrary", "arbitrary"),
        name="moe_combine",
    )(first, busy, xa, mod, slot_t, y, final_g)


def _swa_head_perm():
    return [kv * SWA_GROUP + g for g in range(SWA_GROUP) for kv in range(SWA_KV)]


def _split_w_in(w_in):
    depth = w_in.shape[0]
    o = [0, D, 2 * D, 3 * D, 6 * D, 7 * D, 7 * D + SWA_KV * HEAD, 7 * D + 2 * SWA_KV * HEAD, 10 * D + 2 * SWA_KV * HEAD]
    dq, dk, dv, hy, sq, sk, sv, gt = (w_in[:, :, o[k]:o[k + 1]] for k in range(8))
    scale = HEAD ** -0.5 * math.log2(math.e)
    sq = sq.reshape(depth, D, SWA_KV * SWA_GROUP, HEAD)[:, :, jnp.array(_swa_head_perm())].reshape(depth, D, D)
    w_attn = jnp.concatenate([dq * scale, dk, dv, sq * scale, sk, sv], axis=-1).astype(BF16)
    w_hg = jnp.concatenate([hy, gt], axis=-1).astype(BF16)
    return w_attn, w_hg


def kernel(x, c, ctx, c_ctx, w_mod, b_mod, norm1_g, norm2_g, w_in, diff_lambda, diff_subln_g, hy_conv_w,
           hy_conv_b, hy_ff_w1, hy_ff_b1, hy_ff_w2, hy_ff_b2, hy_ff_w3, hy_sin_freq, hy_bias, swa_sink,
           w_branch, w_out, router_w, moe_w1, moe_w3, moe_w2, final_g):
    B, L, _ = x.shape
    Lc = ctx.shape[1]
    depth = w_mod.shape[0]
    T = L + Lc
    assert Lc == TM and L % (4 * TM) == 0 and x.shape[2] == D
    n_lat_tiles = L // TM
    cap_l = CAPACITY_FACTOR * L // N_EXPERTS
    cap_c = CAPACITY_FACTOR * Lc // N_EXPERTS
    n_slots = cap_l + cap_c
    segments = ((0, L, cap_l, 0), (L, Lc, cap_c, cap_l))

    xa = jnp.concatenate([x, ctx], axis=1)
    mods = _mods(c, c_ctx, w_mod, b_mod)
    cos_t, sin_t = _rope_tables(L, Lc)
    tabs = _dft_tables(L, Lc)
    w_attn, w_hg = _split_w_in(w_in)
    rope_flags = tuple([True] * 16 + [False] * 8 + [True] * 8 + [True] * 2 + [False] * 2)
    perm = jnp.array(_swa_head_perm())
    wb = w_branch.at[:, 2].set(
        w_branch[:, 2].reshape(depth, SWA_KV * SWA_GROUP, HEAD, D)[:, perm].reshape(depth, D, D)).astype(BF16)
    wo = w_out.astype(BF16)
    w1, w3, w2 = moe_w1, moe_w3, moe_w2
    rw_pad = jnp.pad(router_w, ((0, 0), (0, 0), (0, LANES - N_EXPERTS)))
    w1p = jnp.pad(hy_ff_w1, ((0, 0), (0, LANES - HY_EMB), (0, 0)))

    for l in range(depth):
        lambda_init = 0.8 - 0.6 * math.exp(-0.3 * l)
        mod = mods[l]
        p_attn = _norm_proj(xa, mod, norm1_g[l][None], w_attn[l], cos_t, sin_t, rope_flags, BF16,
                            n_lat_tiles, "proj_attn")
        p_hg = _norm_proj(xa, mod, norm1_g[l][None], w_hg[l], cos_t, sin_t, (False,) * (N_HG // LANES), F32,
                          n_lat_tiles, "proj_hg")
        y_diff = _diff_attn(p_attn, diff_lambda, diff_subln_g, L, Lc, lambda_init, l)
        y_swa = _swa(p_attn, swa_sink, L, Lc, l)
        hp = dict(w1p=w1p[l], b1=hy_ff_b1[l][None], w2=hy_ff_w2[l], b2=hy_ff_b2[l][None],
                  fr=hy_sin_freq[l][None], w3=hy_ff_w3[l], conv_w=hy_conv_w[l], conv_b=hy_conv_b[l][None],
                  bias=hy_bias[l][None])
        hy = _hyena(p_hg, hp, tabs, L, Lc)
        xa, h2, logits_t = _merge(xa, mod, y_diff, hy, hp['bias'], y_swa, p_hg, wb[l], wo[l], norm2_g[l][None],
                                  rw_pad[l], n_lat_tiles)
        slot, gate, first = _router(logits_t, segments)
        first = jnp.concatenate([first[:, :, :n_lat_tiles], jnp.full((B, N_EXPERTS, 1), cap_l, F32)], axis=2)
        first = first.astype(jnp.int32)
        ends = jnp.concatenate([first[:, :, 1:n_lat_tiles], jnp.full((B, N_EXPERTS, 1), cap_l, jnp.int32),
                                jnp.full((B, N_EXPERTS, 1), n_slots, jnp.int32)], axis=2)
        hits = ends - first
        first = first.reshape(-1)
        y_e = _moe_ffn(first, jnp.max(hits, axis=2).reshape(-1), slot, gate, h2, w1, w3, w2, l, n_slots)
        xa = _moe_combine(first, jnp.max(hits, axis=1).reshape(-1), xa, mod, jnp.swapaxes(slot, 1, 2), y_e,
                          final_g[None], n_lat_tiles, n_slots, final=l == depth - 1)
    return xa
```

```python
import functools
import math

import jax
import jax.numpy as jnp
from jax import lax
from jax.experimental import pallas as pl
from jax.experimental.pallas import tpu as pltpu

F32 = jnp.float32
BF16 = jnp.bfloat16
HIGHEST = lax.Precision.HIGHEST

D = 1024
HEAD = 64
GRID_W = 64
ROPE_BASE = 10000.0
EPS = 1e-6
N_DIFF_HEADS = D // (2 * HEAD)
DIFF_V = 2 * HEAD
SWA_KV = 4
SWA_GROUP = 4
WINDOW = 128
N_EXPERTS = 16
CAPACITY_FACTOR = 2
HY_BANDS = 16
HY_EMB = 1 + 2 * HY_BANDS
HY_FF = 64
HY_FAST_DECAY = 0.3
HY_SLOW_DECAY = 1.5
HY_TARGET = 1e-2

LANES = 128
TM = 256
DFT_N2 = 128
VMEM_LIMIT = 56 * 1024 * 1024

SPARSE_HITS = 64


def _params(*sem):
    return pltpu.CompilerParams(dimension_semantics=sem, vmem_limit_bytes=VMEM_LIMIT)


def _dot(a, b, precision=None):
    return jnp.dot(a, b, preferred_element_type=F32, precision=precision)


def _dot_nt(a, b, precision=None):
    return lax.dot_general(a, b, (((1,), (1,)), ((), ())), preferred_element_type=F32, precision=precision)


def _rms(x):
    return x * lax.rsqrt(jnp.mean(x * x, axis=-1, keepdims=True) + EPS)


def _mod_kernel(c_ref, w_ref, b_ref, o_ref):
    c = c_ref[...]
    sc = c * jax.nn.sigmoid(c)
    o_ref[0] = _dot(sc, w_ref[0], HIGHEST) + b_ref[0]


def _mods(c, c_ctx, w_mod, b_mod):
    B = c.shape[0]
    depth = w_mod.shape[0]
    rows = -(-(B + 1) // 8) * 8
    cc = jnp.zeros((rows, D), F32).at[:B].set(c).at[B].set(c_ctx)
    out = pl.pallas_call(
        _mod_kernel,
        grid=(depth, 6),
        in_specs=[pl.BlockSpec((rows, D), lambda l, j: (0, 0)),
                  pl.BlockSpec((1, D, D), lambda l, j: (l, 0, j)),
                  pl.BlockSpec((1, 1, D), lambda l, j: (l, 0, j))],
        out_specs=pl.BlockSpec((1, rows, D), lambda l, j: (l, 0, j)),
        out_shape=jax.ShapeDtypeStruct((depth, rows, 6 * D), F32),
        compiler_params=_params("arbitrary", "arbitrary"),
        name="mods",
    )(cc, w_mod, b_mod.reshape(depth, 1, 6 * D))
    return out.reshape(depth, rows, 6, D)


def _norm_proj_kernel(x_ref, mod_ref, g_ref, wa_ref, wh_ref, cos_ref, sin_ref, oa_ref, oh_ref, *, rope_flags):
    x = x_ref[0]
    h = _rms(x) * g_ref[...]
    h = h * (1.0 + mod_ref[0, 1:2, :]) + mod_ref[0, 0:1, :]
    hb = h.astype(BF16)
    sub_per_chunk = 4
    cos = cos_ref[...]
    sin = sin_ref[...]
    lane = lax.broadcasted_iota(jnp.int32, cos.shape, 1)
    first_half = (lane % (HEAD // 2)) < (HEAD // 4)
    for w_ref, o_ref, flags in ((wa_ref, oa_ref, rope_flags), (wh_ref, oh_ref, (False,) * (wh_ref.shape[1] // LANES))):
        n_sub = len(flags)
        for c0 in range(0, n_sub, sub_per_chunk):
            c1 = min(c0 + sub_per_chunk, n_sub)
            acc = _dot(hb, w_ref[:, c0 * LANES:c1 * LANES])
            for s in range(c0, c1):
                a = acc[:, (s - c0) * LANES:(s - c0 + 1) * LANES]
                if flags[s]:
                    partner = jnp.where(first_half, pltpu.roll(a, LANES - HEAD // 4, 1), pltpu.roll(a, HEAD // 4, 1))
                    a = a * cos + partner * sin
                o_ref[0, :, s * LANES:(s + 1) * LANES] = a.astype(o_ref.dtype)


def _norm_proj(xa, mod, g, w_attn, w_hg, cos_t, sin_t, rope_flags, n_lat_tiles):
    B, T, _ = xa.shape
    na, nh = w_attn.shape[1], w_hg.shape[1]
    ctx_row = B
    resident = lambda n: pl.BlockSpec((D, n), lambda b, i: (0, 0), pipeline_mode=pl.Buffered(1))
    return pl.pallas_call(
        functools.partial(_norm_proj_kernel, rope_flags=rope_flags),
        grid=(B, T // TM),
        in_specs=[pl.BlockSpec((1, TM, D), lambda b, i: (b, i, 0)),
                  pl.BlockSpec((1, 6, D), lambda b, i: (jnp.where(i >= n_lat_tiles, ctx_row, b), 0, 0)),
                  pl.BlockSpec((1, D), lambda b, i: (0, 0)),
                  resident(na), resident(nh),
                  pl.BlockSpec((TM, LANES), lambda b, i: (i, 0)),
                  pl.BlockSpec((TM, LANES), lambda b, i: (i, 0))],
        out_specs=[pl.BlockSpec((1, TM, na), lambda b, i: (b, i, 0)),
                   pl.BlockSpec((1, TM, nh), lambda b, i: (b, i, 0))],
        out_shape=[jax.ShapeDtypeStruct((B, T, na), BF16), jax.ShapeDtypeStruct((B, T, nh), BF16)],
        compiler_params=_params("arbitrary", "arbitrary"),
        name="proj",
    )(xa, mod, g, w_attn, w_hg, cos_t, sin_t)


def _rope_tables(L, Lc):
    t = jnp.arange(L)
    pos = jnp.stack([(t // GRID_W).astype(F32), (t % GRID_W).astype(F32)], axis=1)
    n_freq = HEAD // 4
    inv = ROPE_BASE ** (-jnp.arange(n_freq, dtype=F32) / n_freq)
    ang = pos[:, :, None] * inv
    cos = jnp.cos(ang)
    sin = jnp.sin(ang)
    cos_h = jnp.concatenate([cos, cos], axis=-1).reshape(L, HEAD)
    sin_h = jnp.concatenate([-sin, sin], axis=-1).reshape(L, HEAD)
    cos_t = jnp.concatenate([jnp.tile(cos_h, (1, LANES // HEAD)), jnp.ones((Lc, LANES), F32)], axis=0)
    sin_t = jnp.concatenate([jnp.tile(sin_h, (1, LANES // HEAD)), jnp.zeros((Lc, LANES), F32)], axis=0)
    return cos_t, sin_t


def _lane_block_reduce(x, op):
    out = x[:, :LANES]
    for j in range(1, x.shape[1] // LANES):
        out = op(out, x[:, j * LANES:(j + 1) * LANES])
    return out


def _diff_attn_kernel(lam_ref, q_ref, k_ref, v_ref, g_ref, o_ref, s_lat, s_ctx, m_run, qq_scr, l_scr, acc_scr,
                      *, L, Lc, tg, lambda_init):
    tq = TM
    n_groups = (L + Lc) // tg
    n_lat = L // tq
    n_tiles = (L + Lc) // tq
    rows = (2 * tq, LANES)
    lowest = jnp.full(rows, -jnp.inf, F32)
    nothing = jnp.zeros(rows, F32)
    lp = lam_ref[0]
    lam = (jnp.exp(jnp.sum(lp[0:1] * lp[1:2], axis=-1, keepdims=True))
           - jnp.exp(jnp.sum(lp[2:3] * lp[3:4], axis=-1, keepdims=True)) + lambda_init)

    def stage(t, _):
        cur = t % 2
        prv = 1 - cur
        score_any = t < n_tiles
        score_lat = t < n_lat
        weigh_any = t >= 1
        weigh_lat = jnp.logical_and(weigh_any, t <= n_lat)

        @pl.when(score_any)
        def _():
            q = q_ref[0, pl.ds(pl.multiple_of(t * tq, tq), tq), :]
            lane = lax.broadcasted_iota(jnp.int32, q.shape, 1)
            zero = jnp.zeros_like(q)
            qq_scr[...] = jnp.concatenate([jnp.where(lane < HEAD, q, zero), jnp.where(lane >= HEAD, q, zero)],
                                          axis=0)

        @pl.when(jnp.logical_and(score_any, jnp.logical_not(score_lat)))
        def _():
            s_c = _dot_nt(qq_scr[...], k_ref[0, L:L + Lc, :])
            s_ctx[...] = s_c
            m_run[cur] = _lane_block_reduce(s_c, jnp.maximum)

        def row_max():
            return jnp.max(m_run[prv], axis=-1, keepdims=True)

        @pl.when(jnp.logical_and(weigh_any, jnp.logical_not(weigh_lat)))
        def _():
            p = jnp.exp2(s_ctx[...] - row_max())
            l_scr[...] = _lane_block_reduce(p, jnp.add)
            acc_scr[...] = _dot(p.astype(BF16), v_ref[0, L:L + Lc, :])

        def scores(gi, mrun):
            start = pl.multiple_of(gi * tg, LANES)
            s = _dot_nt(qq_scr[...], k_ref[0, pl.ds(start, tg), :])
            s_lat[cur, gi] = s
            return jnp.maximum(mrun, _lane_block_reduce(s, jnp.maximum))

        def weighted(gi, lrun, acc, m):
            start = pl.multiple_of(gi * tg, LANES)
            p = jnp.exp2(s_lat[prv, gi] - m)
            return (lrun + _lane_block_reduce(p, jnp.add),
                    acc + _dot(p.astype(BF16), v_ref[0, pl.ds(start, tg), :]))

        @pl.when(jnp.logical_and(score_lat, weigh_lat))
        def _():
            m = row_max()

            def both(gi, carry):
                mrun, lrun, acc = carry
                lrun, acc = weighted(gi, lrun, acc, m)
                return scores(gi, mrun), lrun, acc

            m_run[cur], l_scr[...], acc_scr[...] = lax.fori_loop(0, n_groups, both, (lowest, nothing, nothing))

        @pl.when(jnp.logical_and(score_lat, jnp.logical_not(weigh_lat)))
        def _():
            m_run[cur] = lax.fori_loop(0, n_groups, scores, lowest)

        @pl.when(jnp.logical_and(jnp.logical_not(score_lat), weigh_lat))
        def _():
            m = row_max()
            l_scr[...], acc_scr[...] = lax.fori_loop(0, n_groups, lambda gi, c: weighted(gi, c[0], c[1], m),
                                                      (nothing, nothing))

        @pl.when(weigh_any)
        def _():
            o = acc_scr[...] / jnp.sum(l_scr[...], axis=-1, keepdims=True)
            o = _rms(o[:tq] - lam * o[tq:])
            out_rows = pl.ds(pl.multiple_of((t - 1) * tq, tq), tq)
            o_ref[0, out_rows, :] = (o * g_ref[0] * (1.0 - lambda_init)).astype(o_ref.dtype)

        return 0

    lax.fori_loop(0, n_tiles + 1, stage, 0)


def _diff_attn(p_attn, diff_lambda, subln_g, L, Lc, lambda_init, l):
    B, T, _ = p_attn.shape
    tg = T // 2
    nh = N_DIFF_HEADS
    head = lambda c0: pl.BlockSpec((1, T, DIFF_V), lambda b, h: (b, 0, c0 + h))
    return pl.pallas_call(
        functools.partial(_diff_attn_kernel, L=L, Lc=Lc, tg=tg, lambda_init=lambda_init),
        grid=(B, nh),
        in_specs=[pl.BlockSpec((1, 4, HEAD), lambda b, h: (l, 0, 0)),
                  head(0), head(nh), head(2 * nh),
                  pl.BlockSpec((1, 1, DIFF_V), lambda b, h: (l, 0, 0))],
        out_specs=head(0),
        out_shape=jax.ShapeDtypeStruct((B, T, D), BF16),
        scratch_shapes=[pltpu.VMEM((2, T // tg, 2 * TM, tg), F32), pltpu.VMEM((2 * TM, Lc), F32),
                        pltpu.VMEM((2, 2 * TM, LANES), F32), pltpu.VMEM((2 * TM, DIFF_V), BF16),
                        pltpu.VMEM((2 * TM, LANES), F32), pltpu.VMEM((2 * TM, DIFF_V), F32)],
        compiler_params=_params("arbitrary", "arbitrary"),
        name="diff_attn",
    )(diff_lambda, p_attn, p_attn, p_attn, subln_g.reshape(-1, 1, DIFF_V))


def _swa_kernel(sink_ref, q_ref, k_ref, v_ref, o_ref, *, L, Lc, l):
    i = pl.program_id(1)
    tq = q_ref.shape[1]
    kvw = SWA_KV * HEAD
    win = tq + 2 * WINDOW
    lane_head = lax.broadcasted_iota(jnp.int32, (tq, kvw), 1) // HEAD
    k_ctx = k_ref[0, L:L + Lc, :]
    v_ctx = v_ref[0, L:L + Lc, :]

    def run(latent):
        if latent:
            start = pl.multiple_of(jnp.clip(i * tq - WINDOW, 0, L - win), WINDOW)
            k_win = k_ref[0, pl.ds(start, win), :]
            v_win = v_ref[0, pl.ds(start, win), :]
            q_pos = i * tq + lax.broadcasted_iota(jnp.int32, (tq, win), 0)
            k_pos = start + lax.broadcasted_iota(jnp.int32, (tq, win), 1)
            band_bias = jnp.where(jnp.abs(k_pos - q_pos) <= WINDOW, 0.0, -jnp.inf)
        for g in range(SWA_GROUP):
            qg = q_ref[0, :, g * kvw:(g + 1) * kvw]
            acc = jnp.zeros((tq, kvw), F32)
            for kv in range(SWA_KV):
                qe = jnp.where(lane_head == kv, qg, jnp.zeros_like(qg))
                snk = sink_ref[l, kv * SWA_GROUP + g] * math.log2(math.e)
                s_c = _dot_nt(qe, k_ctx)
                m = jnp.maximum(jnp.max(s_c, axis=-1, keepdims=True), snk)
                if latent:
                    s_w = _dot_nt(qe, k_win) + band_bias
                    m = jnp.maximum(m, jnp.max(s_w, axis=-1, keepdims=True))
                p_c = jnp.exp2(s_c - m)
                den = jnp.sum(p_c, axis=-1, keepdims=True) + jnp.exp2(snk - m)
                o = _dot(p_c.astype(BF16), v_ctx)
                if latent:
                    p_w = jnp.exp2(s_w - m)
                    den = den + jnp.sum(p_w, axis=-1, keepdims=True)
                    o = o + _dot(p_w.astype(BF16), v_win)
                acc = acc + jnp.where(lane_head == kv, o / den, 0.0)
            o_ref[0, :, g * kvw:(g + 1) * kvw] = acc.astype(o_ref.dtype)

    @pl.when(i < L // tq)
    def _():
        run(True)

    @pl.when(i >= L // tq)
    def _():
        run(False)


def _swa(p_attn, sink, L, Lc, l):
    B, T, _ = p_attn.shape
    kvw = SWA_KV * HEAD
    return pl.pallas_call(
        functools.partial(_swa_kernel, L=L, Lc=Lc, l=l),
        grid=(B, T // TM),
        in_specs=[pl.BlockSpec(memory_space=pltpu.SMEM),
                  pl.BlockSpec((1, TM, D), lambda b, i: (b, i, 3)),
                  pl.BlockSpec((1, T, kvw), lambda b, i: (b, 0, 4 * D // kvw)),
                  pl.BlockSpec((1, T, kvw), lambda b, i: (b, 0, 4 * D // kvw + 1))],
        out_specs=pl.BlockSpec((1, TM, D), lambda b, i: (b, i, 0)),
        out_shape=jax.ShapeDtypeStruct((B, T, D), BF16),
        compiler_params=_params("arbitrary", "arbitrary"),
        name="swa",
    )(sink, p_attn, p_attn, p_attn)


def _hy_filter_kernel(emb_ref, w1_ref, b1_ref, w2_ref, b2_ref, fr_ref, w3f_ref, w3b_ref, dl_ref, o_ref, *, L):
    rows = TM
    fr = fr_ref[...]

    def raw(i, total):
        r0 = pl.multiple_of(i * rows, rows)
        e = emb_ref[pl.ds(r0, rows), :]
        z = jnp.sin(fr * (_dot(e, w1_ref[...], HIGHEST) + b1_ref[...]))
        z = jnp.sin(fr * (_dot(z, w2_ref[...], HIGHEST) + b2_ref[...]))
        row = r0 + lax.broadcasted_iota(jnp.int32, (rows, 1), 0)
        h = jnp.where(row < L, _dot(z, w3f_ref[...], HIGHEST), _dot(z, w3b_ref[...], HIGHEST))
        h = h * jnp.exp(-e[:, 0:1] * dl_ref[...])
        o_ref[pl.ds(r0, rows), :] = h
        return total + jnp.sum(jnp.abs(h), axis=0, keepdims=True)

    n_chunks = emb_ref.shape[0] // rows
    norm = lax.fori_loop(0, n_chunks, raw, jnp.zeros((1, o_ref.shape[1]), F32)) + EPS

    def scale(i, _):
        r0 = pl.multiple_of(i * rows, rows)
        row = r0 + lax.broadcasted_iota(jnp.int32, (rows, 1), 0)
        o_ref[pl.ds(r0, rows), :] = jnp.where(row == L, 0.0, o_ref[pl.ds(r0, rows), :] / norm)
        return 0

    lax.fori_loop(0, n_chunks, scale, 0)


def _hy_embedding(L):
    n = jnp.arange(2 * L)
    lag = jnp.where(n < L, n, jnp.where(n == L, 0, 2 * L - n))
    t = jnp.linspace(0.0, 1.0, L, dtype=F32)[lag][:, None]
    w = (2.0 * math.pi * jnp.arange(L, dtype=F32) / L)[lag][:, None]
    f = jnp.linspace(1e-4, HY_BANDS - 1, HY_BANDS, dtype=F32)[None, :]
    emb = jnp.concatenate([t, jnp.cos(f * w), -jnp.sin(f * w)], axis=-1)
    return jnp.pad(emb, ((0, 0), (0, LANES - HY_EMB)))


def _hy_filter(L, w1p, b1, w2, b2, fr, w3, deltas):
    ct = 512
    emb = _hy_embedding(L)
    full = lambda shape: pl.BlockSpec(shape, lambda j: (0, 0))
    return pl.pallas_call(
        functools.partial(_hy_filter_kernel, L=L),
        grid=(D // ct,),
        in_specs=[full((2 * L, LANES)), full((LANES, HY_FF)), full((1, HY_FF)), full((HY_FF, HY_FF)),
                  full((1, HY_FF)), full((1, HY_FF)),
                  pl.BlockSpec((HY_FF, ct), lambda j: (0, j)),
                  pl.BlockSpec((HY_FF, ct), lambda j: (0, D // ct + j)),
                  pl.BlockSpec((1, ct), lambda j: (0, j))],
        out_specs=pl.BlockSpec((2 * L, ct), lambda j: (0, j)),
        out_shape=jax.ShapeDtypeStruct((2 * L, D), F32),
        compiler_params=_params("arbitrary"),
        name="hy_filter",
    )(emb, w1p, b1, w2, b2, fr, w3, w3, deltas)


def _short_conv(x, w, b):
    n = x.shape[0]
    row = lax.broadcasted_iota(jnp.int32, x.shape, 0)
    prev = jnp.where(row == 0, 0.0, pltpu.roll(x, 1, 0))
    nxt = jnp.where(row == n - 1, 0.0, pltpu.roll(x, n - 1, 0))
    return prev * w[0:1] + x * w[1:2] + nxt * w[2:3] + b


def _hy_pre_kernel(x0_ref, x1_ref, v_ref, w0_ref, w1_ref, w2_ref, b0_ref, b1_ref, b2_ref, x0c_ref, u_ref):
    x0c_ref[0] = _short_conv(x0_ref[0].astype(F32), w0_ref[...], b0_ref[...]).astype(x0c_ref.dtype)
    u = (_short_conv(x1_ref[0].astype(F32), w1_ref[...], b1_ref[...])
         * _short_conv(v_ref[0].astype(F32), w2_ref[...], b2_ref[...]))
    u_ref[0] = u.astype(u_ref.dtype)


def _hy_pre(p_hg, conv_w, conv_b, L):
    B = p_hg.shape[0]
    ct = LANES
    nb = D // ct
    col = lambda k: pl.BlockSpec((1, L, ct), lambda b, j: (b, 0, k * nb + j))
    wsp = lambda k: pl.BlockSpec((3, ct), lambda b, j: (0, k * nb + j))
    bsp = lambda k: pl.BlockSpec((1, ct), lambda b, j: (0, k * nb + j))
    out = pl.BlockSpec((1, L, ct), lambda b, j: (b, 0, j))
    return pl.pallas_call(
        _hy_pre_kernel,
        grid=(B, nb),
        in_specs=[col(0), col(1), col(2), wsp(0), wsp(1), wsp(2), bsp(0), bsp(1), bsp(2)],
        out_specs=[out, out],
        out_shape=[jax.ShapeDtypeStruct((B, L, D), BF16)] * 2,
        compiler_params=_params("arbitrary", "arbitrary"),
        name="hy_pre",
    )(p_hg, p_hg, p_hg, conv_w, conv_w, conv_w, conv_b, conv_b, conv_b)


def _lmat_kernel(m_ref, x_ref, o_ref, *, precision):
    x = x_ref[0]
    if precision is None:
        x = x.astype(BF16)
    o_ref[0] = _dot(m_ref[...], x, precision).astype(o_ref.dtype)


def _lmat(mat, x, out_dtype, precision, wc, name):
    Bn, K, W = x.shape
    M = mat.shape[0]
    return pl.pallas_call(
        functools.partial(_lmat_kernel, precision=precision),
        grid=(Bn, W // wc),
        in_specs=[pl.BlockSpec((M, K), lambda b, j: (0, 0)),
                  pl.BlockSpec((1, K, wc), lambda b, j: (b, 0, j))],
        out_specs=pl.BlockSpec((1, M, wc), lambda b, j: (b, 0, j)),
        out_shape=jax.ShapeDtypeStruct((Bn, M, W), out_dtype),
        compiler_params=_params("arbitrary", "arbitrary"),
        name=name,
    )(mat, x)


def _hy_s1_kernel(m_ref, x_ref, o_ref, *, precision):
    k, nb = x_ref.shape[1], x_ref.shape[2]
    res = _dot(m_ref[...], x_ref[0].reshape(k, nb * D), precision)
    o_ref[0] = res.reshape(2, res.shape[0] // 2, nb, D).astype(o_ref.dtype)


def _hy_s1(mat, x, nb, out_dtype=BF16, precision=None, name="hy_s1"):
    Bn, K, n2, _ = x.shape
    M = mat.shape[0]
    return pl.pallas_call(
        functools.partial(_hy_s1_kernel, precision=precision),
        grid=(Bn, n2 // nb),
        in_specs=[pl.BlockSpec((M, K), lambda b, j: (0, 0)),
                  pl.BlockSpec((1, K, nb, D), lambda b, j: (b, 0, j, 0))],
        out_specs=pl.BlockSpec((1, 2, M // 2, nb, D), lambda b, j: (b, 0, 0, j, 0)),
        out_shape=jax.ShapeDtypeStruct((Bn, 2, M // 2, n2, D), out_dtype),
        compiler_params=_params("arbitrary", "arbitrary"),
        name=name,
    )(mat, x)


def _hy_s3_kernel(g_ref, z_ref, o_ref):
    two, n1, nb, _ = z_ref.shape[1:]
    conv = _dot(g_ref[...], z_ref[0].reshape(two * n1, nb * D))
    o_ref[0] = conv.reshape(conv.shape[0], nb, D).astype(o_ref.dtype)


def _hy_s3(g_mat, z, nb):
    Bn, _, n1, n2, _ = z.shape
    R = g_mat.shape[0]
    return pl.pallas_call(
        _hy_s3_kernel,
        grid=(Bn, n2 // nb),
        in_specs=[pl.BlockSpec((R, 2 * n1), lambda b, j: (0, 0)),
                  pl.BlockSpec((1, 2, n1, nb, D), lambda b, j: (b, 0, 0, j, 0))],
        out_specs=pl.BlockSpec((1, R, nb, D), lambda b, j: (b, 0, j, 0)),
        out_shape=jax.ShapeDtypeStruct((Bn, R, n2, D), BF16),
        compiler_params=_params("arbitrary", "arbitrary"),
        name="hy_s3",
    )(g_mat, z)


def _cmul(xr, xi, kr, ki):
    return xr * kr - xi * ki, xr * ki + xi * kr


def _hy_spec_kernel(a_ref, m_ref, o_ref):
    a = jnp.concatenate([a_ref[0, 0], a_ref[1, 0]], axis=0)
    o_ref[0] = _dot(m_ref[0], a, HIGHEST)


def _hy_spectrum(a, m_tab):
    n1 = a.shape[1]
    n2 = DFT_N2
    return pl.pallas_call(
        _hy_spec_kernel,
        grid=(n1,),
        in_specs=[pl.BlockSpec((2, 1, n2, D), lambda k: (0, k, 0, 0)),
                  pl.BlockSpec((1, 2 * n2, 2 * n2), lambda k: (k, 0, 0))],
        out_specs=pl.BlockSpec((1, 2 * n2, D), lambda k: (k, 0, 0)),
        out_shape=jax.ShapeDtypeStruct((n1, 2 * n2, D), F32),
        compiler_params=_params("arbitrary"),
        name="hy_spectrum",
    )(a, m_tab)


def _hy_mid_kernel(a_ref, m_ref, mt_ref, kf_ref, o_ref):
    n2 = DFT_N2
    kr = kf_ref[0, :n2]
    ki = kf_ref[0, n2:]

    samples = range(a_ref.shape[0])
    a = jnp.concatenate([jnp.concatenate([a_ref[b, 0, 0], a_ref[b, 1, 0]], axis=0) for b in samples], axis=1)
    x = _dot(m_ref[0], a)
    ys = []
    for b in samples:
        yr, yi = _cmul(x[:n2, b * D:(b + 1) * D], x[n2:, b * D:(b + 1) * D], kr, ki)
        ys.append(jnp.concatenate([yr, yi], axis=0).astype(BF16))
    z = _dot(mt_ref[0], jnp.concatenate(ys, axis=1)).astype(o_ref.dtype)
    for b in samples:
        o_ref[b, 0, 0] = z[:n2, b * D:(b + 1) * D]
        o_ref[b, 1, 0] = z[n2:, b * D:(b + 1) * D]


def _hy_mid(a, m_bf, mt_bf, kf):
    B, _, n1, n2, _ = a.shape
    blk = pl.BlockSpec((B, 2, 1, n2, D), lambda k: (0, 0, k, 0, 0))
    tab = pl.BlockSpec((1, 2 * n2, 2 * n2), lambda k: (k, 0, 0))
    return pl.pallas_call(
        _hy_mid_kernel,
        grid=(n1,),
        in_specs=[blk, tab, tab, pl.BlockSpec((1, 2 * n2, D), lambda k: (k, 0, 0))],
        out_specs=blk,
        out_shape=jax.ShapeDtypeStruct(a.shape, BF16),
        compiler_params=_params("arbitrary"),
        name="hy_mid",
    )(a, m_bf, mt_bf, kf)


def _hy_output(x0c, conv, u, bias):
    return x0c * (conv + u * bias)


def _hy_ctx_kernel(x0_ref, x1_ref, v_ref, w0_ref, w1_ref, w2_ref, b0_ref, b1_ref, b2_ref,
                   f_ref, kf_ref, g_ref, bias_ref, o_ref):
    x0 = _short_conv(x0_ref[0].astype(F32), w0_ref[...], b0_ref[...])
    u = (_short_conv(x1_ref[0].astype(F32), w1_ref[...], b1_ref[...])
         * _short_conv(v_ref[0].astype(F32), w2_ref[...], b2_ref[...]))
    x = _dot(f_ref[...], u.astype(BF16))
    nf = x.shape[0] // 2
    yr, yi = _cmul(x[:nf], x[nf:], kf_ref[:nf], kf_ref[nf:])
    conv = _dot(g_ref[...], jnp.concatenate([yr, yi], axis=0).astype(BF16))
    o_ref[0] = _hy_output(x0, conv, u, bias_ref[...]).astype(o_ref.dtype)


def _hy_ctx(p_hg, conv_w, conv_b, f_c, kf_c, g_c, bias, L, Lc):
    B = p_hg.shape[0]
    ct = 256
    nb = D // ct
    rb = L // Lc
    col = lambda k: pl.BlockSpec((1, Lc, ct), lambda b, j: (b, rb, k * nb + j))
    wsp = lambda k: pl.BlockSpec((3, ct), lambda b, j: (0, k * nb + j))
    bsp = lambda k: pl.BlockSpec((1, ct), lambda b, j: (0, k * nb + j))
    return pl.pallas_call(
        _hy_ctx_kernel,
        grid=(B, nb),
        in_specs=[col(0), col(1), col(2), wsp(0), wsp(1), wsp(2), bsp(0), bsp(1), bsp(2),
                  pl.BlockSpec(f_c.shape, lambda b, j: (0, 0)),
                  pl.BlockSpec((kf_c.shape[0], ct), lambda b, j: (0, j)),
                  pl.BlockSpec(g_c.shape, lambda b, j: (0, 0)),
                  pl.BlockSpec((1, ct), lambda b, j: (0, j))],
        out_specs=pl.BlockSpec((1, Lc, ct), lambda b, j: (b, 0, j)),
        out_shape=jax.ShapeDtypeStruct((B, Lc, D), BF16),
        compiler_params=_params("arbitrary", "arbitrary"),
        name="hy_ctx",
    )(p_hg, p_hg, p_hg, conv_w, conv_w, conv_w, conv_b, conv_b, conv_b, f_c, kf_c, g_c, bias)


def _dft_tables(L, Lc):
    n = 2 * L
    n2 = DFT_N2
    n1 = n // n2
    two_pi = 2.0 * math.pi

    def cs(idx, period):
        ang = (idx % period).astype(F32) * (two_pi / period)
        return jnp.cos(ang), jnp.sin(ang)

    k1 = jnp.arange(n1)[:, None]
    c, s = cs(k1 * jnp.arange(n1)[None, :], n1)
    f1_full = jnp.concatenate([c, -s], axis=0)
    f1_half = f1_full[:, :n1 // 2]
    kk1 = jnp.arange(n1)[:, None, None]
    kk2 = jnp.arange(n2)[None, :, None]
    nn2 = jnp.arange(n2)[None, None, :]
    c, s = cs(nn2 * kk1 + n1 * nn2 * kk2, n)
    ci = -s
    m_tab = jnp.concatenate([jnp.concatenate([c, -ci], axis=2), jnp.concatenate([ci, c], axis=2)], axis=1)
    c, s = cs(jnp.arange(n1 // 2)[:, None] * jnp.arange(n1)[None, :], n1)
    g_mat = jnp.concatenate([c, -s], axis=1) / n
    nc = 2 * Lc
    kc = jnp.arange(nc)[:, None]
    c, s = cs(kc * jnp.arange(nc)[None, :], nc)
    fc_full = jnp.concatenate([c, -s], axis=0)
    c, s = cs(jnp.arange(Lc)[:, None] * jnp.arange(nc)[None, :], nc)
    g_c = jnp.concatenate([c, -s], axis=1) / nc
    return dict(f1_full=f1_full, f1_half=f1_half.astype(BF16), m_tab=m_tab, m_bf=m_tab.astype(BF16),
                mt_bf=jnp.swapaxes(m_tab, 1, 2).astype(BF16), g_mat=g_mat.astype(BF16),
                fc_full=fc_full, fc_half=fc_full[:, :Lc].astype(BF16), g_c=g_c.astype(BF16))


def _hyena(p_hg, hp, tabs, L, Lc):
    B = p_hg.shape[0]
    n2 = DFT_N2
    n1 = 2 * L // n2
    wc = min(16 * D, n2 * D)
    deltas = jnp.abs(jnp.linspace(math.log(HY_TARGET) / HY_FAST_DECAY, math.log(HY_TARGET) / HY_SLOW_DECAY,
                                  D, dtype=F32))[None, :]
    filt = functools.partial(_hy_filter, w1p=hp['w1p'], b1=hp['b1'], w2=hp['w2'], b2=hp['b2'], fr=hp['fr'],
                             w3=hp['w3'], deltas=deltas)
    k_lat = filt(L)
    a_f = _hy_s1(tabs['f1_full'], k_lat.reshape(1, n1, n2, D), wc // D, F32, HIGHEST, "hy_filt_s1")
    kf = _hy_spectrum(a_f[0], tabs['m_tab'])
    k_ctx = filt(Lc)
    kf_c = _lmat(tabs['fc_full'], k_ctx[None], F32, HIGHEST, D, "hy_filt_ctx")[0]
    x0c, u = _hy_pre(p_hg, hp['conv_w'], hp['conv_b'], L)
    a = _hy_s1(tabs['f1_half'], u.reshape(B, n1 // 2, n2, D), wc // D)
    z = _hy_mid(a, tabs['m_bf'], tabs['mt_bf'], kf)
    conv = _hy_s3(tabs['g_mat'], z, wc // D).reshape(B, L, D)
    y_c = _hy_ctx(p_hg, hp['conv_w'], hp['conv_b'], tabs['fc_half'], kf_c, tabs['g_c'], hp['bias'], L, Lc)
    return conv, x0c, u, y_c


def _merge_kernel(x_ref, mod_ref, yd_ref, cv_ref, x0_ref, u_ref, yc_ref, hb_ref, ys_ref, gd_ref, gh_ref, gs_ref,
                  wb_ref, wo_ref, g2_ref, rw_ref, xo_ref, h2_ref, lg_ref, *, n_lat_tiles):
    latent = pl.program_id(1) < n_lat_tiles
    half = x_ref.shape[1] // 2
    for rows in (slice(0, half), slice(half, 2 * half)):
        y_hy = jnp.where(latent,
                         _hy_output(x0_ref[0, rows].astype(F32), cv_ref[0, rows].astype(F32),
                                    u_ref[0, rows].astype(F32), hb_ref[...]),
                         yc_ref[0, rows].astype(F32)).astype(BF16)
        merged = None
        for k, (y, gt_ref) in enumerate(((yd_ref[0, rows], gd_ref), (y_hy, gh_ref), (ys_ref[0, rows], gs_ref))):
            term = jax.nn.sigmoid(gt_ref[0, rows].astype(F32)) * _dot(y, wb_ref[k])
            merged = term if merged is None else merged + term
        y = _dot(merged.astype(BF16), wo_ref[...])
        x = x_ref[0, rows] + mod_ref[0, 2:3, :] * y
        xo_ref[0, rows] = x
        h2 = _rms(x) * g2_ref[...]
        h2 = h2 * (1.0 + mod_ref[0, 4:5, :]) + mod_ref[0, 3:4, :]
        h2_ref[0, rows] = h2.astype(h2_ref.dtype)
        lg_ref[0, :, rows] = _dot(h2, rw_ref[...], HIGHEST).T[:N_EXPERTS]


def _merge(xa, mod, y_diff, hy, hy_bias, y_swa, p_hg, w_branch, w_out, g2, rw_pad, n_lat_tiles):
    B, T, _ = xa.shape
    ctx_row = B
    conv, x0c, u, y_hy_c = hy
    tile = lambda c: pl.BlockSpec((1, TM, D), lambda b, i: (b, i, c))
    lat = pl.BlockSpec((1, TM, D), lambda b, i: (b, jnp.minimum(i, n_lat_tiles - 1), 0))
    return pl.pallas_call(
        functools.partial(_merge_kernel, n_lat_tiles=n_lat_tiles),
        grid=(B, T // TM),
        in_specs=[tile(0),
                  pl.BlockSpec((1, 6, D), lambda b, i: (jnp.where(i >= n_lat_tiles, ctx_row, b), 0, 0)),
                  tile(0), lat, lat, lat,
                  pl.BlockSpec((1, y_hy_c.shape[1], D), lambda b, i: (b, 0, 0)),
                  pl.BlockSpec((1, D), lambda b, i: (0, 0)),
                  tile(0), tile(3), tile(4), tile(5),
                  pl.BlockSpec((3, D, D), lambda b, i: (0, 0, 0)),
                  pl.BlockSpec((D, D), lambda b, i: (0, 0)),
                  pl.BlockSpec((1, D), lambda b, i: (0, 0)),
                  pl.BlockSpec((D, LANES), lambda b, i: (0, 0))],
        out_specs=[tile(0), tile(0), pl.BlockSpec((1, N_EXPERTS, TM), lambda b, i: (b, 0, i))],
        out_shape=[jax.ShapeDtypeStruct((B, T, D), F32), jax.ShapeDtypeStruct((B, T, D), BF16),
                   jax.ShapeDtypeStruct((B, N_EXPERTS, T), F32)],
        compiler_params=_params("arbitrary", "arbitrary"),
        name="merge",
    )(xa, mod, y_diff, conv, x0c, u, y_hy_c, hy_bias, y_swa, p_hg, p_hg, p_hg, w_branch, w_out, g2, rw_pad)


def _router_kernel(lg_ref, slot_ref, gate_ref, first_ref, *, segments):
    lg = lg_ref[0]
    e = jnp.exp(lg - jnp.max(lg, axis=0, keepdims=True))
    aff = e / jnp.sum(e, axis=0, keepdims=True)
    r = lax.broadcasted_iota(jnp.int32, (LANES, LANES), 0)
    c = lax.broadcasted_iota(jnp.int32, (LANES, LANES), 1)
    before = (r < c).astype(BF16)

    def prefix_count(mask, n):
        run = jnp.zeros((N_EXPERTS, 1), F32)
        outs = []
        for j in range(n // LANES):
            blk = mask[:, j * LANES:(j + 1) * LANES].astype(F32).astype(BF16)
            outs.append(_dot(blk, before) + run)
            run = run + jnp.sum(blk.astype(F32), axis=1, keepdims=True)
        return jnp.concatenate(outs, axis=1)

    for start, n, cap, slot_base in segments:
        a = aff[:, start:start + n]
        bits = pltpu.bitcast(a, jnp.int32)

        def search(_, bounds):
            lo, hi = bounds
            mid = lo + ((hi - lo + 1) >> 1)
            cnt = jnp.sum((bits >= mid).astype(F32), axis=1, keepdims=True)
            ok = cnt >= cap
            return jnp.where(ok, mid, lo), jnp.where(ok, hi, mid - 1)

        lo0 = jnp.zeros((N_EXPERTS, 1), jnp.int32)
        hi0 = jnp.full((N_EXPERTS, 1), 0x7F800000, jnp.int32)
        thr, _ = lax.fori_loop(0, 32, search, (lo0, hi0))
        above = bits > thr
        tie = bits == thr
        need = cap - jnp.sum(above.astype(F32), axis=1, keepdims=True)
        sel = above | (tie & (prefix_count(tie, n) < need))
        pos = prefix_count(sel, n)
        slot_ref[0, :, start:start + n] = jnp.where(sel, pos + slot_base, -1.0)
        gate_ref[0, :, start:start + n] = jnp.where(sel, a, 0.0)
        if start == 0:
            t_idx = lax.broadcasted_iota(jnp.int32, (n, LANES), 0)
            tile_start = lax.broadcasted_iota(jnp.int32, (n, LANES), 1) * TM
            before_tile = jnp.where(t_idx < tile_start, 1.0, 0.0).astype(BF16)
            first_ref[0] = _dot(jnp.where(sel, 1.0, 0.0).astype(BF16), before_tile)


def _router(logits_t, segments):
    B, E, T = logits_t.shape
    blk = pl.BlockSpec((1, E, T), lambda b: (b, 0, 0))
    return pl.pallas_call(
        functools.partial(_router_kernel, segments=segments),
        grid=(B,),
        in_specs=[blk],
        out_specs=[blk, blk, pl.BlockSpec((1, E, LANES), lambda b: (b, 0, 0))],
        out_shape=[jax.ShapeDtypeStruct((B, E, T), F32)] * 2 + [jax.ShapeDtypeStruct((B, E, LANES), F32)],
        compiler_params=_params("arbitrary"),
        name="router",
    )(logits_t)


def _slot_window(first, align, rows, n_slots):
    return pl.multiple_of(jnp.minimum(first // align * align, n_slots - rows), align)


def _moe_ffn_kernel(first_ref, busy_ref, slot_ref, gate_ref, h2_ref, w1_ref, w3_ref, w2_ref, o_ref, xg_scr, g_scr,
                    *, n_slots, n_tiles):
    pair = pl.program_id(0) * N_EXPERTS + pl.program_id(1)
    base = pair * n_tiles
    xg_scr[...] = jnp.zeros_like(xg_scr)
    g_scr[...] = jnp.zeros_like(g_scr)

    def gather(rows):
        row_id = lax.broadcasted_iota(jnp.int32, (rows, TM), 0)
        for c in range(n_tiles):
            tok = slice(c * TM, (c + 1) * TM)
            w0 = _slot_window(first_ref[base + c], 8, rows, n_slots)
            hit = slot_ref[0, 0, :, tok] == (row_id + w0).astype(F32)
            win = pl.ds(w0, rows)
            xg_scr[win, :] += _dot(jnp.where(hit, 1.0, 0.0).astype(BF16), h2_ref[0, tok, :])
            g_scr[win, :] += jnp.sum(jnp.where(hit, gate_ref[0, 0, :, tok], 0.0), axis=1, keepdims=True)

    sparse = busy_ref[pair] <= SPARSE_HITS

    @pl.when(sparse)
    def _():
        gather(min(SPARSE_HITS + 8, n_slots))

    @pl.when(jnp.logical_not(sparse))
    def _():
        gather(min(TM + 8, n_slots))

    xb = xg_scr[...].astype(BF16)
    a = _dot(xb, w1_ref[0, 0].astype(BF16))
    b = _dot(xb, w3_ref[0, 0].astype(BF16))
    hidden = (a * jax.nn.sigmoid(a) * b).astype(BF16)
    o_ref[0, 0] = (_dot(hidden, w2_ref[0, 0].astype(BF16)) * g_scr[...]).astype(o_ref.dtype)


def _moe_ffn(first, busy, slot, gate, h2, w1, w3, w2, l, n_slots):
    B, E, T = slot.shape
    row = pl.BlockSpec((1, 1, 1, T), lambda b, e: (b, e, 0, 0))
    wsp = pl.BlockSpec((1, 1, D, D), lambda b, e: (l, e, 0, 0))
    smem = pl.BlockSpec(memory_space=pltpu.SMEM)
    return pl.pallas_call(
        functools.partial(_moe_ffn_kernel, n_slots=n_slots, n_tiles=T // TM),
        grid=(B, E),
        in_specs=[smem, smem, row, row,
                  pl.BlockSpec((1, T, D), lambda b, e: (b, 0, 0), pipeline_mode=pl.Buffered(1)), wsp, wsp, wsp],
        out_specs=pl.BlockSpec((1, 1, n_slots, D), lambda b, e: (b, e, 0, 0)),
        out_shape=jax.ShapeDtypeStruct((B, E, n_slots, D), BF16),
        scratch_shapes=[pltpu.VMEM((n_slots, D), F32), pltpu.VMEM((n_slots, 1), F32)],
        compiler_params=_params("arbitrary", "arbitrary"),
        name="moe_ffn",
    )(first, busy, slot.reshape(B, E, 1, T), gate.reshape(B, E, 1, T), h2, w1, w3, w2)


def _moe_combine_kernel(first_ref, busy_ref, x_ref, mod_ref, slot_ref, y_ref, fg_ref, o_ref, acc_scr,
                        *, n_slots, n_tiles, n_lat_tiles, final):
    tm = x_ref.shape[1]

    def combine(rows):
        col_id = lax.broadcasted_iota(jnp.int32, (tm, rows), 1)
        acc = jnp.zeros((tm, D), F32)
        for e in range(N_EXPERTS):
            w0 = _slot_window(first_ref[(pl.program_id(0) * N_EXPERTS + e) * n_tiles + pl.program_id(1)], 16, rows,
                              n_slots)
            hit = slot_ref[0, :, e:e + 1] == (col_id + w0).astype(F32)
            acc = acc + _dot(jnp.where(hit, 1.0, 0.0).astype(BF16), y_ref[0, e, pl.ds(w0, rows), :])
        acc_scr[...] = acc

    sparse = busy_ref[pl.program_id(0) * n_tiles + pl.program_id(1)] <= SPARSE_HITS

    @pl.when(sparse)
    def _():
        combine(min(SPARSE_HITS + 16, n_slots))

    @pl.when(jnp.logical_not(sparse))
    def _():
        combine(min(tm + 16, n_slots))

    x = x_ref[0] + mod_ref[0, 5:6, :] * acc_scr[...]
    if final:
        @pl.when(pl.program_id(1) < n_lat_tiles)
        def _():
            o_ref[0] = _rms(x) * fg_ref[...]
    else:
        o_ref[0] = x


def _moe_combine(first, busy, xa, mod, slot_t, y, final_g, n_lat_tiles, n_slots, final):
    B, T, _ = xa.shape
    ctx_row = B
    tile = pl.BlockSpec((1, TM, D), lambda b, i: (b, i, 0))
    smem = pl.BlockSpec(memory_space=pltpu.SMEM)
    if final:
        out_spec = pl.BlockSpec((1, TM, D), lambda b, i: (b, jnp.minimum(i, n_lat_tiles - 1), 0))
        out_rows = n_lat_tiles * TM
    else:
        out_spec, out_rows = tile, T
    return pl.pallas_call(
        functools.partial(_moe_combine_kernel, n_slots=n_slots, n_tiles=T // TM, n_lat_tiles=n_lat_tiles,
                          final=final),
        grid=(B, T // TM),
        in_specs=[smem, smem, tile,
                  pl.BlockSpec((1, 6, D), lambda b, i: (jnp.where(i >= n_lat_tiles, ctx_row, b), 0, 0)),
                  pl.BlockSpec((1, TM, N_EXPERTS), lambda b, i: (b, i, 0)),
                  pl.BlockSpec((1, N_EXPERTS, n_slots, D), lambda b, i: (b, 0, 0, 0)),
                  pl.BlockSpec((1, D), lambda b, i: (0, 0))],
        out_specs=out_spec,
        out_shape=jax.ShapeDtypeStruct((B, out_rows, D), F32),
        scratch_shapes=[pltpu.VMEM((TM, D), F32)],
        compiler_params=_params("arbitrary", "arbitrary"),
        name="moe_combine",
    )(first, busy, xa, mod, slot_t, y, final_g)


def _swa_head_perm():
    return [kv * SWA_GROUP + g for g in range(SWA_GROUP) for kv in range(SWA_KV)]


def _split_w_in(w_in):
    depth = w_in.shape[0]
    o = [0, D, 2 * D, 3 * D, 6 * D, 7 * D, 7 * D + SWA_KV * HEAD, 7 * D + 2 * SWA_KV * HEAD, 10 * D + 2 * SWA_KV * HEAD]
    dq, dk, dv, hy, sq, sk, sv, gt = (w_in[:, :, o[k]:o[k + 1]] for k in range(8))
    scale = HEAD ** -0.5 * math.log2(math.e)
    sq = sq.reshape(depth, D, SWA_KV * SWA_GROUP, HEAD)[:, :, jnp.array(_swa_head_perm())].reshape(depth, D, D)
    w_attn = jnp.concatenate([dq * scale, dk, dv, sq * scale, sk, sv], axis=-1).astype(BF16)
    w_hg = jnp.concatenate([hy, gt], axis=-1).astype(BF16)
    return w_attn, w_hg


def kernel(x, c, ctx, c_ctx, w_mod, b_mod, norm1_g, norm2_g, w_in, diff_lambda, diff_subln_g, hy_conv_w,
           hy_conv_b, hy_ff_w1, hy_ff_b1, hy_ff_w2, hy_ff_b2, hy_ff_w3, hy_sin_freq, hy_bias, swa_sink,
           w_branch, w_out, router_w, moe_w1, moe_w3, moe_w2, final_g):
    B, L, _ = x.shape
    Lc = ctx.shape[1]
    depth = w_mod.shape[0]
    T = L + Lc
    assert Lc == TM and L % (4 * TM) == 0 and x.shape[2] == D
    n_lat_tiles = L // TM
    cap_l = CAPACITY_FACTOR * L // N_EXPERTS
    cap_c = CAPACITY_FACTOR * Lc // N_EXPERTS
    n_slots = cap_l + cap_c
    segments = ((0, L, cap_l, 0), (L, Lc, cap_c, cap_l))

    xa = jnp.concatenate([x, ctx], axis=1)
    mods = _mods(c, c_ctx, w_mod, b_mod)
    cos_t, sin_t = _rope_tables(L, Lc)
    tabs = _dft_tables(L, Lc)
    w_attn, w_hg = _split_w_in(w_in)
    rope_flags = tuple([True] * 16 + [False] * 8 + [True] * 8 + [True] * 2 + [False] * 2)
    perm = jnp.array(_swa_head_perm())
    wb = w_branch.at[:, 2].set(
        w_branch[:, 2].reshape(depth, SWA_KV * SWA_GROUP, HEAD, D)[:, perm].reshape(depth, D, D)).astype(BF16)
    wo = w_out.astype(BF16)
    w1, w3, w2 = moe_w1, moe_w3, moe_w2
    rw_pad = jnp.pad(router_w, ((0, 0), (0, 0), (0, LANES - N_EXPERTS)))
    w1p = jnp.pad(hy_ff_w1, ((0, 0), (0, LANES - HY_EMB), (0, 0)))

    for l in range(depth):
        lambda_init = 0.8 - 0.6 * math.exp(-0.3 * l)
        mod = mods[l]
        p_attn, p_hg = _norm_proj(xa, mod, norm1_g[l][None], w_attn[l], w_hg[l], cos_t, sin_t, rope_flags,
                                  n_lat_tiles)
        y_diff = _diff_attn(p_attn, diff_lambda, diff_subln_g, L, Lc, lambda_init, l)
        y_swa = _swa(p_attn, swa_sink, L, Lc, l)
        hp = dict(w1p=w1p[l], b1=hy_ff_b1[l][None], w2=hy_ff_w2[l], b2=hy_ff_b2[l][None],
                  fr=hy_sin_freq[l][None], w3=hy_ff_w3[l], conv_w=hy_conv_w[l], conv_b=hy_conv_b[l][None],
                  bias=hy_bias[l][None])
        hy = _hyena(p_hg, hp, tabs, L, Lc)
        xa, h2, logits_t = _merge(xa, mod, y_diff, hy, hp['bias'], y_swa, p_hg, wb[l], wo[l], norm2_g[l][None],
                                  rw_pad[l], n_lat_tiles)
        slot, gate, first = _router(logits_t, segments)
        first = jnp.concatenate([first[:, :, :n_lat_tiles], jnp.full((B, N_EXPERTS, 1), cap_l, F32)], axis=2)
        first = first.astype(jnp.int32)
        ends = jnp.concatenate([first[:, :, 1:n_lat_tiles], jnp.full((B, N_EXPERTS, 1), cap_l, jnp.int32),
                                jnp.full((B, N_EXPERTS, 1), n_slots, jnp.int32)], axis=2)
        hits = ends - first
        first = first.reshape(-1)
        y_e = _moe_ffn(first, jnp.max(hits, axis=2).reshape(-1), slot, gate, h2, w1, w3, w2, l, n_slots)
        xa = _moe_combine(first, jnp.max(hits, axis=1).reshape(-1), xa, mod, jnp.swapaxes(slot, 1, 2), y_e,
                          final_g[None], n_lat_tiles, n_slots, final=l == depth - 1)
    return xa
```

```python
import functools
import math

import jax
import jax.numpy as jnp
from jax import lax
from jax.experimental import pallas as pl
from jax.experimental.pallas import tpu as pltpu

F32 = jnp.float32
BF16 = jnp.bfloat16
HIGHEST = lax.Precision.HIGHEST

D = 1024
HEAD = 64
GRID_W = 64
ROPE_BASE = 10000.0
EPS = 1e-6
N_DIFF_HEADS = D // (2 * HEAD)
DIFF_V = 2 * HEAD
SWA_KV = 4
SWA_GROUP = 4
WINDOW = 128
N_EXPERTS = 16
CAPACITY_FACTOR = 2
HY_BANDS = 16
HY_EMB = 1 + 2 * HY_BANDS
HY_FF = 64
HY_FAST_DECAY = 0.3
HY_SLOW_DECAY = 1.5
HY_TARGET = 1e-2

LANES = 128
ROWS_F32 = 8
ROWS_BF16 = 16
TM = 256
DFT_N2 = 128
VMEM_LIMIT = 56 * 1024 * 1024

SPARSE_HITS = 64


def _params(*sem):
    return pltpu.CompilerParams(dimension_semantics=sem, vmem_limit_bytes=VMEM_LIMIT)


def _dot(a, b, precision=None):
    return jnp.dot(a, b, preferred_element_type=F32, precision=precision)


def _dot_nt(a, b, precision=None):
    return lax.dot_general(a, b, (((1,), (1,)), ((), ())), preferred_element_type=F32, precision=precision)


def _rms(x):
    return x * lax.rsqrt(jnp.mean(x * x, axis=-1, keepdims=True) + EPS)


def _mod_kernel(c_ref, w_ref, b_ref, o_ref):
    c = c_ref[...]
    sc = c * jax.nn.sigmoid(c)
    o_ref[0] = _dot(sc, w_ref[0], HIGHEST) + b_ref[0]


def _mods(c, c_ctx, w_mod, b_mod):
    B = c.shape[0]
    depth = w_mod.shape[0]
    rows = -(-(B + 1) // 8) * 8
    cc = jnp.zeros((rows, D), F32).at[:B].set(c).at[B].set(c_ctx)
    out = pl.pallas_call(
        _mod_kernel,
        grid=(depth, 6),
        in_specs=[pl.BlockSpec((rows, D), lambda l, j: (0, 0)),
                  pl.BlockSpec((1, D, D), lambda l, j: (l, 0, j)),
                  pl.BlockSpec((1, 1, D), lambda l, j: (l, 0, j))],
        out_specs=pl.BlockSpec((1, rows, D), lambda l, j: (l, 0, j)),
        out_shape=jax.ShapeDtypeStruct((depth, rows, 6 * D), F32),
        compiler_params=_params("arbitrary", "arbitrary"),
        name="mods",
    )(cc, w_mod, b_mod.reshape(depth, 1, 6 * D))
    return out.reshape(depth, rows, 6, D)


def _norm_proj_kernel(x_ref, mod_ref, g_ref, wa_ref, wh_ref, cos_ref, sin_ref, oa_ref, oh_ref, *, rope_flags):
    x = x_ref[0]
    h = _rms(x) * g_ref[...]
    h = h * (1.0 + mod_ref[0, 1:2, :]) + mod_ref[0, 0:1, :]
    hb = h.astype(BF16)
    sub_per_chunk = 4
    cos = cos_ref[...]
    sin = sin_ref[...]
    lane = lax.broadcasted_iota(jnp.int32, cos.shape, 1)
    first_half = (lane % (HEAD // 2)) < (HEAD // 4)
    for w_ref, o_ref, flags in ((wa_ref, oa_ref, rope_flags), (wh_ref, oh_ref, (False,) * (wh_ref.shape[1] // LANES))):
        n_sub = len(flags)
        for c0 in range(0, n_sub, sub_per_chunk):
            c1 = min(c0 + sub_per_chunk, n_sub)
            acc = _dot(hb, w_ref[:, c0 * LANES:c1 * LANES])
            for s in range(c0, c1):
                a = acc[:, (s - c0) * LANES:(s - c0 + 1) * LANES]
                if flags[s]:
                    partner = jnp.where(first_half, pltpu.roll(a, LANES - HEAD // 4, 1), pltpu.roll(a, HEAD // 4, 1))
                    a = a * cos + partner * sin
                o_ref[0, :, s * LANES:(s + 1) * LANES] = a.astype(o_ref.dtype)


def _norm_proj(xa, mod, g, w_attn, w_hg, cos_t, sin_t, rope_flags, n_lat_tiles):
    B, T, _ = xa.shape
    na, nh = w_attn.shape[1], w_hg.shape[1]
    ctx_row = B
    resident = lambda n: pl.BlockSpec((D, n), lambda b, i: (0, 0), pipeline_mode=pl.Buffered(1))
    return pl.pallas_call(
        functools.partial(_norm_proj_kernel, rope_flags=rope_flags),
        grid=(B, T // TM),
        in_specs=[pl.BlockSpec((1, TM, D), lambda b, i: (b, i, 0)),
                  pl.BlockSpec((1, 6, D), lambda b, i: (jnp.where(i >= n_lat_tiles, ctx_row, b), 0, 0)),
                  pl.BlockSpec((1, D), lambda b, i: (0, 0)),
                  resident(na), resident(nh),
                  pl.BlockSpec((TM, LANES), lambda b, i: (i, 0)),
                  pl.BlockSpec((TM, LANES), lambda b, i: (i, 0))],
        out_specs=[pl.BlockSpec((1, TM, na), lambda b, i: (b, i, 0)),
                   pl.BlockSpec((1, TM, nh), lambda b, i: (b, i, 0))],
        out_shape=[jax.ShapeDtypeStruct((B, T, na), BF16), jax.ShapeDtypeStruct((B, T, nh), BF16)],
        compiler_params=_params("arbitrary", "arbitrary"),
        name="proj",
    )(xa, mod, g, w_attn, w_hg, cos_t, sin_t)


def _rope_tables(L, Lc):
    t = jnp.arange(L)
    pos = jnp.stack([(t // GRID_W).astype(F32), (t % GRID_W).astype(F32)], axis=1)
    n_freq = HEAD // 4
    inv = ROPE_BASE ** (-jnp.arange(n_freq, dtype=F32) / n_freq)
    ang = pos[:, :, None] * inv
    cos = jnp.cos(ang)
    sin = jnp.sin(ang)
    cos_h = jnp.concatenate([cos, cos], axis=-1).reshape(L, HEAD)
    sin_h = jnp.concatenate([-sin, sin], axis=-1).reshape(L, HEAD)
    cos_t = jnp.concatenate([jnp.tile(cos_h, (1, LANES // HEAD)), jnp.ones((Lc, LANES), F32)], axis=0)
    sin_t = jnp.concatenate([jnp.tile(sin_h, (1, LANES // HEAD)), jnp.zeros((Lc, LANES), F32)], axis=0)
    return cos_t, sin_t


def _lane_block_reduce(x, op):
    out = x[:, :LANES]
    for j in range(1, x.shape[1] // LANES):
        out = op(out, x[:, j * LANES:(j + 1) * LANES])
    return out


def _diff_attn_kernel(lam_ref, q_ref, k_ref, v_ref, g_ref, o_ref, s_lat, s_ctx, m_run, qq_scr, l_scr, acc_scr,
                      *, L, Lc, tg, lambda_init):
    tq = TM
    n_groups = (L + Lc) // tg
    n_lat = L // tq
    n_tiles = (L + Lc) // tq
    rows = (2 * tq, LANES)
    lowest = jnp.full(rows, -jnp.inf, F32)
    nothing = jnp.zeros(rows, F32)
    lp = lam_ref[0]
    lam = (jnp.exp(jnp.sum(lp[0:1] * lp[1:2], axis=-1, keepdims=True))
           - jnp.exp(jnp.sum(lp[2:3] * lp[3:4], axis=-1, keepdims=True)) + lambda_init)

    def stage(t, _):
        cur = t % 2
        prv = 1 - cur
        score_any = t < n_tiles
        score_lat = t < n_lat
        weigh_any = t >= 1
        weigh_lat = jnp.logical_and(weigh_any, t <= n_lat)

        @pl.when(score_any)
        def _():
            q = q_ref[0, pl.ds(pl.multiple_of(t * tq, tq), tq), :]
            lane = lax.broadcasted_iota(jnp.int32, q.shape, 1)
            zero = jnp.zeros_like(q)
            qq_scr[...] = jnp.concatenate([jnp.where(lane < HEAD, q, zero), jnp.where(lane >= HEAD, q, zero)],
                                          axis=0)

        @pl.when(jnp.logical_and(score_any, jnp.logical_not(score_lat)))
        def _():
            s_c = _dot_nt(qq_scr[...], k_ref[0, L:L + Lc, :])
            s_ctx[...] = s_c
            m_run[cur] = _lane_block_reduce(s_c, jnp.maximum)

        def row_max():
            return jnp.max(m_run[prv], axis=-1, keepdims=True)

        @pl.when(jnp.logical_and(weigh_any, jnp.logical_not(weigh_lat)))
        def _():
            p = jnp.exp2(s_ctx[...] - row_max())
            l_scr[...] = _lane_block_reduce(p, jnp.add)
            acc_scr[...] = _dot(p.astype(BF16), v_ref[0, L:L + Lc, :])

        def scores(gi, mrun):
            start = pl.multiple_of(gi * tg, LANES)
            s = _dot_nt(qq_scr[...], k_ref[0, pl.ds(start, tg), :])
            s_lat[cur, gi] = s
            return jnp.maximum(mrun, _lane_block_reduce(s, jnp.maximum))

        def weighted(gi, lrun, acc, m):
            start = pl.multiple_of(gi * tg, LANES)
            p = jnp.exp2(s_lat[prv, gi] - m)
            return (lrun + _lane_block_reduce(p, jnp.add),
                    acc + _dot(p.astype(BF16), v_ref[0, pl.ds(start, tg), :]))

        @pl.when(jnp.logical_and(score_lat, weigh_lat))
        def _():
            m = row_max()

            def both(gi, carry):
                mrun, lrun, acc = carry
                lrun, acc = weighted(gi, lrun, acc, m)
                return scores(gi, mrun), lrun, acc

            m_run[cur], l_scr[...], acc_scr[...] = lax.fori_loop(0, n_groups, both, (lowest, nothing, nothing))

        @pl.when(jnp.logical_and(score_lat, jnp.logical_not(weigh_lat)))
        def _():
            m_run[cur] = lax.fori_loop(0, n_groups, scores, lowest)

        @pl.when(jnp.logical_and(jnp.logical_not(score_lat), weigh_lat))
        def _():
            m = row_max()
            l_scr[...], acc_scr[...] = lax.fori_loop(0, n_groups, lambda gi, c: weighted(gi, c[0], c[1], m),
                                                      (nothing, nothing))

        @pl.when(weigh_any)
        def _():
            o = acc_scr[...] / jnp.sum(l_scr[...], axis=-1, keepdims=True)
            o = _rms(o[:tq] - lam * o[tq:])
            out_rows = pl.ds(pl.multiple_of((t - 1) * tq, tq), tq)
            o_ref[0, out_rows, :] = (o * g_ref[0] * (1.0 - lambda_init)).astype(o_ref.dtype)

        return 0

    lax.fori_loop(0, n_tiles + 1, stage, 0)


def _diff_attn(p_attn, diff_lambda, subln_g, L, Lc, lambda_init, l):
    B, T, _ = p_attn.shape
    tg = T // 2
    nh = N_DIFF_HEADS
    head = lambda c0: pl.BlockSpec((1, T, DIFF_V), lambda b, h: (b, 0, c0 + h))
    return pl.pallas_call(
        functools.partial(_diff_attn_kernel, L=L, Lc=Lc, tg=tg, lambda_init=lambda_init),
        grid=(B, nh),
        in_specs=[pl.BlockSpec((1, 4, HEAD), lambda b, h: (l, 0, 0)),
                  head(0), head(nh), head(2 * nh),
                  pl.BlockSpec((1, 1, DIFF_V), lambda b, h: (l, 0, 0))],
        out_specs=head(0),
        out_shape=jax.ShapeDtypeStruct((B, T, D), BF16),
        scratch_shapes=[pltpu.VMEM((2, T // tg, 2 * TM, tg), F32), pltpu.VMEM((2 * TM, Lc), F32),
                        pltpu.VMEM((2, 2 * TM, LANES), F32), pltpu.VMEM((2 * TM, DIFF_V), BF16),
                        pltpu.VMEM((2 * TM, LANES), F32), pltpu.VMEM((2 * TM, DIFF_V), F32)],
        compiler_params=_params("arbitrary", "arbitrary"),
        name="diff_attn",
    )(diff_lambda, p_attn, p_attn, p_attn, subln_g.reshape(-1, 1, DIFF_V))


def _swa_kernel(sink_ref, q_ref, k_ref, v_ref, o_ref, *, L, Lc, l):
    i = pl.program_id(1)
    tq = q_ref.shape[1]
    kvw = SWA_KV * HEAD
    win = tq + 2 * WINDOW
    lane_head = lax.broadcasted_iota(jnp.int32, (tq, kvw), 1) // HEAD
    k_ctx = k_ref[0, L:L + Lc, :]
    v_ctx = v_ref[0, L:L + Lc, :]

    def run(latent):
        if latent:
            start = pl.multiple_of(jnp.clip(i * tq - WINDOW, 0, L - win), WINDOW)
            k_win = k_ref[0, pl.ds(start, win), :]
            v_win = v_ref[0, pl.ds(start, win), :]
            q_pos = i * tq + lax.broadcasted_iota(jnp.int32, (tq, win), 0)
            k_pos = start + lax.broadcasted_iota(jnp.int32, (tq, win), 1)
            band_bias = jnp.where(jnp.abs(k_pos - q_pos) <= WINDOW, 0.0, -jnp.inf)
        for g in range(SWA_GROUP):
            qg = q_ref[0, :, g * kvw:(g + 1) * kvw]
            acc = jnp.zeros((tq, kvw), F32)
            for kv in range(SWA_KV):
                qe = jnp.where(lane_head == kv, qg, jnp.zeros_like(qg))
                snk = sink_ref[l, kv * SWA_GROUP + g] * math.log2(math.e)
                s_c = _dot_nt(qe, k_ctx)
                m = jnp.maximum(jnp.max(s_c, axis=-1, keepdims=True), snk)
                if latent:
                    s_w = _dot_nt(qe, k_win) + band_bias
                    m = jnp.maximum(m, jnp.max(s_w, axis=-1, keepdims=True))
                p_c = jnp.exp2(s_c - m)
                den = jnp.sum(p_c, axis=-1, keepdims=True) + jnp.exp2(snk - m)
                o = _dot(p_c.astype(BF16), v_ctx)
                if latent:
                    p_w = jnp.exp2(s_w - m)
                    den = den + jnp.sum(p_w, axis=-1, keepdims=True)
                    o = o + _dot(p_w.astype(BF16), v_win)
                acc = acc + jnp.where(lane_head == kv, o / den, 0.0)
            o_ref[0, :, g * kvw:(g + 1) * kvw] = acc.astype(o_ref.dtype)

    @pl.when(i < L // tq)
    def _():
        run(True)

    @pl.when(i >= L // tq)
    def _():
        run(False)


def _swa(p_attn, sink, L, Lc, l):
    B, T, _ = p_attn.shape
    kvw = SWA_KV * HEAD
    return pl.pallas_call(
        functools.partial(_swa_kernel, L=L, Lc=Lc, l=l),
        grid=(B, T // TM),
        in_specs=[pl.BlockSpec(memory_space=pltpu.SMEM),
                  pl.BlockSpec((1, TM, D), lambda b, i: (b, i, 3)),
                  pl.BlockSpec((1, T, kvw), lambda b, i: (b, 0, 4 * D // kvw)),
                  pl.BlockSpec((1, T, kvw), lambda b, i: (b, 0, 4 * D // kvw + 1))],
        out_specs=pl.BlockSpec((1, TM, D), lambda b, i: (b, i, 0)),
        out_shape=jax.ShapeDtypeStruct((B, T, D), BF16),
        compiler_params=_params("arbitrary", "arbitrary"),
        name="swa",
    )(sink, p_attn, p_attn, p_attn)


def _hy_filter_kernel(emb_ref, w1_ref, b1_ref, w2_ref, b2_ref, fr_ref, w3f_ref, w3b_ref, dl_ref, o_ref, *, L):
    rows = TM
    fr = fr_ref[...]

    def raw(i, total):
        r0 = pl.multiple_of(i * rows, rows)
        e = emb_ref[pl.ds(r0, rows), :]
        z = jnp.sin(fr * (_dot(e, w1_ref[...], HIGHEST) + b1_ref[...]))
        z = jnp.sin(fr * (_dot(z, w2_ref[...], HIGHEST) + b2_ref[...]))
        row = r0 + lax.broadcasted_iota(jnp.int32, (rows, 1), 0)
        h = jnp.where(row < L, _dot(z, w3f_ref[...], HIGHEST), _dot(z, w3b_ref[...], HIGHEST))
        h = h * jnp.exp(-e[:, 0:1] * dl_ref[...])
        o_ref[pl.ds(r0, rows), :] = h
        return total + jnp.sum(jnp.abs(h), axis=0, keepdims=True)

    n_chunks = emb_ref.shape[0] // rows
    norm = lax.fori_loop(0, n_chunks, raw, jnp.zeros((1, o_ref.shape[1]), F32)) + EPS

    def scale(i, _):
        r0 = pl.multiple_of(i * rows, rows)
        row = r0 + lax.broadcasted_iota(jnp.int32, (rows, 1), 0)
        o_ref[pl.ds(r0, rows), :] = jnp.where(row == L, 0.0, o_ref[pl.ds(r0, rows), :] / norm)
        return 0

    lax.fori_loop(0, n_chunks, scale, 0)


def _hy_embedding(L):
    n = jnp.arange(2 * L)
    lag = jnp.where(n < L, n, jnp.where(n == L, 0, 2 * L - n))
    t = jnp.linspace(0.0, 1.0, L, dtype=F32)[lag][:, None]
    w = (2.0 * math.pi * jnp.arange(L, dtype=F32) / L)[lag][:, None]
    f = jnp.linspace(1e-4, HY_BANDS - 1, HY_BANDS, dtype=F32)[None, :]
    emb = jnp.concatenate([t, jnp.cos(f * w), -jnp.sin(f * w)], axis=-1)
    return jnp.pad(emb, ((0, 0), (0, LANES - HY_EMB)))


def _hy_filter(L, w1p, b1, w2, b2, fr, w3, deltas):
    ct = 512
    emb = _hy_embedding(L)
    full = lambda shape: pl.BlockSpec(shape, lambda j: (0, 0))
    return pl.pallas_call(
        functools.partial(_hy_filter_kernel, L=L),
        grid=(D // ct,),
        in_specs=[full((2 * L, LANES)), full((LANES, HY_FF)), full((1, HY_FF)), full((HY_FF, HY_FF)),
                  full((1, HY_FF)), full((1, HY_FF)),
                  pl.BlockSpec((HY_FF, ct), lambda j: (0, j)),
                  pl.BlockSpec((HY_FF, ct), lambda j: (0, D // ct + j)),
                  pl.BlockSpec((1, ct), lambda j: (0, j))],
        out_specs=pl.BlockSpec((2 * L, ct), lambda j: (0, j)),
        out_shape=jax.ShapeDtypeStruct((2 * L, D), F32),
        compiler_params=_params("arbitrary"),
        name="hy_filter",
    )(emb, w1p, b1, w2, b2, fr, w3, w3, deltas)


def _short_conv(x, w, b):
    n = x.shape[0]
    row = lax.broadcasted_iota(jnp.int32, x.shape, 0)
    prev = jnp.where(row == 0, 0.0, pltpu.roll(x, 1, 0))
    nxt = jnp.where(row == n - 1, 0.0, pltpu.roll(x, n - 1, 0))
    return prev * w[0:1] + x * w[1:2] + nxt * w[2:3] + b


def _hy_pre_kernel(x0_ref, x1_ref, v_ref, w0_ref, w1_ref, w2_ref, b0_ref, b1_ref, b2_ref, x0c_ref, u_ref):
    x0c_ref[0] = _short_conv(x0_ref[0].astype(F32), w0_ref[...], b0_ref[...]).astype(x0c_ref.dtype)
    u = (_short_conv(x1_ref[0].astype(F32), w1_ref[...], b1_ref[...])
         * _short_conv(v_ref[0].astype(F32), w2_ref[...], b2_ref[...]))
    u_ref[0] = u.astype(u_ref.dtype)


def _hy_pre(p_hg, conv_w, conv_b, L):
    B = p_hg.shape[0]
    ct = LANES
    nb = D // ct
    col = lambda k: pl.BlockSpec((1, L, ct), lambda b, j: (b, 0, k * nb + j))
    wsp = lambda k: pl.BlockSpec((3, ct), lambda b, j: (0, k * nb + j))
    bsp = lambda k: pl.BlockSpec((1, ct), lambda b, j: (0, k * nb + j))
    out = pl.BlockSpec((1, L, ct), lambda b, j: (b, 0, j))
    return pl.pallas_call(
        _hy_pre_kernel,
        grid=(B, nb),
        in_specs=[col(0), col(1), col(2), wsp(0), wsp(1), wsp(2), bsp(0), bsp(1), bsp(2)],
        out_specs=[out, out],
        out_shape=[jax.ShapeDtypeStruct((B, L, D), BF16)] * 2,
        compiler_params=_params("arbitrary", "arbitrary"),
        name="hy_pre",
    )(p_hg, p_hg, p_hg, conv_w, conv_w, conv_w, conv_b, conv_b, conv_b)


def _lmat_kernel(m_ref, x_ref, o_ref, *, precision):
    x = x_ref[0]
    if precision is None:
        x = x.astype(BF16)
    o_ref[0] = _dot(m_ref[...], x, precision).astype(o_ref.dtype)


def _lmat(mat, x, out_dtype, precision, wc, name):
    Bn, K, W = x.shape
    M = mat.shape[0]
    return pl.pallas_call(
        functools.partial(_lmat_kernel, precision=precision),
        grid=(Bn, W // wc),
        in_specs=[pl.BlockSpec((M, K), lambda b, j: (0, 0)),
                  pl.BlockSpec((1, K, wc), lambda b, j: (b, 0, j))],
        out_specs=pl.BlockSpec((1, M, wc), lambda b, j: (b, 0, j)),
        out_shape=jax.ShapeDtypeStruct((Bn, M, W), out_dtype),
        compiler_params=_params("arbitrary", "arbitrary"),
        name=name,
    )(mat, x)


def _hy_s1_kernel(m_ref, x_ref, o_ref, *, precision):
    k, nb = x_ref.shape[1], x_ref.shape[2]
    res = _dot(m_ref[...], x_ref[0].reshape(k, nb * D), precision)
    o_ref[0] = res.reshape(2, res.shape[0] // 2, nb, D).astype(o_ref.dtype)


def _hy_s1(mat, x, nb, out_dtype=BF16, precision=None, name="hy_s1"):
    Bn, K, n2, _ = x.shape
    M = mat.shape[0]
    return pl.pallas_call(
        functools.partial(_hy_s1_kernel, precision=precision),
        grid=(Bn, n2 // nb),
        in_specs=[pl.BlockSpec((M, K), lambda b, j: (0, 0)),
                  pl.BlockSpec((1, K, nb, D), lambda b, j: (b, 0, j, 0))],
        out_specs=pl.BlockSpec((1, 2, M // 2, nb, D), lambda b, j: (b, 0, 0, j, 0)),
        out_shape=jax.ShapeDtypeStruct((Bn, 2, M // 2, n2, D), out_dtype),
        compiler_params=_params("arbitrary", "arbitrary"),
        name=name,
    )(mat, x)


def _hy_s3_kernel(g_ref, z_ref, o_ref):
    two, n1, nb, _ = z_ref.shape[1:]
    conv = _dot(g_ref[...], z_ref[0].reshape(two * n1, nb * D))
    o_ref[0] = conv.reshape(conv.shape[0], nb, D).astype(o_ref.dtype)


def _hy_s3(g_mat, z, nb):
    Bn, _, n1, n2, _ = z.shape
    R = g_mat.shape[0]
    return pl.pallas_call(
        _hy_s3_kernel,
        grid=(Bn, n2 // nb),
        in_specs=[pl.BlockSpec((R, 2 * n1), lambda b, j: (0, 0)),
                  pl.BlockSpec((1, 2, n1, nb, D), lambda b, j: (b, 0, 0, j, 0))],
        out_specs=pl.BlockSpec((1, R, nb, D), lambda b, j: (b, 0, j, 0)),
        out_shape=jax.ShapeDtypeStruct((Bn, R, n2, D), BF16),
        compiler_params=_params("arbitrary", "arbitrary"),
        name="hy_s3",
    )(g_mat, z)


def _cmul(xr, xi, kr, ki):
    return xr * kr - xi * ki, xr * ki + xi * kr


def _hy_spec_kernel(a_ref, m_ref, o_ref):
    a = jnp.concatenate([a_ref[0, 0], a_ref[1, 0]], axis=0)
    o_ref[0] = _dot(m_ref[0], a, HIGHEST)


def _hy_spectrum(a, m_tab):
    n1 = a.shape[1]
    n2 = DFT_N2
    return pl.pallas_call(
        _hy_spec_kernel,
        grid=(n1,),
        in_specs=[pl.BlockSpec((2, 1, n2, D), lambda k: (0, k, 0, 0)),
                  pl.BlockSpec((1, 2 * n2, 2 * n2), lambda k: (k, 0, 0))],
        out_specs=pl.BlockSpec((1, 2 * n2, D), lambda k: (k, 0, 0)),
        out_shape=jax.ShapeDtypeStruct((n1, 2 * n2, D), F32),
        compiler_params=_params("arbitrary"),
        name="hy_spectrum",
    )(a, m_tab)


def _hy_mid_kernel(a_ref, m_ref, mt_ref, kf_ref, o_ref):
    n2 = DFT_N2
    kr = kf_ref[0, :n2]
    ki = kf_ref[0, n2:]

    samples = range(a_ref.shape[0])
    a = jnp.concatenate([jnp.concatenate([a_ref[b, 0, 0], a_ref[b, 1, 0]], axis=0) for b in samples], axis=1)
    x = _dot(m_ref[0], a)
    ys = []
    for b in samples:
        yr, yi = _cmul(x[:n2, b * D:(b + 1) * D], x[n2:, b * D:(b + 1) * D], kr, ki)
        ys.append(jnp.concatenate([yr, yi], axis=0).astype(BF16))
    z = _dot(mt_ref[0], jnp.concatenate(ys, axis=1)).astype(o_ref.dtype)
    for b in samples:
        o_ref[b, 0, 0] = z[:n2, b * D:(b + 1) * D]
        o_ref[b, 1, 0] = z[n2:, b * D:(b + 1) * D]


def _hy_mid(a, m_bf, mt_bf, kf):
    B, _, n1, n2, _ = a.shape
    blk = pl.BlockSpec((B, 2, 1, n2, D), lambda k: (0, 0, k, 0, 0))
    tab = pl.BlockSpec((1, 2 * n2, 2 * n2), lambda k: (k, 0, 0))
    return pl.pallas_call(
        _hy_mid_kernel,
        grid=(n1,),
        in_specs=[blk, tab, tab, pl.BlockSpec((1, 2 * n2, D), lambda k: (k, 0, 0))],
        out_specs=blk,
        out_shape=jax.ShapeDtypeStruct(a.shape, BF16),
        compiler_params=_params("arbitrary"),
        name="hy_mid",
    )(a, m_bf, mt_bf, kf)


def _hy_output(x0c, conv, u, bias):
    return x0c * (conv + u * bias)


def _hy_ctx_kernel(x0_ref, x1_ref, v_ref, w0_ref, w1_ref, w2_ref, b0_ref, b1_ref, b2_ref,
                   f_ref, kf_ref, g_ref, bias_ref, o_ref):
    x0 = _short_conv(x0_ref[0].astype(F32), w0_ref[...], b0_ref[...])
    u = (_short_conv(x1_ref[0].astype(F32), w1_ref[...], b1_ref[...])
         * _short_conv(v_ref[0].astype(F32), w2_ref[...], b2_ref[...]))
    x = _dot(f_ref[...], u.astype(BF16))
    nf = x.shape[0] // 2
    yr, yi = _cmul(x[:nf], x[nf:], kf_ref[:nf], kf_ref[nf:])
    conv = _dot(g_ref[...], jnp.concatenate([yr, yi], axis=0).astype(BF16))
    o_ref[0] = _hy_output(x0, conv, u, bias_ref[...]).astype(o_ref.dtype)


def _hy_ctx(p_hg, conv_w, conv_b, f_c, kf_c, g_c, bias, L, Lc):
    B = p_hg.shape[0]
    ct = 256
    nb = D // ct
    rb = L // Lc
    col = lambda k: pl.BlockSpec((1, Lc, ct), lambda b, j: (b, rb, k * nb + j))
    wsp = lambda k: pl.BlockSpec((3, ct), lambda b, j: (0, k * nb + j))
    bsp = lambda k: pl.BlockSpec((1, ct), lambda b, j: (0, k * nb + j))
    return pl.pallas_call(
        _hy_ctx_kernel,
        grid=(B, nb),
        in_specs=[col(0), col(1), col(2), wsp(0), wsp(1), wsp(2), bsp(0), bsp(1), bsp(2),
                  pl.BlockSpec(f_c.shape, lambda b, j: (0, 0)),
                  pl.BlockSpec((kf_c.shape[0], ct), lambda b, j: (0, j)),
                  pl.BlockSpec(g_c.shape, lambda b, j: (0, 0)),
                  pl.BlockSpec((1, ct), lambda b, j: (0, j))],
        out_specs=pl.BlockSpec((1, Lc, ct), lambda b, j: (b, 0, j)),
        out_shape=jax.ShapeDtypeStruct((B, Lc, D), BF16),
        compiler_params=_params("arbitrary", "arbitrary"),
        name="hy_ctx",
    )(p_hg, p_hg, p_hg, conv_w, conv_w, conv_w, conv_b, conv_b, conv_b, f_c, kf_c, g_c, bias)


def _dft_tables(L, Lc):
    n = 2 * L
    n2 = DFT_N2
    n1 = n // n2
    two_pi = 2.0 * math.pi

    def cs(idx, period):
        ang = (idx % period).astype(F32) * (two_pi / period)
        return jnp.cos(ang), jnp.sin(ang)

    k1 = jnp.arange(n1)[:, None]
    c, s = cs(k1 * jnp.arange(n1)[None, :], n1)
    f1_full = jnp.concatenate([c, -s], axis=0)
    f1_half = f1_full[:, :n1 // 2]
    kk1 = jnp.arange(n1)[:, None, None]
    kk2 = jnp.arange(n2)[None, :, None]
    nn2 = jnp.arange(n2)[None, None, :]
    c, s = cs(nn2 * kk1 + n1 * nn2 * kk2, n)
    ci = -s
    m_tab = jnp.concatenate([jnp.concatenate([c, -ci], axis=2), jnp.concatenate([ci, c], axis=2)], axis=1)
    c, s = cs(jnp.arange(n1 // 2)[:, None] * jnp.arange(n1)[None, :], n1)
    g_mat = jnp.concatenate([c, -s], axis=1) / n
    nc = 2 * Lc
    kc = jnp.arange(nc)[:, None]
    c, s = cs(kc * jnp.arange(nc)[None, :], nc)
    fc_full = jnp.concatenate([c, -s], axis=0)
    c, s = cs(jnp.arange(Lc)[:, None] * jnp.arange(nc)[None, :], nc)
    g_c = jnp.concatenate([c, -s], axis=1) / nc
    return dict(f1_full=f1_full, f1_half=f1_half.astype(BF16), m_tab=m_tab, m_bf=m_tab.astype(BF16),
                mt_bf=jnp.swapaxes(m_tab, 1, 2).astype(BF16), g_mat=g_mat.astype(BF16),
                fc_full=fc_full, fc_half=fc_full[:, :Lc].astype(BF16), g_c=g_c.astype(BF16))


def _hyena(p_hg, hp, tabs, L, Lc):
    B = p_hg.shape[0]
    n2 = DFT_N2
    n1 = 2 * L // n2
    wc = min(16 * D, n2 * D)
    deltas = jnp.abs(jnp.linspace(math.log(HY_TARGET) / HY_FAST_DECAY, math.log(HY_TARGET) / HY_SLOW_DECAY,
                                  D, dtype=F32))[None, :]
    filt = functools.partial(_hy_filter, w1p=hp['w1p'], b1=hp['b1'], w2=hp['w2'], b2=hp['b2'], fr=hp['fr'],
                             w3=hp['w3'], deltas=deltas)
    k_lat = filt(L)
    a_f = _hy_s1(tabs['f1_full'], k_lat.reshape(1, n1, n2, D), wc // D, F32, HIGHEST, "hy_filt_s1")
    kf = _hy_spectrum(a_f[0], tabs['m_tab'])
    k_ctx = filt(Lc)
    kf_c = _lmat(tabs['fc_full'], k_ctx[None], F32, HIGHEST, D, "hy_filt_ctx")[0]
    x0c, u = _hy_pre(p_hg, hp['conv_w'], hp['conv_b'], L)
    a = _hy_s1(tabs['f1_half'], u.reshape(B, n1 // 2, n2, D), wc // D)
    z = _hy_mid(a, tabs['m_bf'], tabs['mt_bf'], kf)
    conv = _hy_s3(tabs['g_mat'], z, wc // D).reshape(B, L, D)
    y_c = _hy_ctx(p_hg, hp['conv_w'], hp['conv_b'], tabs['fc_half'], kf_c, tabs['g_c'], hp['bias'], L, Lc)
    return conv, x0c, u, y_c


def _merge_kernel(x_ref, mod_ref, yd_ref, cv_ref, x0_ref, u_ref, yc_ref, hb_ref, ys_ref, gd_ref, gh_ref, gs_ref,
                  wb_ref, wo_ref, g2_ref, rw_ref, xo_ref, h2_ref, lg_ref, *, n_lat_tiles):
    latent = pl.program_id(1) < n_lat_tiles
    half = x_ref.shape[1] // 2
    for rows in (slice(0, half), slice(half, 2 * half)):
        y_hy = jnp.where(latent,
                         _hy_output(x0_ref[0, rows].astype(F32), cv_ref[0, rows].astype(F32),
                                    u_ref[0, rows].astype(F32), hb_ref[...]),
                         yc_ref[0, rows].astype(F32)).astype(BF16)
        merged = None
        for k, (y, gt_ref) in enumerate(((yd_ref[0, rows], gd_ref), (y_hy, gh_ref), (ys_ref[0, rows], gs_ref))):
            term = jax.nn.sigmoid(gt_ref[0, rows].astype(F32)) * _dot(y, wb_ref[k])
            merged = term if merged is None else merged + term
        y = _dot(merged.astype(BF16), wo_ref[...])
        x = x_ref[0, rows] + mod_ref[0, 2:3, :] * y
        xo_ref[0, rows] = x
        h2 = _rms(x) * g2_ref[...]
        h2 = h2 * (1.0 + mod_ref[0, 4:5, :]) + mod_ref[0, 3:4, :]
        h2_ref[0, rows] = h2.astype(h2_ref.dtype)
        lg_ref[0, :, rows] = _dot(h2, rw_ref[...], HIGHEST).T[:N_EXPERTS]


def _merge(xa, mod, y_diff, hy, hy_bias, y_swa, p_hg, w_branch, w_out, g2, rw_pad, n_lat_tiles):
    B, T, _ = xa.shape
    ctx_row = B
    conv, x0c, u, y_hy_c = hy
    tile = lambda c: pl.BlockSpec((1, TM, D), lambda b, i: (b, i, c))
    lat = pl.BlockSpec((1, TM, D), lambda b, i: (b, jnp.minimum(i, n_lat_tiles - 1), 0))
    return pl.pallas_call(
        functools.partial(_merge_kernel, n_lat_tiles=n_lat_tiles),
        grid=(B, T // TM),
        in_specs=[tile(0),
                  pl.BlockSpec((1, 6, D), lambda b, i: (jnp.where(i >= n_lat_tiles, ctx_row, b), 0, 0)),
                  tile(0), lat, lat, lat,
                  pl.BlockSpec((1, y_hy_c.shape[1], D), lambda b, i: (b, 0, 0)),
                  pl.BlockSpec((1, D), lambda b, i: (0, 0)),
                  tile(0), tile(3), tile(4), tile(5),
                  pl.BlockSpec((3, D, D), lambda b, i: (0, 0, 0)),
                  pl.BlockSpec((D, D), lambda b, i: (0, 0)),
                  pl.BlockSpec((1, D), lambda b, i: (0, 0)),
                  pl.BlockSpec((D, LANES), lambda b, i: (0, 0))],
        out_specs=[tile(0), tile(0), pl.BlockSpec((1, N_EXPERTS, TM), lambda b, i: (b, 0, i))],
        out_shape=[jax.ShapeDtypeStruct((B, T, D), F32), jax.ShapeDtypeStruct((B, T, D), BF16),
                   jax.ShapeDtypeStruct((B, N_EXPERTS, T), F32)],
        compiler_params=_params("arbitrary", "arbitrary"),
        name="merge",
    )(xa, mod, y_diff, conv, x0c, u, y_hy_c, hy_bias, y_swa, p_hg, p_hg, p_hg, w_branch, w_out, g2, rw_pad)


def _router_kernel(lg_ref, slot_ref, gate_ref, first_ref, *, segments):
    lg = lg_ref[0]
    e = jnp.exp(lg - jnp.max(lg, axis=0, keepdims=True))
    aff = e / jnp.sum(e, axis=0, keepdims=True)
    r = lax.broadcasted_iota(jnp.int32, (LANES, LANES), 0)
    c = lax.broadcasted_iota(jnp.int32, (LANES, LANES), 1)
    before = (r < c).astype(BF16)

    def prefix_count(mask, n):
        run = jnp.zeros((N_EXPERTS, 1), F32)
        outs = []
        for j in range(n // LANES):
            blk = mask[:, j * LANES:(j + 1) * LANES].astype(F32).astype(BF16)
            outs.append(_dot(blk, before) + run)
            run = run + jnp.sum(blk.astype(F32), axis=1, keepdims=True)
        return jnp.concatenate(outs, axis=1)

    for start, n, cap, slot_base in segments:
        a = aff[:, start:start + n]
        bits = pltpu.bitcast(a, jnp.int32)

        def search(_, bounds):
            lo, hi = bounds
            mid = lo + ((hi - lo + 1) >> 1)
            cnt = jnp.sum((bits >= mid).astype(F32), axis=1, keepdims=True)
            ok = cnt >= cap
            return jnp.where(ok, mid, lo), jnp.where(ok, hi, mid - 1)

        lo0 = jnp.zeros((N_EXPERTS, 1), jnp.int32)
        hi0 = jnp.full((N_EXPERTS, 1), 0x7F800000, jnp.int32)
        thr, _ = lax.fori_loop(0, 32, search, (lo0, hi0))
        above = bits > thr
        tie = bits == thr
        need = cap - jnp.sum(above.astype(F32), axis=1, keepdims=True)
        sel = above | (tie & (prefix_count(tie, n) < need))
        pos = prefix_count(sel, n)
        slot_ref[0, :, start:start + n] = jnp.where(sel, pos + slot_base, -1.0)
        gate_ref[0, :, start:start + n] = jnp.where(sel, a, 0.0)
        if start == 0:
            t_idx = lax.broadcasted_iota(jnp.int32, (n, LANES), 0)
            tile_start = lax.broadcasted_iota(jnp.int32, (n, LANES), 1) * TM
            before_tile = jnp.where(t_idx < tile_start, 1.0, 0.0).astype(BF16)
            first_ref[0] = _dot(jnp.where(sel, 1.0, 0.0).astype(BF16), before_tile)


def _router(logits_t, segments):
    B, E, T = logits_t.shape
    blk = pl.BlockSpec((1, E, T), lambda b: (b, 0, 0))
    return pl.pallas_call(
        functools.partial(_router_kernel, segments=segments),
        grid=(B,),
        in_specs=[blk],
        out_specs=[blk, blk, pl.BlockSpec((1, E, LANES), lambda b: (b, 0, 0))],
        out_shape=[jax.ShapeDtypeStruct((B, E, T), F32)] * 2 + [jax.ShapeDtypeStruct((B, E, LANES), F32)],
        compiler_params=_params("arbitrary"),
        name="router",
    )(logits_t)


def _slot_window(first, align, rows, n_slots):
    return pl.multiple_of(jnp.minimum(first // align * align, n_slots - rows), align)


def _moe_ffn_kernel(first_ref, busy_ref, slot_ref, gate_ref, h2_ref, w1_ref, w3_ref, w2_ref, o_ref, xg_scr, g_scr,
                    *, n_slots, n_tiles):
    pair = pl.program_id(0) * N_EXPERTS + pl.program_id(1)
    base = pair * n_tiles
    xg_scr[...] = jnp.zeros_like(xg_scr)
    g_scr[...] = jnp.zeros_like(g_scr)

    def gather(rows):
        row_id = lax.broadcasted_iota(jnp.int32, (rows, TM), 0)
        for c in range(n_tiles):
            tok = slice(c * TM, (c + 1) * TM)
            w0 = _slot_window(first_ref[base + c], ROWS_F32, rows, n_slots)
            hit = slot_ref[0, 0, :, tok] == (row_id + w0).astype(F32)
            win = pl.ds(w0, rows)
            xg_scr[win, :] += _dot(jnp.where(hit, 1.0, 0.0).astype(BF16), h2_ref[0, tok, :])
            g_scr[win, :] += jnp.sum(jnp.where(hit, gate_ref[0, 0, :, tok], 0.0), axis=1, keepdims=True)

    sparse = busy_ref[pair] <= SPARSE_HITS

    @pl.when(sparse)
    def _():
        gather(min(SPARSE_HITS + ROWS_F32, n_slots))

    @pl.when(jnp.logical_not(sparse))
    def _():
        gather(min(TM + ROWS_F32, n_slots))

    xb = xg_scr[...].astype(BF16)
    a = _dot(xb, w1_ref[0, 0].astype(BF16))
    b = _dot(xb, w3_ref[0, 0].astype(BF16))
    hidden = (a * jax.nn.sigmoid(a) * b).astype(BF16)
    o_ref[0, 0] = (_dot(hidden, w2_ref[0, 0].astype(BF16)) * g_scr[...]).astype(o_ref.dtype)


def _moe_ffn(first, busy, slot, gate, h2, w1, w3, w2, l, n_slots):
    B, E, T = slot.shape
    row = pl.BlockSpec((1, 1, 1, T), lambda b, e: (b, e, 0, 0))
    wsp = pl.BlockSpec((1, 1, D, D), lambda b, e: (l, e, 0, 0))
    smem = pl.BlockSpec(memory_space=pltpu.SMEM)
    return pl.pallas_call(
        functools.partial(_moe_ffn_kernel, n_slots=n_slots, n_tiles=T // TM),
        grid=(B, E),
        in_specs=[smem, smem, row, row,
                  pl.BlockSpec((1, T, D), lambda b, e: (b, 0, 0), pipeline_mode=pl.Buffered(1)), wsp, wsp, wsp],
        out_specs=pl.BlockSpec((1, 1, n_slots, D), lambda b, e: (b, e, 0, 0)),
        out_shape=jax.ShapeDtypeStruct((B, E, n_slots, D), BF16),
        scratch_shapes=[pltpu.VMEM((n_slots, D), F32), pltpu.VMEM((n_slots, 1), F32)],
        compiler_params=_params("arbitrary", "arbitrary"),
        name="moe_ffn",
    )(first, busy, slot.reshape(B, E, 1, T), gate.reshape(B, E, 1, T), h2, w1, w3, w2)


def _moe_combine_kernel(first_ref, busy_ref, x_ref, mod_ref, slot_ref, y_ref, fg_ref, o_ref, acc_scr,
                        *, n_slots, n_tiles, n_lat_tiles, final):
    tm = x_ref.shape[1]

    def combine(rows):
        col_id = lax.broadcasted_iota(jnp.int32, (tm, rows), 1)
        acc = jnp.zeros((tm, D), F32)
        for e in range(N_EXPERTS):
            w0 = _slot_window(first_ref[(pl.program_id(0) * N_EXPERTS + e) * n_tiles + pl.program_id(1)], ROWS_BF16,
                              rows, n_slots)
            hit = slot_ref[0, :, e:e + 1] == (col_id + w0).astype(F32)
            acc = acc + _dot(jnp.where(hit, 1.0, 0.0).astype(BF16), y_ref[0, e, pl.ds(w0, rows), :])
        acc_scr[...] = acc

    sparse = busy_ref[pl.program_id(0) * n_tiles + pl.program_id(1)] <= SPARSE_HITS

    @pl.when(sparse)
    def _():
        combine(min(SPARSE_HITS + ROWS_BF16, n_slots))

    @pl.when(jnp.logical_not(sparse))
    def _():
        combine(min(tm + ROWS_BF16, n_slots))

    x = x_ref[0] + mod_ref[0, 5:6, :] * acc_scr[...]
    if final:
        @pl.when(pl.program_id(1) < n_lat_tiles)
        def _():
            o_ref[0] = _rms(x) * fg_ref[...]
    else:
        o_ref[0] = x


def _moe_combine(first, busy, xa, mod, slot_t, y, final_g, n_lat_tiles, n_slots, final):
    B, T, _ = xa.shape
    ctx_row = B
    tile = pl.BlockSpec((1, TM, D), lambda b, i: (b, i, 0))
    smem = pl.BlockSpec(memory_space=pltpu.SMEM)
    if final:
        out_spec = pl.BlockSpec((1, TM, D), lambda b, i: (b, jnp.minimum(i, n_lat_tiles - 1), 0))
        out_rows = n_lat_tiles * TM
    else:
        out_spec, out_rows = tile, T
    return pl.pallas_call(
        functools.partial(_moe_combine_kernel, n_slots=n_slots, n_tiles=T // TM, n_lat_tiles=n_lat_tiles,
                          final=final),
        grid=(B, T // TM),
        in_specs=[smem, smem, tile,
                  pl.BlockSpec((1, 6, D), lambda b, i: (jnp.where(i >= n_lat_tiles, ctx_row, b), 0, 0)),
                  pl.BlockSpec((1, TM, N_EXPERTS), lambda b, i: (b, i, 0)),
                  pl.BlockSpec((1, N_EXPERTS, n_slots, D), lambda b, i: (b, 0, 0, 0)),
                  pl.BlockSpec((1, D), lambda b, i: (0, 0))],
        out_specs=out_spec,
        out_shape=jax.ShapeDtypeStruct((B, out_rows, D), F32),
        scratch_shapes=[pltpu.VMEM((TM, D), F32)],
        compiler_params=_params("arbitrary", "arbitrary"),
        name="moe_combine",
    )(first, busy, xa, mod, slot_t, y, final_g)


def _swa_head_perm():
    return [kv * SWA_GROUP + g for g in range(SWA_GROUP) for kv in range(SWA_KV)]


def _split_w_in(w_in):
    depth = w_in.shape[0]
    o = [0, D, 2 * D, 3 * D, 6 * D, 7 * D, 7 * D + SWA_KV * HEAD, 7 * D + 2 * SWA_KV * HEAD, 10 * D + 2 * SWA_KV * HEAD]
    dq, dk, dv, hy, sq, sk, sv, gt = (w_in[:, :, o[k]:o[k + 1]] for k in range(8))
    scale = HEAD ** -0.5 * math.log2(math.e)
    sq = sq.reshape(depth, D, SWA_KV * SWA_GROUP, HEAD)[:, :, jnp.array(_swa_head_perm())].reshape(depth, D, D)
    w_attn = jnp.concatenate([dq * scale, dk, dv, sq * scale, sk, sv], axis=-1).astype(BF16)
    w_hg = jnp.concatenate([hy, gt], axis=-1).astype(BF16)
    return w_attn, w_hg


def kernel(x, c, ctx, c_ctx, w_mod, b_mod, norm1_g, norm2_g, w_in, diff_lambda, diff_subln_g, hy_conv_w,
           hy_conv_b, hy_ff_w1, hy_ff_b1, hy_ff_w2, hy_ff_b2, hy_ff_w3, hy_sin_freq, hy_bias, swa_sink,
           w_branch, w_out, router_w, moe_w1, moe_w3, moe_w2, final_g):
    B, L, _ = x.shape
    Lc = ctx.shape[1]
    depth = w_mod.shape[0]
    T = L + Lc
    assert Lc == TM and L % (4 * TM) == 0 and x.shape[2] == D
    n_lat_tiles = L // TM
    cap_l = CAPACITY_FACTOR * L // N_EXPERTS
    cap_c = CAPACITY_FACTOR * Lc // N_EXPERTS
    n_slots = cap_l + cap_c
    segments = ((0, L, cap_l, 0), (L, Lc, cap_c, cap_l))

    xa = jnp.concatenate([x, ctx], axis=1)
    mods = _mods(c, c_ctx, w_mod, b_mod)
    cos_t, sin_t = _rope_tables(L, Lc)
    tabs = _dft_tables(L, Lc)
    w_attn, w_hg = _split_w_in(w_in)
    rope_flags = tuple([True] * 16 + [False] * 8 + [True] * 8 + [True] * 2 + [False] * 2)
    perm = jnp.array(_swa_head_perm())
    wb = w_branch.at[:, 2].set(
        w_branch[:, 2].reshape(depth, SWA_KV * SWA_GROUP, HEAD, D)[:, perm].reshape(depth, D, D)).astype(BF16)
    wo = w_out.astype(BF16)
    w1, w3, w2 = moe_w1, moe_w3, moe_w2
    rw_pad = jnp.pad(router_w, ((0, 0), (0, 0), (0, LANES - N_EXPERTS)))
    w1p = jnp.pad(hy_ff_w1, ((0, 0), (0, LANES - HY_EMB), (0, 0)))

    for l in range(depth):
        lambda_init = 0.8 - 0.6 * math.exp(-0.3 * l)
        mod = mods[l]
        p_attn, p_hg = _norm_proj(xa, mod, norm1_g[l][None], w_attn[l], w_hg[l], cos_t, sin_t, rope_flags,
                                  n_lat_tiles)
        y_diff = _diff_attn(p_attn, diff_lambda, diff_subln_g, L, Lc, lambda_init, l)
        y_swa = _swa(p_attn, swa_sink, L, Lc, l)
        hp = dict(w1p=w1p[l], b1=hy_ff_b1[l][None], w2=hy_ff_w2[l], b2=hy_ff_b2[l][None],
                  fr=hy_sin_freq[l][None], w3=hy_ff_w3[l], conv_w=hy_conv_w[l], conv_b=hy_conv_b[l][None],
                  bias=hy_bias[l][None])
        hy = _hyena(p_hg, hp, tabs, L, Lc)
        xa, h2, logits_t = _merge(xa, mod, y_diff, hy, hp['bias'], y_swa, p_hg, wb[l], wo[l], norm2_g[l][None],
                                  rw_pad[l], n_lat_tiles)
        slot, gate, first = _router(logits_t, segments)
        first = jnp.concatenate([first[:, :, :n_lat_tiles], jnp.full((B, N_EXPERTS, 1), cap_l, F32)], axis=2)
        first = first.astype(jnp.int32)
        ends = jnp.concatenate([first[:, :, 1:n_lat_tiles], jnp.full((B, N_EXPERTS, 1), cap_l, jnp.int32),
                                jnp.full((B, N_EXPERTS, 1), n_slots, jnp.int32)], axis=2)
        hits = ends - first
        first = first.reshape(-1)
        y_e = _moe_ffn(first, jnp.max(hits, axis=2).reshape(-1), slot, gate, h2, w1, w3, w2, l, n_slots)
        xa = _moe_combine(first, jnp.max(hits, axis=1).reshape(-1), xa, mod, jnp.swapaxes(slot, 1, 2), y_e,
                          final_g[None], n_lat_tiles, n_slots, final=l == depth - 1)
    return xa
```

```python
import functools
import math

import jax
import jax.numpy as jnp
from jax import lax
from jax.experimental import pallas as pl
from jax.experimental.pallas import tpu as pltpu

F32 = jnp.float32
BF16 = jnp.bfloat16
HIGHEST = lax.Precision.HIGHEST

D = 1024
HEAD = 64
GRID_W = 64
ROPE_BASE = 10000.0
EPS = 1e-6
N_DIFF_HEADS = D // (2 * HEAD)
DIFF_V = 2 * HEAD
SWA_KV = 4
SWA_GROUP = 4
WINDOW = 128
N_EXPERTS = 16
CAPACITY_FACTOR = 2
HY_BANDS = 16
HY_EMB = 1 + 2 * HY_BANDS
HY_FF = 64
HY_FAST_DECAY = 0.3
HY_SLOW_DECAY = 1.5
HY_TARGET = 1e-2

LANES = 128
ROWS_F32 = 8
ROWS_BF16 = 16
TM = 256
DFT_N2 = 128
VMEM_LIMIT = 56 * 1024 * 1024

SPARSE_HITS = 64


def _params(*sem):
    return pltpu.CompilerParams(dimension_semantics=sem, vmem_limit_bytes=VMEM_LIMIT)


def _dot(a, b, precision=None):
    return jnp.dot(a, b, preferred_element_type=F32, precision=precision)


def _dot_nt(a, b, precision=None):
    return lax.dot_general(a, b, (((1,), (1,)), ((), ())), preferred_element_type=F32, precision=precision)


def _rms(x):
    return x * lax.rsqrt(jnp.mean(x * x, axis=-1, keepdims=True) + EPS)


def _mod_kernel(c_ref, w_ref, b_ref, o_ref):
    c = c_ref[...]
    sc = c * jax.nn.sigmoid(c)
    o_ref[0] = _dot(sc, w_ref[0], HIGHEST) + b_ref[0]


def _mods(c, c_ctx, w_mod, b_mod):
    B = c.shape[0]
    depth = w_mod.shape[0]
    rows = -(-(B + 1) // 8) * 8
    cc = jnp.zeros((rows, D), F32).at[:B].set(c).at[B].set(c_ctx)
    out = pl.pallas_call(
        _mod_kernel,
        grid=(depth, 6),
        in_specs=[pl.BlockSpec((rows, D), lambda l, j: (0, 0)),
                  pl.BlockSpec((1, D, D), lambda l, j: (l, 0, j)),
                  pl.BlockSpec((1, 1, D), lambda l, j: (l, 0, j))],
        out_specs=pl.BlockSpec((1, rows, D), lambda l, j: (l, 0, j)),
        out_shape=jax.ShapeDtypeStruct((depth, rows, 6 * D), F32),
        compiler_params=_params("arbitrary", "arbitrary"),
        name="mods",
    )(cc, w_mod, b_mod.reshape(depth, 1, 6 * D))
    return out.reshape(depth, rows, 6, D)


def _norm_proj_kernel(x_ref, mod_ref, g_ref, wa_ref, wh_ref, cos_ref, sin_ref, oa_ref, oh_ref, *, rope_flags):
    x = x_ref[0]
    h = _rms(x) * g_ref[...]
    h = h * (1.0 + mod_ref[0, 1:2, :]) + mod_ref[0, 0:1, :]
    hb = h.astype(BF16)
    sub_per_chunk = 4
    cos = cos_ref[...]
    sin = sin_ref[...]
    lane = lax.broadcasted_iota(jnp.int32, cos.shape, 1)
    first_half = (lane % (HEAD // 2)) < (HEAD // 4)
    for w_ref, o_ref, flags in ((wa_ref, oa_ref, rope_flags), (wh_ref, oh_ref, (False,) * (wh_ref.shape[1] // LANES))):
        n_sub = len(flags)
        for c0 in range(0, n_sub, sub_per_chunk):
            c1 = min(c0 + sub_per_chunk, n_sub)
            acc = _dot(hb, w_ref[:, c0 * LANES:c1 * LANES])
            for s in range(c0, c1):
                a = acc[:, (s - c0) * LANES:(s - c0 + 1) * LANES]
                if flags[s]:
                    partner = jnp.where(first_half, pltpu.roll(a, LANES - HEAD // 4, 1), pltpu.roll(a, HEAD // 4, 1))
                    a = a * cos + partner * sin
                o_ref[0, :, s * LANES:(s + 1) * LANES] = a.astype(o_ref.dtype)


def _norm_proj(xa, mod, g, w_attn, w_hg, cos_t, sin_t, rope_flags, n_lat_tiles):
    B, T, _ = xa.shape
    na, nh = w_attn.shape[1], w_hg.shape[1]
    ctx_row = B
    resident = lambda n: pl.BlockSpec((D, n), lambda b, i: (0, 0), pipeline_mode=pl.Buffered(1))
    return pl.pallas_call(
        functools.partial(_norm_proj_kernel, rope_flags=rope_flags),
        grid=(B, T // TM),
        in_specs=[pl.BlockSpec((1, TM, D), lambda b, i: (b, i, 0)),
                  pl.BlockSpec((1, 6, D), lambda b, i: (jnp.where(i >= n_lat_tiles, ctx_row, b), 0, 0)),
                  pl.BlockSpec((1, D), lambda b, i: (0, 0)),
                  resident(na), resident(nh),
                  pl.BlockSpec((TM, LANES), lambda b, i: (i, 0)),
                  pl.BlockSpec((TM, LANES), lambda b, i: (i, 0))],
        out_specs=[pl.BlockSpec((1, TM, na), lambda b, i: (b, i, 0)),
                   pl.BlockSpec((1, TM, nh), lambda b, i: (b, i, 0))],
        out_shape=[jax.ShapeDtypeStruct((B, T, na), BF16), jax.ShapeDtypeStruct((B, T, nh), BF16)],
        compiler_params=_params("arbitrary", "arbitrary"),
        name="proj",
    )(xa, mod, g, w_attn, w_hg, cos_t, sin_t)


def _rope_tables(L, Lc):
    t = jnp.arange(L)
    pos = jnp.stack([(t // GRID_W).astype(F32), (t % GRID_W).astype(F32)], axis=1)
    n_freq = HEAD // 4
    inv = ROPE_BASE ** (-jnp.arange(n_freq, dtype=F32) / n_freq)
    ang = pos[:, :, None] * inv
    cos = jnp.cos(ang)
    sin = jnp.sin(ang)
    cos_h = jnp.concatenate([cos, cos], axis=-1).reshape(L, HEAD)
    sin_h = jnp.concatenate([-sin, sin], axis=-1).reshape(L, HEAD)
    cos_t = jnp.concatenate([jnp.tile(cos_h, (1, LANES // HEAD)), jnp.ones((Lc, LANES), F32)], axis=0)
    sin_t = jnp.concatenate([jnp.tile(sin_h, (1, LANES // HEAD)), jnp.zeros((Lc, LANES), F32)], axis=0)
    return cos_t, sin_t


def _lane_block_reduce(x, op):
    out = x[:, :LANES]
    for j in range(1, x.shape[1] // LANES):
        out = op(out, x[:, j * LANES:(j + 1) * LANES])
    return out


def _diff_attn_kernel(lam_ref, q_ref, k_ref, v_ref, g_ref, o_ref, s_lat, s_ctx, m_run, qq_scr, l_scr, acc_scr,
                      *, L, Lc, tg, lambda_init):
    tq = TM
    n_groups = (L + Lc) // tg
    n_lat = L // tq
    n_tiles = (L + Lc) // tq
    rows = (2 * tq, LANES)
    lowest = jnp.full(rows, -jnp.inf, F32)
    nothing = jnp.zeros(rows, F32)
    lp = lam_ref[0]
    lam = (jnp.exp(jnp.sum(lp[0:1] * lp[1:2], axis=-1, keepdims=True))
           - jnp.exp(jnp.sum(lp[2:3] * lp[3:4], axis=-1, keepdims=True)) + lambda_init)

    def stage(t, _):
        cur = t % 2
        prv = 1 - cur
        score_any = t < n_tiles
        score_lat = t < n_lat
        weigh_any = t >= 1
        weigh_lat = jnp.logical_and(weigh_any, t <= n_lat)

        @pl.when(score_any)
        def _():
            q = q_ref[0, pl.ds(pl.multiple_of(t * tq, tq), tq), :]
            lane = lax.broadcasted_iota(jnp.int32, q.shape, 1)
            zero = jnp.zeros_like(q)
            qq_scr[...] = jnp.concatenate([jnp.where(lane < HEAD, q, zero), jnp.where(lane >= HEAD, q, zero)],
                                          axis=0)

        @pl.when(jnp.logical_and(score_any, jnp.logical_not(score_lat)))
        def _():
            s_c = _dot_nt(qq_scr[...], k_ref[0, L:L + Lc, :])
            s_ctx[...] = s_c
            m_run[cur] = _lane_block_reduce(s_c, jnp.maximum)

        def row_max():
            return jnp.max(m_run[prv], axis=-1, keepdims=True)

        @pl.when(jnp.logical_and(weigh_any, jnp.logical_not(weigh_lat)))
        def _():
            p = jnp.exp2(s_ctx[...] - row_max())
            l_scr[...] = _lane_block_reduce(p, jnp.add)
            acc_scr[...] = _dot(p.astype(BF16), v_ref[0, L:L + Lc, :])

        def scores(gi, mrun):
            start = pl.multiple_of(gi * tg, LANES)
            s = _dot_nt(qq_scr[...], k_ref[0, pl.ds(start, tg), :])
            s_lat[cur, gi] = s
            return jnp.maximum(mrun, _lane_block_reduce(s, jnp.maximum))

        def weighted(gi, lrun, acc, m):
            start = pl.multiple_of(gi * tg, LANES)
            p = jnp.exp2(s_lat[prv, gi] - m)
            return (lrun + _lane_block_reduce(p, jnp.add),
                    acc + _dot(p.astype(BF16), v_ref[0, pl.ds(start, tg), :]))

        @pl.when(jnp.logical_and(score_lat, weigh_lat))
        def _():
            m = row_max()

            def both(gi, carry):
                mrun, lrun, acc = carry
                lrun, acc = weighted(gi, lrun, acc, m)
                return scores(gi, mrun), lrun, acc

            m_run[cur], l_scr[...], acc_scr[...] = lax.fori_loop(0, n_groups, both, (lowest, nothing, nothing))

        @pl.when(jnp.logical_and(score_lat, jnp.logical_not(weigh_lat)))
        def _():
            m_run[cur] = lax.fori_loop(0, n_groups, scores, lowest)

        @pl.when(jnp.logical_and(jnp.logical_not(score_lat), weigh_lat))
        def _():
            m = row_max()
            l_scr[...], acc_scr[...] = lax.fori_loop(0, n_groups, lambda gi, c: weighted(gi, c[0], c[1], m),
                                                      (nothing, nothing))

        @pl.when(weigh_any)
        def _():
            o = acc_scr[...] / jnp.sum(l_scr[...], axis=-1, keepdims=True)
            o = _rms(o[:tq] - lam * o[tq:])
            out_rows = pl.ds(pl.multiple_of((t - 1) * tq, tq), tq)
            o_ref[0, out_rows, :] = (o * g_ref[0] * (1.0 - lambda_init)).astype(o_ref.dtype)

        return 0

    lax.fori_loop(0, n_tiles + 1, stage, 0)


def _diff_attn(p_attn, diff_lambda, subln_g, L, Lc, lambda_init, l):
    B, T, _ = p_attn.shape
    tg = T // 2
    nh = N_DIFF_HEADS
    head = lambda c0: pl.BlockSpec((1, T, DIFF_V), lambda b, h: (b, 0, c0 + h))
    return pl.pallas_call(
        functools.partial(_diff_attn_kernel, L=L, Lc=Lc, tg=tg, lambda_init=lambda_init),
        grid=(B, nh),
        in_specs=[pl.BlockSpec((1, 4, HEAD), lambda b, h: (l, 0, 0)),
                  head(0), head(nh), head(2 * nh),
                  pl.BlockSpec((1, 1, DIFF_V), lambda b, h: (l, 0, 0))],
        out_specs=head(0),
        out_shape=jax.ShapeDtypeStruct((B, T, D), BF16),
        scratch_shapes=[pltpu.VMEM((2, T // tg, 2 * TM, tg), F32), pltpu.VMEM((2 * TM, Lc), F32),
                        pltpu.VMEM((2, 2 * TM, LANES), F32), pltpu.VMEM((2 * TM, DIFF_V), BF16),
                        pltpu.VMEM((2 * TM, LANES), F32), pltpu.VMEM((2 * TM, DIFF_V), F32)],
        compiler_params=_params("arbitrary", "arbitrary"),
        name="diff_attn",
    )(diff_lambda, p_attn, p_attn, p_attn, subln_g.reshape(-1, 1, DIFF_V))


def _swa_kernel(sink_ref, q_ref, k_ref, v_ref, o_ref, *, L, Lc, l):
    i = pl.program_id(1)
    tq = q_ref.shape[1]
    kvw = SWA_KV * HEAD
    win = tq + 2 * WINDOW
    lane_head = lax.broadcasted_iota(jnp.int32, (tq, kvw), 1) // HEAD
    k_ctx = k_ref[0, L:L + Lc, :]
    v_ctx = v_ref[0, L:L + Lc, :]

    def run(latent):
        if latent:
            start = pl.multiple_of(jnp.clip(i * tq - WINDOW, 0, L - win), WINDOW)
            keys = jnp.concatenate([k_ctx, k_ref[0, pl.ds(start, win), :]], axis=0)
            vals = jnp.concatenate([v_ctx, v_ref[0, pl.ds(start, win), :]], axis=0)
            q_pos = i * tq + lax.broadcasted_iota(jnp.int32, (tq, Lc + win), 0)
            col = lax.broadcasted_iota(jnp.int32, (tq, Lc + win), 1)
            in_band = jnp.abs(start + col - Lc - q_pos) <= WINDOW
            bias = jnp.where(jnp.logical_or(col < Lc, in_band), 0.0, -jnp.inf)
        else:
            keys, vals = k_ctx, v_ctx
        for g in range(SWA_GROUP):
            qg = q_ref[0, :, g * kvw:(g + 1) * kvw]
            acc = jnp.zeros((tq, kvw), F32)
            for kv in range(SWA_KV):
                qe = jnp.where(lane_head == kv, qg, jnp.zeros_like(qg))
                snk = sink_ref[l, kv * SWA_GROUP + g] * math.log2(math.e)
                s = _dot_nt(qe, keys)
                if latent:
                    s = s + bias
                m = jnp.maximum(jnp.max(s, axis=-1, keepdims=True), snk)
                p = jnp.exp2(s - m)
                den = jnp.sum(p, axis=-1, keepdims=True) + jnp.exp2(snk - m)
                o = _dot(p.astype(BF16), vals)
                acc = acc + jnp.where(lane_head == kv, o / den, 0.0)
            o_ref[0, :, g * kvw:(g + 1) * kvw] = acc.astype(o_ref.dtype)


    @pl.when(i < L // tq)
    def _():
        run(True)

    @pl.when(i >= L // tq)
    def _():
        run(False)


def _swa(p_attn, sink, L, Lc, l):
    B, T, _ = p_attn.shape
    kvw = SWA_KV * HEAD
    return pl.pallas_call(
        functools.partial(_swa_kernel, L=L, Lc=Lc, l=l),
        grid=(B, T // TM),
        in_specs=[pl.BlockSpec(memory_space=pltpu.SMEM),
                  pl.BlockSpec((1, TM, D), lambda b, i: (b, i, 3)),
                  pl.BlockSpec((1, T, kvw), lambda b, i: (b, 0, 4 * D // kvw)),
                  pl.BlockSpec((1, T, kvw), lambda b, i: (b, 0, 4 * D // kvw + 1))],
        out_specs=pl.BlockSpec((1, TM, D), lambda b, i: (b, i, 0)),
        out_shape=jax.ShapeDtypeStruct((B, T, D), BF16),
        compiler_params=_params("arbitrary", "arbitrary"),
        name="swa",
    )(sink, p_attn, p_attn, p_attn)


def _hy_filter_kernel(emb_ref, w1_ref, b1_ref, w2_ref, b2_ref, fr_ref, w3f_ref, w3b_ref, dl_ref, o_ref, *, L):
    rows = TM
    fr = fr_ref[...]

    def raw(i, total):
        r0 = pl.multiple_of(i * rows, rows)
        e = emb_ref[pl.ds(r0, rows), :]
        z = jnp.sin(fr * (_dot(e, w1_ref[...], HIGHEST) + b1_ref[...]))
        z = jnp.sin(fr * (_dot(z, w2_ref[...], HIGHEST) + b2_ref[...]))
        row = r0 + lax.broadcasted_iota(jnp.int32, (rows, 1), 0)
        h = jnp.where(row < L, _dot(z, w3f_ref[...], HIGHEST), _dot(z, w3b_ref[...], HIGHEST))
        h = h * jnp.exp(-e[:, 0:1] * dl_ref[...])
        o_ref[pl.ds(r0, rows), :] = h
        return total + jnp.sum(jnp.abs(h), axis=0, keepdims=True)

    n_chunks = emb_ref.shape[0] // rows
    norm = lax.fori_loop(0, n_chunks, raw, jnp.zeros((1, o_ref.shape[1]), F32)) + EPS

    def scale(i, _):
        r0 = pl.multiple_of(i * rows, rows)
        row = r0 + lax.broadcasted_iota(jnp.int32, (rows, 1), 0)
        o_ref[pl.ds(r0, rows), :] = jnp.where(row == L, 0.0, o_ref[pl.ds(r0, rows), :] / norm)
        return 0

    lax.fori_loop(0, n_chunks, scale, 0)


def _hy_embedding(L):
    n = jnp.arange(2 * L)
    lag = jnp.where(n < L, n, jnp.where(n == L, 0, 2 * L - n))
    t = jnp.linspace(0.0, 1.0, L, dtype=F32)[lag][:, None]
    w = (2.0 * math.pi * jnp.arange(L, dtype=F32) / L)[lag][:, None]
    f = jnp.linspace(1e-4, HY_BANDS - 1, HY_BANDS, dtype=F32)[None, :]
    emb = jnp.concatenate([t, jnp.cos(f * w), -jnp.sin(f * w)], axis=-1)
    return jnp.pad(emb, ((0, 0), (0, LANES - HY_EMB)))


def _hy_filter(L, w1p, b1, w2, b2, fr, w3, deltas):
    ct = 512
    emb = _hy_embedding(L)
    full = lambda shape: pl.BlockSpec(shape, lambda j: (0, 0))
    return pl.pallas_call(
        functools.partial(_hy_filter_kernel, L=L),
        grid=(D // ct,),
        in_specs=[full((2 * L, LANES)), full((LANES, HY_FF)), full((1, HY_FF)), full((HY_FF, HY_FF)),
                  full((1, HY_FF)), full((1, HY_FF)),
                  pl.BlockSpec((HY_FF, ct), lambda j: (0, j)),
                  pl.BlockSpec((HY_FF, ct), lambda j: (0, D // ct + j)),
                  pl.BlockSpec((1, ct), lambda j: (0, j))],
        out_specs=pl.BlockSpec((2 * L, ct), lambda j: (0, j)),
        out_shape=jax.ShapeDtypeStruct((2 * L, D), F32),
        compiler_params=_params("arbitrary"),
        name="hy_filter",
    )(emb, w1p, b1, w2, b2, fr, w3, w3, deltas)


def _short_conv(x, w, b):
    n = x.shape[0]
    row = lax.broadcasted_iota(jnp.int32, x.shape, 0)
    prev = jnp.where(row == 0, 0.0, pltpu.roll(x, 1, 0))
    nxt = jnp.where(row == n - 1, 0.0, pltpu.roll(x, n - 1, 0))
    return prev * w[0:1] + x * w[1:2] + nxt * w[2:3] + b


def _hy_pre_kernel(x0_ref, x1_ref, v_ref, w0_ref, w1_ref, w2_ref, b0_ref, b1_ref, b2_ref, x0c_ref, u_ref):
    x0c_ref[0] = _short_conv(x0_ref[0].astype(F32), w0_ref[...], b0_ref[...]).astype(x0c_ref.dtype)
    u = (_short_conv(x1_ref[0].astype(F32), w1_ref[...], b1_ref[...])
         * _short_conv(v_ref[0].astype(F32), w2_ref[...], b2_ref[...]))
    u_ref[0] = u.astype(u_ref.dtype)


def _hy_pre(p_hg, conv_w, conv_b, L):
    B = p_hg.shape[0]
    ct = LANES
    nb = D // ct
    col = lambda k: pl.BlockSpec((1, L, ct), lambda b, j: (b, 0, k * nb + j))
    wsp = lambda k: pl.BlockSpec((3, ct), lambda b, j: (0, k * nb + j))
    bsp = lambda k: pl.BlockSpec((1, ct), lambda b, j: (0, k * nb + j))
    out = pl.BlockSpec((1, L, ct), lambda b, j: (b, 0, j))
    return pl.pallas_call(
        _hy_pre_kernel,
        grid=(B, nb),
        in_specs=[col(0), col(1), col(2), wsp(0), wsp(1), wsp(2), bsp(0), bsp(1), bsp(2)],
        out_specs=[out, out],
        out_shape=[jax.ShapeDtypeStruct((B, L, D), BF16)] * 2,
        compiler_params=_params("arbitrary", "arbitrary"),
        name="hy_pre",
    )(p_hg, p_hg, p_hg, conv_w, conv_w, conv_w, conv_b, conv_b, conv_b)


def _lmat_kernel(m_ref, x_ref, o_ref, *, precision):
    x = x_ref[0]
    if precision is None:
        x = x.astype(BF16)
    o_ref[0] = _dot(m_ref[...], x, precision).astype(o_ref.dtype)


def _lmat(mat, x, out_dtype, precision, wc, name):
    Bn, K, W = x.shape
    M = mat.shape[0]
    return pl.pallas_call(
        functools.partial(_lmat_kernel, precision=precision),
        grid=(Bn, W // wc),
        in_specs=[pl.BlockSpec((M, K), lambda b, j: (0, 0)),
                  pl.BlockSpec((1, K, wc), lambda b, j: (b, 0, j))],
        out_specs=pl.BlockSpec((1, M, wc), lambda b, j: (b, 0, j)),
        out_shape=jax.ShapeDtypeStruct((Bn, M, W), out_dtype),
        compiler_params=_params("arbitrary", "arbitrary"),
        name=name,
    )(mat, x)


def _hy_s1_kernel(m_ref, x_ref, o_ref, *, precision):
    k, nb = x_ref.shape[1], x_ref.shape[2]
    res = _dot(m_ref[...], x_ref[0].reshape(k, nb * D), precision)
    o_ref[0] = res.reshape(2, res.shape[0] // 2, nb, D).astype(o_ref.dtype)


def _hy_s1(mat, x, nb, out_dtype=BF16, precision=None, name="hy_s1"):
    Bn, K, n2, _ = x.shape
    M = mat.shape[0]
    return pl.pallas_call(
        functools.partial(_hy_s1_kernel, precision=precision),
        grid=(Bn, n2 // nb),
        in_specs=[pl.BlockSpec((M, K), lambda b, j: (0, 0)),
                  pl.BlockSpec((1, K, nb, D), lambda b, j: (b, 0, j, 0))],
        out_specs=pl.BlockSpec((1, 2, M // 2, nb, D), lambda b, j: (b, 0, 0, j, 0)),
        out_shape=jax.ShapeDtypeStruct((Bn, 2, M // 2, n2, D), out_dtype),
        compiler_params=_params("arbitrary", "arbitrary"),
        name=name,
    )(mat, x)


def _hy_s3_kernel(g_ref, z_ref, o_ref):
    two, n1, nb, _ = z_ref.shape[1:]
    conv = _dot(g_ref[...], z_ref[0].reshape(two * n1, nb * D))
    o_ref[0] = conv.reshape(conv.shape[0], nb, D).astype(o_ref.dtype)


def _hy_s3(g_mat, z, nb):
    Bn, _, n1, n2, _ = z.shape
    R = g_mat.shape[0]
    return pl.pallas_call(
        _hy_s3_kernel,
        grid=(Bn, n2 // nb),
        in_specs=[pl.BlockSpec((R, 2 * n1), lambda b, j: (0, 0)),
                  pl.BlockSpec((1, 2, n1, nb, D), lambda b, j: (b, 0, 0, j, 0))],
        out_specs=pl.BlockSpec((1, R, nb, D), lambda b, j: (b, 0, j, 0)),
        out_shape=jax.ShapeDtypeStruct((Bn, R, n2, D), BF16),
        compiler_params=_params("arbitrary", "arbitrary"),
        name="hy_s3",
    )(g_mat, z)


def _cmul(xr, xi, kr, ki):
    return xr * kr - xi * ki, xr * ki + xi * kr


def _hy_spec_kernel(a_ref, m_ref, o_ref):
    a = jnp.concatenate([a_ref[0, 0], a_ref[1, 0]], axis=0)
    o_ref[0] = _dot(m_ref[0], a, HIGHEST)


def _hy_spectrum(a, m_tab):
    n1 = a.shape[1]
    n2 = DFT_N2
    return pl.pallas_call(
        _hy_spec_kernel,
        grid=(n1,),
        in_specs=[pl.BlockSpec((2, 1, n2, D), lambda k: (0, k, 0, 0)),
                  pl.BlockSpec((1, 2 * n2, 2 * n2), lambda k: (k, 0, 0))],
        out_specs=pl.BlockSpec((1, 2 * n2, D), lambda k: (k, 0, 0)),
        out_shape=jax.ShapeDtypeStruct((n1, 2 * n2, D), F32),
        compiler_params=_params("arbitrary"),
        name="hy_spectrum",
    )(a, m_tab)


def _hy_mid_kernel(a_ref, m_ref, mt_ref, kf_ref, o_ref):
    n2 = DFT_N2
    kr = kf_ref[0, :n2]
    ki = kf_ref[0, n2:]

    samples = range(a_ref.shape[0])
    a = jnp.concatenate([jnp.concatenate([a_ref[b, 0, 0], a_ref[b, 1, 0]], axis=0) for b in samples], axis=1)
    x = _dot(m_ref[0], a)
    ys = []
    for b in samples:
        yr, yi = _cmul(x[:n2, b * D:(b + 1) * D], x[n2:, b * D:(b + 1) * D], kr, ki)
        ys.append(jnp.concatenate([yr, yi], axis=0).astype(BF16))
    z = _dot(mt_ref[0], jnp.concatenate(ys, axis=1)).astype(o_ref.dtype)
    for b in samples:
        o_ref[b, 0, 0] = z[:n2, b * D:(b + 1) * D]
        o_ref[b, 1, 0] = z[n2:, b * D:(b + 1) * D]


def _hy_mid(a, m_bf, mt_bf, kf):
    B, _, n1, n2, _ = a.shape
    blk = pl.BlockSpec((B, 2, 1, n2, D), lambda k: (0, 0, k, 0, 0))
    tab = pl.BlockSpec((1, 2 * n2, 2 * n2), lambda k: (k, 0, 0))
    return pl.pallas_call(
        _hy_mid_kernel,
        grid=(n1,),
        in_specs=[blk, tab, tab, pl.BlockSpec((1, 2 * n2, D), lambda k: (k, 0, 0))],
        out_specs=blk,
        out_shape=jax.ShapeDtypeStruct(a.shape, BF16),
        compiler_params=_params("arbitrary"),
        name="hy_mid",
    )(a, m_bf, mt_bf, kf)


def _hy_output(x0c, conv, u, bias):
    return x0c * (conv + u * bias)


def _hy_ctx_kernel(x0_ref, x1_ref, v_ref, w0_ref, w1_ref, w2_ref, b0_ref, b1_ref, b2_ref,
                   f_ref, kf_ref, g_ref, bias_ref, o_ref):
    x0 = _short_conv(x0_ref[0].astype(F32), w0_ref[...], b0_ref[...])
    u = (_short_conv(x1_ref[0].astype(F32), w1_ref[...], b1_ref[...])
         * _short_conv(v_ref[0].astype(F32), w2_ref[...], b2_ref[...]))
    x = _dot(f_ref[...], u.astype(BF16))
    nf = x.shape[0] // 2
    yr, yi = _cmul(x[:nf], x[nf:], kf_ref[:nf], kf_ref[nf:])
    conv = _dot(g_ref[...], jnp.concatenate([yr, yi], axis=0).astype(BF16))
    o_ref[0] = _hy_output(x0, conv, u, bias_ref[...]).astype(o_ref.dtype)


def _hy_ctx(p_hg, conv_w, conv_b, f_c, kf_c, g_c, bias, L, Lc):
    B = p_hg.shape[0]
    ct = 256
    nb = D // ct
    rb = L // Lc
    col = lambda k: pl.BlockSpec((1, Lc, ct), lambda b, j: (b, rb, k * nb + j))
    wsp = lambda k: pl.BlockSpec((3, ct), lambda b, j: (0, k * nb + j))
    bsp = lambda k: pl.BlockSpec((1, ct), lambda b, j: (0, k * nb + j))
    return pl.pallas_call(
        _hy_ctx_kernel,
        grid=(B, nb),
        in_specs=[col(0), col(1), col(2), wsp(0), wsp(1), wsp(2), bsp(0), bsp(1), bsp(2),
                  pl.BlockSpec(f_c.shape, lambda b, j: (0, 0)),
                  pl.BlockSpec((kf_c.shape[0], ct), lambda b, j: (0, j)),
                  pl.BlockSpec(g_c.shape, lambda b, j: (0, 0)),
                  pl.BlockSpec((1, ct), lambda b, j: (0, j))],
        out_specs=pl.BlockSpec((1, Lc, ct), lambda b, j: (b, 0, j)),
        out_shape=jax.ShapeDtypeStruct((B, Lc, D), BF16),
        compiler_params=_params("arbitrary", "arbitrary"),
        name="hy_ctx",
    )(p_hg, p_hg, p_hg, conv_w, conv_w, conv_w, conv_b, conv_b, conv_b, f_c, kf_c, g_c, bias)


def _dft_tables(L, Lc):
    n = 2 * L
    n2 = DFT_N2
    n1 = n // n2
    two_pi = 2.0 * math.pi

    def cs(idx, period):
        ang = (idx % period).astype(F32) * (two_pi / period)
        return jnp.cos(ang), jnp.sin(ang)

    k1 = jnp.arange(n1)[:, None]
    c, s = cs(k1 * jnp.arange(n1)[None, :], n1)
    f1_full = jnp.concatenate([c, -s], axis=0)
    f1_half = f1_full[:, :n1 // 2]
    kk1 = jnp.arange(n1)[:, None, None]
    kk2 = jnp.arange(n2)[None, :, None]
    nn2 = jnp.arange(n2)[None, None, :]
    c, s = cs(nn2 * kk1 + n1 * nn2 * kk2, n)
    ci = -s
    m_tab = jnp.concatenate([jnp.concatenate([c, -ci], axis=2), jnp.concatenate([ci, c], axis=2)], axis=1)
    c, s = cs(jnp.arange(n1 // 2)[:, None] * jnp.arange(n1)[None, :], n1)
    g_mat = jnp.concatenate([c, -s], axis=1) / n
    nc = 2 * Lc
    kc = jnp.arange(nc)[:, None]
    c, s = cs(kc * jnp.arange(nc)[None, :], nc)
    fc_full = jnp.concatenate([c, -s], axis=0)
    c, s = cs(jnp.arange(Lc)[:, None] * jnp.arange(nc)[None, :], nc)
    g_c = jnp.concatenate([c, -s], axis=1) / nc
    return dict(f1_full=f1_full, f1_half=f1_half.astype(BF16), m_tab=m_tab, m_bf=m_tab.astype(BF16),
                mt_bf=jnp.swapaxes(m_tab, 1, 2).astype(BF16), g_mat=g_mat.astype(BF16),
                fc_full=fc_full, fc_half=fc_full[:, :Lc].astype(BF16), g_c=g_c.astype(BF16))


def _hyena(p_hg, hp, tabs, L, Lc):
    B = p_hg.shape[0]
    n2 = DFT_N2
    n1 = 2 * L // n2
    wc = min(16 * D, n2 * D)
    deltas = jnp.abs(jnp.linspace(math.log(HY_TARGET) / HY_FAST_DECAY, math.log(HY_TARGET) / HY_SLOW_DECAY,
                                  D, dtype=F32))[None, :]
    filt = functools.partial(_hy_filter, w1p=hp['w1p'], b1=hp['b1'], w2=hp['w2'], b2=hp['b2'], fr=hp['fr'],
                             w3=hp['w3'], deltas=deltas)
    k_lat = filt(L)
    a_f = _hy_s1(tabs['f1_full'], k_lat.reshape(1, n1, n2, D), wc // D, F32, HIGHEST, "hy_filt_s1")
    kf = _hy_spectrum(a_f[0], tabs['m_tab'])
    k_ctx = filt(Lc)
    kf_c = _lmat(tabs['fc_full'], k_ctx[None], F32, HIGHEST, D, "hy_filt_ctx")[0]
    x0c, u = _hy_pre(p_hg, hp['conv_w'], hp['conv_b'], L)
    a = _hy_s1(tabs['f1_half'], u.reshape(B, n1 // 2, n2, D), wc // D)
    z = _hy_mid(a, tabs['m_bf'], tabs['mt_bf'], kf)
    conv = _hy_s3(tabs['g_mat'], z, wc // D).reshape(B, L, D)
    y_c = _hy_ctx(p_hg, hp['conv_w'], hp['conv_b'], tabs['fc_half'], kf_c, tabs['g_c'], hp['bias'], L, Lc)
    return conv, x0c, u, y_c


def _merge_kernel(x_ref, mod_ref, yd_ref, cv_ref, x0_ref, u_ref, yc_ref, hb_ref, ys_ref, gd_ref, gh_ref, gs_ref,
                  wb_ref, wo_ref, g2_ref, rw_ref, xo_ref, h2_ref, lg_ref, *, n_lat_tiles):
    latent = pl.program_id(1) < n_lat_tiles
    half = x_ref.shape[1] // 2
    for rows in (slice(0, half), slice(half, 2 * half)):
        y_hy = jnp.where(latent,
                         _hy_output(x0_ref[0, rows].astype(F32), cv_ref[0, rows].astype(F32),
                                    u_ref[0, rows].astype(F32), hb_ref[...]),
                         yc_ref[0, rows].astype(F32)).astype(BF16)
        merged = None
        for k, (y, gt_ref) in enumerate(((yd_ref[0, rows], gd_ref), (y_hy, gh_ref), (ys_ref[0, rows], gs_ref))):
            term = jax.nn.sigmoid(gt_ref[0, rows].astype(F32)) * _dot(y, wb_ref[k])
            merged = term if merged is None else merged + term
        y = _dot(merged.astype(BF16), wo_ref[...])
        x = x_ref[0, rows] + mod_ref[0, 2:3, :] * y
        xo_ref[0, rows] = x
        h2 = _rms(x) * g2_ref[...]
        h2 = h2 * (1.0 + mod_ref[0, 4:5, :]) + mod_ref[0, 3:4, :]
        h2_ref[0, rows] = h2.astype(h2_ref.dtype)
        lg_ref[0, :, rows] = _dot(h2, rw_ref[...], HIGHEST).T[:N_EXPERTS]


def _merge(xa, mod, y_diff, hy, hy_bias, y_swa, p_hg, w_branch, w_out, g2, rw_pad, n_lat_tiles):
    B, T, _ = xa.shape
    ctx_row = B
    conv, x0c, u, y_hy_c = hy
    tile = lambda c: pl.BlockSpec((1, TM, D), lambda b, i: (b, i, c))
    lat = pl.BlockSpec((1, TM, D), lambda b, i: (b, jnp.minimum(i, n_lat_tiles - 1), 0))
    return pl.pallas_call(
        functools.partial(_merge_kernel, n_lat_tiles=n_lat_tiles),
        grid=(B, T // TM),
        in_specs=[tile(0),
                  pl.BlockSpec((1, 6, D), lambda b, i: (jnp.where(i >= n_lat_tiles, ctx_row, b), 0, 0)),
                  tile(0), lat, lat, lat,
                  pl.BlockSpec((1, y_hy_c.shape[1], D), lambda b, i: (b, 0, 0)),
                  pl.BlockSpec((1, D), lambda b, i: (0, 0)),
                  tile(0), tile(3), tile(4), tile(5),
                  pl.BlockSpec((3, D, D), lambda b, i: (0, 0, 0)),
                  pl.BlockSpec((D, D), lambda b, i: (0, 0)),
                  pl.BlockSpec((1, D), lambda b, i: (0, 0)),
                  pl.BlockSpec((D, LANES), lambda b, i: (0, 0))],
        out_specs=[tile(0), tile(0), pl.BlockSpec((1, N_EXPERTS, TM), lambda b, i: (b, 0, i))],
        out_shape=[jax.ShapeDtypeStruct((B, T, D), F32), jax.ShapeDtypeStruct((B, T, D), BF16),
                   jax.ShapeDtypeStruct((B, N_EXPERTS, T), F32)],
        compiler_params=_params("arbitrary", "arbitrary"),
        name="merge",
    )(xa, mod, y_diff, conv, x0c, u, y_hy_c, hy_bias, y_swa, p_hg, p_hg, p_hg, w_branch, w_out, g2, rw_pad)


def _router_kernel(lg_ref, slot_ref, gate_ref, first_ref, *, segments):
    lg = lg_ref[0]
    e = jnp.exp(lg - jnp.max(lg, axis=0, keepdims=True))
    aff = e / jnp.sum(e, axis=0, keepdims=True)
    r = lax.broadcasted_iota(jnp.int32, (LANES, LANES), 0)
    c = lax.broadcasted_iota(jnp.int32, (LANES, LANES), 1)
    before = (r < c).astype(BF16)

    def prefix_count(mask, n):
        run = jnp.zeros((N_EXPERTS, 1), F32)
        outs = []
        for j in range(n // LANES):
            blk = mask[:, j * LANES:(j + 1) * LANES].astype(F32).astype(BF16)
            outs.append(_dot(blk, before) + run)
            run = run + jnp.sum(blk.astype(F32), axis=1, keepdims=True)
        return jnp.concatenate(outs, axis=1)

    for start, n, cap, slot_base in segments:
        a = aff[:, start:start + n]
        bits = pltpu.bitcast(a, jnp.int32)

        def search(_, bounds):
            lo, hi = bounds
            mid = lo + ((hi - lo + 1) >> 1)
            cnt = jnp.sum((bits >= mid).astype(F32), axis=1, keepdims=True)
            ok = cnt >= cap
            return jnp.where(ok, mid, lo), jnp.where(ok, hi, mid - 1)

        lo0 = jnp.zeros((N_EXPERTS, 1), jnp.int32)
        hi0 = jnp.full((N_EXPERTS, 1), 0x7F800000, jnp.int32)
        thr, _ = lax.fori_loop(0, 32, search, (lo0, hi0))
        above = bits > thr
        tie = bits == thr
        need = cap - jnp.sum(above.astype(F32), axis=1, keepdims=True)
        sel = above | (tie & (prefix_count(tie, n) < need))
        pos = prefix_count(sel, n)
        slot_ref[0, :, start:start + n] = jnp.where(sel, pos + slot_base, -1.0)
        gate_ref[0, :, start:start + n] = jnp.where(sel, a, 0.0)
        if start == 0:
            t_idx = lax.broadcasted_iota(jnp.int32, (n, LANES), 0)
            tile_start = lax.broadcasted_iota(jnp.int32, (n, LANES), 1) * TM
            before_tile = jnp.where(t_idx < tile_start, 1.0, 0.0).astype(BF16)
            first_ref[0] = _dot(jnp.where(sel, 1.0, 0.0).astype(BF16), before_tile)


def _router(logits_t, segments):
    B, E, T = logits_t.shape
    blk = pl.BlockSpec((1, E, T), lambda b: (b, 0, 0))
    return pl.pallas_call(
        functools.partial(_router_kernel, segments=segments),
        grid=(B,),
        in_specs=[blk],
        out_specs=[blk, blk, pl.BlockSpec((1, E, LANES), lambda b: (b, 0, 0))],
        out_shape=[jax.ShapeDtypeStruct((B, E, T), F32)] * 2 + [jax.ShapeDtypeStruct((B, E, LANES), F32)],
        compiler_params=_params("arbitrary"),
        name="router",
    )(logits_t)


def _slot_window(first, align, rows, n_slots):
    return pl.multiple_of(jnp.minimum(first // align * align, n_slots - rows), align)


def _moe_ffn_kernel(first_ref, busy_ref, slot_ref, gate_ref, h2_ref, w1_ref, w3_ref, w2_ref, o_ref, xg_scr, g_scr,
                    *, n_slots, n_tiles):
    pair = pl.program_id(0) * N_EXPERTS + pl.program_id(1)
    base = pair * n_tiles
    xg_scr[...] = jnp.zeros_like(xg_scr)
    g_scr[...] = jnp.zeros_like(g_scr)

    def gather(rows):
        row_id = lax.broadcasted_iota(jnp.int32, (rows, TM), 0)
        for c in range(n_tiles):
            tok = slice(c * TM, (c + 1) * TM)
            w0 = _slot_window(first_ref[base + c], ROWS_F32, rows, n_slots)
            hit = slot_ref[0, 0, :, tok] == (row_id + w0).astype(F32)
            win = pl.ds(w0, rows)
            xg_scr[win, :] += _dot(jnp.where(hit, 1.0, 0.0).astype(BF16), h2_ref[0, tok, :])
            g_scr[win, :] += jnp.sum(jnp.where(hit, gate_ref[0, 0, :, tok], 0.0), axis=1, keepdims=True)

    sparse = busy_ref[pair] <= SPARSE_HITS

    @pl.when(sparse)
    def _():
        gather(min(SPARSE_HITS + ROWS_F32, n_slots))

    @pl.when(jnp.logical_not(sparse))
    def _():
        gather(min(TM + ROWS_F32, n_slots))

    xb = xg_scr[...].astype(BF16)
    a = _dot(xb, w1_ref[0, 0].astype(BF16))
    b = _dot(xb, w3_ref[0, 0].astype(BF16))
    hidden = (a * jax.nn.sigmoid(a) * b).astype(BF16)
    o_ref[0, 0] = (_dot(hidden, w2_ref[0, 0].astype(BF16)) * g_scr[...]).astype(o_ref.dtype)


def _moe_ffn(first, busy, slot, gate, h2, w1, w3, w2, l, n_slots):
    B, E, T = slot.shape
    row = pl.BlockSpec((1, 1, 1, T), lambda b, e: (b, e, 0, 0))
    wsp = pl.BlockSpec((1, 1, D, D), lambda b, e: (l, e, 0, 0))
    smem = pl.BlockSpec(memory_space=pltpu.SMEM)
    return pl.pallas_call(
        functools.partial(_moe_ffn_kernel, n_slots=n_slots, n_tiles=T // TM),
        grid=(B, E),
        in_specs=[smem, smem, row, row,
                  pl.BlockSpec((1, T, D), lambda b, e: (b, 0, 0), pipeline_mode=pl.Buffered(1)), wsp, wsp, wsp],
        out_specs=pl.BlockSpec((1, 1, n_slots, D), lambda b, e: (b, e, 0, 0)),
        out_shape=jax.ShapeDtypeStruct((B, E, n_slots, D), BF16),
        scratch_shapes=[pltpu.VMEM((n_slots, D), F32), pltpu.VMEM((n_slots, 1), F32)],
        compiler_params=_params("arbitrary", "arbitrary"),
        name="moe_ffn",
    )(first, busy, slot.reshape(B, E, 1, T), gate.reshape(B, E, 1, T), h2, w1, w3, w2)


def _moe_combine_kernel(first_ref, busy_ref, x_ref, mod_ref, slot_ref, y_ref, fg_ref, o_ref, acc_scr,
                        *, n_slots, n_tiles, n_lat_tiles, final):
    tm = x_ref.shape[1]

    def combine(rows):
        col_id = lax.broadcasted_iota(jnp.int32, (tm, rows), 1)
        acc = jnp.zeros((tm, D), F32)
        for e in range(N_EXPERTS):
            w0 = _slot_window(first_ref[(pl.program_id(0) * N_EXPERTS + e) * n_tiles + pl.program_id(1)], ROWS_BF16,
                              rows, n_slots)
            hit = slot_ref[0, :, e:e + 1] == (col_id + w0).astype(F32)
            acc = acc + _dot(jnp.where(hit, 1.0, 0.0).astype(BF16), y_ref[0, e, pl.ds(w0, rows), :])
        acc_scr[...] = acc

    sparse = busy_ref[pl.program_id(0) * n_tiles + pl.program_id(1)] <= SPARSE_HITS

    @pl.when(sparse)
    def _():
        combine(min(SPARSE_HITS + ROWS_BF16, n_slots))

    @pl.when(jnp.logical_not(sparse))
    def _():
        combine(min(tm + ROWS_BF16, n_slots))

    x = x_ref[0] + mod_ref[0, 5:6, :] * acc_scr[...]
    if final:
        @pl.when(pl.program_id(1) < n_lat_tiles)
        def _():
            o_ref[0] = _rms(x) * fg_ref[...]
    else:
        o_ref[0] = x


def _moe_combine(first, busy, xa, mod, slot_t, y, final_g, n_lat_tiles, n_slots, final):
    B, T, _ = xa.shape
    ctx_row = B
    tile = pl.BlockSpec((1, TM, D), lambda b, i: (b, i, 0))
    smem = pl.BlockSpec(memory_space=pltpu.SMEM)
    if final:
        out_spec = pl.BlockSpec((1, TM, D), lambda b, i: (b, jnp.minimum(i, n_lat_tiles - 1), 0))
        out_rows = n_lat_tiles * TM
    else:
        out_spec, out_rows = tile, T
    return pl.pallas_call(
        functools.partial(_moe_combine_kernel, n_slots=n_slots, n_tiles=T // TM, n_lat_tiles=n_lat_tiles,
                          final=final),
        grid=(B, T // TM),
        in_specs=[smem, smem, tile,
                  pl.BlockSpec((1, 6, D), lambda b, i: (jnp.where(i >= n_lat_tiles, ctx_row, b), 0, 0)),
                  pl.BlockSpec((1, TM, N_EXPERTS), lambda b, i: (b, i, 0)),
                  pl.BlockSpec((1, N_EXPERTS, n_slots, D), lambda b, i: (b, 0, 0, 0)),
                  pl.BlockSpec((1, D), lambda b, i: (0, 0))],
        out_specs=out_spec,
        out_shape=jax.ShapeDtypeStruct((B, out_rows, D), F32),
        scratch_shapes=[pltpu.VMEM((TM, D), F32)],
        compiler_params=_params("arbitrary", "arbitrary"),
        name="moe_combine",
    )(first, busy, xa, mod, slot_t, y, final_g)


def _swa_head_perm():
    return [kv * SWA_GROUP + g for g in range(SWA_GROUP) for kv in range(SWA_KV)]


def _split_w_in(w_in):
    depth = w_in.shape[0]
    o = [0, D, 2 * D, 3 * D, 6 * D, 7 * D, 7 * D + SWA_KV * HEAD, 7 * D + 2 * SWA_KV * HEAD, 10 * D + 2 * SWA_KV * HEAD]
    dq, dk, dv, hy, sq, sk, sv, gt = (w_in[:, :, o[k]:o[k + 1]] for k in range(8))
    scale = HEAD ** -0.5 * math.log2(math.e)
    sq = sq.reshape(depth, D, SWA_KV * SWA_GROUP, HEAD)[:, :, jnp.array(_swa_head_perm())].reshape(depth, D, D)
    w_attn = jnp.concatenate([dq * scale, dk, dv, sq * scale, sk, sv], axis=-1).astype(BF16)
    w_hg = jnp.concatenate([hy, gt], axis=-1).astype(BF16)
    return w_attn, w_hg


def kernel(x, c, ctx, c_ctx, w_mod, b_mod, norm1_g, norm2_g, w_in, diff_lambda, diff_subln_g, hy_conv_w,
           hy_conv_b, hy_ff_w1, hy_ff_b1, hy_ff_w2, hy_ff_b2, hy_ff_w3, hy_sin_freq, hy_bias, swa_sink,
           w_branch, w_out, router_w, moe_w1, moe_w3, moe_w2, final_g):
    B, L, _ = x.shape
    Lc = ctx.shape[1]
    depth = w_mod.shape[0]
    T = L + Lc
    assert Lc == TM and L % (4 * TM) == 0 and x.shape[2] == D
    n_lat_tiles = L // TM
    cap_l = CAPACITY_FACTOR * L // N_EXPERTS
    cap_c = CAPACITY_FACTOR * Lc // N_EXPERTS
    n_slots = cap_l + cap_c
    segments = ((0, L, cap_l, 0), (L, Lc, cap_c, cap_l))

    xa = jnp.concatenate([x, ctx], axis=1)
    mods = _mods(c, c_ctx, w_mod, b_mod)
    cos_t, sin_t = _rope_tables(L, Lc)
    tabs = _dft_tables(L, Lc)
    w_attn, w_hg = _split_w_in(w_in)
    rope_flags = tuple([True] * 16 + [False] * 8 + [True] * 8 + [True] * 2 + [False] * 2)
    perm = jnp.array(_swa_head_perm())
    wb = w_branch.at[:, 2].set(
        w_branch[:, 2].reshape(depth, SWA_KV * SWA_GROUP, HEAD, D)[:, perm].reshape(depth, D, D)).astype(BF16)
    wo = w_out.astype(BF16)
    w1, w3, w2 = moe_w1, moe_w3, moe_w2
    rw_pad = jnp.pad(router_w, ((0, 0), (0, 0), (0, LANES - N_EXPERTS)))
    w1p = jnp.pad(hy_ff_w1, ((0, 0), (0, LANES - HY_EMB), (0, 0)))

    for l in range(depth):
        lambda_init = 0.8 - 0.6 * math.exp(-0.3 * l)
        mod = mods[l]
        p_attn, p_hg = _norm_proj(xa, mod, norm1_g[l][None], w_attn[l], w_hg[l], cos_t, sin_t, rope_flags,
                                  n_lat_tiles)
        y_diff = _diff_attn(p_attn, diff_lambda, diff_subln_g, L, Lc, lambda_init, l)
        y_swa = _swa(p_attn, swa_sink, L, Lc, l)
        hp = dict(w1p=w1p[l], b1=hy_ff_b1[l][None], w2=hy_ff_w2[l], b2=hy_ff_b2[l][None],
                  fr=hy_sin_freq[l][None], w3=hy_ff_w3[l], conv_w=hy_conv_w[l], conv_b=hy_conv_b[l][None],
                  bias=hy_bias[l][None])
        hy = _hyena(p_hg, hp, tabs, L, Lc)
        xa, h2, logits_t = _merge(xa, mod, y_diff, hy, hp['bias'], y_swa, p_hg, wb[l], wo[l], norm2_g[l][None],
                                  rw_pad[l], n_lat_tiles)
        slot, gate, first = _router(logits_t, segments)
        first = jnp.concatenate([first[:, :, :n_lat_tiles], jnp.full((B, N_EXPERTS, 1), cap_l, F32)], axis=2)
        first = first.astype(jnp.int32)
        ends = jnp.concatenate([first[:, :, 1:n_lat_tiles], jnp.full((B, N_EXPERTS, 1), cap_l, jnp.int32),
                                jnp.full((B, N_EXPERTS, 1), n_slots, jnp.int32)], axis=2)
        hits = ends - first
        first = first.reshape(-1)
        y_e = _moe_ffn(first, jnp.max(hits, axis=2).reshape(-1), slot, gate, h2, w1, w3, w2, l, n_slots)
        xa = _moe_combine(first, jnp.max(hits, axis=1).reshape(-1), xa, mod, jnp.swapaxes(slot, 1, 2), y_e,
                          final_g[None], n_lat_tiles, n_slots, final=l == depth - 1)
    return xa
```
